```python
import math
import jax, jax.numpy as jnp
from jax import lax
import numpy as np

D_MODEL = 1024
BATCH = 8
SEQ = 4096
DEPTH = 2

M_HEADS = 4
M_HEAD_DIM = 128
M_WIDTH = M_HEADS * M_HEAD_DIM
M_CONV = 4
M_CHUNK = 128
M_INIT = -1e30
ROPE_DIM = 64
ROPE_THETA = 10000.0
S_HEADS = 8
S_HEAD_DIM = ROPE_DIM
S_WIDTH = S_HEADS * S_HEAD_DIM
S_KV_RANK = 256
IDX_HEADS = 4
IDX_DIM = ROPE_DIM
IDX_TOPK_MAX = 256
S_QBLOCK = 64
DA_HEADS = 4
DA_HEAD_DIM = ROPE_DIM
DA_V_DIM = 2 * DA_HEAD_DIM
DA_WIDTH = DA_HEADS * DA_V_DIM
DA_QBLOCK = 128
N_BRANCH = 3
BRANCH_WIDTH = 512
FF_DIM = ((8 * D_MODEL + 3 * 256 - 1) // (3 * 256)) * 256
EPS = 1e-6

_SEG_SIZES = (
    2 * M_WIDTH,
    M_WIDTH,
    M_WIDTH,
    2 * M_HEADS,
    S_WIDTH,
    S_KV_RANK,
    IDX_HEADS * IDX_DIM,
    IDX_DIM,
    IDX_HEADS,
    2 * DA_HEADS * DA_HEAD_DIM,
    2 * DA_HEADS * DA_HEAD_DIM,
    DA_WIDTH,
    N_BRANCH * D_MODEL,
)
IN_COLS = sum(_SEG_SIZES)

kernel_name = 'hybrid_mlstm_dsa_diffattn_gated'


def _split_cols(z):
    offs = []
    acc = 0
    for s in _SEG_SIZES[:-1]:
        acc += s
        offs.append(acc)
    return jnp.split(z, offs, axis=-1)


def _rms(x, g):
    xf = x.astype(jnp.float32)
    y = xf * lax.rsqrt(jnp.mean(xf * xf, axis=-1, keepdims=True) + EPS)
    return (y * g.astype(jnp.float32)).astype(x.dtype)


def _rope_tables(seq, dim, dtype):
    inv = 1.0 / jnp.power(ROPE_THETA, jnp.arange(0, dim, 2, dtype=jnp.float32) / dim)
    ang = jnp.arange(seq, dtype=jnp.float32)[:, None] * inv[None, :]
    return jnp.cos(ang).astype(dtype), jnp.sin(ang).astype(dtype)


def _rope(x, cos, sin):
    half = x.shape[-1] // 2
    shape = (1, cos.shape[0]) + (1,) * (x.ndim - 3) + (half,)
    c = cos.reshape(shape)
    s = sin.reshape(shape)
    x1, x2 = x[..., :half], x[..., half:]
    return jnp.concatenate([x1 * c - x2 * s, x1 * s + x2 * c], axis=-1)


def _causal_conv(x, w, b):
    k, c = w.shape
    y = lax.conv_general_dilated(x, w[:, None, :], window_strides=(1,), padding=[(k - 1, 0)],
                                 dimension_numbers=('NWC', 'WIO', 'NWC'), feature_group_count=c)
    return y + b


def _mlstm(q, k, v, i_pre, f_pre):
    B, T, H, dh = q.shape
    L = M_CHUNK
    N = T // L
    f32 = jnp.float32

    def chunk(a):
        a = a.astype(f32).reshape((B, N, L, H) + a.shape[3:])
        return jnp.moveaxis(a, 3, 1)

    qc = chunk(q)
    kc = chunk(k) * (dh ** -0.5)
    vc = chunk(v)
    ic = chunk(i_pre)
    lf = jax.nn.log_sigmoid(chunk(f_pre))
    b = jnp.cumsum(lf, axis=-1)
    g = b[..., -1]

    a = g[..., None] - b + ic
    m_loc = a.max(axis=-1)
    w_loc = jnp.exp(a - m_loc[..., None])
    c_loc = jnp.einsum('bhnlv,bhnlk->bhnvk', vc * w_loc[..., None], kc)
    n_loc = jnp.einsum('bhnl,bhnlk->bhnk', w_loc, kc)

    def step(state, xs):
        c_st, n_st, m_st = state
        g_n, c_n, nn_n, m_n = xs
        m_new = jnp.maximum(g_n + m_st, m_n)
        s_old = jnp.exp(g_n + m_st - m_new)
        s_loc = jnp.exp(m_n - m_new)
        c_new = s_old[..., None, None] * c_st + s_loc[..., None, None] * c_n
        n_new = s_old[..., None] * n_st + s_loc[..., None] * nn_n
        return (c_new, n_new, m_new), (c_st, n_st, m_st)

    init = (jnp.zeros((B, H, dh, dh), f32), jnp.zeros((B, H, dh), f32), jnp.full((B, H), M_INIT, f32))
    xs = (jnp.moveaxis(g, 2, 0), jnp.moveaxis(c_loc, 2, 0), jnp.moveaxis(n_loc, 2, 0), jnp.moveaxis(m_loc, 2, 0))
    _, (c_prev, n_prev, m_prev) = lax.scan(step, init, xs)
    c_prev = jnp.moveaxis(c_prev, 0, 2)
    n_prev = jnp.moveaxis(n_prev, 0, 2)
    m_prev = jnp.moveaxis(m_prev, 0, 2)

    t_idx = jnp.arange(L)
    causal = t_idx[:, None] >= t_idx[None, :]
    dmat = b[..., :, None] - b[..., None, :] + ic[..., None, :]
    dmat = jnp.where(causal, dmat, -jnp.inf)
    inter = b + m_prev[..., None]
    m_t = jnp.maximum(inter, dmat.max(axis=-1))
    sw = jnp.exp(dmat - m_t[..., None]) * jnp.einsum('bhntk,bhnsk->bhnts', qc, kc)
    s_inter = jnp.exp(inter - m_t)
    num = jnp.einsum('bhnts,bhnsv->bhntv', sw, vc) + s_inter[..., None] * jnp.einsum('bhntk,bhnvk->bhntv', qc, c_prev)
    den = sw.sum(axis=-1) + s_inter * jnp.einsum('bhntk,bhnk->bhnt', qc, n_prev)
    h = num / jnp.maximum(jnp.abs(den), jnp.exp(-m_t))[..., None]
    return jnp.moveaxis(h, 1, 3).reshape(B, T, H, dh)


def _dsa(q, k, v, qi, ki, wi):
    B, T, H, dh = q.shape
    topk = min(IDX_TOPK_MAX, T // 4)
    nb = T // S_QBLOCK
    f32 = jnp.float32
    bidx = jnp.arange(B)[:, None, None]
    ki32 = ki.astype(f32)
    key_pos = jnp.arange(T)
    idx_scale = (IDX_HEADS * IDX_DIM) ** -0.5

    def blocks(a):
        return jnp.moveaxis(a.reshape((B, nb, S_QBLOCK) + a.shape[2:]), 1, 0)

    def blk(args):
        qb, qib, wib, start = args
        tq = start + jnp.arange(S_QBLOCK)
        causal = key_pos[None, :] <= tq[:, None]
        rel = jax.nn.relu(jnp.einsum('bqhd,bsd->bqhs', qib.astype(f32), ki32))
        score = jnp.einsum('bqh,bqhs->bqs', wib.astype(f32) * idx_scale, rel)
        score = jnp.where(causal[None], score, -jnp.inf)
        _, idx = lax.top_k(score, topk)
        valid = idx <= tq[None, :, None]
        k_sel = k[bidx, idx]
        v_sel = v[bidx, idx]
        s = jnp.einsum('bqhd,bqkhd->bhqk', qb, k_sel).astype(f32) * (dh ** -0.5)
        s = jnp.where(valid[:, None], s, -jnp.inf)
        p = jax.nn.softmax(s, axis=-1).astype(v.dtype)
        return jnp.einsum('bhqk,bqkhd->bqhd', p, v_sel)

    starts = jnp.arange(nb, dtype=jnp.int32) * S_QBLOCK
    out = lax.map(blk, (blocks(q), blocks(qi), blocks(wi), starts))
    return jnp.moveaxis(out, 0, 1).reshape(B, T, H, dh)


def _diff_attn(q, k, v, lam):
    B, T, H, _, d = q.shape
    nb = T // DA_QBLOCK
    f32 = jnp.float32
    key_pos = jnp.arange(T)

    def blk(args):
        qb, start = args
        tq = start + jnp.arange(DA_QBLOCK)
        causal = key_pos[None, :] <= tq[:, None]
        s = jnp.einsum('bqhcd,bshcd->bhcqs', qb, k).astype(f32) * (d ** -0.5)
        s = jnp.where(causal, s, -jnp.inf)
        p = jax.nn.softmax(s, axis=-1)
        a = p[:, :, 0] - lam * p[:, :, 1]
        return jnp.einsum('bhqs,bshv->bqhv', a.astype(v.dtype), v)

    qblk = jnp.moveaxis(q.reshape((B, nb, DA_QBLOCK) + q.shape[2:]), 1, 0)
    starts = jnp.arange(nb, dtype=jnp.int32) * DA_QBLOCK
    out = lax.map(blk, (qblk, starts))
    return jnp.moveaxis(out, 0, 1).reshape(B, T, H, v.shape[-1])


def _layer(x, li, cos, sin, norm_mix_g, w_in, b_gate, conv_w, conv_b, gate_b, m_norm_g,
           kv_norm_g, w_kv_up, sq_g, sk_g, dq_g, dk_g, lam, d_out_g, w_branch, w_out,
           norm_ffn_g, w_gate_up, w_down):
    B, T, _ = x.shape
    f32 = jnp.float32
    xn = _rms(x, norm_mix_g)
    z = jnp.einsum('btd,dc->btc', xn, w_in)
    (m_qk, m_v, m_o, m_if, s_q, s_ckv, i_q, i_k, i_w, d_q, d_k, d_v, g_pre) = _split_cols(z)

    m_qk = jax.nn.silu(_causal_conv(m_qk, conv_w, conv_b))
    mq, mk = jnp.split(m_qk, 2, axis=-1)
    m_if = m_if + gate_b
    h = _mlstm(mq.reshape(B, T, M_HEADS, M_HEAD_DIM), mk.reshape(B, T, M_HEADS, M_HEAD_DIM),
               m_v.reshape(B, T, M_HEADS, M_HEAD_DIM), m_if[..., :M_HEADS], m_if[..., M_HEADS:])
    h = _rms(h, m_norm_g).reshape(B, T, M_WIDTH)
    y_a = (jax.nn.sigmoid(m_o.astype(f32)) * h).astype(x.dtype)

    sq = _rope(_rms(s_q.reshape(B, T, S_HEADS, S_HEAD_DIM), sq_g), cos, sin)
    ckv = _rms(s_ckv, kv_norm_g)
    kv = jnp.einsum('btr,rc->btc', ckv, w_kv_up).reshape(B, T, 2, S_HEADS, S_HEAD_DIM)
    sk = _rope(_rms(kv[:, :, 0], sk_g), cos, sin)
    sv = kv[:, :, 1]
    iq = _rope(i_q.reshape(B, T, IDX_HEADS, IDX_DIM), cos, sin)
    ik = _rope(i_k, cos, sin)
    y_b = _dsa(sq, sk, sv, iq, ik, i_w).reshape(B, T, S_WIDTH)

    dq = _rope(_rms(d_q.reshape(B, T, DA_HEADS, 2, DA_HEAD_DIM), dq_g), cos, sin)
    dk = _rope(_rms(d_k.reshape(B, T, DA_HEADS, 2, DA_HEAD_DIM), dk_g), cos, sin)
    dv = d_v.reshape(B, T, DA_HEADS, DA_V_DIM)
    lamf = lam.astype(f32)
    lam_init = 0.8 - 0.6 * math.exp(-0.3 * li)
    lam_val = jnp.exp(jnp.sum(lamf[0] * lamf[1])) - jnp.exp(jnp.sum(lamf[2] * lamf[3])) + lam_init
    o = _diff_attn(dq, dk, dv, lam_val)
    y_c = (_rms(o, d_out_g) * (1.0 - lam_init)).reshape(B, T, DA_WIDTH)

    gates = jax.nn.sigmoid((g_pre + b_gate).astype(f32)).astype(x.dtype).reshape(B, T, N_BRANCH, D_MODEL)
    merged = (gates[:, :, 0] * jnp.einsum('btc,cd->btd', y_a, w_branch[0])
              + gates[:, :, 1] * jnp.einsum('btc,cd->btd', y_b, w_branch[1])
              + gates[:, :, 2] * jnp.einsum('btc,cd->btd', y_c, w_branch[2]))
    x = x + jnp.einsum('btd,de->bte', merged, w_out)

    xn = _rms(x, norm_ffn_g)
    gu = jnp.einsum('btd,df->btf', xn, w_gate_up)
    g, u = jnp.split(gu, 2, axis=-1)
    return x + jnp.einsum('btf,fd->btd', jax.nn.silu(g) * u, w_down)


def setup_inputs(seed: int = 0) -> dict:
    key = jax.random.key(seed)
    ks = jax.random.split(key, 24)
    nrm = jax.random.normal
    f32 = jnp.float32

    def gain(k, shape):
        return 1.0 + 0.02 * nrm(k, shape, f32)

    i_bias = 0.1 * nrm(ks[6], (DEPTH, M_HEADS), f32)
    f_bias = jnp.linspace(3.0, 6.0, M_HEADS, dtype=f32)[None, :] + 0.1 * nrm(ks[7], (DEPTH, M_HEADS), f32)
    return {
        'x': nrm(ks[0], (BATCH, SEQ, D_MODEL), f32),
        'norm_mix_g': gain(ks[1], (DEPTH, D_MODEL)),
        'w_in': nrm(ks[2], (DEPTH, D_MODEL, IN_COLS), f32) * D_MODEL ** -0.5,
        'b_gate': 0.02 * nrm(ks[3], (DEPTH, N_BRANCH * D_MODEL), f32),
        'mlstm_conv_w': nrm(ks[4], (DEPTH, M_CONV, 2 * M_WIDTH), f32) * M_CONV ** -0.5,
        'mlstm_conv_b': 0.02 * nrm(ks[5], (DEPTH, 2 * M_WIDTH), f32),
        'mlstm_gate_b': jnp.concatenate([i_bias, f_bias], axis=-1),
        'mlstm_norm_g': gain(ks[8], (DEPTH, M_HEADS, M_HEAD_DIM)),
        'dsa_kv_norm_g': gain(ks[9], (DEPTH, S_KV_RANK)),
        'dsa_w_kv_up': nrm(ks[10], (DEPTH, S_KV_RANK, 2 * S_WIDTH), f32) * S_KV_RANK ** -0.5,
        'dsa_q_norm_g': gain(ks[11], (DEPTH, S_HEAD_DIM)),
        'dsa_k_norm_g': gain(ks[12], (DEPTH, S_HEAD_DIM)),
        'diff_q_norm_g': gain(ks[13], (DEPTH, DA_HEAD_DIM)),
        'diff_k_norm_g': gain(ks[14], (DEPTH, DA_HEAD_DIM)),
        'diff_lambda': 0.1 * nrm(ks[15], (DEPTH, 4, DA_HEAD_DIM), f32),
        'diff_out_norm_g': gain(ks[16], (DEPTH, DA_V_DIM)),
        'w_branch': nrm(ks[17], (DEPTH, N_BRANCH, BRANCH_WIDTH, D_MODEL), f32) * BRANCH_WIDTH ** -0.5,
        'w_out': nrm(ks[18], (DEPTH, D_MODEL, D_MODEL), f32) * D_MODEL ** -0.5,
        'norm_ffn_g': gain(ks[19], (DEPTH, D_MODEL)),
        'w_gate_up': nrm(ks[20], (DEPTH, D_MODEL, 2 * FF_DIM), f32) * D_MODEL ** -0.5,
        'w_down': nrm(ks[21], (DEPTH, FF_DIM, D_MODEL), f32) * FF_DIM ** -0.5,
    }


def reference(x, norm_mix_g, w_in, b_gate, mlstm_conv_w, mlstm_conv_b, mlstm_gate_b, mlstm_norm_g,
              dsa_kv_norm_g, dsa_w_kv_up, dsa_q_norm_g, dsa_k_norm_g, diff_q_norm_g, diff_k_norm_g,
              diff_lambda, diff_out_norm_g, w_branch, w_out, norm_ffn_g, w_gate_up, w_down):
    cos, sin = _rope_tables(x.shape[1], ROPE_DIM, x.dtype)
    for li in range(DEPTH):
        x = _layer(x, li, cos, sin, norm_mix_g[li], w_in[li], b_gate[li], mlstm_conv_w[li],
                   mlstm_conv_b[li], mlstm_gate_b[li], mlstm_norm_g[li], dsa_kv_norm_g[li],
                   dsa_w_kv_up[li], dsa_q_norm_g[li], dsa_k_norm_g[li], diff_q_norm_g[li],
                   diff_k_norm_g[li], diff_lambda[li], diff_out_norm_g[li], w_branch[li], w_out[li],
                   norm_ffn_g[li], w_gate_up[li], w_down[li])
    return x
```

```python
import functools
import math

import jax
import jax.numpy as jnp
from jax import lax
from jax.experimental import pallas as pl
from jax.experimental.pallas import tpu as pltpu

F32 = jnp.float32
BF16 = jnp.bfloat16
I32 = jnp.int32

D_MODEL = 1024
EPS = 1e-6
ROPE_DIM = 64
ROPE_HALF = ROPE_DIM // 2
ROPE_THETA = 10000.0

M_HEADS = 4
M_HEAD_DIM = 128
M_WIDTH = M_HEADS * M_HEAD_DIM
M_CONV = 4
M_CHUNK = 128
M_INIT = -1e30

S_HEADS = 8
S_WIDTH = S_HEADS * ROPE_DIM
S_KV_RANK = 256
IDX_HEADS = 4
IDX_WIDTH = IDX_HEADS * ROPE_DIM
IDX_TOPK_MAX = 256

DA_HEADS = 4
DA_MAPS = 2 * DA_HEADS
DA_V_DIM = 2 * ROPE_DIM
DA_WIDTH = DA_HEADS * DA_V_DIM

N_BRANCH = 3
FF_DIM = 2816

NEG = -1e30
INT_MIN = -(2**31)
LANE = 128
HEAD_GROUP = 256

COL_GATE = 0
COL_MQK = 3072
COL_MV = 4096
COL_MO = 4608
COL_SQ = 5120
COL_DQ = 5632
COL_DK = 6144
COL_DV = 6656
COL_CKV = 7168
COL_IQ = 7424
COL_IK4 = 7680
COL_MISC = 7936
MISC_IF = 0
MISC_IW = 8
Z_COLS = 8192

VMEM_LIMIT = 56 * 1024 * 1024


def _cparams(sem):
    return pltpu.CompilerParams(dimension_semantics=sem, vmem_limit_bytes=VMEM_LIMIT)


def _sigmoid(x):
    return 1.0 / (1.0 + jnp.exp(-x))


def _dot(a, b):
    return jnp.dot(a, b, preferred_element_type=F32)


def _dot_nt(a, b):
    return lax.dot_general(a, b, (((1,), (1,)), ((), ())), preferred_element_type=F32)


def _dot_tn(a, b):
    return lax.dot_general(a, b, (((0,), (0,)), ((), ())), preferred_element_type=F32)


def _split3(x):
    hi = x.astype(BF16)
    r1 = x - hi.astype(F32)
    mid = r1.astype(BF16)
    lo = (r1 - mid.astype(F32)).astype(BF16)
    return hi, mid, lo


def _inproj_kernel(x_ref, g_ref, w_ref, z_ref, xn_ref):
    @pl.when(pl.program_id(1) == 0)
    def _():
        x = x_ref[...]
        ms = jnp.mean(x * x, axis=-1, keepdims=True)
        xn_ref[...] = (x * lax.rsqrt(ms + EPS) * g_ref[...]).astype(BF16)

    z_ref[...] = _dot(xn_ref[...], w_ref[...])


def _inproj(x2, g, w, *, tm=1024, tn=1024):
    n = x2.shape[0]
    tm = min(tm, n)
    return pl.pallas_call(
        _inproj_kernel,
        grid=(n // tm, Z_COLS // tn),
        in_specs=[
            pl.BlockSpec((tm, D_MODEL), lambda i, j: (i, 0)),
            pl.BlockSpec((1, D_MODEL), lambda i, j: (0, 0)),
            pl.BlockSpec((D_MODEL, tn), lambda i, j: (0, j)),
        ],
        out_specs=pl.BlockSpec((tm, tn), lambda i, j: (i, j)),
        out_shape=jax.ShapeDtypeStruct((n, Z_COLS), F32),
        scratch_shapes=[pltpu.VMEM((tm, D_MODEL), BF16)],
        compiler_params=_cparams(("parallel", "arbitrary")),
        name="inproj",
    )(x2, g, w)


def _mlstm_kernel(qk_ref, v_ref, o_ref, misc_ref, cw_ref, cb_ref, gb_ref, ng_ref, tri_ref, y_ref,
                  tail_ref, qkc_ref, c_ref, n_ref, m_ref):
    L = M_CHUNK
    dh = M_HEAD_DIM

    @pl.when(pl.program_id(1) == 0)
    def _():
        tail_ref[...] = jnp.zeros_like(tail_ref)
        c_ref[...] = jnp.zeros_like(c_ref)
        n_ref[...] = jnp.zeros_like(n_ref)
        m_ref[...] = jnp.full_like(m_ref, M_INIT)

    x = qk_ref[...]
    tail = tail_ref[...]
    row8 = lax.broadcasted_iota(I32, (8, 1), 0)
    cw = cw_ref[...]
    cb = cb_ref[...]
    acc = x * cw[M_CONV - 1:M_CONV] + cb
    acc_head = x[0:8] * cw[M_CONV - 1:M_CONV] + cb
    for j in range(1, M_CONV):
        wj = cw[M_CONV - 1 - j:M_CONV - j]
        xr = pltpu.roll(x, j, axis=0)
        tr = pltpu.roll(tail, j, axis=0)
        acc = acc + xr * wj
        acc_head = acc_head + jnp.where(row8 < j, tr, xr[0:8]) * wj
    qkc_ref[...] = acc * _sigmoid(acc)
    qkc_ref[0:8, :] = acc_head * _sigmoid(acc_head)
    tail_ref[...] = x[L - 8:L]

    gates = misc_ref[...] + gb_ref[...]
    lf = jnp.minimum(gates, 0.0) - jnp.log(1.0 + jnp.exp(-jnp.abs(gates)))
    tri = tri_ref[...]
    hi, mid, lo = _split3(lf)
    bcum = _dot(tri, hi) + _dot(tri, mid) + _dot(tri, lo)
    bcum_t = bcum.T
    gates_t = gates.T

    t_idx = lax.broadcasted_iota(I32, (L, L), 0)
    s_idx = lax.broadcasted_iota(I32, (L, L), 1)
    causal = t_idx >= s_idx

    for h in range(M_HEADS):
        sl = slice(h * dh, (h + 1) * dh)
        q = qkc_ref[:, h * dh:(h + 1) * dh]
        k = qkc_ref[:, M_WIDTH + h * dh:M_WIDTH + (h + 1) * dh] * (dh ** -0.5)
        v = v_ref[:, sl]
        qb, kb, vb = q.astype(BF16), k.astype(BF16), v.astype(BF16)

        b_col = bcum[:, M_HEADS + h:M_HEADS + h + 1]
        i_col = gates[:, h:h + 1]
        b_row = bcum_t[M_HEADS + h:M_HEADS + h + 1, :]
        i_row = gates_t[h:h + 1, :]
        g_tot = bcum[L - 1:L, M_HEADS + h:M_HEADS + h + 1]

        c_prev = c_ref[h]
        n_prev = n_ref[h]
        m_prev = m_ref[h][0:1, 0:1]

        dmat = jnp.where(causal, b_col - b_row + i_row, -jnp.inf)
        inter = b_col + m_prev
        m_t = jnp.maximum(inter, jnp.max(dmat, axis=-1, keepdims=True))
        sw = jnp.exp(dmat - m_t) * _dot_nt(qb, kb)
        s_inter = jnp.exp(inter - m_t)
        num = _dot(sw.astype(BF16), vb) + s_inter * _dot_nt(qb, c_prev.astype(BF16))
        den = jnp.sum(sw, axis=-1, keepdims=True) + s_inter * jnp.sum(q * n_prev, axis=-1, keepdims=True)
        hh = num / jnp.maximum(jnp.abs(den), jnp.exp(-m_t))
        ms = jnp.mean(hh * hh, axis=-1, keepdims=True)
        hn = hh * lax.rsqrt(ms + EPS) * ng_ref[:, sl]
        y_ref[:, sl] = _sigmoid(o_ref[:, sl]) * hn

        a_col = g_tot - b_col + i_col
        m_loc = jnp.max(a_col, axis=0, keepdims=True)
        w_loc = jnp.exp(a_col - m_loc)
        c_loc = _dot_tn((v * w_loc).astype(BF16), kb)
        n_loc = jnp.sum(k * w_loc, axis=0, keepdims=True)
        m_new = jnp.maximum(g_tot + m_prev, m_loc)
        s_old = jnp.exp(g_tot + m_prev - m_new)
        s_loc = jnp.exp(m_loc - m_new)
        c_ref[h] = s_old * c_prev + s_loc * c_loc
        n_ref[h] = s_old * n_prev + s_loc * n_loc
        m_ref[h] = jnp.broadcast_to(m_new, (8, LANE))


def _mlstm(z, conv_w, conv_b, gate_b_row, norm_g_row, tri, batch, seq):
    n = z.shape[0]
    nc = seq // M_CHUNK
    L = M_CHUNK
    row = lambda b, c: b * nc + c
    return pl.pallas_call(
        _mlstm_kernel,
        grid=(batch, nc),
        in_specs=[
            pl.BlockSpec((L, 2 * M_WIDTH), lambda b, c: (row(b, c), COL_MQK // (2 * M_WIDTH))),
            pl.BlockSpec((L, M_WIDTH), lambda b, c: (row(b, c), COL_MV // M_WIDTH)),
            pl.BlockSpec((L, M_WIDTH), lambda b, c: (row(b, c), COL_MO // M_WIDTH)),
            pl.BlockSpec((L, LANE), lambda b, c: (row(b, c), COL_MISC // LANE)),
            pl.BlockSpec((M_CONV, 2 * M_WIDTH), lambda b, c: (0, 0)),
            pl.BlockSpec((1, 2 * M_WIDTH), lambda b, c: (0, 0)),
            pl.BlockSpec((1, LANE), lambda b, c: (0, 0)),
            pl.BlockSpec((1, M_WIDTH), lambda b, c: (0, 0)),
            pl.BlockSpec((L, L), lambda b, c: (0, 0)),
        ],
        out_specs=pl.BlockSpec((L, M_WIDTH), lambda b, c: (row(b, c), 0)),
        out_shape=jax.ShapeDtypeStruct((n, M_WIDTH), F32),
        scratch_shapes=[
            pltpu.VMEM((8, 2 * M_WIDTH), F32),
            pltpu.VMEM((L, 2 * M_WIDTH), F32),
            pltpu.VMEM((M_HEADS, M_HEAD_DIM, M_HEAD_DIM), F32),
            pltpu.VMEM((M_HEADS, 1, M_HEAD_DIM), F32),
            pltpu.VMEM((M_HEADS, 8, LANE), F32),
        ],
        compiler_params=_cparams(("parallel", "arbitrary")),
        name="mlstm",
    )(z, z, z, z, conv_w, conv_b, gate_b_row, norm_g_row, tri)


def _rope(x, c, s):
    w = x.shape[-1]
    lane = lax.broadcasted_iota(I32, (1, w), 1)
    fwd = pltpu.roll(x, ROPE_HALF, axis=1)
    bwd = pltpu.roll(x, w - ROPE_HALF, axis=1)
    swapped = jnp.where((lane & (ROPE_DIM - 1)) < ROPE_HALF, bwd, fwd)
    return x * c + swapped * s


def _head_rms(x, g, grp):
    sq = x * x
    hi = sq.astype(BF16)
    lo = (sq - hi.astype(F32)).astype(BF16)
    ss = _dot(hi, grp) + _dot(lo, grp)
    return x * lax.rsqrt(ss * (1.0 / ROPE_DIM) + EPS) * g


def _prep_kernel(sq_ref, dq_ref, dk_ref, dv_ref, ckv_ref, iq_ref, ik_ref, c_ref, s_ref, grp_ref,
                 kvg_ref, wkv_ref, sqg_ref, skg_ref, dqg_ref, dkg_ref,
                 sq_o, sk_o, sv_o, iq_o, ik_o, dq_o, dk_o, dv_o):
    c256 = c_ref[...]
    s256 = s_ref[...]
    c512 = jnp.concatenate([c256, c256], axis=1)
    s512 = jnp.concatenate([s256, s256], axis=1)
    grp = grp_ref[...]
    qscale = ROPE_DIM ** -0.5

    sq_o[...] = (_rope(_head_rms(sq_ref[...], sqg_ref[...], grp), c512, s512) * qscale).astype(BF16)

    ckv = ckv_ref[...]
    ms = jnp.mean(ckv * ckv, axis=-1, keepdims=True)
    ckvn = (ckv * lax.rsqrt(ms + EPS) * kvg_ref[...]).astype(BF16)
    kv = _dot(ckvn, wkv_ref[...])
    sk_o[...] = _rope(_head_rms(kv[:, :S_WIDTH], skg_ref[...], grp), c512, s512).astype(BF16)
    sv_o[...] = kv[:, S_WIDTH:].astype(BF16)

    iq_o[...] = _rope(iq_ref[...], c256, s256).astype(BF16)
    ik_o[...] = _rope(ik_ref[...], c256, s256).astype(BF16)

    dq_o[...] = (_rope(_head_rms(dq_ref[...], dqg_ref[...], grp), c512, s512) * qscale).astype(BF16)
    dk_o[...] = _rope(_head_rms(dk_ref[...], dkg_ref[...], grp), c512, s512).astype(BF16)
    dv_o[...] = dv_ref[...].astype(BF16)


def _prep(z, rope_c, rope_s, grp, kv_g, w_kv, sq_g, sk_g, dq_g, dk_g, seq, *, tm=512):
    n = z.shape[0]
    tm = min(tm, seq)
    nt = seq // tm
    zspec = lambda w, col: pl.BlockSpec((tm, w), lambda i: (i, col // w))
    const = lambda shape: pl.BlockSpec(shape, lambda i: (0, 0))
    o512 = pl.BlockSpec((tm, 512), lambda i: (i, 0))
    o256 = pl.BlockSpec((tm, 256), lambda i: (i, 0))
    s512 = jax.ShapeDtypeStruct((n, 512), BF16)
    s256 = jax.ShapeDtypeStruct((n, 256), BF16)
    return pl.pallas_call(
        _prep_kernel,
        grid=(n // tm,),
        in_specs=[
            zspec(512, COL_SQ), zspec(512, COL_DQ), zspec(512, COL_DK), zspec(512, COL_DV),
            zspec(256, COL_CKV), zspec(256, COL_IQ), zspec(256, COL_IK4),
            pl.BlockSpec((tm, 256), lambda i: (i % nt, 0)),
            pl.BlockSpec((tm, 256), lambda i: (i % nt, 0)),
            const((512, 512)),
            const((1, S_KV_RANK)), const((S_KV_RANK, 2 * S_WIDTH)),
            const((1, 512)), const((1, 512)), const((1, 512)), const((1, 512)),
        ],
        out_specs=[o512, o512, o512, o256, o256, o512, o512, o512],
        out_shape=[s512, s512, s512, s256, s256, s512, s512, s512],
        compiler_params=_cparams(("parallel",)),
        name="attn_prep",
    )(z, z, z, z, z, z, z, rope_c, rope_s, grp, kv_g, w_kv, sq_g, sk_g, dq_g, dk_g)


def _sortable(x):
    bits = pltpu.bitcast(x, I32)
    return bits ^ ((bits >> 31) & 0x7FFFFFFF)


def _dsa_kernel(sq_ref, iq_ref, misc_ref, sk_ref, sv_ref, ik_ref, y_ref,
                sc_ref, qm_ref, acc_ref, m_ref, l_ref, cut_ref, *, tq, tk, topk, idx_bits):
    i = pl.program_id(1)
    nkb = ((i + 1) * tq + tk - 1) // tk
    q0 = i * tq
    lane_grp = lax.broadcasted_iota(I32, (1, HEAD_GROUP), 1) // ROPE_DIM
    q_idx = q0 + lax.broadcasted_iota(I32, (tq, 1), 0)
    k_lane = lax.broadcasted_iota(I32, (1, tk), 1)

    iq = iq_ref[...]
    iqm = [jnp.where(lane_grp == h, iq, jnp.zeros_like(iq)) for h in range(IDX_HEADS)]
    w = misc_ref[...] * (IDX_WIDTH ** -0.5)
    wcol = [w[:, MISC_IW + h:MISC_IW + h + 1] for h in range(IDX_HEADS)]

    def score_body(kb, carry):
        ikb = ik_ref[pl.ds(pl.multiple_of(kb * tk, tk), tk), :]
        s = jnp.zeros((tq, tk), F32)
        for h in range(IDX_HEADS):
            s = s + wcol[h] * jnp.maximum(_dot_nt(iqm[h], ikb), 0.0)
        key = _sortable(s)
        sc_ref[kb] = jnp.where(kb * tk + k_lane <= q_idx, key, INT_MIN)
        return carry

    lax.fori_loop(0, nkb, score_body, 0)

    def count(pred):
        def body(kb, acc):
            blk = sc_ref[kb]
            for c in range(tk // LANE):
                acc = acc + jnp.where(pred(blk[:, c * LANE:(c + 1) * LANE], kb * tk + c * LANE), 1, 0)
            return acc
        acc = lax.fori_loop(0, nkb, body, jnp.zeros((tq, LANE), I32))
        return jnp.sum(acc, axis=1, keepdims=True)

    def bit_body(t, cur):
        bit = jnp.left_shift(jnp.int32(1), 31 - t)
        trial = (cur | bit) ^ INT_MIN
        cnt = count(lambda blk, base: blk >= trial)
        return jnp.where(cnt >= topk, cur | bit, cur)

    cur = lax.fori_loop(0, 32, bit_body, jnp.zeros((tq, 1), I32))
    thr = cur ^ INT_MIN
    n_ge = count(lambda blk, base: blk >= thr)
    n_gt = count(lambda blk, base: blk > thr)

    lane_i = lax.broadcasted_iota(I32, (1, LANE), 1)
    need = topk - n_gt
    cut_ref[...] = jnp.full((tq, LANE), 2**30, I32)

    @pl.when(jnp.max(n_ge) > topk)
    def _():
        def idx_body(t, cut):
            bit = jnp.left_shift(jnp.int32(1), idx_bits - 1 - t)
            trial = cut | bit
            cnt = count(lambda blk, base: (blk == thr) & (base + lane_i < trial))
            return jnp.where(cnt < need, trial, cut)
        cut = lax.fori_loop(0, idx_bits, idx_body, jnp.zeros((tq, 1), I32))
        cut_ref[...] = jnp.broadcast_to(cut, (tq, LANE))

    cut = cut_ref[:, 0:1]

    def bias_body(kb, carry):
        key = sc_ref[kb]
        kidx = kb * tk + k_lane
        sel = (key > thr) | ((key == thr) & (kidx <= cut))
        sel = sel & (kidx <= q_idx)
        sc_ref[kb] = pltpu.bitcast(jnp.where(sel, 0.0, NEG).astype(F32), I32)
        return carry

    lax.fori_loop(0, nkb, bias_body, 0)

    for h in range(S_HEADS):
        g = h // 4
        qg = sq_ref[:, g * HEAD_GROUP:(g + 1) * HEAD_GROUP]
        qm_ref[h] = jnp.where(lane_grp == (h % 4), qg, jnp.zeros_like(qg))
    acc_ref[...] = jnp.zeros_like(acc_ref)
    m_ref[...] = jnp.full_like(m_ref, NEG)
    l_ref[...] = jnp.zeros_like(l_ref)

    def attn_body(kb, carry):
        off = pl.multiple_of(kb * tk, tk)
        bias = pltpu.bitcast(sc_ref[kb], F32)
        for h in range(S_HEADS):
            g = h // 4
            kg = sk_ref[pl.ds(off, tk), g * HEAD_GROUP:(g + 1) * HEAD_GROUP]
            vg = sv_ref[pl.ds(off, tk), g * HEAD_GROUP:(g + 1) * HEAD_GROUP]
            s = _dot_nt(qm_ref[h], kg) + bias
            m_old = m_ref[h][:, 0:1]
            m_new = jnp.maximum(m_old, jnp.max(s, axis=-1, keepdims=True))
            alpha = jnp.exp(m_old - m_new)
            p = jnp.exp(s - m_new)
            l_ref[h] = jnp.broadcast_to(alpha * l_ref[h][:, 0:1] + jnp.sum(p, axis=-1, keepdims=True), (tq, LANE))
            m_ref[h] = jnp.broadcast_to(m_new, (tq, LANE))
            acc_ref[h] = alpha * acc_ref[h] + _dot(p.astype(BF16), vg)
        return carry

    lax.fori_loop(0, nkb, attn_body, 0)

    for g in range(S_HEADS // 4):
        out = jnp.zeros((tq, HEAD_GROUP), F32)
        for hh in range(4):
            h = g * 4 + hh
            out = out + jnp.where(lane_grp == hh, acc_ref[h] / l_ref[h][:, 0:1], 0.0)
        y_ref[:, g * HEAD_GROUP:(g + 1) * HEAD_GROUP] = out


def _dsa(z, sq, sk, sv, iq, ik4, batch, seq, *, tq=256, tk=512):
    n = z.shape[0]
    tq = min(tq, seq)
    tk = min(tk, seq)
    nq = seq // tq
    topk = min(IDX_TOPK_MAX, seq // 4)
    kern = functools.partial(_dsa_kernel, tq=tq, tk=tk, topk=topk, idx_bits=seq.bit_length())
    return pl.pallas_call(
        kern,
        grid=(batch, nq),
        in_specs=[
            pl.BlockSpec((tq, S_WIDTH), lambda b, i: (b * nq + i, 0)),
            pl.BlockSpec((tq, IDX_WIDTH), lambda b, i: (b * nq + i, 0)),
            pl.BlockSpec((tq, LANE), lambda b, i: (b * nq + i, COL_MISC // LANE)),
            pl.BlockSpec((seq, S_WIDTH), lambda b, i: (b, 0)),
            pl.BlockSpec((seq, S_WIDTH), lambda b, i: (b, 0)),
            pl.BlockSpec((seq, IDX_WIDTH), lambda b, i: (b, 0)),
        ],
        out_specs=pl.BlockSpec((tq, S_WIDTH), lambda b, i: (b * nq + i, 0)),
        out_shape=jax.ShapeDtypeStruct((n, S_WIDTH), F32),
        scratch_shapes=[
            pltpu.VMEM((seq // tk, tq, tk), I32),
            pltpu.VMEM((S_HEADS, tq, HEAD_GROUP), BF16),
            pltpu.VMEM((S_HEADS, tq, HEAD_GROUP), F32),
            pltpu.VMEM((S_HEADS, tq, LANE), F32),
            pltpu.VMEM((S_HEADS, tq, LANE), F32),
            pltpu.VMEM((tq, LANE), I32),
        ],
        compiler_params=_cparams(("parallel", "arbitrary")),
        name="dsa_attn",
    )(sq, iq, z, sk, sv, ik4)


def _diff_kernel(q_ref, k_ref, v_ref, lam_ref, og_ref, y_ref, qm_ref, acc_ref, m_ref, l_ref,
                 *, tq, tk, lam_init):
    i = pl.program_id(1)
    nkb = ((i + 1) * tq + tk - 1) // tk
    q0 = i * tq
    lane_grp = lax.broadcasted_iota(I32, (1, HEAD_GROUP), 1) // ROPE_DIM
    q_idx = q0 + lax.broadcasted_iota(I32, (tq, 1), 0)
    k_lane = lax.broadcasted_iota(I32, (1, tk), 1)

    for m in range(DA_MAPS):
        g = m // 4
        qg = q_ref[:, g * HEAD_GROUP:(g + 1) * HEAD_GROUP]
        qm_ref[m] = jnp.where(lane_grp == (m % 4), qg, jnp.zeros_like(qg))
    acc_ref[...] = jnp.zeros_like(acc_ref)
    m_ref[...] = jnp.full_like(m_ref, NEG)
    l_ref[...] = jnp.zeros_like(l_ref)

    def body(kb, carry):
        off = pl.multiple_of(kb * tk, tk)
        bias = jnp.where(kb * tk + k_lane <= q_idx, 0.0, NEG).astype(F32)
        for m in range(DA_MAPS):
            g = m // 4
            hd = m // 2
            kg = k_ref[pl.ds(off, tk), g * HEAD_GROUP:(g + 1) * HEAD_GROUP]
            vh = v_ref[pl.ds(off, tk), hd * DA_V_DIM:(hd + 1) * DA_V_DIM]
            s = _dot_nt(qm_ref[m], kg) + bias
            m_old = m_ref[m][:, 0:1]
            m_new = jnp.maximum(m_old, jnp.max(s, axis=-1, keepdims=True))
            alpha = jnp.exp(m_old - m_new)
            p = jnp.exp(s - m_new)
            l_ref[m] = jnp.broadcast_to(alpha * l_ref[m][:, 0:1] + jnp.sum(p, axis=-1, keepdims=True), (tq, LANE))
            m_ref[m] = jnp.broadcast_to(m_new, (tq, LANE))
            acc_ref[m] = alpha * acc_ref[m] + _dot(p.astype(BF16), vh)
        return carry

    lax.fori_loop(0, nkb, body, 0)

    lam = lam_ref[...]
    p01 = jnp.sum(lam[0:1] * lam[1:2], axis=-1, keepdims=True)
    p23 = jnp.sum(lam[2:3] * lam[3:4], axis=-1, keepdims=True)
    lam_val = jnp.exp(p01) - jnp.exp(p23) + lam_init
    for hd in range(DA_HEADS):
        o1 = acc_ref[2 * hd] / l_ref[2 * hd][:, 0:1]
        o2 = acc_ref[2 * hd + 1] / l_ref[2 * hd + 1][:, 0:1]
        o = o1 - lam_val * o2
        ms = jnp.mean(o * o, axis=-1, keepdims=True)
        y_ref[:, hd * DA_V_DIM:(hd + 1) * DA_V_DIM] = o * lax.rsqrt(ms + EPS) * og_ref[...] * (1.0 - lam_init)


def _diff_attn(dq, dk, dv, lam, out_g, batch, seq, lam_init, *, tq=256, tk=512):
    n = dq.shape[0]
    tq = min(tq, seq)
    tk = min(tk, seq)
    nq = seq // tq
    kern = functools.partial(_diff_kernel, tq=tq, tk=tk, lam_init=lam_init)
    return pl.pallas_call(
        kern,
        grid=(batch, nq),
        in_specs=[
            pl.BlockSpec((tq, 512), lambda b, i: (b * nq + i, 0)),
            pl.BlockSpec((seq, 512), lambda b, i: (b, 0)),
            pl.BlockSpec((seq, 512), lambda b, i: (b, 0)),
            pl.BlockSpec((4, ROPE_DIM), lambda b, i: (0, 0)),
            pl.BlockSpec((1, DA_V_DIM), lambda b, i: (0, 0)),
        ],
        out_specs=pl.BlockSpec((tq, DA_WIDTH), lambda b, i: (b * nq + i, 0)),
        out_shape=jax.ShapeDtypeStruct((n, DA_WIDTH), F32),
        scratch_shapes=[
            pltpu.VMEM((DA_MAPS, tq, HEAD_GROUP), BF16),
            pltpu.VMEM((DA_MAPS, tq, DA_V_DIM), F32),
            pltpu.VMEM((DA_MAPS, tq, LANE), F32),
            pltpu.VMEM((DA_MAPS, tq, LANE), F32),
        ],
        compiler_params=_cparams(("parallel", "arbitrary")),
        name="diff_attn",
    )(dq, dk, dv, lam, out_g)


def _merge_kernel(x_ref, ya_ref, yb_ref, yc_ref, gp_ref, bg_ref, wb_ref, wo_ref, ng_ref, xo_ref, xn_ref):
    merged = None
    for br, y_ref in enumerate((ya_ref, yb_ref, yc_ref)):
        sl = slice(br * D_MODEL, (br + 1) * D_MODEL)
        gate = _sigmoid(gp_ref[:, sl] + bg_ref[:, sl])
        term = gate * _dot(y_ref[...].astype(BF16), wb_ref[br])
        merged = term if merged is None else merged + term
    xo = x_ref[...] + _dot(merged.astype(BF16), wo_ref[...])
    xo_ref[...] = xo
    ms = jnp.mean(xo * xo, axis=-1, keepdims=True)
    xn_ref[...] = (xo * lax.rsqrt(ms + EPS) * ng_ref[...]).astype(BF16)


def _merge(x2, ya, yb, yc, z, b_gate, w_branch, w_out, norm_g, *, tm=512):
    n = x2.shape[0]
    tm = min(tm, n)
    row = lambda w: pl.BlockSpec((tm, w), lambda i: (i, 0))
    return pl.pallas_call(
        _merge_kernel,
        grid=(n // tm,),
        in_specs=[
            row(D_MODEL), row(512), row(512), row(512),
            pl.BlockSpec((tm, N_BRANCH * D_MODEL), lambda i: (i, COL_GATE)),
            pl.BlockSpec((1, N_BRANCH * D_MODEL), lambda i: (0, 0)),
            pl.BlockSpec((N_BRANCH, 512, D_MODEL), lambda i: (0, 0, 0)),
            pl.BlockSpec((D_MODEL, D_MODEL), lambda i: (0, 0)),
            pl.BlockSpec((1, D_MODEL), lambda i: (0, 0)),
        ],
        out_specs=[row(D_MODEL), row(D_MODEL)],
        out_shape=[jax.ShapeDtypeStruct((n, D_MODEL), F32), jax.ShapeDtypeStruct((n, D_MODEL), BF16)],
        compiler_params=_cparams(("parallel",)),
        name="merge_out",
    )(x2, ya, yb, yc, z, b_gate, w_branch, w_out, norm_g)


def _ffn_kernel(xn_ref, x_ref, wg_ref, wu_ref, wd_ref, o_ref, acc_ref):
    j = pl.program_id(1)
    xn = xn_ref[...]
    g = _dot(xn, wg_ref[...])
    u = _dot(xn, wu_ref[...])
    part = _dot((g * _sigmoid(g) * u).astype(BF16), wd_ref[...])

    @pl.when(j == 0)
    def _():
        acc_ref[...] = part

    @pl.when(j > 0)
    def _():
        acc_ref[...] += part

    @pl.when(j == pl.num_programs(1) - 1)
    def _():
        o_ref[...] = x_ref[...] + acc_ref[...]


def _ffn(xn, x2, wg, wu, wd, *, tm=512, fc=1408):
    n = x2.shape[0]
    tm = min(tm, n)
    return pl.pallas_call(
        _ffn_kernel,
        grid=(n // tm, FF_DIM // fc),
        in_specs=[
            pl.BlockSpec((tm, D_MODEL), lambda i, j: (i, 0)),
            pl.BlockSpec((tm, D_MODEL), lambda i, j: (i, 0)),
            pl.BlockSpec((D_MODEL, fc), lambda i, j: (0, j)),
            pl.BlockSpec((D_MODEL, fc), lambda i, j: (0, j)),
            pl.BlockSpec((fc, D_MODEL), lambda i, j: (j, 0)),
        ],
        out_specs=pl.BlockSpec((tm, D_MODEL), lambda i, j: (i, 0)),
        out_shape=jax.ShapeDtypeStruct((n, D_MODEL), F32),
        scratch_shapes=[pltpu.VMEM((tm, D_MODEL), F32)],
        compiler_params=_cparams(("parallel", "arbitrary")),
        name="ffn",
    )(xn, x2, wg, wu, wd)


def _layout_w_in(w):
    sizes = (2 * M_WIDTH, M_WIDTH, M_WIDTH, 2 * M_HEADS, S_WIDTH, S_KV_RANK, IDX_WIDTH, ROPE_DIM,
             IDX_HEADS, 512, 512, DA_WIDTH, N_BRANCH * D_MODEL)
    offs = [0]
    for s in sizes:
        offs.append(offs[-1] + s)
    seg = lambda k: w[:, offs[k]:offs[k + 1]]
    m_qk, m_v, m_o, m_if, s_q, s_ckv, i_q, i_k, i_w, d_q, d_k, d_v, g_pre = (seg(k) for k in range(len(sizes)))
    misc_pad = jnp.zeros((w.shape[0], LANE - 2 * M_HEADS - IDX_HEADS), w.dtype)
    tail_pad = jnp.zeros((w.shape[0], Z_COLS - COL_MISC - LANE), w.dtype)
    cols = [g_pre, m_qk, m_v, m_o, s_q, d_q, d_k, d_v, s_ckv, i_q, i_k, i_k, i_k, i_k,
            m_if, i_w, misc_pad, tail_pad]
    return jnp.concatenate(cols, axis=1).astype(BF16)


def _rope_tables(seq):
    inv = 1.0 / jnp.power(ROPE_THETA, jnp.arange(0, ROPE_DIM, 2, dtype=F32) / ROPE_DIM)
    ang = jnp.arange(seq, dtype=F32)[:, None] * inv[None, :]
    cos, sin = jnp.cos(ang), jnp.sin(ang)
    c64 = jnp.concatenate([cos, cos], axis=1)
    s64 = jnp.concatenate([-sin, sin], axis=1)
    return jnp.tile(c64, (1, 4)), jnp.tile(s64, (1, 4))


def _layer(x2, li, batch, seq, consts, norm_mix_g, w_in, b_gate, conv_w, conv_b, gate_b, m_norm_g,
           kv_norm_g, w_kv_up, sq_g, sk_g, dq_g, dk_g, lam, d_out_g, w_branch, w_out,
           norm_ffn_g, w_gate_up, w_down):
    rope_c, rope_s, grp, tri = consts
    lam_init = 0.8 - 0.6 * math.exp(-0.3 * li)
    tile8 = lambda g: jnp.tile(g, 8)[None, :]

    z = _inproj(x2, norm_mix_g[None, :], _layout_w_in(w_in))

    gate_row = jnp.zeros((1, LANE), F32).at[0, MISC_IF:MISC_IF + 2 * M_HEADS].set(gate_b)
    ya = _mlstm(z, conv_w, conv_b[None, :], gate_row, m_norm_g.reshape(1, M_WIDTH), tri, batch, seq)

    sq, sk, sv, iq, ik4, dq, dk, dv = _prep(
        z, rope_c, rope_s, grp, kv_norm_g[None, :], w_kv_up.astype(BF16),
        tile8(sq_g), tile8(sk_g), tile8(dq_g), tile8(dk_g), seq)
    yb = _dsa(z, sq, sk, sv, iq, ik4, batch, seq)
    yc = _diff_attn(dq, dk, dv, lam, d_out_g[None, :], batch, seq, lam_init)

    xo, xn = _merge(x2, ya, yb, yc, z, b_gate[None, :], w_branch.astype(BF16), w_out.astype(BF16),
                    norm_ffn_g[None, :])
    return _ffn(xn, xo, w_gate_up[:, :FF_DIM].astype(BF16), w_gate_up[:, FF_DIM:].astype(BF16),
                w_down.astype(BF16))


def kernel(x, norm_mix_g, w_in, b_gate, mlstm_conv_w, mlstm_conv_b, mlstm_gate_b, mlstm_norm_g,
           dsa_kv_norm_g, dsa_w_kv_up, dsa_q_norm_g, dsa_k_norm_g, diff_q_norm_g, diff_k_norm_g,
           diff_lambda, diff_out_norm_g, w_branch, w_out, norm_ffn_g, w_gate_up, w_down):
    batch, seq, d = x.shape
    depth = w_in.shape[0]
    rope_c, rope_s = _rope_tables(seq)
    gi = jnp.arange(512) // ROPE_DIM
    grp = (gi[:, None] == gi[None, :]).astype(BF16)
    ti = jnp.arange(M_CHUNK)
    tri = (ti[:, None] >= ti[None, :]).astype(BF16)
    consts = (rope_c, rope_s, grp, tri)
    x2 = x.reshape(batch * seq, d)
    for li in range(depth):
        x2 = _layer(x2, li, batch, seq, consts, norm_mix_g[li], w_in[li], b_gate[li], mlstm_conv_w[li],
                    mlstm_conv_b[li], mlstm_gate_b[li], mlstm_norm_g[li], dsa_kv_norm_g[li],
                    dsa_w_kv_up[li], dsa_q_norm_g[li], dsa_k_norm_g[li], diff_q_norm_g[li],
                    diff_k_norm_g[li], diff_lambda[li], diff_out_norm_g[li], w_branch[li], w_out[li],
                    norm_ffn_g[li], w_gate_up[li], w_down[li])
    return x2.reshape(batch, seq, d)
```

```python
import functools
import math

import jax
import jax.numpy as jnp
from jax import lax
from jax.experimental import pallas as pl
from jax.experimental.pallas import tpu as pltpu

F32 = jnp.float32
BF16 = jnp.bfloat16
I32 = jnp.int32
I16 = jnp.int16

D_MODEL = 1024
EPS = 1e-6
ROPE_DIM = 64
ROPE_HALF = ROPE_DIM // 2
ROPE_THETA = 10000.0

M_HEADS = 4
M_HEAD_DIM = 128
M_WIDTH = M_HEADS * M_HEAD_DIM
M_CONV = 4
M_CHUNK = 128
M_INIT = -1e30

S_HEADS = 8
S_WIDTH = S_HEADS * ROPE_DIM
S_KV_RANK = 256
IDX_HEADS = 4
IDX_WIDTH = IDX_HEADS * ROPE_DIM
IDX_TOPK_MAX = 256

DA_HEADS = 4
DA_MAPS = 2 * DA_HEADS
DA_V_DIM = 2 * ROPE_DIM
DA_WIDTH = DA_HEADS * DA_V_DIM

N_BRANCH = 3
FF_DIM = 2816

NEG = -1e30
INT_MIN = -(2**31)
HALF16 = 2**15
LANE = 128
HEAD_GROUP = 256
KSUB = 128

W_IN_SIZES = (2 * M_WIDTH, M_WIDTH, M_WIDTH, 2 * M_HEADS, S_WIDTH, S_KV_RANK, IDX_WIDTH, ROPE_DIM,
              IDX_HEADS, 2 * DA_HEADS * ROPE_DIM, 2 * DA_HEADS * ROPE_DIM, DA_WIDTH, N_BRANCH * D_MODEL)
IN_COLS = sum(W_IN_SIZES)

COL_GATE = 0
COL_MQK = 3072
COL_MV = 4096
COL_MO = 4608
COL_SQ = 5120
COL_DQ = 5632
COL_DK = 6144
COL_DV = 6656
COL_CKV = 7168
COL_IQ = 7424
COL_IK4 = 7680
COL_MISC = 7936
MISC_IF = 0
MISC_IW = 8
Z_COLS = 8192

VMEM_LIMIT = 56 * 1024 * 1024


def _cparams(sem):
    return pltpu.CompilerParams(dimension_semantics=sem, vmem_limit_bytes=VMEM_LIMIT)


def _sigmoid(x):
    return 1.0 / (1.0 + jnp.exp(-x))


def _dot(a, b):
    return jnp.dot(a, b, preferred_element_type=F32)


def _dot_nt(a, b):
    return lax.dot_general(a, b, (((1,), (1,)), ((), ())), preferred_element_type=F32)


def _dot_tn(a, b):
    return lax.dot_general(a, b, (((0,), (0,)), ((), ())), preferred_element_type=F32)


def _split3(x):
    hi = x.astype(BF16)
    r1 = x - hi.astype(F32)
    mid = r1.astype(BF16)
    lo = (r1 - mid.astype(F32)).astype(BF16)
    return hi, mid, lo


def _wlayout_kernel(w_ref, o_ref):
    w = w_ref[...]
    offs = [0]
    for s in W_IN_SIZES:
        offs.append(offs[-1] + s)
    seg = [w[:, offs[k]:offs[k + 1]] for k in range(len(W_IN_SIZES))]
    m_qk, m_v, m_o, m_if, s_q, s_ckv, i_q, i_k, i_w, d_q, d_k, d_v, g_pre = seg
    rows = w.shape[0]
    misc = jnp.concatenate([m_if, i_w, jnp.zeros((rows, LANE - 2 * M_HEADS - IDX_HEADS), F32)], axis=1)
    ik4 = jnp.concatenate([i_k] * IDX_HEADS, axis=1)
    tail = jnp.zeros((rows, Z_COLS - COL_MISC - LANE), F32)
    for col, val in ((COL_GATE, g_pre), (COL_MQK, m_qk), (COL_MV, m_v), (COL_MO, m_o), (COL_SQ, s_q),
                     (COL_DQ, d_q), (COL_DK, d_k), (COL_DV, d_v), (COL_CKV, s_ckv), (COL_IQ, i_q),
                     (COL_IK4, ik4), (COL_MISC, misc), (COL_MISC + LANE, tail)):
        o_ref[:, col:col + val.shape[1]] = val.astype(BF16)


def _wlayout(w_in, *, tr=256):
    depth, rows, cols = w_in.shape
    return pl.pallas_call(
        _wlayout_kernel,
        grid=(depth, rows // tr),
        in_specs=[pl.BlockSpec((None, tr, cols), lambda l, i: (l, i, 0))],
        out_specs=pl.BlockSpec((None, tr, Z_COLS), lambda l, i: (l, i, 0)),
        out_shape=jax.ShapeDtypeStruct((depth, rows, Z_COLS), BF16),
        compiler_params=_cparams(("parallel", "parallel")),
        name="w_layout",
    )(w_in)


def _inproj_kernel(x_ref, g_ref, w_ref, z_ref, xn_ref):
    @pl.when(pl.program_id(1) == 0)
    def _():
        x = x_ref[...]
        ms = jnp.mean(x * x, axis=-1, keepdims=True)
        xn_ref[...] = (x * lax.rsqrt(ms + EPS) * g_ref[...]).astype(BF16)

    z_ref[...] = _dot(xn_ref[...], w_ref[...])


def _inproj(x2, g, w, *, tm=1024, tn=1024):
    n = x2.shape[0]
    tm = min(tm, n)
    return pl.pallas_call(
        _inproj_kernel,
        grid=(n // tm, Z_COLS // tn),
        in_specs=[
            pl.BlockSpec((tm, D_MODEL), lambda i, j: (i, 0)),
            pl.BlockSpec((1, D_MODEL), lambda i, j: (0, 0)),
            pl.BlockSpec((D_MODEL, tn), lambda i, j: (0, j)),
        ],
        out_specs=pl.BlockSpec((tm, tn), lambda i, j: (i, j)),
        out_shape=jax.ShapeDtypeStruct((n, Z_COLS), F32),
        scratch_shapes=[pltpu.VMEM((tm, D_MODEL), BF16)],
        compiler_params=_cparams(("parallel", "arbitrary")),
        name="inproj",
    )(x2, g, w)


def _mlstm_kernel(qk_ref, v_ref, o_ref, misc_ref, cw_ref, cb_ref, gb_ref, ng_ref, tri_ref, y_ref,
                  tail_ref, qkc_ref, c_ref, n_ref, m_ref):
    L = M_CHUNK
    dh = M_HEAD_DIM

    @pl.when(pl.program_id(1) == 0)
    def _():
        tail_ref[...] = jnp.zeros_like(tail_ref)
        c_ref[...] = jnp.zeros_like(c_ref)
        n_ref[...] = jnp.zeros_like(n_ref)
        m_ref[...] = jnp.full_like(m_ref, M_INIT)

    x = qk_ref[...]
    tail = tail_ref[...]
    row8 = lax.broadcasted_iota(I32, (8, 1), 0)
    cw = cw_ref[...]
    cb = cb_ref[...]
    acc = x * cw[M_CONV - 1:M_CONV] + cb
    acc_head = x[0:8] * cw[M_CONV - 1:M_CONV] + cb
    for j in range(1, M_CONV):
        wj = cw[M_CONV - 1 - j:M_CONV - j]
        xr = pltpu.roll(x, j, axis=0)
        tr = pltpu.roll(tail, j, axis=0)
        acc = acc + xr * wj
        acc_head = acc_head + jnp.where(row8 < j, tr, xr[0:8]) * wj
    qkc_ref[...] = acc * _sigmoid(acc)
    qkc_ref[0:8, :] = acc_head * _sigmoid(acc_head)
    tail_ref[...] = x[L - 8:L]

    gates = misc_ref[...] + gb_ref[...]
    lf = jnp.minimum(gates, 0.0) - jnp.log(1.0 + jnp.exp(-jnp.abs(gates)))
    tri = tri_ref[...]
    hi, mid, lo = _split3(lf)
    bcum = _dot(tri, hi) + _dot(tri, mid) + _dot(tri, lo)
    bcum_t = bcum.T
    gates_t = gates.T

    t_idx = lax.broadcasted_iota(I32, (L, L), 0)
    s_idx = lax.broadcasted_iota(I32, (L, L), 1)
    causal = t_idx >= s_idx

    for h in range(M_HEADS):
        sl = slice(h * dh, (h + 1) * dh)
        q = qkc_ref[:, h * dh:(h + 1) * dh]
        k = qkc_ref[:, M_WIDTH + h * dh:M_WIDTH + (h + 1) * dh] * (dh ** -0.5)
        v = v_ref[:, sl]
        qb, kb, vb = q.astype(BF16), k.astype(BF16), v.astype(BF16)

        b_col = bcum[:, M_HEADS + h:M_HEADS + h + 1]
        i_col = gates[:, h:h + 1]
        b_row = bcum_t[M_HEADS + h:M_HEADS + h + 1, :]
        i_row = gates_t[h:h + 1, :]
        g_tot = bcum[L - 1:L, M_HEADS + h:M_HEADS + h + 1]

        c_prev = c_ref[h]
        n_prev = n_ref[h]
        m_prev = m_ref[h][0:1, 0:1]

        dmat = jnp.where(causal, b_col - b_row + i_row, -jnp.inf)
        inter = b_col + m_prev
        m_t = jnp.maximum(inter, jnp.max(dmat, axis=-1, keepdims=True))
        sw = jnp.exp(dmat - m_t) * _dot_nt(qb, kb)
        s_inter = jnp.exp(inter - m_t)
        num = _dot(sw.astype(BF16), vb) + s_inter * _dot_nt(qb, c_prev.astype(BF16))
        den = jnp.sum(sw, axis=-1, keepdims=True) + s_inter * jnp.sum(q * n_prev, axis=-1, keepdims=True)
        hh = num / jnp.maximum(jnp.abs(den), jnp.exp(-m_t))
        ms = jnp.mean(hh * hh, axis=-1, keepdims=True)
        hn = hh * lax.rsqrt(ms + EPS) * ng_ref[:, sl]
        y_ref[:, sl] = _sigmoid(o_ref[:, sl]) * hn

        a_col = g_tot - b_col + i_col
        m_loc = jnp.max(a_col, axis=0, keepdims=True)
        w_loc = jnp.exp(a_col - m_loc)
        c_loc = _dot_tn((v * w_loc).astype(BF16), kb)
        n_loc = jnp.sum(k * w_loc, axis=0, keepdims=True)
        m_new = jnp.maximum(g_tot + m_prev, m_loc)
        s_old = jnp.exp(g_tot + m_prev - m_new)
        s_loc = jnp.exp(m_loc - m_new)
        c_ref[h] = s_old * c_prev + s_loc * c_loc
        n_ref[h] = s_old * n_prev + s_loc * n_loc
        m_ref[h] = jnp.broadcast_to(m_new, (8, LANE))


def _mlstm(z, conv_w, conv_b, gate_b_row, norm_g_row, tri, batch, seq):
    n = z.shape[0]
    nc = seq // M_CHUNK
    L = M_CHUNK
    row = lambda b, c: b * nc + c
    return pl.pallas_call(
        _mlstm_kernel,
        grid=(batch, nc),
        in_specs=[
            pl.BlockSpec((L, 2 * M_WIDTH), lambda b, c: (row(b, c), COL_MQK // (2 * M_WIDTH))),
            pl.BlockSpec((L, M_WIDTH), lambda b, c: (row(b, c), COL_MV // M_WIDTH)),
            pl.BlockSpec((L, M_WIDTH), lambda b, c: (row(b, c), COL_MO // M_WIDTH)),
            pl.BlockSpec((L, LANE), lambda b, c: (row(b, c), COL_MISC // LANE)),
            pl.BlockSpec((M_CONV, 2 * M_WIDTH), lambda b, c: (0, 0)),
            pl.BlockSpec((1, 2 * M_WIDTH), lambda b, c: (0, 0)),
            pl.BlockSpec((1, LANE), lambda b, c: (0, 0)),
            pl.BlockSpec((1, M_WIDTH), lambda b, c: (0, 0)),
            pl.BlockSpec((L, L), lambda b, c: (0, 0)),
        ],
        out_specs=pl.BlockSpec((L, M_WIDTH), lambda b, c: (row(b, c), 0)),
        out_shape=jax.ShapeDtypeStruct((n, M_WIDTH), F32),
        scratch_shapes=[
            pltpu.VMEM((8, 2 * M_WIDTH), F32),
            pltpu.VMEM((L, 2 * M_WIDTH), F32),
            pltpu.VMEM((M_HEADS, M_HEAD_DIM, M_HEAD_DIM), F32),
            pltpu.VMEM((M_HEADS, 1, M_HEAD_DIM), F32),
            pltpu.VMEM((M_HEADS, 8, LANE), F32),
        ],
        compiler_params=_cparams(("parallel", "arbitrary")),
        name="mlstm",
    )(z, z, z, z, conv_w, conv_b, gate_b_row, norm_g_row, tri)


def _rope(x, c, s):
    w = x.shape[-1]
    lane = lax.broadcasted_iota(I32, (1, w), 1)
    fwd = pltpu.roll(x, ROPE_HALF, axis=1)
    bwd = pltpu.roll(x, w - ROPE_HALF, axis=1)
    swapped = jnp.where((lane & (ROPE_DIM - 1)) < ROPE_HALF, bwd, fwd)
    return x * c + swapped * s


def _head_rms(x, g, grp):
    sq = x * x
    hi = sq.astype(BF16)
    lo = (sq - hi.astype(F32)).astype(BF16)
    ss = _dot(hi, grp) + _dot(lo, grp)
    return x * lax.rsqrt(ss * (1.0 / ROPE_DIM) + EPS) * g


def _store_transposed(o_ref, v):
    for j in range(o_ref.shape[0]):
        o_ref[j] = v[j * KSUB:(j + 1) * KSUB, :].T.astype(BF16)


def _prep_kernel(sq_ref, dq_ref, dk_ref, dv_ref, ckv_ref, iq_ref, ik_ref, c_ref, s_ref, grp_ref,
                 kvg_ref, wkv_ref, sqg_ref, skg_ref, dqg_ref, dkg_ref,
                 sq_o, sk_o, sv_o, iq_o, ik_o, dq_o, dk_o, dv_o):
    c256 = c_ref[...]
    s256 = s_ref[...]
    c512 = jnp.concatenate([c256, c256], axis=1)
    s512 = jnp.concatenate([s256, s256], axis=1)
    grp = grp_ref[...]
    qscale = ROPE_DIM ** -0.5

    sq_o[...] = (_rope(_head_rms(sq_ref[...], sqg_ref[...], grp), c512, s512) * qscale).astype(BF16)

    ckv = ckv_ref[...]
    ms = jnp.mean(ckv * ckv, axis=-1, keepdims=True)
    ckvn = (ckv * lax.rsqrt(ms + EPS) * kvg_ref[...]).astype(BF16)
    kv = _dot(ckvn, wkv_ref[...])
    sk_o[...] = _rope(_head_rms(kv[:, :S_WIDTH], skg_ref[...], grp), c512, s512).astype(BF16)
    _store_transposed(sv_o, kv[:, S_WIDTH:])

    iq_o[...] = _rope(iq_ref[...], c256, s256).astype(BF16)
    ik_o[...] = _rope(ik_ref[...], c256, s256).astype(BF16)

    dq_o[...] = (_rope(_head_rms(dq_ref[...], dqg_ref[...], grp), c512, s512) * qscale).astype(BF16)
    dk_o[...] = _rope(_head_rms(dk_ref[...], dkg_ref[...], grp), c512, s512).astype(BF16)
    _store_transposed(dv_o, dv_ref[...])


def _prep(z, rope_c, rope_s, grp, kv_g, w_kv, sq_g, sk_g, dq_g, dk_g, seq, *, tm=512):
    n = z.shape[0]
    tm = min(tm, seq)
    nt = seq // tm
    zspec = lambda w, col: pl.BlockSpec((tm, w), lambda i: (i, col // w))
    const = lambda shape: pl.BlockSpec(shape, lambda i: (0, 0))
    o512 = pl.BlockSpec((tm, 512), lambda i: (i, 0))
    o256 = pl.BlockSpec((tm, 256), lambda i: (i, 0))
    s512 = jax.ShapeDtypeStruct((n, 512), BF16)
    s256 = jax.ShapeDtypeStruct((n, 256), BF16)
    ot = pl.BlockSpec((tm // KSUB, 512, KSUB), lambda i: (i, 0, 0))
    st = jax.ShapeDtypeStruct((n // KSUB, 512, KSUB), BF16)
    return pl.pallas_call(
        _prep_kernel,
        grid=(n // tm,),
        in_specs=[
            zspec(512, COL_SQ), zspec(512, COL_DQ), zspec(512, COL_DK), zspec(512, COL_DV),
            zspec(256, COL_CKV), zspec(256, COL_IQ), zspec(256, COL_IK4),
            pl.BlockSpec((tm, 256), lambda i: (i % nt, 0)),
            pl.BlockSpec((tm, 256), lambda i: (i % nt, 0)),
            const((512, 512)),
            const((1, S_KV_RANK)), const((S_KV_RANK, 2 * S_WIDTH)),
            const((1, 512)), const((1, 512)), const((1, 512)), const((1, 512)),
        ],
        out_specs=[o512, o512, ot, o256, o256, o512, o512, ot],
        out_shape=[s512, s512, st, s256, s256, s512, s512, st],
        compiler_params=_cparams(("parallel",)),
        name="attn_prep",
    )(z, z, z, z, z, z, z, rope_c, rope_s, grp, kv_g, w_kv, sq_g, sk_g, dq_g, dk_g)


def _mask_heads(q_ref, qm_ref, n_heads):
    lane_grp = lax.broadcasted_iota(I32, (1, HEAD_GROUP), 1) // ROPE_DIM
    for h in range(n_heads):
        g = h // 4
        qg = q_ref[:, g * HEAD_GROUP:(g + 1) * HEAD_GROUP]
        qm_ref[h] = jnp.where(lane_grp == (h % 4), qg, jnp.zeros_like(qg))


def _softmax_group(logits, h0, values, dv, acc_ref, m_ref, l_ref):
    nh = len(logits)
    hs = slice(h0, h0 + nh)
    rows = slice(h0 * dv, (h0 + nh) * dv)
    m_old = m_ref[hs, :]
    m_new = jnp.maximum(m_old, jnp.concatenate([jnp.max(s, axis=0, keepdims=True) for s in logits], axis=0))
    alpha = jnp.exp(m_old - m_new)
    ps = [jnp.exp(s - m_new[j:j + 1, :]) for j, s in enumerate(logits)]
    l_ref[hs, :] = alpha * l_ref[hs, :] + jnp.concatenate([jnp.sum(p, axis=0, keepdims=True) for p in ps], axis=0)
    m_ref[hs, :] = m_new
    acc_old = acc_ref[rows, :]
    acc_ref[rows, :] = jnp.concatenate(
        [alpha[j:j + 1, :] * acc_old[j * dv:(j + 1) * dv, :] + _dot(values[j], ps[j].astype(BF16))
         for j in range(nh)], axis=0)


def _sortable(x):
    bits = pltpu.bitcast(x, I32)
    return bits ^ ((bits >> 31) & 0x7FFFFFFF)


def _tree_sum(parts):
    while len(parts) > 1:
        parts = [parts[j] + parts[j + 1] for j in range(0, len(parts), 2)]
    return parts[0]


def _dsa_kernel(sq_ref, iq_ref, misc_ref, sk_ref, svt_ref, ik_ref, y_ref,
                sc_ref, hi_ref, lo_ref, qm_ref, iqm_ref, acc_ref, m_ref, l_ref,
                *, tq, topk, idx_bits):
    i = pl.program_id(1)
    nks = (i + 1) * (tq // KSUB)
    q_idx = i * tq + lax.broadcasted_iota(I32, (1, tq), 1)
    k_sub = lax.broadcasted_iota(I32, (KSUB, 1), 0)

    _mask_heads(iq_ref, iqm_ref, IDX_HEADS)
    w_t = (misc_ref[...] * (IDX_WIDTH ** -0.5)).T
    w_rows = [w_t[MISC_IW + h:MISC_IW + h + 1, :] for h in range(IDX_HEADS)]

    def score_body(ks, carry):
        ikb = ik_ref[pl.ds(pl.multiple_of(ks * KSUB, KSUB), KSUB), :]
        s = jnp.zeros((KSUB, tq), F32)
        for h in range(IDX_HEADS):
            s = s + w_rows[h] * jnp.maximum(_dot_nt(ikb, iqm_ref[h]), 0.0)
        key = jnp.where(ks * KSUB + k_sub <= q_idx, _sortable(s), INT_MIN)
        sc_ref[ks] = key
        hi_ref[ks] = (key >> 16).astype(I16)
        lo_ref[ks] = ((key & 0xFFFF) - HALF16).astype(I16)
        return carry

    lax.fori_loop(0, nks, score_body, 0)

    def count16(ref, trial, strict):
        def body(ks, acc):
            blk = ref[ks]
            hit = (blk > trial) if strict else (blk >= trial)
            c = jnp.where(hit, jnp.int16(1), jnp.int16(0))
            return acc + _tree_sum([c[r * 16:(r + 1) * 16] for r in range(KSUB // 16)])
        acc = lax.fori_loop(0, nks, body, jnp.zeros((16, tq), I16))
        return jnp.sum(acc.astype(I32), axis=0, keepdims=True)

    def search16(ref, need):
        def body(t, cur):
            bit = jnp.left_shift(jnp.int32(1), 15 - t)
            trial = ((cur | bit) - HALF16).astype(I16)
            return jnp.where(count16(ref, trial, False) >= need, cur | bit, cur)
        return lax.fori_loop(0, 16, body, jnp.zeros((1, tq), I32))

    hi_k = search16(hi_ref, topk) - HALF16
    hi_k16 = hi_k.astype(I16)
    need_lo = topk - count16(hi_ref, hi_k16, True)

    def lo_mask_body(ks, carry):
        lo_ref[ks] = jnp.where(hi_ref[ks] == hi_k16, lo_ref[ks], jnp.int16(-HALF16))
        return carry

    lax.fori_loop(0, nks, lo_mask_body, 0)
    thr = (hi_k << 16) | search16(lo_ref, need_lo)

    def count32(pred):
        def body(ks, acc):
            c = jnp.where(pred(sc_ref[ks], ks * KSUB + k_sub), 1, 0)
            return acc + _tree_sum([c[r * 8:(r + 1) * 8] for r in range(KSUB // 8)])
        acc = lax.fori_loop(0, nks, body, jnp.zeros((8, tq), I32))
        return jnp.sum(acc, axis=0, keepdims=True)

    n_ge = count32(lambda blk, kidx: blk >= thr)
    n_gt = count32(lambda blk, kidx: blk > thr)

    need = topk - n_gt
    m_ref[0:1, :] = jnp.full((1, tq), 2.0**30, F32)

    @pl.when(jnp.max(n_ge) > topk)
    def _():
        def idx_body(t, cut):
            bit = jnp.left_shift(jnp.int32(1), idx_bits - 1 - t)
            trial = cut | bit
            cnt = count32(lambda blk, kidx: (blk == thr) & (kidx < trial))
            return jnp.where(cnt < need, trial, cut)
        cut = lax.fori_loop(0, idx_bits, idx_body, jnp.zeros((1, tq), I32))
        m_ref[0:1, :] = cut.astype(F32)

    cut = m_ref[0:1, :].astype(I32)

    def bias_body(ks, carry):
        key = sc_ref[ks]
        kidx = ks * KSUB + k_sub
        sel = (key > thr) | ((key == thr) & (kidx <= cut))
        sel = sel & (kidx <= q_idx)
        sc_ref[ks] = pltpu.bitcast(jnp.where(sel, 0.0, NEG).astype(F32), I32)
        return carry

    lax.fori_loop(0, nks, bias_body, 0)

    _mask_heads(sq_ref, qm_ref, S_HEADS)
    acc_ref[...] = jnp.zeros_like(acc_ref)
    m_ref[...] = jnp.full_like(m_ref, NEG)
    l_ref[...] = jnp.zeros_like(l_ref)

    def attn_body(ks, carry):
        off = pl.multiple_of(ks * KSUB, KSUB)
        bias = pltpu.bitcast(sc_ref[ks], F32)
        kgs = [sk_ref[pl.ds(off, KSUB), g * HEAD_GROUP:(g + 1) * HEAD_GROUP] for g in range(S_HEADS // 4)]
        _softmax_group([_dot_nt(kgs[h // 4], qm_ref[h]) + bias for h in range(S_HEADS)], 0,
                       [svt_ref[ks, h * ROPE_DIM:(h + 1) * ROPE_DIM, :] for h in range(S_HEADS)],
                       ROPE_DIM, acc_ref, m_ref, l_ref)
        return carry

    lax.fori_loop(0, nks, attn_body, 0)

    for h in range(S_HEADS):
        rows = slice(h * ROPE_DIM, (h + 1) * ROPE_DIM)
        acc_ref[rows, :] = acc_ref[rows, :] / l_ref[h:h + 1, :]
    y_ref[...] = acc_ref[...].T


def _dsa(z, sq, sk, svt, iq, ik4, batch, seq, *, tq=256):
    n = z.shape[0]
    tq = min(tq, seq)
    nq = seq // tq
    nsub = seq // KSUB
    topk = min(IDX_TOPK_MAX, seq // 4)
    kern = functools.partial(_dsa_kernel, tq=tq, topk=topk, idx_bits=seq.bit_length())
    return pl.pallas_call(
        kern,
        grid=(batch, nq),
        in_specs=[
            pl.BlockSpec((tq, S_WIDTH), lambda b, i: (b * nq + i, 0)),
            pl.BlockSpec((tq, IDX_WIDTH), lambda b, i: (b * nq + i, 0)),
            pl.BlockSpec((tq, LANE), lambda b, i: (b * nq + i, COL_MISC // LANE)),
            pl.BlockSpec((seq, S_WIDTH), lambda b, i: (b, 0)),
            pl.BlockSpec((nsub, S_WIDTH, KSUB), lambda b, i: (b, 0, 0)),
            pl.BlockSpec((seq, IDX_WIDTH), lambda b, i: (b, 0)),
        ],
        out_specs=pl.BlockSpec((tq, S_WIDTH), lambda b, i: (b * nq + i, 0)),
        out_shape=jax.ShapeDtypeStruct((n, S_WIDTH), F32),
        scratch_shapes=[
            pltpu.VMEM((nsub, KSUB, tq), I32),
            pltpu.VMEM((nsub, KSUB, tq), I16),
            pltpu.VMEM((nsub, KSUB, tq), I16),
            pltpu.VMEM((S_HEADS, tq, HEAD_GROUP), BF16),
            pltpu.VMEM((IDX_HEADS, tq, HEAD_GROUP), BF16),
            pltpu.VMEM((S_WIDTH, tq), F32),
            pltpu.VMEM((S_HEADS, tq), F32),
            pltpu.VMEM((S_HEADS, tq), F32),
        ],
        compiler_params=_cparams(("parallel", "arbitrary")),
        name="dsa_attn",
    )(sq, iq, z, sk, svt, ik4)


def _diff_kernel(q_ref, k_ref, vt_ref, lam_ref, og_ref, y_ref, qm_ref, acc_ref, m_ref, l_ref,
                 *, tq, lam_init):
    i = pl.program_id(1)
    nfull = i * (tq // KSUB)
    q_idx = i * tq + lax.broadcasted_iota(I32, (1, tq), 1)
    k_sub = lax.broadcasted_iota(I32, (KSUB, 1), 0)

    _mask_heads(q_ref, qm_ref, DA_MAPS)
    acc_ref[...] = jnp.zeros_like(acc_ref)
    m_ref[...] = jnp.full_like(m_ref, NEG)
    l_ref[...] = jnp.zeros_like(l_ref)

    def step(ks, masked):
        off = pl.multiple_of(ks * KSUB, KSUB)
        if masked:
            bias = jnp.where(ks * KSUB + k_sub <= q_idx, 0.0, NEG).astype(F32)
        kgs = [k_ref[pl.ds(off, KSUB), g * HEAD_GROUP:(g + 1) * HEAD_GROUP] for g in range(DA_MAPS // 4)]
        logits = [_dot_nt(kgs[m // 4], qm_ref[m]) for m in range(DA_MAPS)]
        if masked:
            logits = [s + bias for s in logits]
        _softmax_group(logits, 0,
                       [vt_ref[ks, (m // 2) * DA_V_DIM:(m // 2 + 1) * DA_V_DIM, :] for m in range(DA_MAPS)],
                       DA_V_DIM, acc_ref, m_ref, l_ref)

    def full_body(ks, carry):
        step(ks, False)
        return carry

    lax.fori_loop(0, nfull, full_body, 0)
    for d in range(tq // KSUB):
        step(nfull + d, True)

    lam = lam_ref[...]
    p01 = jnp.sum(lam[0:1] * lam[1:2], axis=-1, keepdims=True)
    p23 = jnp.sum(lam[2:3] * lam[3:4], axis=-1, keepdims=True)
    lam_val = jnp.exp(p01) - jnp.exp(p23) + lam_init
    for hd in range(DA_HEADS):
        r1 = slice(2 * hd * DA_V_DIM, (2 * hd + 1) * DA_V_DIM)
        r2 = slice((2 * hd + 1) * DA_V_DIM, (2 * hd + 2) * DA_V_DIM)
        o = acc_ref[r1, :] / l_ref[2 * hd:2 * hd + 1, :] - lam_val * (acc_ref[r2, :] / l_ref[2 * hd + 1:2 * hd + 2, :])
        ms = jnp.mean(o * o, axis=0, keepdims=True)
        acc_ref[hd * DA_V_DIM:(hd + 1) * DA_V_DIM, :] = o * lax.rsqrt(ms + EPS) * og_ref[...] * (1.0 - lam_init)
    y_ref[...] = acc_ref[0:DA_WIDTH, :].T


def _diff_attn(dq, dk, dvt, lam, out_g_col, batch, seq, lam_init, *, tq=256):
    n = dq.shape[0]
    tq = min(tq, seq)
    nq = seq // tq
    nsub = seq // KSUB
    kern = functools.partial(_diff_kernel, tq=tq, lam_init=lam_init)
    return pl.pallas_call(
        kern,
        grid=(batch, nq),
        in_specs=[
            pl.BlockSpec((tq, 512), lambda b, i: (b * nq + i, 0)),
            pl.BlockSpec((seq, 512), lambda b, i: (b, 0)),
            pl.BlockSpec((nsub, DA_WIDTH, KSUB), lambda b, i: (b, 0, 0)),
            pl.BlockSpec((4, ROPE_DIM), lambda b, i: (0, 0)),
            pl.BlockSpec((DA_V_DIM, 1), lambda b, i: (0, 0)),
        ],
        out_specs=pl.BlockSpec((tq, DA_WIDTH), lambda b, i: (b * nq + i, 0)),
        out_shape=jax.ShapeDtypeStruct((n, DA_WIDTH), F32),
        scratch_shapes=[
            pltpu.VMEM((DA_MAPS, tq, HEAD_GROUP), BF16),
            pltpu.VMEM((DA_MAPS * DA_V_DIM, tq), F32),
            pltpu.VMEM((DA_MAPS, tq), F32),
            pltpu.VMEM((DA_MAPS, tq), F32),
        ],
        compiler_params=_cparams(("parallel", "arbitrary")),
        name="diff_attn",
    )(dq, dk, dvt, lam, out_g_col)


def _merge_kernel(x_ref, ya_ref, yb_ref, yc_ref, gp_ref, bg_ref, wb_ref, wo_ref, ng_ref, xo_ref, xn_ref):
    merged = None
    for br, y_ref in enumerate((ya_ref, yb_ref, yc_ref)):
        sl = slice(br * D_MODEL, (br + 1) * D_MODEL)
        gate = _sigmoid(gp_ref[:, sl] + bg_ref[:, sl])
        term = gate * _dot(y_ref[...].astype(BF16), wb_ref[br])
        merged = term if merged is None else merged + term
    xo = x_ref[...] + _dot(merged.astype(BF16), wo_ref[...])
    xo_ref[...] = xo
    ms = jnp.mean(xo * xo, axis=-1, keepdims=True)
    xn_ref[...] = (xo * lax.rsqrt(ms + EPS) * ng_ref[...]).astype(BF16)


def _merge(x2, ya, yb, yc, z, b_gate, w_branch, w_out, norm_g, *, tm=512):
    n = x2.shape[0]
    tm = min(tm, n)
    row = lambda w: pl.BlockSpec((tm, w), lambda i: (i, 0))
    return pl.pallas_call(
        _merge_kernel,
        grid=(n // tm,),
        in_specs=[
            row(D_MODEL), row(512), row(512), row(512),
            pl.BlockSpec((tm, N_BRANCH * D_MODEL), lambda i: (i, COL_GATE)),
            pl.BlockSpec((1, N_BRANCH * D_MODEL), lambda i: (0, 0)),
            pl.BlockSpec((N_BRANCH, 512, D_MODEL), lambda i: (0, 0, 0)),
            pl.BlockSpec((D_MODEL, D_MODEL), lambda i: (0, 0)),
            pl.BlockSpec((1, D_MODEL), lambda i: (0, 0)),
        ],
        out_specs=[row(D_MODEL), row(D_MODEL)],
        out_shape=[jax.ShapeDtypeStruct((n, D_MODEL), F32), jax.ShapeDtypeStruct((n, D_MODEL), BF16)],
        compiler_params=_cparams(("parallel",)),
        name="merge_out",
    )(x2, ya, yb, yc, z, b_gate, w_branch, w_out, norm_g)


def _ffn_kernel(xn_ref, x_ref, wg_ref, wu_ref, wd_ref, o_ref, acc_ref):
    j = pl.program_id(1)
    xn = xn_ref[...]
    g = _dot(xn, wg_ref[...])
    u = _dot(xn, wu_ref[...])
    part = _dot((g * _sigmoid(g) * u).astype(BF16), wd_ref[...])

    @pl.when(j == 0)
    def _():
        acc_ref[...] = part

    @pl.when(j > 0)
    def _():
        acc_ref[...] += part

    @pl.when(j == pl.num_programs(1) - 1)
    def _():
        o_ref[...] = x_ref[...] + acc_ref[...]


def _ffn(xn, x2, wg, wu, wd, *, tm=512, fc=1408):
    n = x2.shape[0]
    tm = min(tm, n)
    return pl.pallas_call(
        _ffn_kernel,
        grid=(n // tm, FF_DIM // fc),
        in_specs=[
            pl.BlockSpec((tm, D_MODEL), lambda i, j: (i, 0)),
            pl.BlockSpec((tm, D_MODEL), lambda i, j: (i, 0)),
            pl.BlockSpec((D_MODEL, fc), lambda i, j: (0, j)),
            pl.BlockSpec((D_MODEL, fc), lambda i, j: (0, j)),
            pl.BlockSpec((fc, D_MODEL), lambda i, j: (j, 0)),
        ],
        out_specs=pl.BlockSpec((tm, D_MODEL), lambda i, j: (i, 0)),
        out_shape=jax.ShapeDtypeStruct((n, D_MODEL), F32),
        scratch_shapes=[pltpu.VMEM((tm, D_MODEL), F32)],
        compiler_params=_cparams(("parallel", "arbitrary")),
        name="ffn",
    )(xn, x2, wg, wu, wd)


def _rope_tables(seq):
    inv = 1.0 / jnp.power(ROPE_THETA, jnp.arange(0, ROPE_DIM, 2, dtype=F32) / ROPE_DIM)
    ang = jnp.arange(seq, dtype=F32)[:, None] * inv[None, :]
    cos, sin = jnp.cos(ang), jnp.sin(ang)
    c64 = jnp.concatenate([cos, cos], axis=1)
    s64 = jnp.concatenate([-sin, sin], axis=1)
    return jnp.tile(c64, (1, 4)), jnp.tile(s64, (1, 4))


def _layer(x2, li, batch, seq, consts, norm_mix_g, w_in_l, b_gate, conv_w, conv_b, gate_b, m_norm_g,
           kv_norm_g, w_kv_up, sq_g, sk_g, dq_g, dk_g, lam, d_out_g, w_branch, w_out,
           norm_ffn_g, w_gate_up, w_down):
    rope_c, rope_s, grp, tri = consts
    lam_init = 0.8 - 0.6 * math.exp(-0.3 * li)
    tile8 = lambda g: jnp.tile(g, 8)[None, :]

    z = _inproj(x2, norm_mix_g[None, :], w_in_l)

    gate_row = jnp.zeros((1, LANE), F32).at[0, MISC_IF:MISC_IF + 2 * M_HEADS].set(gate_b)
    ya = _mlstm(z, conv_w, conv_b[None, :], gate_row, m_norm_g.reshape(1, M_WIDTH), tri, batch, seq)

    sq, sk, svt, iq, ik4, dq, dk, dvt = _prep(
        z, rope_c, rope_s, grp, kv_norm_g[None, :], w_kv_up.astype(BF16),
        tile8(sq_g), tile8(sk_g), tile8(dq_g), tile8(dk_g), seq)
    yb = _dsa(z, sq, sk, svt, iq, ik4, batch, seq)
    yc = _diff_attn(dq, dk, dvt, lam, d_out_g[:, None], batch, seq, lam_init)

    xo, xn = _merge(x2, ya, yb, yc, z, b_gate[None, :], w_branch.astype(BF16), w_out.astype(BF16),
                    norm_ffn_g[None, :])
    return _ffn(xn, xo, w_gate_up[:, :FF_DIM].astype(BF16), w_gate_up[:, FF_DIM:].astype(BF16),
                w_down.astype(BF16))


def kernel(x, norm_mix_g, w_in, b_gate, mlstm_conv_w, mlstm_conv_b, mlstm_gate_b, mlstm_norm_g,
           dsa_kv_norm_g, dsa_w_kv_up, dsa_q_norm_g, dsa_k_norm_g, diff_q_norm_g, diff_k_norm_g,
           diff_lambda, diff_out_norm_g, w_branch, w_out, norm_ffn_g, w_gate_up, w_down):
    batch, seq, d = x.shape
    depth = w_in.shape[0]
    assert w_in.shape[2] == IN_COLS and d == D_MODEL
    rope_c, rope_s = _rope_tables(seq)
    gi = jnp.arange(512) // ROPE_DIM
    grp = (gi[:, None] == gi[None, :]).astype(BF16)
    ti = jnp.arange(M_CHUNK)
    tri = (ti[:, None] >= ti[None, :]).astype(BF16)
    consts = (rope_c, rope_s, grp, tri)
    w_in_l = _wlayout(w_in)
    x2 = x.reshape(batch * seq, d)
    for li in range(depth):
        x2 = _layer(x2, li, batch, seq, consts, norm_mix_g[li], w_in_l[li], b_gate[li], mlstm_conv_w[li],
                    mlstm_conv_b[li], mlstm_gate_b[li], mlstm_norm_g[li], dsa_kv_norm_g[li],
                    dsa_w_kv_up[li], dsa_q_norm_g[li], dsa_k_norm_g[li], diff_q_norm_g[li],
                    diff_k_norm_g[li], diff_lambda[li], diff_out_norm_g[li], w_branch[li], w_out[li],
                    norm_ffn_g[li], w_gate_up[li], w_down[li])
    return x2.reshape(batch, seq, d)
```

```python
import functools
import math

import jax
import jax.numpy as jnp
from jax import lax
from jax.experimental import pallas as pl
from jax.experimental.pallas import tpu as pltpu

F32 = jnp.float32
BF16 = jnp.bfloat16
I32 = jnp.int32
I16 = jnp.int16

D_MODEL = 1024
EPS = 1e-6
ROPE_DIM = 64
ROPE_HALF = ROPE_DIM // 2
ROPE_THETA = 10000.0

M_HEADS = 4
M_HEAD_DIM = 128
M_WIDTH = M_HEADS * M_HEAD_DIM
M_CONV = 4
M_CHUNK = 128
M_INIT = -1e30

S_HEADS = 8
S_WIDTH = S_HEADS * ROPE_DIM
S_KV_RANK = 256
IDX_HEADS = 4
IDX_WIDTH = IDX_HEADS * ROPE_DIM
IDX_TOPK_MAX = 256

DA_HEADS = 4
DA_MAPS = 2 * DA_HEADS
DA_V_DIM = 2 * ROPE_DIM
DA_WIDTH = DA_HEADS * DA_V_DIM

N_BRANCH = 3
FF_DIM = 2816

NEG = -1e30
LOG2E = math.log2(math.e)
INT_MIN = -(2**31)
HALF16 = 2**15
LANE = 128
HEAD_GROUP = 256
KSUB = 128

W_IN_SIZES = (2 * M_WIDTH, M_WIDTH, M_WIDTH, 2 * M_HEADS, S_WIDTH, S_KV_RANK, IDX_WIDTH, ROPE_DIM,
              IDX_HEADS, 2 * DA_HEADS * ROPE_DIM, 2 * DA_HEADS * ROPE_DIM, DA_WIDTH, N_BRANCH * D_MODEL)
IN_COLS = sum(W_IN_SIZES)

COL_GATE = 0
COL_MQK = 3072
COL_MV = 4096
COL_MO = 4608
COL_SQ = 5120
COL_DQ = 5632
COL_DK = 6144
COL_DV = 6656
COL_CKV = 7168
COL_IQ = 7424
COL_IK4 = 7680
COL_MISC = 7936
MISC_IF = 0
MISC_IW = 8
Z_COLS = 8192

VMEM_LIMIT = 56 * 1024 * 1024


def _cparams(sem, flags=None):
    return pltpu.CompilerParams(dimension_semantics=sem, vmem_limit_bytes=VMEM_LIMIT, flags=flags)


def _sigmoid(x):
    return 1.0 / (1.0 + jnp.exp(-x))


def _dot(a, b):
    return jnp.dot(a, b, preferred_element_type=F32)


def _dot_nt(a, b):
    return lax.dot_general(a, b, (((1,), (1,)), ((), ())), preferred_element_type=F32)


def _dot_tn(a, b):
    return lax.dot_general(a, b, (((0,), (0,)), ((), ())), preferred_element_type=F32)


def _split3(x):
    hi = x.astype(BF16)
    r1 = x - hi.astype(F32)
    mid = r1.astype(BF16)
    lo = (r1 - mid.astype(F32)).astype(BF16)
    return hi, mid, lo


def _wlayout_kernel(w_ref, o_ref):
    w = w_ref[...]
    offs = [0]
    for s in W_IN_SIZES:
        offs.append(offs[-1] + s)
    seg = [w[:, offs[k]:offs[k + 1]] for k in range(len(W_IN_SIZES))]
    m_qk, m_v, m_o, m_if, s_q, s_ckv, i_q, i_k, i_w, d_q, d_k, d_v, g_pre = seg
    rows = w.shape[0]
    misc = jnp.concatenate([m_if, i_w, jnp.zeros((rows, LANE - 2 * M_HEADS - IDX_HEADS), F32)], axis=1)
    ik4 = jnp.concatenate([i_k] * IDX_HEADS, axis=1)
    tail = jnp.zeros((rows, Z_COLS - COL_MISC - LANE), F32)
    for col, val in ((COL_GATE, g_pre), (COL_MQK, m_qk), (COL_MV, m_v), (COL_MO, m_o), (COL_SQ, s_q),
                     (COL_DQ, d_q), (COL_DK, d_k), (COL_DV, d_v), (COL_CKV, s_ckv), (COL_IQ, i_q),
                     (COL_IK4, ik4), (COL_MISC, misc), (COL_MISC + LANE, tail)):
        o_ref[:, col:col + val.shape[1]] = val.astype(BF16)


def _wlayout(w_in, *, tr=256):
    depth, rows, cols = w_in.shape
    return pl.pallas_call(
        _wlayout_kernel,
        grid=(depth, rows // tr),
        in_specs=[pl.BlockSpec((None, tr, cols), lambda l, i: (l, i, 0))],
        out_specs=pl.BlockSpec((None, tr, Z_COLS), lambda l, i: (l, i, 0)),
        out_shape=jax.ShapeDtypeStruct((depth, rows, Z_COLS), BF16),
        compiler_params=_cparams(("parallel", "parallel")),
        name="w_layout",
    )(w_in)


def _inproj_kernel(x_ref, g_ref, w_ref, z_ref, xn_ref):
    @pl.when(pl.program_id(1) == 0)
    def _():
        x = x_ref[...]
        ms = jnp.mean(x * x, axis=-1, keepdims=True)
        xn_ref[...] = (x * lax.rsqrt(ms + EPS) * g_ref[...]).astype(BF16)

    z_ref[...] = _dot(xn_ref[...], w_ref[...])


def _inproj(x2, g, w, *, tm=1024, tn=1024):
    n = x2.shape[0]
    tm = min(tm, n)
    return pl.pallas_call(
        _inproj_kernel,
        grid=(n // tm, Z_COLS // tn),
        in_specs=[
            pl.BlockSpec((tm, D_MODEL), lambda i, j: (i, 0)),
            pl.BlockSpec((1, D_MODEL), lambda i, j: (0, 0)),
            pl.BlockSpec((D_MODEL, tn), lambda i, j: (0, j)),
        ],
        out_specs=pl.BlockSpec((tm, tn), lambda i, j: (i, j)),
        out_shape=jax.ShapeDtypeStruct((n, Z_COLS), F32),
        scratch_shapes=[pltpu.VMEM((tm, D_MODEL), BF16)],
        compiler_params=_cparams(("parallel", "arbitrary")),
        name="inproj",
    )(x2, g, w)


def _mlstm_kernel(qk_ref, v_ref, o_ref, misc_ref, cw_ref, cb_ref, gb_ref, ng_ref, tri_ref, y_ref,
                  tail_ref, qkc_ref, c_ref, n_ref, m_ref):
    L = M_CHUNK
    dh = M_HEAD_DIM

    @pl.when(pl.program_id(1) == 0)
    def _():
        tail_ref[...] = jnp.zeros_like(tail_ref)
        c_ref[...] = jnp.zeros_like(c_ref)
        n_ref[...] = jnp.zeros_like(n_ref)
        m_ref[...] = jnp.full_like(m_ref, M_INIT)

    x = qk_ref[...]
    tail = tail_ref[...]
    row8 = lax.broadcasted_iota(I32, (8, 1), 0)
    cw = cw_ref[...]
    cb = cb_ref[...]
    acc = x * cw[M_CONV - 1:M_CONV] + cb
    acc_head = x[0:8] * cw[M_CONV - 1:M_CONV] + cb
    for j in range(1, M_CONV):
        wj = cw[M_CONV - 1 - j:M_CONV - j]
        xr = pltpu.roll(x, j, axis=0)
        tr = pltpu.roll(tail, j, axis=0)
        acc = acc + xr * wj
        acc_head = acc_head + jnp.where(row8 < j, tr, xr[0:8]) * wj
    qkc_ref[...] = acc * _sigmoid(acc)
    qkc_ref[0:8, :] = acc_head * _sigmoid(acc_head)
    tail_ref[...] = x[L - 8:L]

    gates = misc_ref[...] + gb_ref[...]
    lf = jnp.minimum(gates, 0.0) - jnp.log(1.0 + jnp.exp(-jnp.abs(gates)))
    tri = tri_ref[...]
    hi, mid, lo = _split3(lf)
    bcum = _dot(tri, hi) + _dot(tri, mid) + _dot(tri, lo)
    bcum_t = bcum.T
    gates_t = gates.T

    t_idx = lax.broadcasted_iota(I32, (L, L), 0)
    s_idx = lax.broadcasted_iota(I32, (L, L), 1)
    causal = t_idx >= s_idx

    ys, cs, ns, ms_new = [], [], [], []
    for h in range(M_HEADS):
        sl = slice(h * dh, (h + 1) * dh)
        q = qkc_ref[:, h * dh:(h + 1) * dh]
        k = qkc_ref[:, M_WIDTH + h * dh:M_WIDTH + (h + 1) * dh] * (dh ** -0.5)
        v = v_ref[:, sl]
        qb, kb, vb = q.astype(BF16), k.astype(BF16), v.astype(BF16)

        b_col = bcum[:, M_HEADS + h:M_HEADS + h + 1]
        i_col = gates[:, h:h + 1]
        b_row = bcum_t[M_HEADS + h:M_HEADS + h + 1, :]
        i_row = gates_t[h:h + 1, :]
        g_tot = bcum[L - 1:L, M_HEADS + h:M_HEADS + h + 1]

        c_prev = c_ref[sl, :]
        n_prev = n_ref[h:h + 1, :]
        m_prev = m_ref[h:h + 1, 0:1]

        dmat = jnp.where(causal, b_col - b_row + i_row, -jnp.inf)
        inter = b_col + m_prev
        m_t = jnp.maximum(inter, jnp.max(dmat, axis=-1, keepdims=True))
        sw = jnp.exp(dmat - m_t) * _dot_nt(qb, kb)
        s_inter = jnp.exp(inter - m_t)
        num = _dot(sw.astype(BF16), vb) + s_inter * _dot_nt(qb, c_prev.astype(BF16))
        den = jnp.sum(sw, axis=-1, keepdims=True) + s_inter * jnp.sum(q * n_prev, axis=-1, keepdims=True)
        hh = num / jnp.maximum(jnp.abs(den), jnp.exp(-m_t))
        ms = jnp.mean(hh * hh, axis=-1, keepdims=True)
        hn = hh * lax.rsqrt(ms + EPS) * ng_ref[:, sl]
        ys.append(_sigmoid(o_ref[:, sl]) * hn)

        a_col = g_tot - b_col + i_col
        m_loc = jnp.max(a_col, axis=0, keepdims=True)
        w_loc = jnp.exp(a_col - m_loc)
        c_loc = _dot_tn((v * w_loc).astype(BF16), kb)
        n_loc = jnp.sum(k * w_loc, axis=0, keepdims=True)
        m_new = jnp.maximum(g_tot + m_prev, m_loc)
        s_old = jnp.exp(g_tot + m_prev - m_new)
        s_loc = jnp.exp(m_loc - m_new)
        cs.append(s_old * c_prev + s_loc * c_loc)
        ns.append(s_old * n_prev + s_loc * n_loc)
        ms_new.append(jnp.broadcast_to(m_new, (1, LANE)))

    y_ref[...] = jnp.concatenate(ys, axis=1)
    c_ref[...] = jnp.concatenate(cs, axis=0)
    n_ref[...] = jnp.concatenate(ns, axis=0)
    m_ref[...] = jnp.concatenate(ms_new, axis=0)


def _mlstm(z, conv_w, conv_b, gate_b_row, norm_g_row, tri, batch, seq):
    n = z.shape[0]
    nc = seq // M_CHUNK
    L = M_CHUNK
    row = lambda b, c: b * nc + c
    return pl.pallas_call(
        _mlstm_kernel,
        grid=(batch, nc),
        in_specs=[
            pl.BlockSpec((L, 2 * M_WIDTH), lambda b, c: (row(b, c), COL_MQK // (2 * M_WIDTH))),
            pl.BlockSpec((L, M_WIDTH), lambda b, c: (row(b, c), COL_MV // M_WIDTH)),
            pl.BlockSpec((L, M_WIDTH), lambda b, c: (row(b, c), COL_MO // M_WIDTH)),
            pl.BlockSpec((L, LANE), lambda b, c: (row(b, c), COL_MISC // LANE)),
            pl.BlockSpec((M_CONV, 2 * M_WIDTH), lambda b, c: (0, 0)),
            pl.BlockSpec((1, 2 * M_WIDTH), lambda b, c: (0, 0)),
            pl.BlockSpec((1, LANE), lambda b, c: (0, 0)),
            pl.BlockSpec((1, M_WIDTH), lambda b, c: (0, 0)),
            pl.BlockSpec((L, L), lambda b, c: (0, 0)),
        ],
        out_specs=pl.BlockSpec((L, M_WIDTH), lambda b, c: (row(b, c), 0)),
        out_shape=jax.ShapeDtypeStruct((n, M_WIDTH), F32),
        scratch_shapes=[
            pltpu.VMEM((8, 2 * M_WIDTH), F32),
            pltpu.VMEM((L, 2 * M_WIDTH), F32),
            pltpu.VMEM((M_HEADS * M_HEAD_DIM, M_HEAD_DIM), F32),
            pltpu.VMEM((M_HEADS, M_HEAD_DIM), F32),
            pltpu.VMEM((M_HEADS, LANE), F32),
        ],
        compiler_params=_cparams(("parallel", "arbitrary")),
        name="mlstm",
    )(z, z, z, z, conv_w, conv_b, gate_b_row, norm_g_row, tri)


def _rope(x, c, s):
    w = x.shape[-1]
    lane = lax.broadcasted_iota(I32, (1, w), 1)
    fwd = pltpu.roll(x, ROPE_HALF, axis=1)
    bwd = pltpu.roll(x, w - ROPE_HALF, axis=1)
    swapped = jnp.where((lane & (ROPE_DIM - 1)) < ROPE_HALF, bwd, fwd)
    return x * c + swapped * s


def _head_rms(x, g, grp):
    sq = x * x
    hi = sq.astype(BF16)
    lo = (sq - hi.astype(F32)).astype(BF16)
    ss = _dot(hi, grp) + _dot(lo, grp)
    return x * lax.rsqrt(ss * (1.0 / ROPE_DIM) + EPS) * g


def _store_transposed(o_ref, v):
    for j in range(o_ref.shape[0]):
        o_ref[j] = v[j * KSUB:(j + 1) * KSUB, :].T.astype(BF16)


def _prep_kernel(sq_ref, dq_ref, dk_ref, dv_ref, ckv_ref, iq_ref, ik_ref, c_ref, s_ref, grp_ref,
                 kvg_ref, wkv_ref, sqg_ref, skg_ref, dqg_ref, dkg_ref,
                 sq_o, sk_o, sv_o, iq_o, ik_o, dq_o, dk_o, dv_o):
    c256 = c_ref[...]
    s256 = s_ref[...]
    c512 = jnp.concatenate([c256, c256], axis=1)
    s512 = jnp.concatenate([s256, s256], axis=1)
    grp = grp_ref[...]
    qscale = ROPE_DIM ** -0.5 * LOG2E

    sq_o[...] = (_rope(_head_rms(sq_ref[...], sqg_ref[...], grp), c512, s512) * qscale).astype(BF16)

    ckv = ckv_ref[...]
    ms = jnp.mean(ckv * ckv, axis=-1, keepdims=True)
    ckvn = (ckv * lax.rsqrt(ms + EPS) * kvg_ref[...]).astype(BF16)
    kv = _dot(ckvn, wkv_ref[...])
    sk_o[...] = _rope(_head_rms(kv[:, :S_WIDTH], skg_ref[...], grp), c512, s512).astype(BF16)
    _store_transposed(sv_o, kv[:, S_WIDTH:])

    iq_o[...] = _rope(iq_ref[...], c256, s256).astype(BF16)
    ik_o[...] = _rope(ik_ref[...], c256, s256).astype(BF16)

    dq_o[...] = (_rope(_head_rms(dq_ref[...], dqg_ref[...], grp), c512, s512) * qscale).astype(BF16)
    dk_o[...] = _rope(_head_rms(dk_ref[...], dkg_ref[...], grp), c512, s512).astype(BF16)
    _store_transposed(dv_o, dv_ref[...])


def _prep(z, rope_c, rope_s, grp, kv_g, w_kv, sq_g, sk_g, dq_g, dk_g, seq, *, tm=512):
    n = z.shape[0]
    tm = min(tm, seq)
    nt = seq // tm
    zspec = lambda w, col: pl.BlockSpec((tm, w), lambda i: (i, col // w))
    const = lambda shape: pl.BlockSpec(shape, lambda i: (0, 0))
    o512 = pl.BlockSpec((tm, 512), lambda i: (i, 0))
    o256 = pl.BlockSpec((tm, 256), lambda i: (i, 0))
    s512 = jax.ShapeDtypeStruct((n, 512), BF16)
    s256 = jax.ShapeDtypeStruct((n, 256), BF16)
    ot = pl.BlockSpec((tm // KSUB, 512, KSUB), lambda i: (i, 0, 0))
    st = jax.ShapeDtypeStruct((n // KSUB, 512, KSUB), BF16)
    return pl.pallas_call(
        _prep_kernel,
        grid=(n // tm,),
        in_specs=[
            zspec(512, COL_SQ), zspec(512, COL_DQ), zspec(512, COL_DK), zspec(512, COL_DV),
            zspec(256, COL_CKV), zspec(256, COL_IQ), zspec(256, COL_IK4),
            pl.BlockSpec((tm, 256), lambda i: (i % nt, 0)),
            pl.BlockSpec((tm, 256), lambda i: (i % nt, 0)),
            const((512, 512)),
            const((1, S_KV_RANK)), const((S_KV_RANK, 2 * S_WIDTH)),
            const((1, 512)), const((1, 512)), const((1, 512)), const((1, 512)),
        ],
        out_specs=[o512, o512, ot, o256, o256, o512, o512, ot],
        out_shape=[s512, s512, st, s256, s256, s512, s512, st],
        compiler_params=_cparams(("parallel",)),
        name="attn_prep",
    )(z, z, z, z, z, z, z, rope_c, rope_s, grp, kv_g, w_kv, sq_g, sk_g, dq_g, dk_g)


def _mask_heads(q_ref, qm_ref, n_heads):
    lane_grp = lax.broadcasted_iota(I32, (1, HEAD_GROUP), 1) // ROPE_DIM
    for h in range(n_heads):
        g = h // 4
        qg = q_ref[:, g * HEAD_GROUP:(g + 1) * HEAD_GROUP]
        qm_ref[h] = jnp.where(lane_grp == (h % 4), qg, jnp.zeros_like(qg))


def _softmax_group(logits, h0, values, dv, acc_ref, m_ref, l_ref):
    nh = len(logits)
    hs = slice(h0, h0 + nh)
    rows = slice(h0 * dv, (h0 + nh) * dv)
    m_old = m_ref[hs, :]
    m_new = jnp.maximum(m_old, jnp.concatenate([jnp.max(s, axis=0, keepdims=True) for s in logits], axis=0))
    alpha = jnp.exp2(m_old - m_new)
    ps = [jnp.exp2(s - m_new[j:j + 1, :]) for j, s in enumerate(logits)]
    l_ref[hs, :] = alpha * l_ref[hs, :] + jnp.concatenate([jnp.sum(p, axis=0, keepdims=True) for p in ps], axis=0)
    m_ref[hs, :] = m_new
    acc_old = acc_ref[rows, :]
    acc_ref[rows, :] = jnp.concatenate(
        [alpha[j:j + 1, :] * acc_old[j * dv:(j + 1) * dv, :] + _dot(values[j], ps[j].astype(BF16))
         for j in range(nh)], axis=0)


def _sortable(x):
    bits = pltpu.bitcast(x, I32)
    return bits ^ ((bits >> 31) & 0x7FFFFFFF)


def _tree_sum(parts):
    while len(parts) > 1:
        parts = [parts[j] + parts[j + 1] for j in range(0, len(parts), 2)]
    return parts[0]


def _dsa_kernel(sq_ref, iq_ref, misc_ref, sk_ref, svt_ref, ik_ref, y_ref,
                sc_ref, hi_ref, lo_ref, qm_ref, iqm_ref, acc_ref, m_ref, l_ref, sa_ref, sb_ref,
                *, tq, topk, idx_bits):
    i = pl.program_id(1)
    nks = (i + 1) * (tq // KSUB)
    q_idx = i * tq + lax.broadcasted_iota(I32, (1, tq), 1)
    k_sub = lax.broadcasted_iota(I32, (KSUB, 1), 0)

    _mask_heads(iq_ref, iqm_ref, IDX_HEADS)
    w_t = (misc_ref[...] * (IDX_WIDTH ** -0.5)).T
    w_rows = [w_t[MISC_IW + h:MISC_IW + h + 1, :] for h in range(IDX_HEADS)]

    k_pair = lax.broadcasted_iota(I32, (2 * KSUB, 1), 0)

    def score_body(j, carry):
        ikb = ik_ref[pl.ds(pl.multiple_of(j * (2 * KSUB), 2 * KSUB), 2 * KSUB), :]
        s = jnp.zeros((2 * KSUB, tq), F32)
        for h in range(IDX_HEADS):
            s = s + w_rows[h] * jnp.maximum(_dot_nt(ikb, iqm_ref[h]), 0.0)
        key = jnp.where(j * (2 * KSUB) + k_pair <= q_idx, _sortable(s), INT_MIN)
        pair = pl.ds(2 * j, 2)
        sc_ref[pair] = key.reshape(2, KSUB, tq)
        hi_ref[pair] = (key >> 16).astype(I16).reshape(2, KSUB, tq)
        lo_ref[pair] = ((key & 0xFFFF) - HALF16).astype(I16).reshape(2, KSUB, tq)
        return carry

    lax.fori_loop(0, nks // 2, score_body, 0)

    def count16(ref, trial, strict):
        def body(j, acc):
            blk = ref[pl.ds(2 * j, 2)].reshape(2 * KSUB, tq)
            hit = (blk > trial) if strict else (blk >= trial)
            c = jnp.where(hit, jnp.int16(1), jnp.int16(0))
            return acc + _tree_sum([c[r * 16:(r + 1) * 16] for r in range(2 * KSUB // 16)])
        acc = lax.fori_loop(0, nks // 2, body, jnp.zeros((16, tq), I16))
        return jnp.sum(acc.astype(I32), axis=0, keepdims=True)

    def search16(ref, need):
        def body(t, cur):
            bit = jnp.left_shift(jnp.int32(1), 15 - t)
            trial = ((cur | bit) - HALF16).astype(I16)
            return jnp.where(count16(ref, trial, False) >= need, cur | bit, cur)
        return lax.fori_loop(0, 16, body, jnp.zeros((1, tq), I32))

    hi_k = search16(hi_ref, topk) - HALF16
    hi_k16 = hi_k.astype(I16)
    need_lo = topk - count16(hi_ref, hi_k16, True)

    def lo_mask_body(ks, carry):
        lo_ref[ks] = jnp.where(hi_ref[ks] == hi_k16, lo_ref[ks], jnp.int16(-HALF16))
        return carry

    lax.fori_loop(0, nks, lo_mask_body, 0)
    thr = (hi_k << 16) | search16(lo_ref, need_lo)

    def count32(pred):
        def body(ks, acc):
            c = jnp.where(pred(sc_ref[ks], ks * KSUB + k_sub), 1, 0)
            return acc + _tree_sum([c[r * 8:(r + 1) * 8] for r in range(KSUB // 8)])
        acc = lax.fori_loop(0, nks, body, jnp.zeros((8, tq), I32))
        return jnp.sum(acc, axis=0, keepdims=True)

    n_ge = count32(lambda blk, kidx: blk >= thr)
    n_gt = count32(lambda blk, kidx: blk > thr)

    need = topk - n_gt
    m_ref[0:1, :] = jnp.full((1, tq), 2.0**30, F32)

    @pl.when(jnp.max(n_ge) > topk)
    def _():
        def idx_body(t, cut):
            bit = jnp.left_shift(jnp.int32(1), idx_bits - 1 - t)
            trial = cut | bit
            cnt = count32(lambda blk, kidx: (blk == thr) & (kidx < trial))
            return jnp.where(cnt < need, trial, cut)
        cut = lax.fori_loop(0, idx_bits, idx_body, jnp.zeros((1, tq), I32))
        m_ref[0:1, :] = cut.astype(F32)

    cut = m_ref[0:1, :].astype(I32)

    thr_sel = jnp.maximum(thr, INT_MIN + 1)

    def bias_body(j, carry):
        pair = pl.ds(2 * j, 2)
        key = sc_ref[pair].reshape(2 * KSUB, tq)
        kidx = j * (2 * KSUB) + k_pair
        sel = key >= thr_sel + jnp.where(kidx <= cut, 0, 1)
        sc_ref[pair] = pltpu.bitcast(jnp.where(sel, 0.0, NEG).astype(F32), I32).reshape(2, KSUB, tq)
        return carry

    lax.fori_loop(0, nks // 2, bias_body, 0)

    _mask_heads(sq_ref, qm_ref, S_HEADS)
    acc_ref[...] = jnp.zeros_like(acc_ref)
    m_ref[...] = jnp.full_like(m_ref, NEG)
    l_ref[...] = jnp.zeros_like(l_ref)

    def qk_stage(ks, dst_ref):
        off = pl.multiple_of(ks * KSUB, KSUB)
        bias = pltpu.bitcast(sc_ref[ks], F32)
        for g in range(S_HEADS // 4):
            kg = sk_ref[pl.ds(off, KSUB), g * HEAD_GROUP:(g + 1) * HEAD_GROUP]
            for h in range(4 * g, 4 * g + 4):
                dst_ref[h] = _dot_nt(kg, qm_ref[h]) + bias

    def sm_stage(ks, src_ref):
        _softmax_group([src_ref[h] for h in range(S_HEADS)], 0,
                       [svt_ref[ks, h * ROPE_DIM:(h + 1) * ROPE_DIM, :] for h in range(S_HEADS)],
                       ROPE_DIM, acc_ref, m_ref, l_ref)

    qk_stage(0, sa_ref)

    def attn_body(j, carry):
        qk_stage(2 * j + 1, sb_ref)
        sm_stage(2 * j, sa_ref)
        qk_stage(jnp.minimum(2 * j + 2, nks - 1), sa_ref)
        sm_stage(2 * j + 1, sb_ref)
        return carry

    lax.fori_loop(0, nks // 2, attn_body, 0)

    for h in range(S_HEADS):
        rows = slice(h * ROPE_DIM, (h + 1) * ROPE_DIM)
        acc_ref[rows, :] = acc_ref[rows, :] / l_ref[h:h + 1, :]
    y_ref[...] = acc_ref[...].T


def _dsa(z, sq, sk, svt, iq, ik4, batch, seq, *, tq=256):
    n = z.shape[0]
    tq = min(tq, seq)
    nq = seq // tq
    nsub = seq // KSUB
    topk = min(IDX_TOPK_MAX, seq // 4)
    kern = functools.partial(_dsa_kernel, tq=tq, topk=topk, idx_bits=seq.bit_length())
    return pl.pallas_call(
        kern,
        grid=(batch, nq),
        in_specs=[
            pl.BlockSpec((tq, S_WIDTH), lambda b, i: (b * nq + i, 0)),
            pl.BlockSpec((tq, IDX_WIDTH), lambda b, i: (b * nq + i, 0)),
            pl.BlockSpec((tq, LANE), lambda b, i: (b * nq + i, COL_MISC // LANE)),
            pl.BlockSpec((seq, S_WIDTH), lambda b, i: (b, 0)),
            pl.BlockSpec((nsub, S_WIDTH, KSUB), lambda b, i: (b, 0, 0)),
            pl.BlockSpec((seq, IDX_WIDTH), lambda b, i: (b, 0)),
        ],
        out_specs=pl.BlockSpec((tq, S_WIDTH), lambda b, i: (b * nq + i, 0)),
        out_shape=jax.ShapeDtypeStruct((n, S_WIDTH), F32),
        scratch_shapes=[
            pltpu.VMEM((nsub, KSUB, tq), I32),
            pltpu.VMEM((nsub, KSUB, tq), I16),
            pltpu.VMEM((nsub, KSUB, tq), I16),
            pltpu.VMEM((S_HEADS, tq, HEAD_GROUP), BF16),
            pltpu.VMEM((IDX_HEADS, tq, HEAD_GROUP), BF16),
            pltpu.VMEM((S_WIDTH, tq), F32),
            pltpu.VMEM((S_HEADS, tq), F32),
            pltpu.VMEM((S_HEADS, tq), F32),
            pltpu.VMEM((S_HEADS, KSUB, tq), F32),
            pltpu.VMEM((S_HEADS, KSUB, tq), F32),
        ],
        compiler_params=_cparams(("parallel", "arbitrary")),
        name="dsa_attn",
    )(sq, iq, z, sk, svt, ik4)


def _diff_kernel(q_ref, k_ref, vt_ref, lam_ref, og_ref, y_ref, qm_ref, acc_ref, m_ref, l_ref,
                 sa_ref, sb_ref, *, tq, lam_init):
    i = pl.program_id(1)
    nfull = i * (tq // KSUB)
    q_idx = i * tq + lax.broadcasted_iota(I32, (1, tq), 1)
    k_sub = lax.broadcasted_iota(I32, (KSUB, 1), 0)

    _mask_heads(q_ref, qm_ref, DA_MAPS)
    acc_ref[...] = jnp.zeros_like(acc_ref)
    m_ref[...] = jnp.full_like(m_ref, NEG)
    l_ref[...] = jnp.zeros_like(l_ref)

    def qk_stage(ks, dst_ref):
        off = pl.multiple_of(ks * KSUB, KSUB)
        for g in range(DA_MAPS // 4):
            kg = k_ref[pl.ds(off, KSUB), g * HEAD_GROUP:(g + 1) * HEAD_GROUP]
            for m in range(4 * g, 4 * g + 4):
                dst_ref[m] = _dot_nt(kg, qm_ref[m])

    def sm_stage(ks, src_ref, masked):
        logits = [src_ref[m] for m in range(DA_MAPS)]
        if masked:
            bias = jnp.where(ks * KSUB + k_sub <= q_idx, 0.0, NEG).astype(F32)
            logits = [s + bias for s in logits]
        _softmax_group(logits, 0,
                       [vt_ref[ks, (m // 2) * DA_V_DIM:(m // 2 + 1) * DA_V_DIM, :] for m in range(DA_MAPS)],
                       DA_V_DIM, acc_ref, m_ref, l_ref)

    assert tq == 2 * KSUB
    qk_stage(0, sa_ref)

    def full_body(j, carry):
        qk_stage(2 * j + 1, sb_ref)
        sm_stage(2 * j, sa_ref, False)
        qk_stage(2 * j + 2, sa_ref)
        sm_stage(2 * j + 1, sb_ref, False)
        return carry

    lax.fori_loop(0, nfull // 2, full_body, 0)
    qk_stage(nfull + 1, sb_ref)
    sm_stage(nfull, sa_ref, True)
    sm_stage(nfull + 1, sb_ref, True)

    lam = lam_ref[...]
    p01 = jnp.sum(lam[0:1] * lam[1:2], axis=-1, keepdims=True)
    p23 = jnp.sum(lam[2:3] * lam[3:4], axis=-1, keepdims=True)
    lam_val = jnp.exp(p01) - jnp.exp(p23) + lam_init
    for hd in range(DA_HEADS):
        r1 = slice(2 * hd * DA_V_DIM, (2 * hd + 1) * DA_V_DIM)
        r2 = slice((2 * hd + 1) * DA_V_DIM, (2 * hd + 2) * DA_V_DIM)
        o = acc_ref[r1, :] / l_ref[2 * hd:2 * hd + 1, :] - lam_val * (acc_ref[r2, :] / l_ref[2 * hd + 1:2 * hd + 2, :])
        ms = jnp.mean(o * o, axis=0, keepdims=True)
        acc_ref[hd * DA_V_DIM:(hd + 1) * DA_V_DIM, :] = o * lax.rsqrt(ms + EPS) * og_ref[...] * (1.0 - lam_init)
    y_ref[...] = acc_ref[0:DA_WIDTH, :].T


def _diff_attn(dq, dk, dvt, lam, out_g_col, batch, seq, lam_init, *, tq=256):
    n = dq.shape[0]
    tq = min(tq, seq)
    nq = seq // tq
    nsub = seq // KSUB
    kern = functools.partial(_diff_kernel, tq=tq, lam_init=lam_init)
    return pl.pallas_call(
        kern,
        grid=(batch, nq),
        in_specs=[
            pl.BlockSpec((tq, 512), lambda b, i: (b * nq + i, 0)),
            pl.BlockSpec((seq, 512), lambda b, i: (b, 0)),
            pl.BlockSpec((nsub, DA_WIDTH, KSUB), lambda b, i: (b, 0, 0)),
            pl.BlockSpec((4, ROPE_DIM), lambda b, i: (0, 0)),
            pl.BlockSpec((DA_V_DIM, 1), lambda b, i: (0, 0)),
        ],
        out_specs=pl.BlockSpec((tq, DA_WIDTH), lambda b, i: (b * nq + i, 0)),
        out_shape=jax.ShapeDtypeStruct((n, DA_WIDTH), F32),
        scratch_shapes=[
            pltpu.VMEM((DA_MAPS, tq, HEAD_GROUP), BF16),
            pltpu.VMEM((DA_MAPS * DA_V_DIM, tq), F32),
            pltpu.VMEM((DA_MAPS, tq), F32),
            pltpu.VMEM((DA_MAPS, tq), F32),
            pltpu.VMEM((DA_MAPS, KSUB, tq), F32),
            pltpu.VMEM((DA_MAPS, KSUB, tq), F32),
        ],
        compiler_params=_cparams(("parallel", "arbitrary")),
        name="diff_attn",
    )(dq, dk, dvt, lam, out_g_col)


def _merge_kernel(x_ref, ya_ref, yb_ref, yc_ref, gp_ref, bg_ref, wb_ref, wo_ref, ng_ref, xo_ref, xn_ref):
    merged = None
    for br, y_ref in enumerate((ya_ref, yb_ref, yc_ref)):
        sl = slice(br * D_MODEL, (br + 1) * D_MODEL)
        gate = _sigmoid(gp_ref[:, sl] + bg_ref[:, sl])
        term = gate * _dot(y_ref[...].astype(BF16), wb_ref[br])
        merged = term if merged is None else merged + term
    xo = x_ref[...] + _dot(merged.astype(BF16), wo_ref[...])
    xo_ref[...] = xo
    ms = jnp.mean(xo * xo, axis=-1, keepdims=True)
    xn_ref[...] = (xo * lax.rsqrt(ms + EPS) * ng_ref[...]).astype(BF16)


def _merge(x2, ya, yb, yc, z, b_gate, w_branch, w_out, norm_g, *, tm=512):
    n = x2.shape[0]
    tm = min(tm, n)
    row = lambda w: pl.BlockSpec((tm, w), lambda i: (i, 0))
    return pl.pallas_call(
        _merge_kernel,
        grid=(n // tm,),
        in_specs=[
            row(D_MODEL), row(512), row(512), row(512),
            pl.BlockSpec((tm, N_BRANCH * D_MODEL), lambda i: (i, COL_GATE)),
            pl.BlockSpec((1, N_BRANCH * D_MODEL), lambda i: (0, 0)),
            pl.BlockSpec((N_BRANCH, 512, D_MODEL), lambda i: (0, 0, 0)),
            pl.BlockSpec((D_MODEL, D_MODEL), lambda i: (0, 0)),
            pl.BlockSpec((1, D_MODEL), lambda i: (0, 0)),
        ],
        out_specs=[row(D_MODEL), row(D_MODEL)],
        out_shape=[jax.ShapeDtypeStruct((n, D_MODEL), F32), jax.ShapeDtypeStruct((n, D_MODEL), BF16)],
        compiler_params=_cparams(("parallel",)),
        name="merge_out",
    )(x2, ya, yb, yc, z, b_gate, w_branch, w_out, norm_g)


def _ffn_kernel(xn_ref, x_ref, wg_ref, wu_ref, wd_ref, o_ref, acc_ref):
    j = pl.program_id(1)
    xn = xn_ref[...]
    g = _dot(xn, wg_ref[...])
    u = _dot(xn, wu_ref[...])
    part = _dot((g * _sigmoid(g) * u).astype(BF16), wd_ref[...])

    @pl.when(j == 0)
    def _():
        acc_ref[...] = part

    @pl.when(j > 0)
    def _():
        acc_ref[...] += part

    @pl.when(j == pl.num_programs(1) - 1)
    def _():
        o_ref[...] = x_ref[...] + acc_ref[...]


def _ffn(xn, x2, wg, wu, wd, *, tm=512, fc=1408):
    n = x2.shape[0]
    tm = min(tm, n)
    return pl.pallas_call(
        _ffn_kernel,
        grid=(n // tm, FF_DIM // fc),
        in_specs=[
            pl.BlockSpec((tm, D_MODEL), lambda i, j: (i, 0)),
            pl.BlockSpec((tm, D_MODEL), lambda i, j: (i, 0)),
            pl.BlockSpec((D_MODEL, fc), lambda i, j: (0, j)),
            pl.BlockSpec((D_MODEL, fc), lambda i, j: (0, j)),
            pl.BlockSpec((fc, D_MODEL), lambda i, j: (j, 0)),
        ],
        out_specs=pl.BlockSpec((tm, D_MODEL), lambda i, j: (i, 0)),
        out_shape=jax.ShapeDtypeStruct((n, D_MODEL), F32),
        scratch_shapes=[pltpu.VMEM((tm, D_MODEL), F32)],
        compiler_params=_cparams(("parallel", "arbitrary")),
        name="ffn",
    )(xn, x2, wg, wu, wd)


def _rope_tables(seq):
    inv = 1.0 / jnp.power(ROPE_THETA, jnp.arange(0, ROPE_DIM, 2, dtype=F32) / ROPE_DIM)
    ang = jnp.arange(seq, dtype=F32)[:, None] * inv[None, :]
    cos, sin = jnp.cos(ang), jnp.sin(ang)
    c64 = jnp.concatenate([cos, cos], axis=1)
    s64 = jnp.concatenate([-sin, sin], axis=1)
    return jnp.tile(c64, (1, 4)), jnp.tile(s64, (1, 4))


def _layer(x2, li, batch, seq, consts, norm_mix_g, w_in_l, b_gate, conv_w, conv_b, gate_b, m_norm_g,
           kv_norm_g, w_kv_up, sq_g, sk_g, dq_g, dk_g, lam, d_out_g, w_branch, w_out,
           norm_ffn_g, w_gate_up, w_down):
    rope_c, rope_s, grp, tri = consts
    lam_init = 0.8 - 0.6 * math.exp(-0.3 * li)
    tile8 = lambda g: jnp.tile(g, 8)[None, :]

    z = _inproj(x2, norm_mix_g[None, :], w_in_l)

    gate_row = jnp.zeros((1, LANE), F32).at[0, MISC_IF:MISC_IF + 2 * M_HEADS].set(gate_b)
    ya = _mlstm(z, conv_w, conv_b[None, :], gate_row, m_norm_g.reshape(1, M_WIDTH), tri, batch, seq)

    sq, sk, svt, iq, ik4, dq, dk, dvt = _prep(
        z, rope_c, rope_s, grp, kv_norm_g[None, :], w_kv_up.astype(BF16),
        tile8(sq_g), tile8(sk_g), tile8(dq_g), tile8(dk_g), seq)
    yb = _dsa(z, sq, sk, svt, iq, ik4, batch, seq)
    yc = _diff_attn(dq, dk, dvt, lam, d_out_g[:, None], batch, seq, lam_init)

    xo, xn = _merge(x2, ya, yb, yc, z, b_gate[None, :], w_branch.astype(BF16), w_out.astype(BF16),
                    norm_ffn_g[None, :])
    return _ffn(xn, xo, w_gate_up[:, :FF_DIM].astype(BF16), w_gate_up[:, FF_DIM:].astype(BF16),
                w_down.astype(BF16))


def kernel(x, norm_mix_g, w_in, b_gate, mlstm_conv_w, mlstm_conv_b, mlstm_gate_b, mlstm_norm_g,
           dsa_kv_norm_g, dsa_w_kv_up, dsa_q_norm_g, dsa_k_norm_g, diff_q_norm_g, diff_k_norm_g,
           diff_lambda, diff_out_norm_g, w_branch, w_out, norm_ffn_g, w_gate_up, w_down):
    batch, seq, d = x.shape
    depth = w_in.shape[0]
    assert w_in.shape[2] == IN_COLS and d == D_MODEL
    rope_c, rope_s = _rope_tables(seq)
    gi = jnp.arange(512) // ROPE_DIM
    grp = (gi[:, None] == gi[None, :]).astype(BF16)
    ti = jnp.arange(M_CHUNK)
    tri = (ti[:, None] >= ti[None, :]).astype(BF16)
    consts = (rope_c, rope_s, grp, tri)
    w_in_l = _wlayout(w_in)
    x2 = x.reshape(batch * seq, d)
    for li in range(depth):
        x2 = _layer(x2, li, batch, seq, consts, norm_mix_g[li], w_in_l[li], b_gate[li], mlstm_conv_w[li],
                    mlstm_conv_b[li], mlstm_gate_b[li], mlstm_norm_g[li], dsa_kv_norm_g[li],
                    dsa_w_kv_up[li], dsa_q_norm_g[li], dsa_k_norm_g[li], diff_q_norm_g[li],
                    diff_k_norm_g[li], diff_lambda[li], diff_out_norm_g[li], w_branch[li], w_out[li],
                    norm_ffn_g[li], w_gate_up[li], w_down[li])
    return x2.reshape(batch, seq, d)
```

```python
import functools
import math

import jax
import jax.numpy as jnp
from jax import lax
from jax.experimental import pallas as pl
from jax.experimental.pallas import tpu as pltpu

F32 = jnp.float32
BF16 = jnp.bfloat16
I32 = jnp.int32
I16 = jnp.int16

D_MODEL = 1024
EPS = 1e-6
ROPE_DIM = 64
ROPE_HALF = ROPE_DIM // 2
ROPE_THETA = 10000.0

M_HEADS = 4
M_HEAD_DIM = 128
M_WIDTH = M_HEADS * M_HEAD_DIM
M_CONV = 4
M_CHUNK = 128
M_INIT = -1e30

S_HEADS = 8
S_WIDTH = S_HEADS * ROPE_DIM
S_KV_RANK = 256
IDX_HEADS = 4
IDX_WIDTH = IDX_HEADS * ROPE_DIM
IDX_TOPK_MAX = 256

DA_HEADS = 4
DA_MAPS = 2 * DA_HEADS
DA_V_DIM = 2 * ROPE_DIM
DA_WIDTH = DA_HEADS * DA_V_DIM

N_BRANCH = 3
FF_DIM = 2816

NEG = -1e30
LOG2E = math.log2(math.e)
INT_MIN = -(2**31)
HALF16 = 2**15
LANE = 128
HEAD_GROUP = 256
KSUB = 128
ONES_ROWS = 16
S_VROWS = ROPE_DIM + ONES_ROWS
DA_VROWS = DA_V_DIM + ONES_ROWS

W_IN_SIZES = (2 * M_WIDTH, M_WIDTH, M_WIDTH, 2 * M_HEADS, S_WIDTH, S_KV_RANK, IDX_WIDTH, ROPE_DIM,
              IDX_HEADS, 2 * DA_HEADS * ROPE_DIM, 2 * DA_HEADS * ROPE_DIM, DA_WIDTH, N_BRANCH * D_MODEL)
IN_COLS = sum(W_IN_SIZES)

COL_GATE = 0
COL_MQK = 3072
COL_MV = 4096
COL_MO = 4608
COL_SQ = 5120
COL_DQ = 5632
COL_DK = 6144
COL_DV = 6656
COL_CKV = 7168
COL_IQ = 7424
COL_IK4 = 7680
COL_MISC = 7936
MISC_IF = 0
MISC_IW = 8
Z_COLS = 8192

VMEM_LIMIT = 56 * 1024 * 1024


def _cparams(sem, flags=None):
    return pltpu.CompilerParams(dimension_semantics=sem, vmem_limit_bytes=VMEM_LIMIT, flags=flags)


def _sigmoid(x):
    return 1.0 / (1.0 + jnp.exp(-x))


def _dot(a, b):
    return jnp.dot(a, b, preferred_element_type=F32)


def _dot_nt(a, b):
    return lax.dot_general(a, b, (((1,), (1,)), ((), ())), preferred_element_type=F32)


def _dot_tn(a, b):
    return lax.dot_general(a, b, (((0,), (0,)), ((), ())), preferred_element_type=F32)


def _split3(x):
    hi = x.astype(BF16)
    r1 = x - hi.astype(F32)
    mid = r1.astype(BF16)
    lo = (r1 - mid.astype(F32)).astype(BF16)
    return hi, mid, lo


def _wlayout_kernel(w_ref, o_ref):
    w = w_ref[...]
    offs = [0]
    for s in W_IN_SIZES:
        offs.append(offs[-1] + s)
    seg = [w[:, offs[k]:offs[k + 1]] for k in range(len(W_IN_SIZES))]
    m_qk, m_v, m_o, m_if, s_q, s_ckv, i_q, i_k, i_w, d_q, d_k, d_v, g_pre = seg
    rows = w.shape[0]
    misc = jnp.concatenate([m_if, i_w, jnp.zeros((rows, LANE - 2 * M_HEADS - IDX_HEADS), F32)], axis=1)
    ik4 = jnp.concatenate([i_k] * IDX_HEADS, axis=1)
    tail = jnp.zeros((rows, Z_COLS - COL_MISC - LANE), F32)
    for col, val in ((COL_GATE, g_pre), (COL_MQK, m_qk), (COL_MV, m_v), (COL_MO, m_o), (COL_SQ, s_q),
                     (COL_DQ, d_q), (COL_DK, d_k), (COL_DV, d_v), (COL_CKV, s_ckv), (COL_IQ, i_q),
                     (COL_IK4, ik4), (COL_MISC, misc), (COL_MISC + LANE, tail)):
        o_ref[:, col:col + val.shape[1]] = val.astype(BF16)


def _wlayout(w_in, *, tr=256):
    depth, rows, cols = w_in.shape
    return pl.pallas_call(
        _wlayout_kernel,
        grid=(depth, rows // tr),
        in_specs=[pl.BlockSpec((None, tr, cols), lambda l, i: (l, i, 0))],
        out_specs=pl.BlockSpec((None, tr, Z_COLS), lambda l, i: (l, i, 0)),
        out_shape=jax.ShapeDtypeStruct((depth, rows, Z_COLS), BF16),
        compiler_params=_cparams(("parallel", "parallel")),
        name="w_layout",
    )(w_in)


def _inproj_kernel(x_ref, g_ref, w_ref, z_ref, xn_ref):
    @pl.when(pl.program_id(1) == 0)
    def _():
        x = x_ref[...]
        ms = jnp.mean(x * x, axis=-1, keepdims=True)
        xn_ref[...] = (x * lax.rsqrt(ms + EPS) * g_ref[...]).astype(BF16)

    z_ref[...] = _dot(xn_ref[...], w_ref[...])


def _inproj(x2, g, w, *, tm=1024, tn=1024):
    n = x2.shape[0]
    tm = min(tm, n)
    return pl.pallas_call(
        _inproj_kernel,
        grid=(n // tm, Z_COLS // tn),
        in_specs=[
            pl.BlockSpec((tm, D_MODEL), lambda i, j: (i, 0)),
            pl.BlockSpec((1, D_MODEL), lambda i, j: (0, 0)),
            pl.BlockSpec((D_MODEL, tn), lambda i, j: (0, j)),
        ],
        out_specs=pl.BlockSpec((tm, tn), lambda i, j: (i, j)),
        out_shape=jax.ShapeDtypeStruct((n, Z_COLS), F32),
        scratch_shapes=[pltpu.VMEM((tm, D_MODEL), BF16)],
        compiler_params=_cparams(("parallel", "arbitrary")),
        name="inproj",
    )(x2, g, w)


def _mlstm_kernel(qk_ref, v_ref, o_ref, misc_ref, cw_ref, cb_ref, gb_ref, ng_ref, tri_ref, y_ref,
                  tail_ref, qkc_ref, c_ref, n_ref, m_ref):
    L = M_CHUNK
    dh = M_HEAD_DIM

    @pl.when(pl.program_id(1) == 0)
    def _():
        tail_ref[...] = jnp.zeros_like(tail_ref)
        c_ref[...] = jnp.zeros_like(c_ref)
        n_ref[...] = jnp.zeros_like(n_ref)
        m_ref[...] = jnp.full_like(m_ref, M_INIT)

    x = qk_ref[...]
    tail = tail_ref[...]
    row8 = lax.broadcasted_iota(I32, (8, 1), 0)
    cw = cw_ref[...]
    cb = cb_ref[...]
    acc = x * cw[M_CONV - 1:M_CONV] + cb
    acc_head = x[0:8] * cw[M_CONV - 1:M_CONV] + cb
    for j in range(1, M_CONV):
        wj = cw[M_CONV - 1 - j:M_CONV - j]
        xr = pltpu.roll(x, j, axis=0)
        tr = pltpu.roll(tail, j, axis=0)
        acc = acc + xr * wj
        acc_head = acc_head + jnp.where(row8 < j, tr, xr[0:8]) * wj
    qkc_ref[...] = acc * _sigmoid(acc)
    qkc_ref[0:8, :] = acc_head * _sigmoid(acc_head)
    tail_ref[...] = x[L - 8:L]

    gates = misc_ref[...] + gb_ref[...]
    lf = jnp.minimum(gates, 0.0) - jnp.log(1.0 + jnp.exp(-jnp.abs(gates)))
    tri = tri_ref[...]
    hi, mid, lo = _split3(lf)
    bcum = _dot(tri, hi) + _dot(tri, mid) + _dot(tri, lo)
    bcum_t = bcum.T
    gates_t = gates.T

    t_idx = lax.broadcasted_iota(I32, (L, L), 0)
    s_idx = lax.broadcasted_iota(I32, (L, L), 1)
    causal = t_idx >= s_idx

    ys, cs, ns, ms_new = [], [], [], []
    for h in range(M_HEADS):
        sl = slice(h * dh, (h + 1) * dh)
        q = qkc_ref[:, h * dh:(h + 1) * dh]
        k = qkc_ref[:, M_WIDTH + h * dh:M_WIDTH + (h + 1) * dh] * (dh ** -0.5)
        v = v_ref[:, sl]
        qb, kb, vb = q.astype(BF16), k.astype(BF16), v.astype(BF16)

        b_col = bcum[:, M_HEADS + h:M_HEADS + h + 1]
        i_col = gates[:, h:h + 1]
        b_row = bcum_t[M_HEADS + h:M_HEADS + h + 1, :]
        i_row = gates_t[h:h + 1, :]
        g_tot = bcum[L - 1:L, M_HEADS + h:M_HEADS + h + 1]

        c_prev = c_ref[sl, :]
        n_prev = n_ref[h:h + 1, :]
        m_prev = m_ref[h:h + 1, 0:1]

        dmat = jnp.where(causal, b_col - b_row + i_row, -jnp.inf)
        inter = b_col + m_prev
        m_t = jnp.maximum(inter, jnp.max(dmat, axis=-1, keepdims=True))
        sw = jnp.exp(dmat - m_t) * _dot_nt(qb, kb)
        s_inter = jnp.exp(inter - m_t)
        num = _dot(sw.astype(BF16), vb) + s_inter * _dot_nt(qb, c_prev.astype(BF16))
        den = jnp.sum(sw, axis=-1, keepdims=True) + s_inter * jnp.sum(q * n_prev, axis=-1, keepdims=True)
        hh = num / jnp.maximum(jnp.abs(den), jnp.exp(-m_t))
        ms = jnp.mean(hh * hh, axis=-1, keepdims=True)
        hn = hh * lax.rsqrt(ms + EPS) * ng_ref[:, sl]
        ys.append(_sigmoid(o_ref[:, sl]) * hn)

        a_col = g_tot - b_col + i_col
        m_loc = jnp.max(a_col, axis=0, keepdims=True)
        w_loc = jnp.exp(a_col - m_loc)
        c_loc = _dot_tn((v * w_loc).astype(BF16), kb)
        n_loc = jnp.sum(k * w_loc, axis=0, keepdims=True)
        m_new = jnp.maximum(g_tot + m_prev, m_loc)
        s_old = jnp.exp(g_tot + m_prev - m_new)
        s_loc = jnp.exp(m_loc - m_new)
        cs.append(s_old * c_prev + s_loc * c_loc)
        ns.append(s_old * n_prev + s_loc * n_loc)
        ms_new.append(jnp.broadcast_to(m_new, (1, LANE)))

    y_ref[...] = jnp.concatenate(ys, axis=1)
    c_ref[...] = jnp.concatenate(cs, axis=0)
    n_ref[...] = jnp.concatenate(ns, axis=0)
    m_ref[...] = jnp.concatenate(ms_new, axis=0)


def _mlstm(z, conv_w, conv_b, gate_b_row, norm_g_row, tri, batch, seq):
    n = z.shape[0]
    nc = seq // M_CHUNK
    L = M_CHUNK
    row = lambda b, c: b * nc + c
    return pl.pallas_call(
        _mlstm_kernel,
        grid=(batch, nc),
        in_specs=[
            pl.BlockSpec((L, 2 * M_WIDTH), lambda b, c: (row(b, c), COL_MQK // (2 * M_WIDTH))),
            pl.BlockSpec((L, M_WIDTH), lambda b, c: (row(b, c), COL_MV // M_WIDTH)),
            pl.BlockSpec((L, M_WIDTH), lambda b, c: (row(b, c), COL_MO // M_WIDTH)),
            pl.BlockSpec((L, LANE), lambda b, c: (row(b, c), COL_MISC // LANE)),
            pl.BlockSpec((M_CONV, 2 * M_WIDTH), lambda b, c: (0, 0)),
            pl.BlockSpec((1, 2 * M_WIDTH), lambda b, c: (0, 0)),
            pl.BlockSpec((1, LANE), lambda b, c: (0, 0)),
            pl.BlockSpec((1, M_WIDTH), lambda b, c: (0, 0)),
            pl.BlockSpec((L, L), lambda b, c: (0, 0)),
        ],
        out_specs=pl.BlockSpec((L, M_WIDTH), lambda b, c: (row(b, c), 0)),
        out_shape=jax.ShapeDtypeStruct((n, M_WIDTH), F32),
        scratch_shapes=[
            pltpu.VMEM((8, 2 * M_WIDTH), F32),
            pltpu.VMEM((L, 2 * M_WIDTH), F32),
            pltpu.VMEM((M_HEADS * M_HEAD_DIM, M_HEAD_DIM), F32),
            pltpu.VMEM((M_HEADS, M_HEAD_DIM), F32),
            pltpu.VMEM((M_HEADS, LANE), F32),
        ],
        compiler_params=_cparams(("parallel", "arbitrary")),
        name="mlstm",
    )(z, z, z, z, conv_w, conv_b, gate_b_row, norm_g_row, tri)


def _rope(x, c, s):
    w = x.shape[-1]
    lane = lax.broadcasted_iota(I32, (1, w), 1)
    fwd = pltpu.roll(x, ROPE_HALF, axis=1)
    bwd = pltpu.roll(x, w - ROPE_HALF, axis=1)
    swapped = jnp.where((lane & (ROPE_DIM - 1)) < ROPE_HALF, bwd, fwd)
    return x * c + swapped * s


def _head_rms(x, g, grp):
    sq = x * x
    hi = sq.astype(BF16)
    lo = (sq - hi.astype(F32)).astype(BF16)
    ss = _dot(hi, grp) + _dot(lo, grp)
    return x * lax.rsqrt(ss * (1.0 / ROPE_DIM) + EPS) * g


def _store_transposed(o_ref, v, dv):
    ones = jnp.ones((ONES_ROWS, KSUB), BF16)
    for j in range(o_ref.shape[0]):
        vt = v[j * KSUB:(j + 1) * KSUB, :].T.astype(BF16)
        parts = []
        for h in range(vt.shape[0] // dv):
            parts += [vt[h * dv:(h + 1) * dv, :], ones]
        o_ref[j] = jnp.concatenate(parts, axis=0)


def _prep_kernel(sq_ref, dq_ref, dk_ref, dv_ref, ckv_ref, iq_ref, ik_ref, c_ref, s_ref, grp_ref,
                 kvg_ref, wkv_ref, sqg_ref, skg_ref, dqg_ref, dkg_ref,
                 sq_o, sk_o, sv_o, iq_o, ik_o, dq_o, dk_o, dv_o):
    c256 = c_ref[...]
    s256 = s_ref[...]
    c512 = jnp.concatenate([c256, c256], axis=1)
    s512 = jnp.concatenate([s256, s256], axis=1)
    grp = grp_ref[...]
    qscale = ROPE_DIM ** -0.5 * LOG2E

    sq_o[...] = (_rope(_head_rms(sq_ref[...], sqg_ref[...], grp), c512, s512) * qscale).astype(BF16)

    ckv = ckv_ref[...]
    ms = jnp.mean(ckv * ckv, axis=-1, keepdims=True)
    ckvn = (ckv * lax.rsqrt(ms + EPS) * kvg_ref[...]).astype(BF16)
    kv = _dot(ckvn, wkv_ref[...])
    sk_o[...] = _rope(_head_rms(kv[:, :S_WIDTH], skg_ref[...], grp), c512, s512).astype(BF16)
    _store_transposed(sv_o, kv[:, S_WIDTH:], ROPE_DIM)

    iq_o[...] = _rope(iq_ref[...], c256, s256).astype(BF16)
    ik_o[...] = _rope(ik_ref[...], c256, s256).astype(BF16)

    dq_o[...] = (_rope(_head_rms(dq_ref[...], dqg_ref[...], grp), c512, s512) * qscale).astype(BF16)
    dk_o[...] = _rope(_head_rms(dk_ref[...], dkg_ref[...], grp), c512, s512).astype(BF16)
    _store_transposed(dv_o, dv_ref[...], DA_V_DIM)


def _prep(z, rope_c, rope_s, grp, kv_g, w_kv, sq_g, sk_g, dq_g, dk_g, seq, *, tm=512):
    n = z.shape[0]
    tm = min(tm, seq)
    nt = seq // tm
    zspec = lambda w, col: pl.BlockSpec((tm, w), lambda i: (i, col // w))
    const = lambda shape: pl.BlockSpec(shape, lambda i: (0, 0))
    o512 = pl.BlockSpec((tm, 512), lambda i: (i, 0))
    o256 = pl.BlockSpec((tm, 256), lambda i: (i, 0))
    s512 = jax.ShapeDtypeStruct((n, 512), BF16)
    s256 = jax.ShapeDtypeStruct((n, 256), BF16)
    tspec = lambda rows: pl.BlockSpec((tm // KSUB, rows, KSUB), lambda i: (i, 0, 0))
    tshape = lambda rows: jax.ShapeDtypeStruct((n // KSUB, rows, KSUB), BF16)
    sv_rows, dv_rows = S_HEADS * S_VROWS, DA_HEADS * DA_VROWS
    return pl.pallas_call(
        _prep_kernel,
        grid=(n // tm,),
        in_specs=[
            zspec(512, COL_SQ), zspec(512, COL_DQ), zspec(512, COL_DK), zspec(512, COL_DV),
            zspec(256, COL_CKV), zspec(256, COL_IQ), zspec(256, COL_IK4),
            pl.BlockSpec((tm, 256), lambda i: (i % nt, 0)),
            pl.BlockSpec((tm, 256), lambda i: (i % nt, 0)),
            const((512, 512)),
            const((1, S_KV_RANK)), const((S_KV_RANK, 2 * S_WIDTH)),
            const((1, 512)), const((1, 512)), const((1, 512)), const((1, 512)),
        ],
        out_specs=[o512, o512, tspec(sv_rows), o256, o256, o512, o512, tspec(dv_rows)],
        out_shape=[s512, s512, tshape(sv_rows), s256, s256, s512, s512, tshape(dv_rows)],
        compiler_params=_cparams(("parallel",)),
        name="attn_prep",
    )(z, z, z, z, z, z, z, rope_c, rope_s, grp, kv_g, w_kv, sq_g, sk_g, dq_g, dk_g)


def _mask_heads(q_ref, qm_ref, n_heads):
    lane_grp = lax.broadcasted_iota(I32, (1, HEAD_GROUP), 1) // ROPE_DIM
    for h in range(n_heads):
        g = h // 4
        qg = q_ref[:, g * HEAD_GROUP:(g + 1) * HEAD_GROUP]
        qm_ref[h] = jnp.where(lane_grp == (h % 4), qg, jnp.zeros_like(qg))


def _softmax_group(logits, values, rows, acc_ref, m_ref):
    nh = len(logits)
    m_old = m_ref[...]
    m_new = jnp.maximum(m_old, jnp.concatenate([jnp.max(s, axis=0, keepdims=True) for s in logits], axis=0))
    alpha = jnp.exp2(m_old - m_new)
    ps = [jnp.exp2((s - m_new[j:j + 1, :]).astype(BF16)) for j, s in enumerate(logits)]
    m_ref[...] = m_new
    acc_old = acc_ref[...]
    acc_ref[...] = jnp.concatenate(
        [alpha[j:j + 1, :] * acc_old[j * rows:(j + 1) * rows, :] + _dot(values[j], ps[j])
         for j in range(nh)], axis=0)


def _sortable(x):
    bits = pltpu.bitcast(x, I32)
    return bits ^ ((bits >> 31) & 0x7FFFFFFF)


def _tree_sum(parts):
    while len(parts) > 1:
        parts = [parts[j] + parts[j + 1] for j in range(0, len(parts), 2)]
    return parts[0]


def _bit_planes(words):
    a = list(words)
    j, m = 16, 0x0000FFFF
    while j:
        k = 0
        while k < 32:
            t = (a[k] ^ lax.shift_right_logical(a[k + j], jnp.int32(j))) & m
            a[k] = a[k] ^ t
            a[k + j] = a[k + j] ^ (t << j)
            k = (k + j + 1) & ~j
        j >>= 1
        m = (m ^ (m << j)) & 0xFFFFFFFF
    return a


def _dsa_kernel(sq_ref, iq_ref, misc_ref, sk_ref, svt_ref, ik_ref, y_ref,
                sc_ref, bp_ref, qm_ref, iqm_ref, acc_ref, m_ref, sa_ref, sb_ref,
                *, tq, topk, idx_bits):
    i = pl.program_id(1)

    @pl.when((pl.program_id(0) == 0) & (i == 0))
    def _():
        bp_ref[...] = jnp.zeros_like(bp_ref)

    nks = (i + 1) * (tq // KSUB)
    q_idx = i * tq + lax.broadcasted_iota(I32, (1, tq), 1)
    k_sub = lax.broadcasted_iota(I32, (KSUB, 1), 0)

    _mask_heads(iq_ref, iqm_ref, IDX_HEADS)
    w_t = (misc_ref[...] * (IDX_WIDTH ** -0.5)).T
    w_rows = [w_t[MISC_IW + h:MISC_IW + h + 1, :] for h in range(IDX_HEADS)]

    k_pair = lax.broadcasted_iota(I32, (2 * KSUB, 1), 0)

    def score_body(j, carry):
        ikb = ik_ref[pl.ds(pl.multiple_of(j * (2 * KSUB), 2 * KSUB), 2 * KSUB), :]
        s = jnp.zeros((2 * KSUB, tq), F32)
        for h in range(IDX_HEADS):
            s = s + w_rows[h] * jnp.maximum(_dot_nt(ikb, iqm_ref[h]), 0.0)
        key = jnp.where(j * (2 * KSUB) + k_pair <= q_idx, _sortable(s), INT_MIN)
        sc_ref[pl.ds(2 * j, 2)] = key.reshape(2, KSUB, tq)
        ob = key ^ INT_MIN
        bp_ref[j] = jnp.concatenate(_bit_planes([ob[v * 8:(v + 1) * 8, :] for v in range(32)]), axis=0)
        return carry

    lax.fori_loop(0, nks // 2, score_body, 0)

    nblk = bp_ref.shape[0]
    live_rows = lax.broadcasted_iota(I32, (nblk * 8, 1), 0) < (nks // 2) * 8

    def bit_body(t, carry):
        alive, need, thr_ob = carry
        plane = bp_ref[:, pl.ds(pl.multiple_of(t * 8, 8), 8), :].reshape(nblk * 8, tq)
        ones = alive & plane
        cnt = jnp.sum(lax.population_count(ones), axis=0, keepdims=True)
        take = cnt >= need
        alive = jnp.where(take, ones, alive ^ ones)
        need = jnp.where(take, need, need - cnt)
        thr_ob = thr_ob | jnp.where(take, jnp.left_shift(jnp.int32(1), 31 - t), 0)
        return alive, need, thr_ob

    alive, need, thr_ob = lax.fori_loop(
        0, 32, bit_body,
        (jnp.broadcast_to(jnp.where(live_rows, -1, 0), (nblk * 8, tq)),
         jnp.full((1, tq), topk, I32), jnp.zeros((1, tq), I32)))
    thr = thr_ob ^ INT_MIN
    n_ties = jnp.sum(lax.population_count(alive), axis=0, keepdims=True)

    def count32(pred):
        def body(ks, acc):
            c = jnp.where(pred(sc_ref[ks], ks * KSUB + k_sub), 1, 0)
            return acc + _tree_sum([c[r * 8:(r + 1) * 8] for r in range(KSUB // 8)])
        acc = lax.fori_loop(0, nks, body, jnp.zeros((8, tq), I32))
        return jnp.sum(acc, axis=0, keepdims=True)

    m_ref[0:1, :] = jnp.full((1, tq), 2.0**30, F32)

    @pl.when(jnp.max(n_ties - need) > 0)
    def _():
        def idx_body(t, cut):
            bit = jnp.left_shift(jnp.int32(1), idx_bits - 1 - t)
            trial = cut | bit
            cnt = count32(lambda blk, kidx: (blk == thr) & (kidx < trial))
            return jnp.where(cnt < need, trial, cut)
        cut = lax.fori_loop(0, idx_bits, idx_body, jnp.zeros((1, tq), I32))
        m_ref[0:1, :] = cut.astype(F32)

    cut = m_ref[0:1, :].astype(I32)

    thr_sel = jnp.maximum(thr, INT_MIN + 1)

    def bias_body(j, carry):
        pair = pl.ds(2 * j, 2)
        key = sc_ref[pair].reshape(2 * KSUB, tq)
        kidx = j * (2 * KSUB) + k_pair
        sel = key >= thr_sel + jnp.where(kidx <= cut, 0, 1)
        sc_ref[pair] = pltpu.bitcast(jnp.where(sel, 0.0, NEG).astype(F32), I32).reshape(2, KSUB, tq)
        return carry

    lax.fori_loop(0, nks // 2, bias_body, 0)

    _mask_heads(sq_ref, qm_ref, S_HEADS)
    acc_ref[...] = jnp.zeros_like(acc_ref)
    m_ref[...] = jnp.full_like(m_ref, NEG)

    def qk_stage(ks, dst_ref):
        off = pl.multiple_of(ks * KSUB, KSUB)
        bias = pltpu.bitcast(sc_ref[ks], F32)
        for g in range(S_HEADS // 4):
            kg = sk_ref[pl.ds(off, KSUB), g * HEAD_GROUP:(g + 1) * HEAD_GROUP]
            for h in range(4 * g, 4 * g + 4):
                dst_ref[h] = _dot_nt(kg, qm_ref[h]) + bias

    def sm_stage(ks, src_ref):
        _softmax_group([src_ref[h] for h in range(S_HEADS)],
                       [svt_ref[ks, h * S_VROWS:(h + 1) * S_VROWS, :] for h in range(S_HEADS)],
                       S_VROWS, acc_ref, m_ref)

    qk_stage(0, sa_ref)

    def attn_body(j, carry):
        qk_stage(2 * j + 1, sb_ref)
        sm_stage(2 * j, sa_ref)
        qk_stage(jnp.minimum(2 * j + 2, nks - 1), sa_ref)
        sm_stage(2 * j + 1, sb_ref)
        return carry

    lax.fori_loop(0, nks // 2, attn_body, 0)

    outs = []
    for h in range(S_HEADS):
        r0 = h * S_VROWS
        outs.append(acc_ref[r0:r0 + ROPE_DIM, :] / acc_ref[r0 + ROPE_DIM:r0 + ROPE_DIM + 1, :])
    y_ref[...] = jnp.concatenate(outs, axis=0).T


def _dsa(z, sq, sk, svt, iq, ik4, batch, seq, *, tq=256):
    n = z.shape[0]
    tq = min(tq, seq)
    nq = seq // tq
    nsub = seq // KSUB
    topk = min(IDX_TOPK_MAX, seq // 4)
    kern = functools.partial(_dsa_kernel, tq=tq, topk=topk, idx_bits=seq.bit_length())
    return pl.pallas_call(
        kern,
        grid=(batch, nq),
        in_specs=[
            pl.BlockSpec((tq, S_WIDTH), lambda b, i: (b * nq + i, 0)),
            pl.BlockSpec((tq, IDX_WIDTH), lambda b, i: (b * nq + i, 0)),
            pl.BlockSpec((tq, LANE), lambda b, i: (b * nq + i, COL_MISC // LANE)),
            pl.BlockSpec((seq, S_WIDTH), lambda b, i: (b, 0)),
            pl.BlockSpec((nsub, S_HEADS * S_VROWS, KSUB), lambda b, i: (b, 0, 0)),
            pl.BlockSpec((seq, IDX_WIDTH), lambda b, i: (b, 0)),
        ],
        out_specs=pl.BlockSpec((tq, S_WIDTH), lambda b, i: (b * nq + i, 0)),
        out_shape=jax.ShapeDtypeStruct((n, S_WIDTH), F32),
        scratch_shapes=[
            pltpu.VMEM((nsub, KSUB, tq), I32),
            pltpu.VMEM((nsub // 2, 2 * KSUB, tq), I32),
            pltpu.VMEM((S_HEADS, tq, HEAD_GROUP), BF16),
            pltpu.VMEM((IDX_HEADS, tq, HEAD_GROUP), BF16),
            pltpu.VMEM((S_HEADS * S_VROWS, tq), F32),
            pltpu.VMEM((S_HEADS, tq), F32),
            pltpu.VMEM((S_HEADS, KSUB, tq), F32),
            pltpu.VMEM((S_HEADS, KSUB, tq), F32),
        ],
        compiler_params=_cparams(("arbitrary", "arbitrary")),
        name="dsa_attn",
    )(sq, iq, z, sk, svt, ik4)


def _diff_kernel(q_ref, k_ref, vt_ref, lam_ref, og_ref, y_ref, qm_ref, acc_ref, m_ref,
                 sa_ref, sb_ref, *, tq, lam_init):
    i = pl.program_id(1)
    nfull = i * (tq // KSUB)
    q_idx = i * tq + lax.broadcasted_iota(I32, (1, tq), 1)
    k_sub = lax.broadcasted_iota(I32, (KSUB, 1), 0)

    _mask_heads(q_ref, qm_ref, DA_MAPS)
    acc_ref[...] = jnp.zeros_like(acc_ref)
    m_ref[...] = jnp.full_like(m_ref, NEG)

    def qk_stage(ks, dst_ref):
        off = pl.multiple_of(ks * KSUB, KSUB)
        for g in range(DA_MAPS // 4):
            kg = k_ref[pl.ds(off, KSUB), g * HEAD_GROUP:(g + 1) * HEAD_GROUP]
            for m in range(4 * g, 4 * g + 4):
                dst_ref[m] = _dot_nt(kg, qm_ref[m])

    def sm_stage(ks, src_ref, masked):
        logits = [src_ref[m] for m in range(DA_MAPS)]
        if masked:
            bias = jnp.where(ks * KSUB + k_sub <= q_idx, 0.0, NEG).astype(F32)
            logits = [s + bias for s in logits]
        _softmax_group(logits,
                       [vt_ref[ks, (m // 2) * DA_VROWS:(m // 2 + 1) * DA_VROWS, :] for m in range(DA_MAPS)],
                       DA_VROWS, acc_ref, m_ref)

    assert tq == 2 * KSUB
    qk_stage(0, sa_ref)

    def full_body(j, carry):
        qk_stage(2 * j + 1, sb_ref)
        sm_stage(2 * j, sa_ref, False)
        qk_stage(2 * j + 2, sa_ref)
        sm_stage(2 * j + 1, sb_ref, False)
        return carry

    lax.fori_loop(0, nfull // 2, full_body, 0)
    qk_stage(nfull + 1, sb_ref)
    sm_stage(nfull, sa_ref, True)
    sm_stage(nfull + 1, sb_ref, True)

    lam = lam_ref[...]
    p01 = jnp.sum(lam[0:1] * lam[1:2], axis=-1, keepdims=True)
    p23 = jnp.sum(lam[2:3] * lam[3:4], axis=-1, keepdims=True)
    lam_val = jnp.exp(p01) - jnp.exp(p23) + lam_init
    def normalised(m):
        r0 = m * DA_VROWS
        return acc_ref[r0:r0 + DA_V_DIM, :] / acc_ref[r0 + DA_V_DIM:r0 + DA_V_DIM + 1, :]

    outs = []
    for hd in range(DA_HEADS):
        o = normalised(2 * hd) - lam_val * normalised(2 * hd + 1)
        ms = jnp.mean(o * o, axis=0, keepdims=True)
        outs.append(o * lax.rsqrt(ms + EPS) * og_ref[...] * (1.0 - lam_init))
    y_ref[...] = jnp.concatenate(outs, axis=0).T


def _diff_attn(dq, dk, dvt, lam, out_g_col, batch, seq, lam_init, *, tq=256):
    n = dq.shape[0]
    tq = min(tq, seq)
    nq = seq // tq
    nsub = seq // KSUB
    kern = functools.partial(_diff_kernel, tq=tq, lam_init=lam_init)
    return pl.pallas_call(
        kern,
        grid=(batch, nq),
        in_specs=[
            pl.BlockSpec((tq, 512), lambda b, i: (b * nq + i, 0)),
            pl.BlockSpec((seq, 512), lambda b, i: (b, 0)),
            pl.BlockSpec((nsub, DA_HEADS * DA_VROWS, KSUB), lambda b, i: (b, 0, 0)),
            pl.BlockSpec((4, ROPE_DIM), lambda b, i: (0, 0)),
            pl.BlockSpec((DA_V_DIM, 1), lambda b, i: (0, 0)),
        ],
        out_specs=pl.BlockSpec((tq, DA_WIDTH), lambda b, i: (b * nq + i, 0)),
        out_shape=jax.ShapeDtypeStruct((n, DA_WIDTH), F32),
        scratch_shapes=[
            pltpu.VMEM((DA_MAPS, tq, HEAD_GROUP), BF16),
            pltpu.VMEM((DA_MAPS * DA_VROWS, tq), F32),
            pltpu.VMEM((DA_MAPS, tq), F32),
            pltpu.VMEM((DA_MAPS, KSUB, tq), F32),
            pltpu.VMEM((DA_MAPS, KSUB, tq), F32),
        ],
        compiler_params=_cparams(("parallel", "arbitrary")),
        name="diff_attn",
    )(dq, dk, dvt, lam, out_g_col)


def _merge_kernel(x_ref, ya_ref, yb_ref, yc_ref, gp_ref, bg_ref, wb_ref, wo_ref, ng_ref, xo_ref, xn_ref):
    merged = None
    for br, y_ref in enumerate((ya_ref, yb_ref, yc_ref)):
        sl = slice(br * D_MODEL, (br + 1) * D_MODEL)
        gate = _sigmoid(gp_ref[:, sl] + bg_ref[:, sl])
        term = gate * _dot(y_ref[...].astype(BF16), wb_ref[br])
        merged = term if merged is None else merged + term
    xo = x_ref[...] + _dot(merged.astype(BF16), wo_ref[...])
    xo_ref[...] = xo
    ms = jnp.mean(xo * xo, axis=-1, keepdims=True)
    xn_ref[...] = (xo * lax.rsqrt(ms + EPS) * ng_ref[...]).astype(BF16)


def _merge(x2, ya, yb, yc, z, b_gate, w_branch, w_out, norm_g, *, tm=512):
    n = x2.shape[0]
    tm = min(tm, n)
    row = lambda w: pl.BlockSpec((tm, w), lambda i: (i, 0))
    return pl.pallas_call(
        _merge_kernel,
        grid=(n // tm,),
        in_specs=[
            row(D_MODEL), row(512), row(512), row(512),
            pl.BlockSpec((tm, N_BRANCH * D_MODEL), lambda i: (i, COL_GATE)),
            pl.BlockSpec((1, N_BRANCH * D_MODEL), lambda i: (0, 0)),
            pl.BlockSpec((N_BRANCH, 512, D_MODEL), lambda i: (0, 0, 0)),
            pl.BlockSpec((D_MODEL, D_MODEL), lambda i: (0, 0)),
            pl.BlockSpec((1, D_MODEL), lambda i: (0, 0)),
        ],
        out_specs=[row(D_MODEL), row(D_MODEL)],
        out_shape=[jax.ShapeDtypeStruct((n, D_MODEL), F32), jax.ShapeDtypeStruct((n, D_MODEL), BF16)],
        compiler_params=_cparams(("parallel",)),
        name="merge_out",
    )(x2, ya, yb, yc, z, b_gate, w_branch, w_out, norm_g)


def _ffn_kernel(xn_ref, x_ref, wg_ref, wu_ref, wd_ref, o_ref, acc_ref):
    j = pl.program_id(1)
    xn = xn_ref[...]
    g = _dot(xn, wg_ref[...])
    u = _dot(xn, wu_ref[...])
    part = _dot((g * _sigmoid(g) * u).astype(BF16), wd_ref[...])

    @pl.when(j == 0)
    def _():
        acc_ref[...] = part

    @pl.when(j > 0)
    def _():
        acc_ref[...] += part

    @pl.when(j == pl.num_programs(1) - 1)
    def _():
        o_ref[...] = x_ref[...] + acc_ref[...]


def _ffn(xn, x2, wg, wu, wd, *, tm=512, fc=1408):
    n = x2.shape[0]
    tm = min(tm, n)
    return pl.pallas_call(
        _ffn_kernel,
        grid=(n // tm, FF_DIM // fc),
        in_specs=[
            pl.BlockSpec((tm, D_MODEL), lambda i, j: (i, 0)),
            pl.BlockSpec((tm, D_MODEL), lambda i, j: (i, 0)),
            pl.BlockSpec((D_MODEL, fc), lambda i, j: (0, j)),
            pl.BlockSpec((D_MODEL, fc), lambda i, j: (0, j)),
            pl.BlockSpec((fc, D_MODEL), lambda i, j: (j, 0)),
        ],
        out_specs=pl.BlockSpec((tm, D_MODEL), lambda i, j: (i, 0)),
        out_shape=jax.ShapeDtypeStruct((n, D_MODEL), F32),
        scratch_shapes=[pltpu.VMEM((tm, D_MODEL), F32)],
        compiler_params=_cparams(("parallel", "arbitrary")),
        name="ffn",
    )(xn, x2, wg, wu, wd)


def _rope_tables(seq):
    inv = 1.0 / jnp.power(ROPE_THETA, jnp.arange(0, ROPE_DIM, 2, dtype=F32) / ROPE_DIM)
    ang = jnp.arange(seq, dtype=F32)[:, None] * inv[None, :]
    cos, sin = jnp.cos(ang), jnp.sin(ang)
    c64 = jnp.concatenate([cos, cos], axis=1)
    s64 = jnp.concatenate([-sin, sin], axis=1)
    return jnp.tile(c64, (1, 4)), jnp.tile(s64, (1, 4))


def _layer(x2, li, batch, seq, consts, norm_mix_g, w_in_l, b_gate, conv_w, conv_b, gate_b, m_norm_g,
           kv_norm_g, w_kv_up, sq_g, sk_g, dq_g, dk_g, lam, d_out_g, w_branch, w_out,
           norm_ffn_g, w_gate_up, w_down):
    rope_c, rope_s, grp, tri = consts
    lam_init = 0.8 - 0.6 * math.exp(-0.3 * li)
    tile8 = lambda g: jnp.tile(g, 8)[None, :]

    z = _inproj(x2, norm_mix_g[None, :], w_in_l)

    gate_row = jnp.zeros((1, LANE), F32).at[0, MISC_IF:MISC_IF + 2 * M_HEADS].set(gate_b)
    ya = _mlstm(z, conv_w, conv_b[None, :], gate_row, m_norm_g.reshape(1, M_WIDTH), tri, batch, seq)

    sq, sk, svt, iq, ik4, dq, dk, dvt = _prep(
        z, rope_c, rope_s, grp, kv_norm_g[None, :], w_kv_up.astype(BF16),
        tile8(sq_g), tile8(sk_g), tile8(dq_g), tile8(dk_g), seq)
    yb = _dsa(z, sq, sk, svt, iq, ik4, batch, seq)
    yc = _diff_attn(dq, dk, dvt, lam, d_out_g[:, None], batch, seq, lam_init)

    xo, xn = _merge(x2, ya, yb, yc, z, b_gate[None, :], w_branch.astype(BF16), w_out.astype(BF16),
                    norm_ffn_g[None, :])
    return _ffn(xn, xo, w_gate_up[:, :FF_DIM].astype(BF16), w_gate_up[:, FF_DIM:].astype(BF16),
                w_down.astype(BF16))


def kernel(x, norm_mix_g, w_in, b_gate, mlstm_conv_w, mlstm_conv_b, mlstm_gate_b, mlstm_norm_g,
           dsa_kv_norm_g, dsa_w_kv_up, dsa_q_norm_g, dsa_k_norm_g, diff_q_norm_g, diff_k_norm_g,
           diff_lambda, diff_out_norm_g, w_branch, w_out, norm_ffn_g, w_gate_up, w_down):
    batch, seq, d = x.shape
    depth = w_in.shape[0]
    assert w_in.shape[2] == IN_COLS and d == D_MODEL
    rope_c, rope_s = _rope_tables(seq)
    gi = jnp.arange(512) // ROPE_DIM
    grp = (gi[:, None] == gi[None, :]).astype(BF16)
    ti = jnp.arange(M_CHUNK)
    tri = (ti[:, None] >= ti[None, :]).astype(BF16)
    consts = (rope_c, rope_s, grp, tri)
    w_in_l = _wlayout(w_in)
    x2 = x.reshape(batch * seq, d)
    for li in range(depth):
        x2 = _layer(x2, li, batch, seq, consts, norm_mix_g[li], w_in_l[li], b_gate[li], mlstm_conv_w[li],
                    mlstm_conv_b[li], mlstm_gate_b[li], mlstm_norm_g[li], dsa_kv_norm_g[li],
                    dsa_w_kv_up[li], dsa_q_norm_g[li], dsa_k_norm_g[li], diff_q_norm_g[li],
                    diff_k_norm_g[li], diff_lambda[li], diff_out_norm_g[li], w_branch[li], w_out[li],
                    norm_ffn_g[li], w_gate_up[li], w_down[li])
    return x2.reshape(batch, seq, d)
```

```python
import functools
import math

import jax
import jax.numpy as jnp
from jax import lax
from jax.experimental import pallas as pl
from jax.experimental.pallas import tpu as pltpu

F32 = jnp.float32
BF16 = jnp.bfloat16
I32 = jnp.int32
I16 = jnp.int16

D_MODEL = 1024
EPS = 1e-6
ROPE_DIM = 64
ROPE_HALF = ROPE_DIM // 2
ROPE_THETA = 10000.0

M_HEADS = 4
M_HEAD_DIM = 128
M_WIDTH = M_HEADS * M_HEAD_DIM
M_CONV = 4
M_CHUNK = 128
M_INIT = -1e30

S_HEADS = 8
S_WIDTH = S_HEADS * ROPE_DIM
S_KV_RANK = 256
IDX_HEADS = 4
IDX_WIDTH = IDX_HEADS * ROPE_DIM
IDX_TOPK_MAX = 256

DA_HEADS = 4
DA_MAPS = 2 * DA_HEADS
DA_V_DIM = 2 * ROPE_DIM
DA_WIDTH = DA_HEADS * DA_V_DIM

N_BRANCH = 3
FF_DIM = 2816

NEG = -1e30
LOG2E = math.log2(math.e)
INT_MIN = -(2**31)
HALF16 = 2**15
LANE = 128
HEAD_GROUP = 256
KSUB = 128
ONES_ROWS = 16
S_VROWS = ROPE_DIM + ONES_ROWS
DA_VROWS = DA_V_DIM + ONES_ROWS

W_IN_SIZES = (2 * M_WIDTH, M_WIDTH, M_WIDTH, 2 * M_HEADS, S_WIDTH, S_KV_RANK, IDX_WIDTH, ROPE_DIM,
              IDX_HEADS, 2 * DA_HEADS * ROPE_DIM, 2 * DA_HEADS * ROPE_DIM, DA_WIDTH, N_BRANCH * D_MODEL)
IN_COLS = sum(W_IN_SIZES)

COL_GATE = 0
COL_MQK = 3072
COL_MV = 4096
COL_MO = 4608
COL_SQ = 5120
COL_DQ = 5632
COL_DK = 6144
COL_DV = 6656
COL_CKV = 7168
COL_IQ = 7424
COL_IK4 = 7680
COL_MISC = 7936
MISC_IF = 0
MISC_IW = 8
Z_COLS = 8192

VMEM_LIMIT = 56 * 1024 * 1024


def _cparams(sem, flags=None):
    return pltpu.CompilerParams(dimension_semantics=sem, vmem_limit_bytes=VMEM_LIMIT, flags=flags)


def _sigmoid(x):
    return 1.0 / (1.0 + jnp.exp(-x))


def _dot(a, b):
    return jnp.dot(a, b, preferred_element_type=F32)


def _dot_nt(a, b):
    return lax.dot_general(a, b, (((1,), (1,)), ((), ())), preferred_element_type=F32)


def _dot_tn(a, b):
    return lax.dot_general(a, b, (((0,), (0,)), ((), ())), preferred_element_type=F32)


def _split3(x):
    hi = x.astype(BF16)
    r1 = x - hi.astype(F32)
    mid = r1.astype(BF16)
    lo = (r1 - mid.astype(F32)).astype(BF16)
    return hi, mid, lo


def _wlayout_kernel(w_ref, o_ref):
    w = w_ref[...]
    offs = [0]
    for s in W_IN_SIZES:
        offs.append(offs[-1] + s)
    seg = [w[:, offs[k]:offs[k + 1]] for k in range(len(W_IN_SIZES))]
    m_qk, m_v, m_o, m_if, s_q, s_ckv, i_q, i_k, i_w, d_q, d_k, d_v, g_pre = seg
    rows = w.shape[0]
    misc = jnp.concatenate([m_if, i_w, jnp.zeros((rows, LANE - 2 * M_HEADS - IDX_HEADS), F32)], axis=1)
    ik4 = jnp.concatenate([i_k] * IDX_HEADS, axis=1)
    tail = jnp.zeros((rows, Z_COLS - COL_MISC - LANE), F32)
    for col, val in ((COL_GATE, g_pre), (COL_MQK, m_qk), (COL_MV, m_v), (COL_MO, m_o), (COL_SQ, s_q),
                     (COL_DQ, d_q), (COL_DK, d_k), (COL_DV, d_v), (COL_CKV, s_ckv), (COL_IQ, i_q),
                     (COL_IK4, ik4), (COL_MISC, misc), (COL_MISC + LANE, tail)):
        o_ref[:, col:col + val.shape[1]] = val.astype(BF16)


def _wlayout(w_in, *, tr=256):
    depth, rows, cols = w_in.shape
    return pl.pallas_call(
        _wlayout_kernel,
        grid=(depth, rows // tr),
        in_specs=[pl.BlockSpec((None, tr, cols), lambda l, i: (l, i, 0))],
        out_specs=pl.BlockSpec((None, tr, Z_COLS), lambda l, i: (l, i, 0)),
        out_shape=jax.ShapeDtypeStruct((depth, rows, Z_COLS), BF16),
        compiler_params=_cparams(("parallel", "parallel")),
        name="w_layout",
    )(w_in)


def _inproj_kernel(x_ref, g_ref, w_ref, z_ref, xn_ref):
    @pl.when(pl.program_id(1) == 0)
    def _():
        x = x_ref[...]
        ms = jnp.mean(x * x, axis=-1, keepdims=True)
        xn_ref[...] = (x * lax.rsqrt(ms + EPS) * g_ref[...]).astype(BF16)

    z_ref[...] = _dot(xn_ref[...], w_ref[...])


def _inproj(x2, g, w, *, tm=1024, tn=1024):
    n = x2.shape[0]
    tm = min(tm, n)
    return pl.pallas_call(
        _inproj_kernel,
        grid=(n // tm, Z_COLS // tn),
        in_specs=[
            pl.BlockSpec((tm, D_MODEL), lambda i, j: (i, 0)),
            pl.BlockSpec((1, D_MODEL), lambda i, j: (0, 0)),
            pl.BlockSpec((D_MODEL, tn), lambda i, j: (0, j)),
        ],
        out_specs=pl.BlockSpec((tm, tn), lambda i, j: (i, j)),
        out_shape=jax.ShapeDtypeStruct((n, Z_COLS), F32),
        scratch_shapes=[pltpu.VMEM((tm, D_MODEL), BF16)],
        compiler_params=_cparams(("parallel", "arbitrary")),
        name="inproj",
    )(x2, g, w)


def _mlstm_kernel(qk_ref, v_ref, o_ref, misc_ref, cw_ref, cb_ref, gb_ref, ng_ref, tri_ref, y_ref,
                  tail_ref, qkc_ref, c_ref, n_ref, m_ref):
    L = M_CHUNK
    dh = M_HEAD_DIM

    @pl.when(pl.program_id(1) == 0)
    def _():
        tail_ref[...] = jnp.zeros_like(tail_ref)
        c_ref[...] = jnp.zeros_like(c_ref)
        n_ref[...] = jnp.zeros_like(n_ref)
        m_ref[...] = jnp.full_like(m_ref, M_INIT)

    x = qk_ref[...]
    tail = tail_ref[...]
    row8 = lax.broadcasted_iota(I32, (8, 1), 0)
    cw = cw_ref[...]
    cb = cb_ref[...]
    acc = x * cw[M_CONV - 1:M_CONV] + cb
    acc_head = x[0:8] * cw[M_CONV - 1:M_CONV] + cb
    for j in range(1, M_CONV):
        wj = cw[M_CONV - 1 - j:M_CONV - j]
        xr = pltpu.roll(x, j, axis=0)
        tr = pltpu.roll(tail, j, axis=0)
        acc = acc + xr * wj
        acc_head = acc_head + jnp.where(row8 < j, tr, xr[0:8]) * wj
    qkc_ref[...] = acc * _sigmoid(acc)
    qkc_ref[0:8, :] = acc_head * _sigmoid(acc_head)
    tail_ref[...] = x[L - 8:L]

    gates = misc_ref[...] + gb_ref[...]
    lf = jnp.minimum(gates, 0.0) - jnp.log(1.0 + jnp.exp(-jnp.abs(gates)))
    tri = tri_ref[...]
    hi, mid, lo = _split3(lf)
    bcum = _dot(tri, hi) + _dot(tri, mid) + _dot(tri, lo)
    bcum_t = bcum.T
    gates_t = gates.T

    t_idx = lax.broadcasted_iota(I32, (L, L), 0)
    s_idx = lax.broadcasted_iota(I32, (L, L), 1)
    causal = t_idx >= s_idx

    ys, cs, ns, ms_new = [], [], [], []
    for h in range(M_HEADS):
        sl = slice(h * dh, (h + 1) * dh)
        q = qkc_ref[:, h * dh:(h + 1) * dh]
        k = qkc_ref[:, M_WIDTH + h * dh:M_WIDTH + (h + 1) * dh] * (dh ** -0.5)
        v = v_ref[:, sl]
        qb, kb, vb = q.astype(BF16), k.astype(BF16), v.astype(BF16)

        b_col = bcum[:, M_HEADS + h:M_HEADS + h + 1]
        i_col = gates[:, h:h + 1]
        b_row = bcum_t[M_HEADS + h:M_HEADS + h + 1, :]
        i_row = gates_t[h:h + 1, :]
        g_tot = bcum[L - 1:L, M_HEADS + h:M_HEADS + h + 1]

        c_prev = c_ref[sl, :]
        n_prev = n_ref[h:h + 1, :]
        m_prev = m_ref[h:h + 1, 0:1]

        dmat = jnp.where(causal, b_col - b_row + i_row, -jnp.inf)
        inter = b_col + m_prev
        m_t = jnp.maximum(inter, jnp.max(dmat, axis=-1, keepdims=True))
        sw = jnp.exp(dmat - m_t) * _dot_nt(qb, kb)
        s_inter = jnp.exp(inter - m_t)
        num = _dot(sw.astype(BF16), vb) + s_inter * _dot_nt(qb, c_prev.astype(BF16))
        den = jnp.sum(sw, axis=-1, keepdims=True) + s_inter * jnp.sum(q * n_prev, axis=-1, keepdims=True)
        hh = num / jnp.maximum(jnp.abs(den), jnp.exp(-m_t))
        ms = jnp.mean(hh * hh, axis=-1, keepdims=True)
        hn = hh * lax.rsqrt(ms + EPS) * ng_ref[:, sl]
        ys.append(_sigmoid(o_ref[:, sl]) * hn)

        a_col = g_tot - b_col + i_col
        m_loc = jnp.max(a_col, axis=0, keepdims=True)
        w_loc = jnp.exp(a_col - m_loc)
        c_loc = _dot_tn((v * w_loc).astype(BF16), kb)
        n_loc = jnp.sum(k * w_loc, axis=0, keepdims=True)
        m_new = jnp.maximum(g_tot + m_prev, m_loc)
        s_old = jnp.exp(g_tot + m_prev - m_new)
        s_loc = jnp.exp(m_loc - m_new)
        cs.append(s_old * c_prev + s_loc * c_loc)
        ns.append(s_old * n_prev + s_loc * n_loc)
        ms_new.append(jnp.broadcast_to(m_new, (1, LANE)))

    y_ref[...] = jnp.concatenate(ys, axis=1)
    c_ref[...] = jnp.concatenate(cs, axis=0)
    n_ref[...] = jnp.concatenate(ns, axis=0)
    m_ref[...] = jnp.concatenate(ms_new, axis=0)


def _mlstm(z, conv_w, conv_b, gate_b_row, norm_g_row, tri, batch, seq):
    n = z.shape[0]
    nc = seq // M_CHUNK
    L = M_CHUNK
    row = lambda b, c: b * nc + c
    return pl.pallas_call(
        _mlstm_kernel,
        grid=(batch, nc),
        in_specs=[
            pl.BlockSpec((L, 2 * M_WIDTH), lambda b, c: (row(b, c), COL_MQK // (2 * M_WIDTH))),
            pl.BlockSpec((L, M_WIDTH), lambda b, c: (row(b, c), COL_MV // M_WIDTH)),
            pl.BlockSpec((L, M_WIDTH), lambda b, c: (row(b, c), COL_MO // M_WIDTH)),
            pl.BlockSpec((L, LANE), lambda b, c: (row(b, c), COL_MISC // LANE)),
            pl.BlockSpec((M_CONV, 2 * M_WIDTH), lambda b, c: (0, 0)),
            pl.BlockSpec((1, 2 * M_WIDTH), lambda b, c: (0, 0)),
            pl.BlockSpec((1, LANE), lambda b, c: (0, 0)),
            pl.BlockSpec((1, M_WIDTH), lambda b, c: (0, 0)),
            pl.BlockSpec((L, L), lambda b, c: (0, 0)),
        ],
        out_specs=pl.BlockSpec((L, M_WIDTH), lambda b, c: (row(b, c), 0)),
        out_shape=jax.ShapeDtypeStruct((n, M_WIDTH), F32),
        scratch_shapes=[
            pltpu.VMEM((8, 2 * M_WIDTH), F32),
            pltpu.VMEM((L, 2 * M_WIDTH), F32),
            pltpu.VMEM((M_HEADS * M_HEAD_DIM, M_HEAD_DIM), F32),
            pltpu.VMEM((M_HEADS, M_HEAD_DIM), F32),
            pltpu.VMEM((M_HEADS, LANE), F32),
        ],
        compiler_params=_cparams(("parallel", "arbitrary")),
        name="mlstm",
    )(z, z, z, z, conv_w, conv_b, gate_b_row, norm_g_row, tri)


def _rope(x, c, s):
    w = x.shape[-1]
    lane = lax.broadcasted_iota(I32, (1, w), 1)
    fwd = pltpu.roll(x, ROPE_HALF, axis=1)
    bwd = pltpu.roll(x, w - ROPE_HALF, axis=1)
    swapped = jnp.where((lane & (ROPE_DIM - 1)) < ROPE_HALF, bwd, fwd)
    return x * c + swapped * s


def _head_rms(x, g, grp):
    sq = x * x
    hi = sq.astype(BF16)
    lo = (sq - hi.astype(F32)).astype(BF16)
    ss = _dot(hi, grp) + _dot(lo, grp)
    return x * lax.rsqrt(ss * (1.0 / ROPE_DIM) + EPS) * g


def _store_transposed(o_ref, v, dv):
    ones = jnp.ones((ONES_ROWS, KSUB), BF16)
    for j in range(o_ref.shape[0]):
        vt = v[j * KSUB:(j + 1) * KSUB, :].T.astype(BF16)
        parts = []
        for h in range(vt.shape[0] // dv):
            parts += [vt[h * dv:(h + 1) * dv, :], ones]
        o_ref[j] = jnp.concatenate(parts, axis=0)


def _prep_kernel(sq_ref, dq_ref, dk_ref, dv_ref, ckv_ref, iq_ref, ik_ref, c_ref, s_ref, grp_ref,
                 kvg_ref, wkv_ref, sqg_ref, skg_ref, dqg_ref, dkg_ref,
                 sq_o, sk_o, sv_o, iq_o, ik_o, dq_o, dk_o, dv_o):
    c256 = c_ref[...]
    s256 = s_ref[...]
    c512 = jnp.concatenate([c256, c256], axis=1)
    s512 = jnp.concatenate([s256, s256], axis=1)
    grp = grp_ref[...]
    qscale = ROPE_DIM ** -0.5 * LOG2E

    sq_o[...] = (_rope(_head_rms(sq_ref[...], sqg_ref[...], grp), c512, s512) * qscale).astype(BF16)

    ckv = ckv_ref[...]
    ms = jnp.mean(ckv * ckv, axis=-1, keepdims=True)
    ckvn = (ckv * lax.rsqrt(ms + EPS) * kvg_ref[...]).astype(BF16)
    kv = _dot(ckvn, wkv_ref[...])
    sk_o[...] = _rope(_head_rms(kv[:, :S_WIDTH], skg_ref[...], grp), c512, s512).astype(BF16)
    _store_transposed(sv_o, kv[:, S_WIDTH:], ROPE_DIM)

    iq_o[...] = _rope(iq_ref[...], c256, s256).astype(BF16)
    ik_o[...] = _rope(ik_ref[...], c256, s256).astype(BF16)

    dq_o[...] = (_rope(_head_rms(dq_ref[...], dqg_ref[...], grp), c512, s512) * qscale).astype(BF16)
    dk_o[...] = _rope(_head_rms(dk_ref[...], dkg_ref[...], grp), c512, s512).astype(BF16)
    _store_transposed(dv_o, dv_ref[...], DA_V_DIM)


def _prep(z, rope_c, rope_s, grp, kv_g, w_kv, sq_g, sk_g, dq_g, dk_g, seq, *, tm=512):
    n = z.shape[0]
    tm = min(tm, seq)
    nt = seq // tm
    zspec = lambda w, col: pl.BlockSpec((tm, w), lambda i: (i, col // w))
    const = lambda shape: pl.BlockSpec(shape, lambda i: (0, 0))
    o512 = pl.BlockSpec((tm, 512), lambda i: (i, 0))
    o256 = pl.BlockSpec((tm, 256), lambda i: (i, 0))
    s512 = jax.ShapeDtypeStruct((n, 512), BF16)
    s256 = jax.ShapeDtypeStruct((n, 256), BF16)
    tspec = lambda rows: pl.BlockSpec((tm // KSUB, rows, KSUB), lambda i: (i, 0, 0))
    tshape = lambda rows: jax.ShapeDtypeStruct((n // KSUB, rows, KSUB), BF16)
    sv_rows, dv_rows = S_HEADS * S_VROWS, DA_HEADS * DA_VROWS
    return pl.pallas_call(
        _prep_kernel,
        grid=(n // tm,),
        in_specs=[
            zspec(512, COL_SQ), zspec(512, COL_DQ), zspec(512, COL_DK), zspec(512, COL_DV),
            zspec(256, COL_CKV), zspec(256, COL_IQ), zspec(256, COL_IK4),
            pl.BlockSpec((tm, 256), lambda i: (i % nt, 0)),
            pl.BlockSpec((tm, 256), lambda i: (i % nt, 0)),
            const((512, 512)),
            const((1, S_KV_RANK)), const((S_KV_RANK, 2 * S_WIDTH)),
            const((1, 512)), const((1, 512)), const((1, 512)), const((1, 512)),
        ],
        out_specs=[o512, o512, tspec(sv_rows), o256, o256, o512, o512, tspec(dv_rows)],
        out_shape=[s512, s512, tshape(sv_rows), s256, s256, s512, s512, tshape(dv_rows)],
        compiler_params=_cparams(("parallel",)),
        name="attn_prep",
    )(z, z, z, z, z, z, z, rope_c, rope_s, grp, kv_g, w_kv, sq_g, sk_g, dq_g, dk_g)


def _mask_heads(q_ref, qm_ref, n_heads):
    lane_grp = lax.broadcasted_iota(I32, (1, HEAD_GROUP), 1) // ROPE_DIM
    for h in range(n_heads):
        g = h // 4
        qg = q_ref[:, g * HEAD_GROUP:(g + 1) * HEAD_GROUP].astype(F32)
        qm_ref[h] = jnp.where(lane_grp == (h % 4), qg, 0.0).T.astype(BF16)


def _softmax_group(logits, values, rows, acc_ref, m_ref):
    nh = len(logits)
    m_old = m_ref[...]
    m_new = jnp.maximum(m_old, jnp.concatenate([jnp.max(s, axis=0, keepdims=True) for s in logits], axis=0))
    alpha = jnp.exp2(m_old - m_new)
    ps = [jnp.exp2((s - m_new[j:j + 1, :]).astype(BF16)) for j, s in enumerate(logits)]
    m_ref[...] = m_new
    acc_old = acc_ref[...]
    acc_ref[...] = jnp.concatenate(
        [alpha[j:j + 1, :] * acc_old[j * rows:(j + 1) * rows, :] + _dot(values[j], ps[j])
         for j in range(nh)], axis=0)


def _sortable(x):
    bits = pltpu.bitcast(x, I32)
    return bits ^ ((bits >> 31) & 0x7FFFFFFF)


def _tree_sum(parts):
    while len(parts) > 1:
        parts = [parts[j] + parts[j + 1] for j in range(0, len(parts), 2)]
    return parts[0]


def _bit_planes(words):
    a = list(words)
    j, m = 16, 0x0000FFFF
    while j:
        k = 0
        while k < 32:
            t = (a[k] ^ lax.shift_right_logical(a[k + j], jnp.int32(j))) & m
            a[k] = a[k] ^ t
            a[k + j] = a[k + j] ^ (t << j)
            k = (k + j + 1) & ~j
        j >>= 1
        m = (m ^ (m << j)) & 0xFFFFFFFF
    return a


def _dsa_kernel(sq_ref, iq_ref, misc_ref, sk_ref, svt_ref, ik_ref, y_ref,
                sc_ref, bp_ref, qm_ref, iqm_ref, acc_ref, m_ref, sa_ref, sb_ref,
                *, tq, topk, idx_bits):
    i = pl.program_id(1)

    @pl.when((pl.program_id(0) == 0) & (i == 0))
    def _():
        bp_ref[...] = jnp.zeros_like(bp_ref)

    nks = (i + 1) * (tq // KSUB)
    q_idx = i * tq + lax.broadcasted_iota(I32, (1, tq), 1)
    k_sub = lax.broadcasted_iota(I32, (KSUB, 1), 0)

    _mask_heads(iq_ref, iqm_ref, IDX_HEADS)
    w_t = (misc_ref[...] * (IDX_WIDTH ** -0.5)).T
    w_rows = [w_t[MISC_IW + h:MISC_IW + h + 1, :] for h in range(IDX_HEADS)]

    k_pair = lax.broadcasted_iota(I32, (2 * KSUB, 1), 0)

    def score_body(j, carry):
        ikb = ik_ref[pl.ds(pl.multiple_of(j * (2 * KSUB), 2 * KSUB), 2 * KSUB), :]
        s = jnp.zeros((2 * KSUB, tq), F32)
        for h in range(IDX_HEADS):
            s = s + w_rows[h] * jnp.maximum(_dot(ikb, iqm_ref[h]), 0.0)
        key = jnp.where(j * (2 * KSUB) + k_pair <= q_idx, _sortable(s), INT_MIN)
        sc_ref[pl.ds(2 * j, 2)] = key.reshape(2, KSUB, tq)
        ob = key ^ INT_MIN
        bp_ref[j] = jnp.concatenate(_bit_planes([ob[v * 8:(v + 1) * 8, :] for v in range(32)]), axis=0)
        return carry

    lax.fori_loop(0, nks // 2, score_body, 0)

    nblk = bp_ref.shape[0]
    live_rows = lax.broadcasted_iota(I32, (nblk * 8, 1), 0) < (nks // 2) * 8

    def bit_body(t, carry):
        alive, need, thr_ob = carry
        plane = bp_ref[:, pl.ds(pl.multiple_of(t * 8, 8), 8), :].reshape(nblk * 8, tq)
        ones = alive & plane
        cnt = jnp.sum(lax.population_count(ones), axis=0, keepdims=True)
        take = cnt >= need
        alive = jnp.where(take, ones, alive ^ ones)
        need = jnp.where(take, need, need - cnt)
        thr_ob = thr_ob | jnp.where(take, jnp.left_shift(jnp.int32(1), 31 - t), 0)
        return alive, need, thr_ob

    alive, need, thr_ob = lax.fori_loop(
        0, 32, bit_body,
        (jnp.broadcast_to(jnp.where(live_rows, -1, 0), (nblk * 8, tq)),
         jnp.full((1, tq), topk, I32), jnp.zeros((1, tq), I32)))
    thr = thr_ob ^ INT_MIN
    n_ties = jnp.sum(lax.population_count(alive), axis=0, keepdims=True)

    def count32(pred):
        def body(ks, acc):
            c = jnp.where(pred(sc_ref[ks], ks * KSUB + k_sub), 1, 0)
            return acc + _tree_sum([c[r * 8:(r + 1) * 8] for r in range(KSUB // 8)])
        acc = lax.fori_loop(0, nks, body, jnp.zeros((8, tq), I32))
        return jnp.sum(acc, axis=0, keepdims=True)

    m_ref[0:1, :] = jnp.full((1, tq), 2.0**30, F32)

    @pl.when(jnp.max(n_ties - need) > 0)
    def _():
        def idx_body(t, cut):
            bit = jnp.left_shift(jnp.int32(1), idx_bits - 1 - t)
            trial = cut | bit
            cnt = count32(lambda blk, kidx: (blk == thr) & (kidx < trial))
            return jnp.where(cnt < need, trial, cut)
        cut = lax.fori_loop(0, idx_bits, idx_body, jnp.zeros((1, tq), I32))
        m_ref[0:1, :] = cut.astype(F32)

    cut = m_ref[0:1, :].astype(I32)

    thr_sel = jnp.maximum(thr, INT_MIN + 1)

    def bias_body(j, carry):
        pair = pl.ds(2 * j, 2)
        key = sc_ref[pair].reshape(2 * KSUB, tq)
        kidx = j * (2 * KSUB) + k_pair
        sel = key >= thr_sel + jnp.where(kidx <= cut, 0, 1)
        sc_ref[pair] = pltpu.bitcast(jnp.where(sel, 0.0, NEG).astype(F32), I32).reshape(2, KSUB, tq)
        return carry

    lax.fori_loop(0, nks // 2, bias_body, 0)

    _mask_heads(sq_ref, qm_ref, S_HEADS)
    acc_ref[...] = jnp.zeros_like(acc_ref)
    m_ref[...] = jnp.full_like(m_ref, NEG)

    def qk_stage(ks, dst_ref):
        off = pl.multiple_of(ks * KSUB, KSUB)
        bias = pltpu.bitcast(sc_ref[ks], F32)
        for g in range(S_HEADS // 4):
            kg = sk_ref[pl.ds(off, KSUB), g * HEAD_GROUP:(g + 1) * HEAD_GROUP]
            for h in range(4 * g, 4 * g + 4):
                dst_ref[h] = _dot(kg, qm_ref[h]) + bias

    def sm_stage(ks, src_ref):
        _softmax_group([src_ref[h] for h in range(S_HEADS)],
                       [svt_ref[ks, h * S_VROWS:(h + 1) * S_VROWS, :] for h in range(S_HEADS)],
                       S_VROWS, acc_ref, m_ref)

    qk_stage(0, sa_ref)

    def attn_body(j, carry):
        qk_stage(2 * j + 1, sb_ref)
        sm_stage(2 * j, sa_ref)
        qk_stage(jnp.minimum(2 * j + 2, nks - 1), sa_ref)
        sm_stage(2 * j + 1, sb_ref)
        return carry

    lax.fori_loop(0, nks // 2, attn_body, 0)

    outs = []
    for h in range(S_HEADS):
        r0 = h * S_VROWS
        outs.append(acc_ref[r0:r0 + ROPE_DIM, :] / acc_ref[r0 + ROPE_DIM:r0 + ROPE_DIM + 1, :])
    y_ref[...] = jnp.concatenate(outs, axis=0).T


def _dsa(z, sq, sk, svt, iq, ik4, batch, seq, *, tq=256):
    n = z.shape[0]
    tq = min(tq, seq)
    nq = seq // tq
    nsub = seq // KSUB
    topk = min(IDX_TOPK_MAX, seq // 4)
    kern = functools.partial(_dsa_kernel, tq=tq, topk=topk, idx_bits=seq.bit_length())
    return pl.pallas_call(
        kern,
        grid=(batch, nq),
        in_specs=[
            pl.BlockSpec((tq, S_WIDTH), lambda b, i: (b * nq + i, 0)),
            pl.BlockSpec((tq, IDX_WIDTH), lambda b, i: (b * nq + i, 0)),
            pl.BlockSpec((tq, LANE), lambda b, i: (b * nq + i, COL_MISC // LANE)),
            pl.BlockSpec((seq, S_WIDTH), lambda b, i: (b, 0)),
            pl.BlockSpec((nsub, S_HEADS * S_VROWS, KSUB), lambda b, i: (b, 0, 0)),
            pl.BlockSpec((seq, IDX_WIDTH), lambda b, i: (b, 0)),
        ],
        out_specs=pl.BlockSpec((tq, S_WIDTH), lambda b, i: (b * nq + i, 0)),
        out_shape=jax.ShapeDtypeStruct((n, S_WIDTH), F32),
        scratch_shapes=[
            pltpu.VMEM((nsub, KSUB, tq), I32),
            pltpu.VMEM((nsub // 2, 2 * KSUB, tq), I32),
            pltpu.VMEM((S_HEADS, HEAD_GROUP, tq), BF16),
            pltpu.VMEM((IDX_HEADS, HEAD_GROUP, tq), BF16),
            pltpu.VMEM((S_HEADS * S_VROWS, tq), F32),
            pltpu.VMEM((S_HEADS, tq), F32),
            pltpu.VMEM((S_HEADS, KSUB, tq), F32),
            pltpu.VMEM((S_HEADS, KSUB, tq), F32),
        ],
        compiler_params=_cparams(("arbitrary", "arbitrary")),
        name="dsa_attn",
    )(sq, iq, z, sk, svt, ik4)


def _diff_kernel(q_ref, k_ref, vt_ref, lam_ref, og_ref, y_ref, qm_ref, acc_ref, m_ref,
                 sa_ref, sb_ref, *, tq, lam_init):
    i = pl.program_id(1)
    nfull = i * (tq // KSUB)
    q_idx = i * tq + lax.broadcasted_iota(I32, (1, tq), 1)
    k_sub = lax.broadcasted_iota(I32, (KSUB, 1), 0)

    _mask_heads(q_ref, qm_ref, DA_MAPS)
    acc_ref[...] = jnp.zeros_like(acc_ref)
    m_ref[...] = jnp.full_like(m_ref, NEG)

    def qk_stage(ks, dst_ref):
        off = pl.multiple_of(ks * KSUB, KSUB)
        for g in range(DA_MAPS // 4):
            kg = k_ref[pl.ds(off, KSUB), g * HEAD_GROUP:(g + 1) * HEAD_GROUP]
            for m in range(4 * g, 4 * g + 4):
                dst_ref[m] = _dot(kg, qm_ref[m])

    def sm_stage(ks, src_ref, masked):
        logits = [src_ref[m] for m in range(DA_MAPS)]
        if masked:
            bias = jnp.where(ks * KSUB + k_sub <= q_idx, 0.0, NEG).astype(F32)
            logits = [s + bias for s in logits]
        _softmax_group(logits,
                       [vt_ref[ks, (m // 2) * DA_VROWS:(m // 2 + 1) * DA_VROWS, :] for m in range(DA_MAPS)],
                       DA_VROWS, acc_ref, m_ref)

    assert tq == 2 * KSUB
    qk_stage(0, sa_ref)

    def full_body(j, carry):
        qk_stage(2 * j + 1, sb_ref)
        sm_stage(2 * j, sa_ref, False)
        qk_stage(2 * j + 2, sa_ref)
        sm_stage(2 * j + 1, sb_ref, False)
        return carry

    lax.fori_loop(0, nfull // 2, full_body, 0)
    qk_stage(nfull + 1, sb_ref)
    sm_stage(nfull, sa_ref, True)
    sm_stage(nfull + 1, sb_ref, True)

    lam = lam_ref[...]
    p01 = jnp.sum(lam[0:1] * lam[1:2], axis=-1, keepdims=True)
    p23 = jnp.sum(lam[2:3] * lam[3:4], axis=-1, keepdims=True)
    lam_val = jnp.exp(p01) - jnp.exp(p23) + lam_init
    def normalised(m):
        r0 = m * DA_VROWS
        return acc_ref[r0:r0 + DA_V_DIM, :] / acc_ref[r0 + DA_V_DIM:r0 + DA_V_DIM + 1, :]

    outs = []
    for hd in range(DA_HEADS):
        o = normalised(2 * hd) - lam_val * normalised(2 * hd + 1)
        ms = jnp.mean(o * o, axis=0, keepdims=True)
        outs.append(o * lax.rsqrt(ms + EPS) * og_ref[...] * (1.0 - lam_init))
    y_ref[...] = jnp.concatenate(outs, axis=0).T


def _diff_attn(dq, dk, dvt, lam, out_g_col, batch, seq, lam_init, *, tq=256):
    n = dq.shape[0]
    tq = min(tq, seq)
    nq = seq // tq
    nsub = seq // KSUB
    kern = functools.partial(_diff_kernel, tq=tq, lam_init=lam_init)
    return pl.pallas_call(
        kern,
        grid=(batch, nq),
        in_specs=[
            pl.BlockSpec((tq, 512), lambda b, i: (b * nq + i, 0)),
            pl.BlockSpec((seq, 512), lambda b, i: (b, 0)),
            pl.BlockSpec((nsub, DA_HEADS * DA_VROWS, KSUB), lambda b, i: (b, 0, 0)),
            pl.BlockSpec((4, ROPE_DIM), lambda b, i: (0, 0)),
            pl.BlockSpec((DA_V_DIM, 1), lambda b, i: (0, 0)),
        ],
        out_specs=pl.BlockSpec((tq, DA_WIDTH), lambda b, i: (b * nq + i, 0)),
        out_shape=jax.ShapeDtypeStruct((n, DA_WIDTH), F32),
        scratch_shapes=[
            pltpu.VMEM((DA_MAPS, HEAD_GROUP, tq), BF16),
            pltpu.VMEM((DA_MAPS * DA_VROWS, tq), F32),
            pltpu.VMEM((DA_MAPS, tq), F32),
            pltpu.VMEM((DA_MAPS, KSUB, tq), F32),
            pltpu.VMEM((DA_MAPS, KSUB, tq), F32),
        ],
        compiler_params=_cparams(("parallel", "arbitrary")),
        name="diff_attn",
    )(dq, dk, dvt, lam, out_g_col)


def _merge_kernel(x_ref, ya_ref, yb_ref, yc_ref, gp_ref, bg_ref, wb_ref, wo_ref, ng_ref, xo_ref, xn_ref):
    merged = None
    for br, y_ref in enumerate((ya_ref, yb_ref, yc_ref)):
        sl = slice(br * D_MODEL, (br + 1) * D_MODEL)
        gate = _sigmoid(gp_ref[:, sl] + bg_ref[:, sl])
        term = gate * _dot(y_ref[...].astype(BF16), wb_ref[br])
        merged = term if merged is None else merged + term
    xo = x_ref[...] + _dot(merged.astype(BF16), wo_ref[...])
    xo_ref[...] = xo
    ms = jnp.mean(xo * xo, axis=-1, keepdims=True)
    xn_ref[...] = (xo * lax.rsqrt(ms + EPS) * ng_ref[...]).astype(BF16)


def _merge(x2, ya, yb, yc, z, b_gate, w_branch, w_out, norm_g, *, tm=512):
    n = x2.shape[0]
    tm = min(tm, n)
    row = lambda w: pl.BlockSpec((tm, w), lambda i: (i, 0))
    return pl.pallas_call(
        _merge_kernel,
        grid=(n // tm,),
        in_specs=[
            row(D_MODEL), row(512), row(512), row(512),
            pl.BlockSpec((tm, N_BRANCH * D_MODEL), lambda i: (i, COL_GATE)),
            pl.BlockSpec((1, N_BRANCH * D_MODEL), lambda i: (0, 0)),
            pl.BlockSpec((N_BRANCH, 512, D_MODEL), lambda i: (0, 0, 0)),
            pl.BlockSpec((D_MODEL, D_MODEL), lambda i: (0, 0)),
            pl.BlockSpec((1, D_MODEL), lambda i: (0, 0)),
        ],
        out_specs=[row(D_MODEL), row(D_MODEL)],
        out_shape=[jax.ShapeDtypeStruct((n, D_MODEL), F32), jax.ShapeDtypeStruct((n, D_MODEL), BF16)],
        compiler_params=_cparams(("parallel",)),
        name="merge_out",
    )(x2, ya, yb, yc, z, b_gate, w_branch, w_out, norm_g)


def _ffn_kernel(xn_ref, x_ref, wg_ref, wu_ref, wd_ref, o_ref, acc_ref):
    j = pl.program_id(1)
    xn = xn_ref[...]
    g = _dot(xn, wg_ref[...])
    u = _dot(xn, wu_ref[...])
    part = _dot((g * _sigmoid(g) * u).astype(BF16), wd_ref[...])

    @pl.when(j == 0)
    def _():
        acc_ref[...] = part

    @pl.when(j > 0)
    def _():
        acc_ref[...] += part

    @pl.when(j == pl.num_programs(1) - 1)
    def _():
        o_ref[...] = x_ref[...] + acc_ref[...]


def _ffn(xn, x2, wg, wu, wd, *, tm=512, fc=1408):
    n = x2.shape[0]
    tm = min(tm, n)
    return pl.pallas_call(
        _ffn_kernel,
        grid=(n // tm, FF_DIM // fc),
        in_specs=[
            pl.BlockSpec((tm, D_MODEL), lambda i, j: (i, 0)),
            pl.BlockSpec((tm, D_MODEL), lambda i, j: (i, 0)),
            pl.BlockSpec((D_MODEL, fc), lambda i, j: (0, j)),
            pl.BlockSpec((D_MODEL, fc), lambda i, j: (0, j)),
            pl.BlockSpec((fc, D_MODEL), lambda i, j: (j, 0)),
        ],
        out_specs=pl.BlockSpec((tm, D_MODEL), lambda i, j: (i, 0)),
        out_shape=jax.ShapeDtypeStruct((n, D_MODEL), F32),
        scratch_shapes=[pltpu.VMEM((tm, D_MODEL), F32)],
        compiler_params=_cparams(("parallel", "arbitrary")),
        name="ffn",
    )(xn, x2, wg, wu, wd)


def _rope_tables(seq):
    inv = 1.0 / jnp.power(ROPE_THETA, jnp.arange(0, ROPE_DIM, 2, dtype=F32) / ROPE_DIM)
    ang = jnp.arange(seq, dtype=F32)[:, None] * inv[None, :]
    cos, sin = jnp.cos(ang), jnp.sin(ang)
    c64 = jnp.concatenate([cos, cos], axis=1)
    s64 = jnp.concatenate([-sin, sin], axis=1)
    return jnp.tile(c64, (1, 4)), jnp.tile(s64, (1, 4))


def _layer(x2, li, batch, seq, consts, norm_mix_g, w_in_l, b_gate, conv_w, conv_b, gate_b, m_norm_g,
           kv_norm_g, w_kv_up, sq_g, sk_g, dq_g, dk_g, lam, d_out_g, w_branch, w_out,
           norm_ffn_g, w_gate_up, w_down):
    rope_c, rope_s, grp, tri = consts
    lam_init = 0.8 - 0.6 * math.exp(-0.3 * li)
    tile8 = lambda g: jnp.tile(g, 8)[None, :]

    z = _inproj(x2, norm_mix_g[None, :], w_in_l)

    gate_row = jnp.zeros((1, LANE), F32).at[0, MISC_IF:MISC_IF + 2 * M_HEADS].set(gate_b)
    ya = _mlstm(z, conv_w, conv_b[None, :], gate_row, m_norm_g.reshape(1, M_WIDTH), tri, batch, seq)

    sq, sk, svt, iq, ik4, dq, dk, dvt = _prep(
        z, rope_c, rope_s, grp, kv_norm_g[None, :], w_kv_up.astype(BF16),
        tile8(sq_g), tile8(sk_g), tile8(dq_g), tile8(dk_g), seq)
    yb = _dsa(z, sq, sk, svt, iq, ik4, batch, seq)
    yc = _diff_attn(dq, dk, dvt, lam, d_out_g[:, None], batch, seq, lam_init)

    xo, xn = _merge(x2, ya, yb, yc, z, b_gate[None, :], w_branch.astype(BF16), w_out.astype(BF16),
                    norm_ffn_g[None, :])
    return _ffn(xn, xo, w_gate_up[:, :FF_DIM].astype(BF16), w_gate_up[:, FF_DIM:].astype(BF16),
                w_down.astype(BF16))


def kernel(x, norm_mix_g, w_in, b_gate, mlstm_conv_w, mlstm_conv_b, mlstm_gate_b, mlstm_norm_g,
           dsa_kv_norm_g, dsa_w_kv_up, dsa_q_norm_g, dsa_k_norm_g, diff_q_norm_g, diff_k_norm_g,
           diff_lambda, diff_out_norm_g, w_branch, w_out, norm_ffn_g, w_gate_up, w_down):
    batch, seq, d = x.shape
    depth = w_in.shape[0]
    assert w_in.shape[2] == IN_COLS and d == D_MODEL
    rope_c, rope_s = _rope_tables(seq)
    gi = jnp.arange(512) // ROPE_DIM
    grp = (gi[:, None] == gi[None, :]).astype(BF16)
    ti = jnp.arange(M_CHUNK)
    tri = (ti[:, None] >= ti[None, :]).astype(BF16)
    consts = (rope_c, rope_s, grp, tri)
    w_in_l = _wlayout(w_in)
    x2 = x.reshape(batch * seq, d)
    for li in range(depth):
        x2 = _layer(x2, li, batch, seq, consts, norm_mix_g[li], w_in_l[li], b_gate[li], mlstm_conv_w[li],
                    mlstm_conv_b[li], mlstm_gate_b[li], mlstm_norm_g[li], dsa_kv_norm_g[li],
                    dsa_w_kv_up[li], dsa_q_norm_g[li], dsa_k_norm_g[li], diff_q_norm_g[li],
                    diff_k_norm_g[li], diff_lambda[li], diff_out_norm_g[li], w_branch[li], w_out[li],
                    norm_ffn_g[li], w_gate_up[li], w_down[li])
    return x2.reshape(batch, seq, d)
```

```python
import functools
import math

import jax
import jax.numpy as jnp
from jax import lax
from jax.experimental import pallas as pl
from jax.experimental.pallas import tpu as pltpu

F32 = jnp.float32
BF16 = jnp.bfloat16
I32 = jnp.int32
I16 = jnp.int16

D_MODEL = 1024
EPS = 1e-6
ROPE_DIM = 64
ROPE_HALF = ROPE_DIM // 2
ROPE_THETA = 10000.0

M_HEADS = 4
M_HEAD_DIM = 128
M_WIDTH = M_HEADS * M_HEAD_DIM
M_CONV = 4
M_CHUNK = 128
M_INIT = -1e30

S_HEADS = 8
S_WIDTH = S_HEADS * ROPE_DIM
S_KV_RANK = 256
IDX_HEADS = 4
IDX_WIDTH = IDX_HEADS * ROPE_DIM
IDX_TOPK_MAX = 256

DA_HEADS = 4
DA_MAPS = 2 * DA_HEADS
DA_V_DIM = 2 * ROPE_DIM
DA_WIDTH = DA_HEADS * DA_V_DIM

N_BRANCH = 3
FF_DIM = 2816

NEG = -1e30
LOG2E = math.log2(math.e)
INT_MIN = -(2**31)
HALF16 = 2**15
LANE = 128
HEAD_GROUP = 256
KSUB = 128
ONES_ROWS = 16
S_VROWS = ROPE_DIM + ONES_ROWS
DA_VROWS = DA_V_DIM + ONES_ROWS

W_IN_SIZES = (2 * M_WIDTH, M_WIDTH, M_WIDTH, 2 * M_HEADS, S_WIDTH, S_KV_RANK, IDX_WIDTH, ROPE_DIM,
              IDX_HEADS, 2 * DA_HEADS * ROPE_DIM, 2 * DA_HEADS * ROPE_DIM, DA_WIDTH, N_BRANCH * D_MODEL)
IN_COLS = sum(W_IN_SIZES)

COL_GATE = 0
COL_MQK = 3072
COL_MV = 4096
COL_MO = 4608
COL_SQ = 5120
COL_DQ = 5632
COL_DK = 6144
COL_DV = 6656
COL_CKV = 7168
COL_IQ = 7424
COL_IK4 = 7680
COL_MISC = 7936
MISC_IF = 0
MISC_IW = 8
Z_COLS = 8192

VMEM_LIMIT = 56 * 1024 * 1024


def _cparams(sem, flags=None):
    return pltpu.CompilerParams(dimension_semantics=sem, vmem_limit_bytes=VMEM_LIMIT, flags=flags)


def _sigmoid(x):
    return 1.0 / (1.0 + jnp.exp(-x))


def _dot(a, b):
    return jnp.dot(a, b, preferred_element_type=F32)


def _dot_nt(a, b):
    return lax.dot_general(a, b, (((1,), (1,)), ((), ())), preferred_element_type=F32)


def _dot_tn(a, b):
    return lax.dot_general(a, b, (((0,), (0,)), ((), ())), preferred_element_type=F32)


def _split3(x):
    hi = x.astype(BF16)
    r1 = x - hi.astype(F32)
    mid = r1.astype(BF16)
    lo = (r1 - mid.astype(F32)).astype(BF16)
    return hi, mid, lo


def _wlayout_kernel(w_ref, o_ref):
    w = w_ref[...]
    offs = [0]
    for s in W_IN_SIZES:
        offs.append(offs[-1] + s)
    seg = [w[:, offs[k]:offs[k + 1]] for k in range(len(W_IN_SIZES))]
    m_qk, m_v, m_o, m_if, s_q, s_ckv, i_q, i_k, i_w, d_q, d_k, d_v, g_pre = seg
    rows = w.shape[0]
    misc = jnp.concatenate([m_if, i_w, jnp.zeros((rows, LANE - 2 * M_HEADS - IDX_HEADS), F32)], axis=1)
    ik4 = jnp.concatenate([i_k] * IDX_HEADS, axis=1)
    tail = jnp.zeros((rows, Z_COLS - COL_MISC - LANE), F32)
    for col, val in ((COL_GATE, g_pre), (COL_MQK, m_qk), (COL_MV, m_v), (COL_MO, m_o), (COL_SQ, s_q),
                     (COL_DQ, d_q), (COL_DK, d_k), (COL_DV, d_v), (COL_CKV, s_ckv), (COL_IQ, i_q),
                     (COL_IK4, ik4), (COL_MISC, misc), (COL_MISC + LANE, tail)):
        o_ref[:, col:col + val.shape[1]] = val.astype(BF16)


def _wlayout(w_in, *, tr=256):
    depth, rows, cols = w_in.shape
    return pl.pallas_call(
        _wlayout_kernel,
        grid=(depth, rows // tr),
        in_specs=[pl.BlockSpec((None, tr, cols), lambda l, i: (l, i, 0))],
        out_specs=pl.BlockSpec((None, tr, Z_COLS), lambda l, i: (l, i, 0)),
        out_shape=jax.ShapeDtypeStruct((depth, rows, Z_COLS), BF16),
        compiler_params=_cparams(("parallel", "parallel")),
        name="w_layout",
    )(w_in)


def _inproj_kernel(x_ref, g_ref, w_ref, z_ref, misc_ref, *, tn):
    x = x_ref[...]
    ms = jnp.mean(x * x, axis=-1, keepdims=True)
    xn = (x * lax.rsqrt(ms + EPS) * g_ref[...]).astype(BF16)
    for c in range(Z_COLS // tn):
        r = _dot(xn, w_ref[:, c * tn:(c + 1) * tn])
        z_ref[:, c * tn:(c + 1) * tn] = r.astype(BF16)
        if c == COL_MISC // tn:
            misc_ref[...] = r[:, COL_MISC % tn:COL_MISC % tn + LANE]


def _inproj(x2, g, w, *, tm=512, tn=1024):
    n = x2.shape[0]
    tm = min(tm, n)
    return pl.pallas_call(
        functools.partial(_inproj_kernel, tn=tn),
        grid=(n // tm,),
        in_specs=[
            pl.BlockSpec((tm, D_MODEL), lambda i: (i, 0)),
            pl.BlockSpec((1, D_MODEL), lambda i: (0, 0)),
            pl.BlockSpec((D_MODEL, Z_COLS), lambda i: (0, 0), pipeline_mode=pl.Buffered(1)),
        ],
        out_specs=[pl.BlockSpec((tm, Z_COLS), lambda i: (i, 0)),
                   pl.BlockSpec((tm, LANE), lambda i: (i, 0))],
        out_shape=[jax.ShapeDtypeStruct((n, Z_COLS), BF16), jax.ShapeDtypeStruct((n, LANE), F32)],
        compiler_params=_cparams(("parallel",)),
        name="inproj",
    )(x2, g, w)


def _mlstm_kernel(qk_ref, v_ref, o_ref, misc_ref, cw_ref, cb_ref, gb_ref, ng_ref, tri_ref, y_ref,
                  tail_ref, qkc_ref, c_ref, n_ref, m_ref):
    L = M_CHUNK
    dh = M_HEAD_DIM

    @pl.when(pl.program_id(1) == 0)
    def _():
        tail_ref[...] = jnp.zeros_like(tail_ref)
        c_ref[...] = jnp.zeros_like(c_ref)
        n_ref[...] = jnp.zeros_like(n_ref)
        m_ref[...] = jnp.full_like(m_ref, M_INIT)

    x = qk_ref[...].astype(F32)
    tail = tail_ref[...]
    row8 = lax.broadcasted_iota(I32, (8, 1), 0)
    cw = cw_ref[...]
    cb = cb_ref[...]
    acc = x * cw[M_CONV - 1:M_CONV] + cb
    acc_head = x[0:8] * cw[M_CONV - 1:M_CONV] + cb
    for j in range(1, M_CONV):
        wj = cw[M_CONV - 1 - j:M_CONV - j]
        xr = pltpu.roll(x, j, axis=0)
        tr = pltpu.roll(tail, j, axis=0)
        acc = acc + xr * wj
        acc_head = acc_head + jnp.where(row8 < j, tr, xr[0:8]) * wj
    qkc_ref[...] = acc * _sigmoid(acc)
    qkc_ref[0:8, :] = acc_head * _sigmoid(acc_head)
    tail_ref[...] = x[L - 8:L]

    gates = misc_ref[...] + gb_ref[...]
    lf = jnp.minimum(gates, 0.0) - jnp.log(1.0 + jnp.exp(-jnp.abs(gates)))
    tri = tri_ref[...]
    hi, mid, lo = _split3(lf)
    bcum = _dot(tri, hi) + _dot(tri, mid) + _dot(tri, lo)
    bcum_t = bcum.T
    gates_t = gates.T

    t_idx = lax.broadcasted_iota(I32, (L, L), 0)
    s_idx = lax.broadcasted_iota(I32, (L, L), 1)
    causal = t_idx >= s_idx

    ys, cs, ns, ms_new = [], [], [], []
    for h in range(M_HEADS):
        sl = slice(h * dh, (h + 1) * dh)
        q = qkc_ref[:, h * dh:(h + 1) * dh]
        k = qkc_ref[:, M_WIDTH + h * dh:M_WIDTH + (h + 1) * dh] * (dh ** -0.5)
        vb = v_ref[:, sl]
        v = vb.astype(F32)
        qb, kb = q.astype(BF16), k.astype(BF16)

        b_col = bcum[:, M_HEADS + h:M_HEADS + h + 1]
        i_col = gates[:, h:h + 1]
        b_row = bcum_t[M_HEADS + h:M_HEADS + h + 1, :]
        i_row = gates_t[h:h + 1, :]
        g_tot = bcum[L - 1:L, M_HEADS + h:M_HEADS + h + 1]

        c_prev = c_ref[sl, :]
        n_prev = n_ref[h:h + 1, :]
        m_prev = m_ref[h:h + 1, 0:1]

        dmat = jnp.where(causal, b_col - b_row + i_row, -jnp.inf)
        inter = b_col + m_prev
        m_t = jnp.maximum(inter, jnp.max(dmat, axis=-1, keepdims=True))
        sw = jnp.exp(dmat - m_t) * _dot_nt(qb, kb)
        s_inter = jnp.exp(inter - m_t)
        num = _dot(sw.astype(BF16), vb) + s_inter * _dot_nt(qb, c_prev.astype(BF16))
        den = jnp.sum(sw, axis=-1, keepdims=True) + s_inter * jnp.sum(q * n_prev, axis=-1, keepdims=True)
        hh = num / jnp.maximum(jnp.abs(den), jnp.exp(-m_t))
        ms = jnp.mean(hh * hh, axis=-1, keepdims=True)
        hn = hh * lax.rsqrt(ms + EPS) * ng_ref[:, sl]
        ys.append(_sigmoid(o_ref[:, sl].astype(F32)) * hn)

        a_col = g_tot - b_col + i_col
        m_loc = jnp.max(a_col, axis=0, keepdims=True)
        w_loc = jnp.exp(a_col - m_loc)
        c_loc = _dot_tn((v * w_loc).astype(BF16), kb)
        n_loc = jnp.sum(k * w_loc, axis=0, keepdims=True)
        m_new = jnp.maximum(g_tot + m_prev, m_loc)
        s_old = jnp.exp(g_tot + m_prev - m_new)
        s_loc = jnp.exp(m_loc - m_new)
        cs.append(s_old * c_prev + s_loc * c_loc)
        ns.append(s_old * n_prev + s_loc * n_loc)
        ms_new.append(jnp.broadcast_to(m_new, (1, LANE)))

    y_ref[...] = jnp.concatenate(ys, axis=1)
    c_ref[...] = jnp.concatenate(cs, axis=0)
    n_ref[...] = jnp.concatenate(ns, axis=0)
    m_ref[...] = jnp.concatenate(ms_new, axis=0)


def _mlstm(z, misc, conv_w, conv_b, gate_b_row, norm_g_row, tri, batch, seq):
    n = z.shape[0]
    nc = seq // M_CHUNK
    L = M_CHUNK
    row = lambda b, c: b * nc + c
    return pl.pallas_call(
        _mlstm_kernel,
        grid=(batch, nc),
        in_specs=[
            pl.BlockSpec((L, 2 * M_WIDTH), lambda b, c: (row(b, c), COL_MQK // (2 * M_WIDTH))),
            pl.BlockSpec((L, M_WIDTH), lambda b, c: (row(b, c), COL_MV // M_WIDTH)),
            pl.BlockSpec((L, M_WIDTH), lambda b, c: (row(b, c), COL_MO // M_WIDTH)),
            pl.BlockSpec((L, LANE), lambda b, c: (row(b, c), 0)),
            pl.BlockSpec((M_CONV, 2 * M_WIDTH), lambda b, c: (0, 0)),
            pl.BlockSpec((1, 2 * M_WIDTH), lambda b, c: (0, 0)),
            pl.BlockSpec((1, LANE), lambda b, c: (0, 0)),
            pl.BlockSpec((1, M_WIDTH), lambda b, c: (0, 0)),
            pl.BlockSpec((L, L), lambda b, c: (0, 0)),
        ],
        out_specs=pl.BlockSpec((L, M_WIDTH), lambda b, c: (row(b, c), 0)),
        out_shape=jax.ShapeDtypeStruct((n, M_WIDTH), F32),
        scratch_shapes=[
            pltpu.VMEM((8, 2 * M_WIDTH), F32),
            pltpu.VMEM((L, 2 * M_WIDTH), F32),
            pltpu.VMEM((M_HEADS * M_HEAD_DIM, M_HEAD_DIM), F32),
            pltpu.VMEM((M_HEADS, M_HEAD_DIM), F32),
            pltpu.VMEM((M_HEADS, LANE), F32),
        ],
        compiler_params=_cparams(("parallel", "arbitrary")),
        name="mlstm",
    )(z, z, z, misc, conv_w, conv_b, gate_b_row, norm_g_row, tri)


def _rope(x, c, s):
    w = x.shape[-1]
    lane = lax.broadcasted_iota(I32, (1, w), 1)
    fwd = pltpu.roll(x, ROPE_HALF, axis=1)
    bwd = pltpu.roll(x, w - ROPE_HALF, axis=1)
    swapped = jnp.where((lane & (ROPE_DIM - 1)) < ROPE_HALF, bwd, fwd)
    return x * c + swapped * s


def _head_rms(x, g, grp):
    sq = x * x
    hi = sq.astype(BF16)
    lo = (sq - hi.astype(F32)).astype(BF16)
    ss = _dot(hi, grp) + _dot(lo, grp)
    return x * lax.rsqrt(ss * (1.0 / ROPE_DIM) + EPS) * g


def _store_transposed(o_ref, v, dv):
    ones = jnp.ones((ONES_ROWS, KSUB), BF16)
    for j in range(o_ref.shape[0]):
        vt = v[j * KSUB:(j + 1) * KSUB, :].T.astype(BF16)
        parts = []
        for h in range(vt.shape[0] // dv):
            parts += [vt[h * dv:(h + 1) * dv, :], ones]
        o_ref[j] = jnp.concatenate(parts, axis=0)


def _prep_kernel(sq_ref, dq_ref, dk_ref, dv_ref, ckv_ref, iq_ref, ik_ref, c_ref, s_ref, grp_ref,
                 kvg_ref, wkv_ref, sqg_ref, skg_ref, dqg_ref, dkg_ref,
                 sq_o, sk_o, sv_o, iq_o, ik_o, dq_o, dk_o, dv_o):
    c256 = c_ref[...]
    s256 = s_ref[...]
    c512 = jnp.concatenate([c256, c256], axis=1)
    s512 = jnp.concatenate([s256, s256], axis=1)
    grp = grp_ref[...]
    qscale = ROPE_DIM ** -0.5 * LOG2E

    f32 = lambda ref: ref[...].astype(F32)

    sq_o[...] = (_rope(_head_rms(f32(sq_ref), sqg_ref[...], grp), c512, s512) * qscale).astype(BF16)

    ckv = f32(ckv_ref)
    ms = jnp.mean(ckv * ckv, axis=-1, keepdims=True)
    ckvn = (ckv * lax.rsqrt(ms + EPS) * kvg_ref[...]).astype(BF16)
    kv = _dot(ckvn, wkv_ref[...])
    sk_o[...] = _rope(_head_rms(kv[:, :S_WIDTH], skg_ref[...], grp), c512, s512).astype(BF16)
    _store_transposed(sv_o, kv[:, S_WIDTH:], ROPE_DIM)

    iq_o[...] = _rope(f32(iq_ref), c256, s256).astype(BF16)
    ik_o[...] = _rope(f32(ik_ref), c256, s256).astype(BF16)

    dq_o[...] = (_rope(_head_rms(f32(dq_ref), dqg_ref[...], grp), c512, s512) * qscale).astype(BF16)
    dk_o[...] = _rope(_head_rms(f32(dk_ref), dkg_ref[...], grp), c512, s512).astype(BF16)
    _store_transposed(dv_o, f32(dv_ref), DA_V_DIM)


def _prep(z, rope_c, rope_s, grp, kv_g, w_kv, sq_g, sk_g, dq_g, dk_g, seq, *, tm=512):
    n = z.shape[0]
    tm = min(tm, seq)
    nt = seq // tm
    zspec = lambda w, col: pl.BlockSpec((tm, w), lambda i: (i, col // w))
    const = lambda shape: pl.BlockSpec(shape, lambda i: (0, 0))
    o512 = pl.BlockSpec((tm, 512), lambda i: (i, 0))
    o256 = pl.BlockSpec((tm, 256), lambda i: (i, 0))
    s512 = jax.ShapeDtypeStruct((n, 512), BF16)
    s256 = jax.ShapeDtypeStruct((n, 256), BF16)
    tspec = lambda rows: pl.BlockSpec((tm // KSUB, rows, KSUB), lambda i: (i, 0, 0))
    tshape = lambda rows: jax.ShapeDtypeStruct((n // KSUB, rows, KSUB), BF16)
    sv_rows, dv_rows = S_HEADS * S_VROWS, DA_HEADS * DA_VROWS
    return pl.pallas_call(
        _prep_kernel,
        grid=(n // tm,),
        in_specs=[
            zspec(512, COL_SQ), zspec(512, COL_DQ), zspec(512, COL_DK), zspec(512, COL_DV),
            zspec(256, COL_CKV), zspec(256, COL_IQ), zspec(256, COL_IK4),
            pl.BlockSpec((tm, 256), lambda i: (i % nt, 0)),
            pl.BlockSpec((tm, 256), lambda i: (i % nt, 0)),
            const((512, 512)),
            const((1, S_KV_RANK)), const((S_KV_RANK, 2 * S_WIDTH)),
            const((1, 512)), const((1, 512)), const((1, 512)), const((1, 512)),
        ],
        out_specs=[o512, o512, tspec(sv_rows), o256, o256, o512, o512, tspec(dv_rows)],
        out_shape=[s512, s512, tshape(sv_rows), s256, s256, s512, s512, tshape(dv_rows)],
        compiler_params=_cparams(("parallel",)),
        name="attn_prep",
    )(z, z, z, z, z, z, z, rope_c, rope_s, grp, kv_g, w_kv, sq_g, sk_g, dq_g, dk_g)


def _mask_heads(q_ref, qm_ref, n_heads):
    lane_grp = lax.broadcasted_iota(I32, (1, HEAD_GROUP), 1) // ROPE_DIM
    for h in range(n_heads):
        g = h // 4
        qg = q_ref[:, g * HEAD_GROUP:(g + 1) * HEAD_GROUP].astype(F32)
        qm_ref[h] = jnp.where(lane_grp == (h % 4), qg, 0.0).T.astype(BF16)


def _softmax_group(logits, values, rows, acc_ref, m_ref):
    nh = len(logits)
    m_old = m_ref[...]
    m_new = jnp.maximum(m_old, jnp.concatenate([jnp.max(s, axis=0, keepdims=True) for s in logits], axis=0))
    alpha = jnp.exp2(m_old - m_new)
    ps = [jnp.exp2((s - m_new[j:j + 1, :]).astype(BF16)) for j, s in enumerate(logits)]
    m_ref[...] = m_new
    acc_old = acc_ref[...]
    acc_ref[...] = jnp.concatenate(
        [alpha[j:j + 1, :] * acc_old[j * rows:(j + 1) * rows, :] + _dot(values[j], ps[j])
         for j in range(nh)], axis=0)


def _sortable(x):
    bits = pltpu.bitcast(x, I32)
    return bits ^ ((bits >> 31) & 0x7FFFFFFF)


def _tree_sum(parts):
    while len(parts) > 1:
        parts = [parts[j] + parts[j + 1] for j in range(0, len(parts), 2)]
    return parts[0]


def _bit_planes(words):
    a = list(words)
    j, m = 16, 0x0000FFFF
    while j:
        k = 0
        while k < 32:
            t = (a[k] ^ lax.shift_right_logical(a[k + j], jnp.int32(j))) & m
            a[k] = a[k] ^ t
            a[k + j] = a[k + j] ^ (t << j)
            k = (k + j + 1) & ~j
        j >>= 1
        m = (m ^ (m << j)) & 0xFFFFFFFF
    return a


def _dsa_kernel(sq_ref, iq_ref, misc_ref, sk_ref, svt_ref, ik_ref, y_ref,
                sc_ref, bp_ref, qm_ref, iqm_ref, acc_ref, m_ref, sa_ref, sb_ref,
                *, tq, topk, idx_bits):
    i = pl.program_id(1)

    @pl.when((pl.program_id(0) == 0) & (i == 0))
    def _():
        bp_ref[...] = jnp.zeros_like(bp_ref)

    nks = (i + 1) * (tq // KSUB)
    q_idx = i * tq + lax.broadcasted_iota(I32, (1, tq), 1)
    k_sub = lax.broadcasted_iota(I32, (KSUB, 1), 0)

    _mask_heads(iq_ref, iqm_ref, IDX_HEADS)
    w_t = (misc_ref[...] * (IDX_WIDTH ** -0.5)).T
    w_rows = [w_t[MISC_IW + h:MISC_IW + h + 1, :] for h in range(IDX_HEADS)]

    k_pair = lax.broadcasted_iota(I32, (2 * KSUB, 1), 0)

    def score_body(j, carry):
        ikb = ik_ref[pl.ds(pl.multiple_of(j * (2 * KSUB), 2 * KSUB), 2 * KSUB), :]
        s = jnp.zeros((2 * KSUB, tq), F32)
        for h in range(IDX_HEADS):
            s = s + w_rows[h] * jnp.maximum(_dot(ikb, iqm_ref[h]), 0.0)
        key = jnp.where(j * (2 * KSUB) + k_pair <= q_idx, _sortable(s), INT_MIN)
        sc_ref[pl.ds(2 * j, 2)] = key.reshape(2, KSUB, tq)
        ob = key ^ INT_MIN
        bp_ref[j] = jnp.concatenate(_bit_planes([ob[v * 8:(v + 1) * 8, :] for v in range(32)]), axis=0)
        return carry

    lax.fori_loop(0, nks // 2, score_body, 0)

    nblk = bp_ref.shape[0]
    live_rows = lax.broadcasted_iota(I32, (nblk * 8, 1), 0) < (nks // 2) * 8

    def bit_body(t, carry):
        alive, need, thr_ob = carry
        plane = bp_ref[:, pl.ds(pl.multiple_of(t * 8, 8), 8), :].reshape(nblk * 8, tq)
        ones = alive & plane
        cnt = jnp.sum(lax.population_count(ones), axis=0, keepdims=True)
        take = cnt >= need
        alive = jnp.where(take, ones, alive ^ ones)
        need = jnp.where(take, need, need - cnt)
        thr_ob = thr_ob | jnp.where(take, jnp.left_shift(jnp.int32(1), 31 - t), 0)
        return alive, need, thr_ob

    alive, need, thr_ob = lax.fori_loop(
        0, 32, bit_body,
        (jnp.broadcast_to(jnp.where(live_rows, -1, 0), (nblk * 8, tq)),
         jnp.full((1, tq), topk, I32), jnp.zeros((1, tq), I32)))
    thr = thr_ob ^ INT_MIN
    n_ties = jnp.sum(lax.population_count(alive), axis=0, keepdims=True)

    def count32(pred):
        def body(ks, acc):
            c = jnp.where(pred(sc_ref[ks], ks * KSUB + k_sub), 1, 0)
            return acc + _tree_sum([c[r * 8:(r + 1) * 8] for r in range(KSUB // 8)])
        acc = lax.fori_loop(0, nks, body, jnp.zeros((8, tq), I32))
        return jnp.sum(acc, axis=0, keepdims=True)

    m_ref[0:1, :] = jnp.full((1, tq), 2.0**30, F32)

    @pl.when(jnp.max(n_ties - need) > 0)
    def _():
        def idx_body(t, cut):
            bit = jnp.left_shift(jnp.int32(1), idx_bits - 1 - t)
            trial = cut | bit
            cnt = count32(lambda blk, kidx: (blk == thr) & (kidx < trial))
            return jnp.where(cnt < need, trial, cut)
        cut = lax.fori_loop(0, idx_bits, idx_body, jnp.zeros((1, tq), I32))
        m_ref[0:1, :] = cut.astype(F32)

    cut = m_ref[0:1, :].astype(I32)

    thr_sel = jnp.maximum(thr, INT_MIN + 1)

    def bias_body(j, carry):
        pair = pl.ds(2 * j, 2)
        key = sc_ref[pair].reshape(2 * KSUB, tq)
        kidx = j * (2 * KSUB) + k_pair
        sel = key >= thr_sel + jnp.where(kidx <= cut, 0, 1)
        sc_ref[pair] = pltpu.bitcast(jnp.where(sel, 0.0, NEG).astype(F32), I32).reshape(2, KSUB, tq)
        return carry

    lax.fori_loop(0, nks // 2, bias_body, 0)

    _mask_heads(sq_ref, qm_ref, S_HEADS)
    acc_ref[...] = jnp.zeros_like(acc_ref)
    m_ref[...] = jnp.full_like(m_ref, NEG)

    def qk_stage(ks, dst_ref):
        off = pl.multiple_of(ks * KSUB, KSUB)
        bias = pltpu.bitcast(sc_ref[ks], F32)
        for g in range(S_HEADS // 4):
            kg = sk_ref[pl.ds(off, KSUB), g * HEAD_GROUP:(g + 1) * HEAD_GROUP]
            for h in range(4 * g, 4 * g + 4):
                dst_ref[h] = _dot(kg, qm_ref[h]) + bias

    def sm_stage(ks, src_ref):
        _softmax_group([src_ref[h] for h in range(S_HEADS)],
                       [svt_ref[ks, h * S_VROWS:(h + 1) * S_VROWS, :] for h in range(S_HEADS)],
                       S_VROWS, acc_ref, m_ref)

    qk_stage(0, sa_ref)

    def attn_body(j, carry):
        qk_stage(2 * j + 1, sb_ref)
        sm_stage(2 * j, sa_ref)
        qk_stage(jnp.minimum(2 * j + 2, nks - 1), sa_ref)
        sm_stage(2 * j + 1, sb_ref)
        return carry

    lax.fori_loop(0, nks // 2, attn_body, 0)

    outs = []
    for h in range(S_HEADS):
        r0 = h * S_VROWS
        outs.append(acc_ref[r0:r0 + ROPE_DIM, :] / acc_ref[r0 + ROPE_DIM:r0 + ROPE_DIM + 1, :])
    y_ref[...] = jnp.concatenate(outs, axis=0).T


def _dsa(misc, sq, sk, svt, iq, ik4, batch, seq, *, tq=256):
    n = misc.shape[0]
    tq = min(tq, seq)
    nq = seq // tq
    nsub = seq // KSUB
    topk = min(IDX_TOPK_MAX, seq // 4)
    kern = functools.partial(_dsa_kernel, tq=tq, topk=topk, idx_bits=seq.bit_length())
    return pl.pallas_call(
        kern,
        grid=(batch, nq),
        in_specs=[
            pl.BlockSpec((tq, S_WIDTH), lambda b, i: (b * nq + i, 0)),
            pl.BlockSpec((tq, IDX_WIDTH), lambda b, i: (b * nq + i, 0)),
            pl.BlockSpec((tq, LANE), lambda b, i: (b * nq + i, 0)),
            pl.BlockSpec((seq, S_WIDTH), lambda b, i: (b, 0)),
            pl.BlockSpec((nsub, S_HEADS * S_VROWS, KSUB), lambda b, i: (b, 0, 0)),
            pl.BlockSpec((seq, IDX_WIDTH), lambda b, i: (b, 0)),
        ],
        out_specs=pl.BlockSpec((tq, S_WIDTH), lambda b, i: (b * nq + i, 0)),
        out_shape=jax.ShapeDtypeStruct((n, S_WIDTH), F32),
        scratch_shapes=[
            pltpu.VMEM((nsub, KSUB, tq), I32),
            pltpu.VMEM((nsub // 2, 2 * KSUB, tq), I32),
            pltpu.VMEM((S_HEADS, HEAD_GROUP, tq), BF16),
            pltpu.VMEM((IDX_HEADS, HEAD_GROUP, tq), BF16),
            pltpu.VMEM((S_HEADS * S_VROWS, tq), F32),
            pltpu.VMEM((S_HEADS, tq), F32),
            pltpu.VMEM((S_HEADS, KSUB, tq), F32),
            pltpu.VMEM((S_HEADS, KSUB, tq), F32),
        ],
        compiler_params=_cparams(("arbitrary", "arbitrary")),
        name="dsa_attn",
    )(sq, iq, misc, sk, svt, ik4)


def _diff_kernel(q_ref, k_ref, vt_ref, lam_ref, og_ref, y_ref, qm_ref, acc_ref, m_ref,
                 sa_ref, sb_ref, *, tq, lam_init):
    i = pl.program_id(1)
    nfull = i * (tq // KSUB)
    q_idx = i * tq + lax.broadcasted_iota(I32, (1, tq), 1)
    k_sub = lax.broadcasted_iota(I32, (KSUB, 1), 0)

    _mask_heads(q_ref, qm_ref, DA_MAPS)
    acc_ref[...] = jnp.zeros_like(acc_ref)
    m_ref[...] = jnp.full_like(m_ref, NEG)

    def qk_stage(ks, dst_ref):
        off = pl.multiple_of(ks * KSUB, KSUB)
        for g in range(DA_MAPS // 4):
            kg = k_ref[pl.ds(off, KSUB), g * HEAD_GROUP:(g + 1) * HEAD_GROUP]
            for m in range(4 * g, 4 * g + 4):
                dst_ref[m] = _dot(kg, qm_ref[m])

    def sm_stage(ks, src_ref, masked):
        logits = [src_ref[m] for m in range(DA_MAPS)]
        if masked:
            bias = jnp.where(ks * KSUB + k_sub <= q_idx, 0.0, NEG).astype(F32)
            logits = [s + bias for s in logits]
        _softmax_group(logits,
                       [vt_ref[ks, (m // 2) * DA_VROWS:(m // 2 + 1) * DA_VROWS, :] for m in range(DA_MAPS)],
                       DA_VROWS, acc_ref, m_ref)

    assert tq == 2 * KSUB
    qk_stage(0, sa_ref)

    def full_body(j, carry):
        qk_stage(2 * j + 1, sb_ref)
        sm_stage(2 * j, sa_ref, False)
        qk_stage(2 * j + 2, sa_ref)
        sm_stage(2 * j + 1, sb_ref, False)
        return carry

    lax.fori_loop(0, nfull // 2, full_body, 0)
    qk_stage(nfull + 1, sb_ref)
    sm_stage(nfull, sa_ref, True)
    sm_stage(nfull + 1, sb_ref, True)

    lam = lam_ref[...]
    p01 = jnp.sum(lam[0:1] * lam[1:2], axis=-1, keepdims=True)
    p23 = jnp.sum(lam[2:3] * lam[3:4], axis=-1, keepdims=True)
    lam_val = jnp.exp(p01) - jnp.exp(p23) + lam_init
    def normalised(m):
        r0 = m * DA_VROWS
        return acc_ref[r0:r0 + DA_V_DIM, :] / acc_ref[r0 + DA_V_DIM:r0 + DA_V_DIM + 1, :]

    outs = []
    for hd in range(DA_HEADS):
        o = normalised(2 * hd) - lam_val * normalised(2 * hd + 1)
        ms = jnp.mean(o * o, axis=0, keepdims=True)
        outs.append(o * lax.rsqrt(ms + EPS) * og_ref[...] * (1.0 - lam_init))
    y_ref[...] = jnp.concatenate(outs, axis=0).T


def _diff_attn(dq, dk, dvt, lam, out_g_col, batch, seq, lam_init, *, tq=256):
    n = dq.shape[0]
    tq = min(tq, seq)
    nq = seq // tq
    nsub = seq // KSUB
    kern = functools.partial(_diff_kernel, tq=tq, lam_init=lam_init)
    return pl.pallas_call(
        kern,
        grid=(batch, nq),
        in_specs=[
            pl.BlockSpec((tq, 512), lambda b, i: (b * nq + i, 0)),
            pl.BlockSpec((seq, 512), lambda b, i: (b, 0)),
            pl.BlockSpec((nsub, DA_HEADS * DA_VROWS, KSUB), lambda b, i: (b, 0, 0)),
            pl.BlockSpec((4, ROPE_DIM), lambda b, i: (0, 0)),
            pl.BlockSpec((DA_V_DIM, 1), lambda b, i: (0, 0)),
        ],
        out_specs=pl.BlockSpec((tq, DA_WIDTH), lambda b, i: (b * nq + i, 0)),
        out_shape=jax.ShapeDtypeStruct((n, DA_WIDTH), F32),
        scratch_shapes=[
            pltpu.VMEM((DA_MAPS, HEAD_GROUP, tq), BF16),
            pltpu.VMEM((DA_MAPS * DA_VROWS, tq), F32),
            pltpu.VMEM((DA_MAPS, tq), F32),
            pltpu.VMEM((DA_MAPS, KSUB, tq), F32),
            pltpu.VMEM((DA_MAPS, KSUB, tq), F32),
        ],
        compiler_params=_cparams(("parallel", "arbitrary")),
        name="diff_attn",
    )(dq, dk, dvt, lam, out_g_col)


def _merge_kernel(x_ref, ya_ref, yb_ref, yc_ref, gp_ref, bg_ref, wb_ref, wo_ref, ng_ref, xo_ref, xn_ref):
    merged = None
    for br, y_ref in enumerate((ya_ref, yb_ref, yc_ref)):
        sl = slice(br * D_MODEL, (br + 1) * D_MODEL)
        gate = _sigmoid(gp_ref[:, sl].astype(F32) + bg_ref[:, sl])
        term = gate * _dot(y_ref[...].astype(BF16), wb_ref[br])
        merged = term if merged is None else merged + term
    xo = x_ref[...] + _dot(merged.astype(BF16), wo_ref[...])
    xo_ref[...] = xo
    ms = jnp.mean(xo * xo, axis=-1, keepdims=True)
    xn_ref[...] = (xo * lax.rsqrt(ms + EPS) * ng_ref[...]).astype(BF16)


def _merge(x2, ya, yb, yc, z, b_gate, w_branch, w_out, norm_g, *, tm=512):
    n = x2.shape[0]
    tm = min(tm, n)
    row = lambda w: pl.BlockSpec((tm, w), lambda i: (i, 0))
    return pl.pallas_call(
        _merge_kernel,
        grid=(n // tm,),
        in_specs=[
            row(D_MODEL), row(512), row(512), row(512),
            pl.BlockSpec((tm, N_BRANCH * D_MODEL), lambda i: (i, COL_GATE)),
            pl.BlockSpec((1, N_BRANCH * D_MODEL), lambda i: (0, 0)),
            pl.BlockSpec((N_BRANCH, 512, D_MODEL), lambda i: (0, 0, 0)),
            pl.BlockSpec((D_MODEL, D_MODEL), lambda i: (0, 0)),
            pl.BlockSpec((1, D_MODEL), lambda i: (0, 0)),
        ],
        out_specs=[row(D_MODEL), row(D_MODEL)],
        out_shape=[jax.ShapeDtypeStruct((n, D_MODEL), F32), jax.ShapeDtypeStruct((n, D_MODEL), BF16)],
        compiler_params=_cparams(("parallel",)),
        name="merge_out",
    )(x2, ya, yb, yc, z, b_gate, w_branch, w_out, norm_g)


def _ffn_kernel(xn_ref, x_ref, wg_ref, wu_ref, wd_ref, o_ref, *, fc):
    xn = xn_ref[...]
    acc = x_ref[...]
    for c in range(FF_DIM // fc):
        cols = slice(c * fc, (c + 1) * fc)
        g = _dot(xn, wg_ref[:, cols])
        u = _dot(xn, wu_ref[:, cols])
        acc = acc + _dot((g * _sigmoid(g) * u).astype(BF16), wd_ref[cols, :])
    o_ref[...] = acc


def _ffn(xn, x2, wg, wu, wd, *, tm=512, fc=256):
    n = x2.shape[0]
    tm = min(tm, n)
    resident = lambda shape: pl.BlockSpec(shape, lambda i: (0, 0), pipeline_mode=pl.Buffered(1))
    return pl.pallas_call(
        functools.partial(_ffn_kernel, fc=fc),
        grid=(n // tm,),
        in_specs=[
            pl.BlockSpec((tm, D_MODEL), lambda i: (i, 0)),
            pl.BlockSpec((tm, D_MODEL), lambda i: (i, 0)),
            resident((D_MODEL, FF_DIM)),
            resident((D_MODEL, FF_DIM)),
            resident((FF_DIM, D_MODEL)),
        ],
        out_specs=pl.BlockSpec((tm, D_MODEL), lambda i: (i, 0)),
        out_shape=jax.ShapeDtypeStruct((n, D_MODEL), F32),
        compiler_params=_cparams(("parallel",)),
        name="ffn",
    )(xn, x2, wg, wu, wd)


def _rope_tables(seq):
    inv = 1.0 / jnp.power(ROPE_THETA, jnp.arange(0, ROPE_DIM, 2, dtype=F32) / ROPE_DIM)
    ang = jnp.arange(seq, dtype=F32)[:, None] * inv[None, :]
    cos, sin = jnp.cos(ang), jnp.sin(ang)
    c64 = jnp.concatenate([cos, cos], axis=1)
    s64 = jnp.concatenate([-sin, sin], axis=1)
    return jnp.tile(c64, (1, 4)), jnp.tile(s64, (1, 4))


def _layer(x2, li, batch, seq, consts, norm_mix_g, w_in_l, b_gate, conv_w, conv_b, gate_b, m_norm_g,
           kv_norm_g, w_kv_up, sq_g, sk_g, dq_g, dk_g, lam, d_out_g, w_branch, w_out,
           norm_ffn_g, w_gate_up, w_down):
    rope_c, rope_s, grp, tri = consts
    lam_init = 0.8 - 0.6 * math.exp(-0.3 * li)
    tile8 = lambda g: jnp.tile(g, 8)[None, :]

    z, misc = _inproj(x2, norm_mix_g[None, :], w_in_l)

    gate_row = jnp.zeros((1, LANE), F32).at[0, MISC_IF:MISC_IF + 2 * M_HEADS].set(gate_b)
    ya = _mlstm(z, misc, conv_w, conv_b[None, :], gate_row, m_norm_g.reshape(1, M_WIDTH), tri, batch, seq)

    sq, sk, svt, iq, ik4, dq, dk, dvt = _prep(
        z, rope_c, rope_s, grp, kv_norm_g[None, :], w_kv_up.astype(BF16),
        tile8(sq_g), tile8(sk_g), tile8(dq_g), tile8(dk_g), seq)
    yb = _dsa(misc, sq, sk, svt, iq, ik4, batch, seq)
    yc = _diff_attn(dq, dk, dvt, lam, d_out_g[:, None], batch, seq, lam_init)

    xo, xn = _merge(x2, ya, yb, yc, z, b_gate[None, :], w_branch.astype(BF16), w_out.astype(BF16),
                    norm_ffn_g[None, :])
    return _ffn(xn, xo, w_gate_up[:, :FF_DIM].astype(BF16), w_gate_up[:, FF_DIM:].astype(BF16),
                w_down.astype(BF16))


def kernel(x, norm_mix_g, w_in, b_gate, mlstm_conv_w, mlstm_conv_b, mlstm_gate_b, mlstm_norm_g,
           dsa_kv_norm_g, dsa_w_kv_up, dsa_q_norm_g, dsa_k_norm_g, diff_q_norm_g, diff_k_norm_g,
           diff_lambda, diff_out_norm_g, w_branch, w_out, norm_ffn_g, w_gate_up, w_down):
    batch, seq, d = x.shape
    depth = w_in.shape[0]
    assert w_in.shape[2] == IN_COLS and d == D_MODEL
    rope_c, rope_s = _rope_tables(seq)
    gi = jnp.arange(512) // ROPE_DIM
    grp = (gi[:, None] == gi[None, :]).astype(BF16)
    ti = jnp.arange(M_CHUNK)
    tri = (ti[:, None] >= ti[None, :]).astype(BF16)
    consts = (rope_c, rope_s, grp, tri)
    w_in_l = _wlayout(w_in)
    x2 = x.reshape(batch * seq, d)
    for li in range(depth):
        x2 = _layer(x2, li, batch, seq, consts, norm_mix_g[li], w_in_l[li], b_gate[li], mlstm_conv_w[li],
                    mlstm_conv_b[li], mlstm_gate_b[li], mlstm_norm_g[li], dsa_kv_norm_g[li],
                    dsa_w_kv_up[li], dsa_q_norm_g[li], dsa_k_norm_g[li], diff_q_norm_g[li],
                    diff_k_norm_g[li], diff_lambda[li], diff_out_norm_g[li], w_branch[li], w_out[li],
                    norm_ffn_g[li], w_gate_up[li], w_down[li])
    return x2.reshape(batch, seq, d)
```

```python
import functools
import math

import jax
import jax.numpy as jnp
from jax import lax
from jax.experimental import pallas as pl
from jax.experimental.pallas import tpu as pltpu

F32 = jnp.float32
BF16 = jnp.bfloat16
I32 = jnp.int32
I16 = jnp.int16

D_MODEL = 1024
EPS = 1e-6
ROPE_DIM = 64
ROPE_HALF = ROPE_DIM // 2
ROPE_THETA = 10000.0

M_HEADS = 4
M_HEAD_DIM = 128
M_WIDTH = M_HEADS * M_HEAD_DIM
M_CONV = 4
M_CHUNK = 128
M_INIT = -1e30

S_HEADS = 8
S_WIDTH = S_HEADS * ROPE_DIM
S_KV_RANK = 256
IDX_HEADS = 4
IDX_WIDTH = IDX_HEADS * ROPE_DIM
IDX_TOPK_MAX = 256

DA_HEADS = 4
DA_MAPS = 2 * DA_HEADS
DA_V_DIM = 2 * ROPE_DIM
DA_WIDTH = DA_HEADS * DA_V_DIM

N_BRANCH = 3
FF_DIM = 2816

NEG = -1e30
LOG2E = math.log2(math.e)
INT_MIN = -(2**31)
HALF16 = 2**15
LANE = 128
HEAD_GROUP = 256
KSUB = 128
ONES_ROWS = 16
S_VROWS = ROPE_DIM + ONES_ROWS
DA_VROWS = DA_V_DIM + ONES_ROWS

W_IN_SIZES = (2 * M_WIDTH, M_WIDTH, M_WIDTH, 2 * M_HEADS, S_WIDTH, S_KV_RANK, IDX_WIDTH, ROPE_DIM,
              IDX_HEADS, 2 * DA_HEADS * ROPE_DIM, 2 * DA_HEADS * ROPE_DIM, DA_WIDTH, N_BRANCH * D_MODEL)
IN_COLS = sum(W_IN_SIZES)

COL_GATE = 0
COL_MQK = 3072
COL_MV = 4096
COL_MO = 4608
COL_SQ = 5120
COL_DQ = 5632
COL_DK = 6144
COL_DV = 6656
COL_CKV = 7168
COL_IQ = 7424
COL_IK4 = 7680
COL_MISC = 7936
MISC_IF = 0
MISC_IW = 8
Z_COLS = 8192

VMEM_LIMIT = 56 * 1024 * 1024


def _cparams(sem, flags=None):
    return pltpu.CompilerParams(dimension_semantics=sem, vmem_limit_bytes=VMEM_LIMIT, flags=flags)


def _sigmoid(x):
    return 1.0 / (1.0 + jnp.exp(-x))


def _dot(a, b):
    return jnp.dot(a, b, preferred_element_type=F32)


def _dot_nt(a, b):
    return lax.dot_general(a, b, (((1,), (1,)), ((), ())), preferred_element_type=F32)


def _dot_tn(a, b):
    return lax.dot_general(a, b, (((0,), (0,)), ((), ())), preferred_element_type=F32)


def _split3(x):
    hi = x.astype(BF16)
    r1 = x - hi.astype(F32)
    mid = r1.astype(BF16)
    lo = (r1 - mid.astype(F32)).astype(BF16)
    return hi, mid, lo


def _wlayout_kernel(w_ref, o_ref):
    w = w_ref[...]
    offs = [0]
    for s in W_IN_SIZES:
        offs.append(offs[-1] + s)
    seg = [w[:, offs[k]:offs[k + 1]] for k in range(len(W_IN_SIZES))]
    m_qk, m_v, m_o, m_if, s_q, s_ckv, i_q, i_k, i_w, d_q, d_k, d_v, g_pre = seg
    rows = w.shape[0]
    misc = jnp.concatenate([m_if, i_w, jnp.zeros((rows, LANE - 2 * M_HEADS - IDX_HEADS), F32)], axis=1)
    ik4 = jnp.concatenate([i_k] * IDX_HEADS, axis=1)
    tail = jnp.zeros((rows, Z_COLS - COL_MISC - LANE), F32)
    for col, val in ((COL_GATE, g_pre), (COL_MQK, m_qk), (COL_MV, m_v), (COL_MO, m_o), (COL_SQ, s_q),
                     (COL_DQ, d_q), (COL_DK, d_k), (COL_DV, d_v), (COL_CKV, s_ckv), (COL_IQ, i_q),
                     (COL_IK4, ik4), (COL_MISC, misc), (COL_MISC + LANE, tail)):
        o_ref[:, col:col + val.shape[1]] = val.astype(BF16)


def _wlayout(w_in, *, tr=256):
    depth, rows, cols = w_in.shape
    return pl.pallas_call(
        _wlayout_kernel,
        grid=(depth, rows // tr),
        in_specs=[pl.BlockSpec((None, tr, cols), lambda l, i: (l, i, 0))],
        out_specs=pl.BlockSpec((None, tr, Z_COLS), lambda l, i: (l, i, 0)),
        out_shape=jax.ShapeDtypeStruct((depth, rows, Z_COLS), BF16),
        compiler_params=_cparams(("parallel", "parallel")),
        name="w_layout",
    )(w_in)


def _inproj_kernel(x_ref, g_ref, w_ref, z_ref, misc_ref, *, tn):
    x = x_ref[...]
    ms = jnp.mean(x * x, axis=-1, keepdims=True)
    xn = (x * lax.rsqrt(ms + EPS) * g_ref[...]).astype(BF16)
    for c in range(Z_COLS // tn):
        r = _dot(xn, w_ref[:, c * tn:(c + 1) * tn])
        z_ref[:, c * tn:(c + 1) * tn] = r.astype(BF16)
        if c == COL_MISC // tn:
            misc_ref[...] = r[:, COL_MISC % tn:COL_MISC % tn + LANE]


def _inproj(x2, g, w, *, tm=512, tn=1024):
    n = x2.shape[0]
    tm = min(tm, n)
    return pl.pallas_call(
        functools.partial(_inproj_kernel, tn=tn),
        grid=(n // tm,),
        in_specs=[
            pl.BlockSpec((tm, D_MODEL), lambda i: (i, 0)),
            pl.BlockSpec((1, D_MODEL), lambda i: (0, 0)),
            pl.BlockSpec((D_MODEL, Z_COLS), lambda i: (0, 0), pipeline_mode=pl.Buffered(1)),
        ],
        out_specs=[pl.BlockSpec((tm, Z_COLS), lambda i: (i, 0)),
                   pl.BlockSpec((tm, LANE), lambda i: (i, 0))],
        out_shape=[jax.ShapeDtypeStruct((n, Z_COLS), BF16), jax.ShapeDtypeStruct((n, LANE), F32)],
        compiler_params=_cparams(("parallel",)),
        name="inproj",
    )(x2, g, w)


def _mlstm_kernel(qk_ref, v_ref, o_ref, misc_ref, cw_ref, cb_ref, gb_ref, ng_ref, tri_ref, y_ref,
                  tail_ref, qkc_ref, c_ref, n_ref, m_ref):
    L = M_CHUNK
    dh = M_HEAD_DIM

    @pl.when(pl.program_id(1) == 0)
    def _():
        tail_ref[...] = jnp.zeros_like(tail_ref)
        c_ref[...] = jnp.zeros_like(c_ref)
        n_ref[...] = jnp.zeros_like(n_ref)
        m_ref[...] = jnp.full_like(m_ref, M_INIT)

    x = qk_ref[...].astype(F32)
    tail = tail_ref[...]
    row8 = lax.broadcasted_iota(I32, (8, 1), 0)
    cw = cw_ref[...]
    cb = cb_ref[...]
    acc = x * cw[M_CONV - 1:M_CONV] + cb
    acc_head = x[0:8] * cw[M_CONV - 1:M_CONV] + cb
    for j in range(1, M_CONV):
        wj = cw[M_CONV - 1 - j:M_CONV - j]
        xr = pltpu.roll(x, j, axis=0)
        tr = pltpu.roll(tail, j, axis=0)
        acc = acc + xr * wj
        acc_head = acc_head + jnp.where(row8 < j, tr, xr[0:8]) * wj
    qkc_ref[...] = acc * _sigmoid(acc)
    qkc_ref[0:8, :] = acc_head * _sigmoid(acc_head)
    tail_ref[...] = x[L - 8:L]

    gates = misc_ref[...] + gb_ref[...]
    lf = jnp.minimum(gates, 0.0) - jnp.log(1.0 + jnp.exp(-jnp.abs(gates)))
    tri = tri_ref[...]
    hi, mid, lo = _split3(lf)
    bcum = _dot(tri, hi) + _dot(tri, mid) + _dot(tri, lo)
    bcum_t = bcum.T
    gates_t = gates.T

    t_idx = lax.broadcasted_iota(I32, (L, L), 0)
    s_idx = lax.broadcasted_iota(I32, (L, L), 1)
    causal = t_idx >= s_idx

    ys, cs, ns, ms_new = [], [], [], []
    for h in range(M_HEADS):
        sl = slice(h * dh, (h + 1) * dh)
        q = qkc_ref[:, h * dh:(h + 1) * dh]
        k = qkc_ref[:, M_WIDTH + h * dh:M_WIDTH + (h + 1) * dh] * (dh ** -0.5)
        vb = v_ref[:, sl]
        v = vb.astype(F32)
        qb, kb = q.astype(BF16), k.astype(BF16)

        b_col = bcum[:, M_HEADS + h:M_HEADS + h + 1]
        i_col = gates[:, h:h + 1]
        b_row = bcum_t[M_HEADS + h:M_HEADS + h + 1, :]
        i_row = gates_t[h:h + 1, :]
        g_tot = bcum[L - 1:L, M_HEADS + h:M_HEADS + h + 1]

        c_prev = c_ref[sl, :]
        n_prev = n_ref[h:h + 1, :]
        m_prev = m_ref[h:h + 1, 0:1]

        dmat = jnp.where(causal, b_col - b_row + i_row, -jnp.inf)
        inter = b_col + m_prev
        m_t = jnp.maximum(inter, jnp.max(dmat, axis=-1, keepdims=True))
        sw = jnp.exp(dmat - m_t) * _dot_nt(qb, kb)
        s_inter = jnp.exp(inter - m_t)
        num = _dot(sw.astype(BF16), vb) + s_inter * _dot_nt(qb, c_prev.astype(BF16))
        den = jnp.sum(sw, axis=-1, keepdims=True) + s_inter * jnp.sum(q * n_prev, axis=-1, keepdims=True)
        hh = num / jnp.maximum(jnp.abs(den), jnp.exp(-m_t))
        ms = jnp.mean(hh * hh, axis=-1, keepdims=True)
        hn = hh * lax.rsqrt(ms + EPS) * ng_ref[:, sl]
        ys.append(_sigmoid(o_ref[:, sl].astype(F32)) * hn)

        a_col = g_tot - b_col + i_col
        m_loc = jnp.max(a_col, axis=0, keepdims=True)
        w_loc = jnp.exp(a_col - m_loc)
        c_loc = _dot_tn((v * w_loc).astype(BF16), kb)
        n_loc = jnp.sum(k * w_loc, axis=0, keepdims=True)
        m_new = jnp.maximum(g_tot + m_prev, m_loc)
        s_old = jnp.exp(g_tot + m_prev - m_new)
        s_loc = jnp.exp(m_loc - m_new)
        cs.append(s_old * c_prev + s_loc * c_loc)
        ns.append(s_old * n_prev + s_loc * n_loc)
        ms_new.append(jnp.broadcast_to(m_new, (1, LANE)))

    y_ref[...] = jnp.concatenate(ys, axis=1).astype(BF16)
    c_ref[...] = jnp.concatenate(cs, axis=0)
    n_ref[...] = jnp.concatenate(ns, axis=0)
    m_ref[...] = jnp.concatenate(ms_new, axis=0)


def _mlstm(z, misc, conv_w, conv_b, gate_b_row, norm_g_row, tri, batch, seq):
    n = z.shape[0]
    nc = seq // M_CHUNK
    L = M_CHUNK
    row = lambda b, c: b * nc + c
    return pl.pallas_call(
        _mlstm_kernel,
        grid=(batch, nc),
        in_specs=[
            pl.BlockSpec((L, 2 * M_WIDTH), lambda b, c: (row(b, c), COL_MQK // (2 * M_WIDTH))),
            pl.BlockSpec((L, M_WIDTH), lambda b, c: (row(b, c), COL_MV // M_WIDTH)),
            pl.BlockSpec((L, M_WIDTH), lambda b, c: (row(b, c), COL_MO // M_WIDTH)),
            pl.BlockSpec((L, LANE), lambda b, c: (row(b, c), 0)),
            pl.BlockSpec((M_CONV, 2 * M_WIDTH), lambda b, c: (0, 0)),
            pl.BlockSpec((1, 2 * M_WIDTH), lambda b, c: (0, 0)),
            pl.BlockSpec((1, LANE), lambda b, c: (0, 0)),
            pl.BlockSpec((1, M_WIDTH), lambda b, c: (0, 0)),
            pl.BlockSpec((L, L), lambda b, c: (0, 0)),
        ],
        out_specs=pl.BlockSpec((L, M_WIDTH), lambda b, c: (row(b, c), 0)),
        out_shape=jax.ShapeDtypeStruct((n, M_WIDTH), BF16),
        scratch_shapes=[
            pltpu.VMEM((8, 2 * M_WIDTH), F32),
            pltpu.VMEM((L, 2 * M_WIDTH), F32),
            pltpu.VMEM((M_HEADS * M_HEAD_DIM, M_HEAD_DIM), F32),
            pltpu.VMEM((M_HEADS, M_HEAD_DIM), F32),
            pltpu.VMEM((M_HEADS, LANE), F32),
        ],
        compiler_params=_cparams(("parallel", "arbitrary")),
        name="mlstm",
    )(z, z, z, misc, conv_w, conv_b, gate_b_row, norm_g_row, tri)


def _rope(x, c, s):
    w = x.shape[-1]
    lane = lax.broadcasted_iota(I32, (1, w), 1)
    fwd = pltpu.roll(x, ROPE_HALF, axis=1)
    bwd = pltpu.roll(x, w - ROPE_HALF, axis=1)
    swapped = jnp.where((lane & (ROPE_DIM - 1)) < ROPE_HALF, bwd, fwd)
    return x * c + swapped * s


def _head_rms(x, g, grp):
    sq = x * x
    hi = sq.astype(BF16)
    lo = (sq - hi.astype(F32)).astype(BF16)
    ss = _dot(hi, grp) + _dot(lo, grp)
    return x * lax.rsqrt(ss * (1.0 / ROPE_DIM) + EPS) * g


def _store_transposed(o_ref, v, dv):
    ones = jnp.ones((ONES_ROWS, KSUB), BF16)
    for j in range(o_ref.shape[0]):
        vt = v[j * KSUB:(j + 1) * KSUB, :].T.astype(BF16)
        parts = []
        for h in range(vt.shape[0] // dv):
            parts += [vt[h * dv:(h + 1) * dv, :], ones]
        o_ref[j] = jnp.concatenate(parts, axis=0)


def _prep_kernel(sq_ref, dq_ref, dk_ref, dv_ref, ckv_ref, iq_ref, ik_ref, c_ref, s_ref, grp_ref,
                 kvg_ref, wkv_ref, sqg_ref, skg_ref, dqg_ref, dkg_ref,
                 sq_o, sk_o, sv_o, iq_o, ik_o, dq_o, dk_o, dv_o):
    c256 = c_ref[...]
    s256 = s_ref[...]
    c512 = jnp.concatenate([c256, c256], axis=1)
    s512 = jnp.concatenate([s256, s256], axis=1)
    grp = grp_ref[...]
    qscale = ROPE_DIM ** -0.5 * LOG2E

    f32 = lambda ref: ref[...].astype(F32)

    sq_o[...] = (_rope(_head_rms(f32(sq_ref), sqg_ref[...], grp), c512, s512) * qscale).astype(BF16)

    ckv = f32(ckv_ref)
    ms = jnp.mean(ckv * ckv, axis=-1, keepdims=True)
    ckvn = (ckv * lax.rsqrt(ms + EPS) * kvg_ref[...]).astype(BF16)
    kv = _dot(ckvn, wkv_ref[...])
    sk_o[...] = _rope(_head_rms(kv[:, :S_WIDTH], skg_ref[...], grp), c512, s512).astype(BF16)
    _store_transposed(sv_o, kv[:, S_WIDTH:], ROPE_DIM)

    iq_o[...] = _rope(f32(iq_ref), c256, s256).astype(BF16)
    ik_o[...] = _rope(f32(ik_ref), c256, s256).astype(BF16)

    dq_o[...] = (_rope(_head_rms(f32(dq_ref), dqg_ref[...], grp), c512, s512) * qscale).astype(BF16)
    dk_o[...] = _rope(_head_rms(f32(dk_ref), dkg_ref[...], grp), c512, s512).astype(BF16)
    _store_transposed(dv_o, f32(dv_ref), DA_V_DIM)


def _prep(z, rope_c, rope_s, grp, kv_g, w_kv, sq_g, sk_g, dq_g, dk_g, seq, *, tm=512):
    n = z.shape[0]
    tm = min(tm, seq)
    nt = seq // tm
    zspec = lambda w, col: pl.BlockSpec((tm, w), lambda i: (i, col // w))
    const = lambda shape: pl.BlockSpec(shape, lambda i: (0, 0))
    o512 = pl.BlockSpec((tm, 512), lambda i: (i, 0))
    o256 = pl.BlockSpec((tm, 256), lambda i: (i, 0))
    s512 = jax.ShapeDtypeStruct((n, 512), BF16)
    s256 = jax.ShapeDtypeStruct((n, 256), BF16)
    tspec = lambda rows: pl.BlockSpec((tm // KSUB, rows, KSUB), lambda i: (i, 0, 0))
    tshape = lambda rows: jax.ShapeDtypeStruct((n // KSUB, rows, KSUB), BF16)
    sv_rows, dv_rows = S_HEADS * S_VROWS, DA_HEADS * DA_VROWS
    return pl.pallas_call(
        _prep_kernel,
        grid=(n // tm,),
        in_specs=[
            zspec(512, COL_SQ), zspec(512, COL_DQ), zspec(512, COL_DK), zspec(512, COL_DV),
            zspec(256, COL_CKV), zspec(256, COL_IQ), zspec(256, COL_IK4),
            pl.BlockSpec((tm, 256), lambda i: (i % nt, 0)),
            pl.BlockSpec((tm, 256), lambda i: (i % nt, 0)),
            const((512, 512)),
            const((1, S_KV_RANK)), const((S_KV_RANK, 2 * S_WIDTH)),
            const((1, 512)), const((1, 512)), const((1, 512)), const((1, 512)),
        ],
        out_specs=[o512, o512, tspec(sv_rows), o256, o256, o512, o512, tspec(dv_rows)],
        out_shape=[s512, s512, tshape(sv_rows), s256, s256, s512, s512, tshape(dv_rows)],
        compiler_params=_cparams(("parallel",)),
        name="attn_prep",
    )(z, z, z, z, z, z, z, rope_c, rope_s, grp, kv_g, w_kv, sq_g, sk_g, dq_g, dk_g)


def _mask_heads(q_ref, qm_ref, n_heads):
    lane_grp = lax.broadcasted_iota(I32, (1, HEAD_GROUP), 1) // ROPE_DIM
    for h in range(n_heads):
        g = h // 4
        qg = q_ref[:, g * HEAD_GROUP:(g + 1) * HEAD_GROUP].astype(F32)
        qm_ref[h] = jnp.where(lane_grp == (h % 4), qg, 0.0).T.astype(BF16)


def _softmax_group(logits, values, rows, acc_ref, m_ref):
    nh = len(logits)
    m_old = m_ref[...]
    m_new = jnp.maximum(m_old, jnp.concatenate([jnp.max(s, axis=0, keepdims=True) for s in logits], axis=0))
    alpha = jnp.exp2(m_old - m_new)
    ps = [jnp.exp2((s - m_new[j:j + 1, :]).astype(BF16)) for j, s in enumerate(logits)]
    m_ref[...] = m_new
    acc_old = acc_ref[...]
    acc_ref[...] = jnp.concatenate(
        [alpha[j:j + 1, :] * acc_old[j * rows:(j + 1) * rows, :] + _dot(values[j], ps[j])
         for j in range(nh)], axis=0)


def _attend(n_blocks, logits_fn, values_fn, buf0_ref, buf1_ref, rows, acc_ref, m_ref):
    last = n_blocks - 1

    def logits_stage(j, dst_ref):
        pen = jnp.where(j <= last, 0.0, NEG).astype(F32)
        for h, s in enumerate(logits_fn(jnp.minimum(j, last), pen)):
            dst_ref[h] = s

    def softmax_stage(j, src_ref):
        _softmax_group([src_ref[h] for h in range(src_ref.shape[0])], values_fn(jnp.minimum(j, last)),
                       rows, acc_ref, m_ref)

    logits_stage(0, buf0_ref)

    def body(jj, carry):
        logits_stage(2 * jj + 1, buf1_ref)
        softmax_stage(2 * jj, buf0_ref)
        logits_stage(2 * jj + 2, buf0_ref)
        softmax_stage(2 * jj + 1, buf1_ref)
        return carry

    lax.fori_loop(0, (n_blocks + 1) // 2, body, 0)


def _sortable(x):
    bits = pltpu.bitcast(x, I32)
    return bits ^ ((bits >> 31) & 0x7FFFFFFF)


def _tree_sum(parts):
    while len(parts) > 1:
        parts = [parts[j] + parts[j + 1] for j in range(0, len(parts), 2)]
    return parts[0]


def _bit_planes(words):
    a = list(words)
    j, m = 16, 0x0000FFFF
    while j:
        k = 0
        while k < 32:
            t = (a[k] ^ lax.shift_right_logical(a[k + j], jnp.int32(j))) & m
            a[k] = a[k] ^ t
            a[k + j] = a[k + j] ^ (t << j)
            k = (k + j + 1) & ~j
        j >>= 1
        m = (m ^ (m << j)) & 0xFFFFFFFF
    return a


def _dsa_kernel(sq_ref, iq_ref, misc_ref, sk_ref, svt_ref, ik_ref, y_ref,
                sc_ref, bp_ref, qm_ref, iqm_ref, acc_ref, m_ref, sa_ref, sb_ref,
                *, tq, topk, idx_bits):
    i = pl.program_id(1)

    @pl.when((pl.program_id(0) == 0) & (i == 0))
    def _():
        bp_ref[...] = jnp.zeros_like(bp_ref)

    nks = (i + 1) * (tq // KSUB)
    q_idx = i * tq + lax.broadcasted_iota(I32, (1, tq), 1)
    k_sub = lax.broadcasted_iota(I32, (KSUB, 1), 0)

    _mask_heads(iq_ref, iqm_ref, IDX_HEADS)
    w_t = (misc_ref[...] * (IDX_WIDTH ** -0.5)).T
    w_rows = [w_t[MISC_IW + h:MISC_IW + h + 1, :] for h in range(IDX_HEADS)]

    k_pair = lax.broadcasted_iota(I32, (2 * KSUB, 1), 0)

    def score_body(j, carry):
        ikb = ik_ref[pl.ds(pl.multiple_of(j * (2 * KSUB), 2 * KSUB), 2 * KSUB), :]
        s = jnp.zeros((2 * KSUB, tq), F32)
        for h in range(IDX_HEADS):
            s = s + w_rows[h] * jnp.maximum(_dot(ikb, iqm_ref[h]), 0.0)
        key = jnp.where(j * (2 * KSUB) + k_pair <= q_idx, _sortable(s), INT_MIN)
        sc_ref[pl.ds(2 * j, 2)] = key.reshape(2, KSUB, tq)
        ob = key ^ INT_MIN
        bp_ref[j] = jnp.concatenate(_bit_planes([ob[v * 8:(v + 1) * 8, :] for v in range(32)]), axis=0)
        return carry

    lax.fori_loop(0, nks // 2, score_body, 0)

    nblk = bp_ref.shape[0]
    live_rows = lax.broadcasted_iota(I32, (nblk * 8, 1), 0) < (nks // 2) * 8

    def bit_body(t, carry):
        alive, need, thr_ob = carry
        plane = bp_ref[:, pl.ds(pl.multiple_of(t * 8, 8), 8), :].reshape(nblk * 8, tq)
        ones = alive & plane
        cnt = jnp.sum(lax.population_count(ones), axis=0, keepdims=True)
        take = cnt >= need
        alive = jnp.where(take, ones, alive ^ ones)
        need = jnp.where(take, need, need - cnt)
        thr_ob = thr_ob | jnp.where(take, jnp.left_shift(jnp.int32(1), 31 - t), 0)
        return alive, need, thr_ob

    alive, need, thr_ob = lax.fori_loop(
        0, 32, bit_body,
        (jnp.broadcast_to(jnp.where(live_rows, -1, 0), (nblk * 8, tq)),
         jnp.full((1, tq), topk, I32), jnp.zeros((1, tq), I32)))
    thr = thr_ob ^ INT_MIN
    n_ties = jnp.sum(lax.population_count(alive), axis=0, keepdims=True)

    def count32(pred):
        def body(ks, acc):
            c = jnp.where(pred(sc_ref[ks], ks * KSUB + k_sub), 1, 0)
            return acc + _tree_sum([c[r * 8:(r + 1) * 8] for r in range(KSUB // 8)])
        acc = lax.fori_loop(0, nks, body, jnp.zeros((8, tq), I32))
        return jnp.sum(acc, axis=0, keepdims=True)

    m_ref[0:1, :] = jnp.full((1, tq), 2.0**30, F32)

    @pl.when(jnp.max(n_ties - need) > 0)
    def _():
        def idx_body(t, cut):
            bit = jnp.left_shift(jnp.int32(1), idx_bits - 1 - t)
            trial = cut | bit
            cnt = count32(lambda blk, kidx: (blk == thr) & (kidx < trial))
            return jnp.where(cnt < need, trial, cut)
        cut = lax.fori_loop(0, idx_bits, idx_body, jnp.zeros((1, tq), I32))
        m_ref[0:1, :] = cut.astype(F32)

    cut = m_ref[0:1, :].astype(I32)

    thr_sel = jnp.maximum(thr, INT_MIN + 1)

    def bias_body(j, carry):
        pair = pl.ds(2 * j, 2)
        key = sc_ref[pair].reshape(2 * KSUB, tq)
        kidx = j * (2 * KSUB) + k_pair
        sel = key >= thr_sel + jnp.where(kidx <= cut, 0, 1)
        sc_ref[pair] = pltpu.bitcast(jnp.where(sel, 0.0, NEG).astype(F32), I32).reshape(2, KSUB, tq)
        return carry

    lax.fori_loop(0, nks // 2, bias_body, 0)

    _mask_heads(sq_ref, qm_ref, S_HEADS)
    acc_ref[...] = jnp.zeros_like(acc_ref)
    m_ref[...] = jnp.full_like(m_ref, NEG)

    def logits_fn(ks, pen):
        off = pl.multiple_of(ks * KSUB, KSUB)
        bias = pltpu.bitcast(sc_ref[ks], F32) + pen
        kgs = [sk_ref[pl.ds(off, KSUB), g * HEAD_GROUP:(g + 1) * HEAD_GROUP] for g in range(S_HEADS // 4)]
        return [_dot(kgs[h // 4], qm_ref[h]) + bias for h in range(S_HEADS)]

    def values_fn(ks):
        return [svt_ref[ks, h * S_VROWS:(h + 1) * S_VROWS, :] for h in range(S_HEADS)]

    _attend(nks, logits_fn, values_fn, sa_ref, sb_ref, S_VROWS, acc_ref, m_ref)

    outs = []
    for h in range(S_HEADS):
        r0 = h * S_VROWS
        outs.append(acc_ref[r0:r0 + ROPE_DIM, :] / acc_ref[r0 + ROPE_DIM:r0 + ROPE_DIM + 1, :])
    y_ref[...] = jnp.concatenate(outs, axis=0).T.astype(BF16)


def _dsa(misc, sq, sk, svt, iq, ik4, batch, seq, *, tq=256):
    n = misc.shape[0]
    tq = min(tq, seq)
    nq = seq // tq
    nsub = seq // KSUB
    topk = min(IDX_TOPK_MAX, seq // 4)
    kern = functools.partial(_dsa_kernel, tq=tq, topk=topk, idx_bits=seq.bit_length())
    return pl.pallas_call(
        kern,
        grid=(batch, nq),
        in_specs=[
            pl.BlockSpec((tq, S_WIDTH), lambda b, i: (b * nq + i, 0)),
            pl.BlockSpec((tq, IDX_WIDTH), lambda b, i: (b * nq + i, 0)),
            pl.BlockSpec((tq, LANE), lambda b, i: (b * nq + i, 0)),
            pl.BlockSpec((seq, S_WIDTH), lambda b, i: (b, 0)),
            pl.BlockSpec((nsub, S_HEADS * S_VROWS, KSUB), lambda b, i: (b, 0, 0)),
            pl.BlockSpec((seq, IDX_WIDTH), lambda b, i: (b, 0)),
        ],
        out_specs=pl.BlockSpec((tq, S_WIDTH), lambda b, i: (b * nq + i, 0)),
        out_shape=jax.ShapeDtypeStruct((n, S_WIDTH), BF16),
        scratch_shapes=[
            pltpu.VMEM((nsub, KSUB, tq), I32),
            pltpu.VMEM((nsub // 2, 2 * KSUB, tq), I32),
            pltpu.VMEM((S_HEADS, HEAD_GROUP, tq), BF16),
            pltpu.VMEM((IDX_HEADS, HEAD_GROUP, tq), BF16),
            pltpu.VMEM((S_HEADS * S_VROWS, tq), F32),
            pltpu.VMEM((S_HEADS, tq), F32),
            pltpu.VMEM((S_HEADS, KSUB, tq), F32),
            pltpu.VMEM((S_HEADS, KSUB, tq), F32),
        ],
        compiler_params=_cparams(("arbitrary", "arbitrary")),
        name="dsa_attn",
    )(sq, iq, misc, sk, svt, ik4)


def _diff_kernel(q_ref, k_ref, vt_ref, lam_ref, og_ref, y_ref, qm_ref, acc_ref, m_ref,
                 sa_ref, sb_ref, *, tq, lam_init):
    i = pl.program_id(1)
    q_idx = i * tq + lax.broadcasted_iota(I32, (1, tq), 1)
    k_pair = lax.broadcasted_iota(I32, (2 * KSUB, 1), 0)

    _mask_heads(q_ref, qm_ref, DA_MAPS)
    acc_ref[...] = jnp.zeros_like(acc_ref)
    m_ref[...] = jnp.full_like(m_ref, NEG)

    def logits_fn(j, pen):
        off = pl.multiple_of(j * (2 * KSUB), 2 * KSUB)
        bias = jnp.where(j * (2 * KSUB) + k_pair <= q_idx, 0.0, NEG).astype(F32) + pen
        kgs = [k_ref[pl.ds(off, 2 * KSUB), g * HEAD_GROUP:(g + 1) * HEAD_GROUP] for g in range(DA_MAPS // 4)]
        return [_dot(kgs[m // 4], qm_ref[m]) + bias for m in range(DA_MAPS)]

    def values_fn(j):
        heads = [jnp.concatenate([vt_ref[2 * j, hd * DA_VROWS:(hd + 1) * DA_VROWS, :],
                                  vt_ref[2 * j + 1, hd * DA_VROWS:(hd + 1) * DA_VROWS, :]], axis=1)
                 for hd in range(DA_HEADS)]
        return [heads[m // 2] for m in range(DA_MAPS)]

    assert tq == 2 * KSUB
    _attend(i + 1, logits_fn, values_fn, sa_ref, sb_ref, DA_VROWS, acc_ref, m_ref)

    lam = lam_ref[...]
    p01 = jnp.sum(lam[0:1] * lam[1:2], axis=-1, keepdims=True)
    p23 = jnp.sum(lam[2:3] * lam[3:4], axis=-1, keepdims=True)
    lam_val = jnp.exp(p01) - jnp.exp(p23) + lam_init
    def normalised(m):
        r0 = m * DA_VROWS
        return acc_ref[r0:r0 + DA_V_DIM, :] / acc_ref[r0 + DA_V_DIM:r0 + DA_V_DIM + 1, :]

    outs = []
    for hd in range(DA_HEADS):
        o = normalised(2 * hd) - lam_val * normalised(2 * hd + 1)
        ms = jnp.mean(o * o, axis=0, keepdims=True)
        outs.append(o * lax.rsqrt(ms + EPS) * og_ref[...] * (1.0 - lam_init))
    y_ref[...] = jnp.concatenate(outs, axis=0).T.astype(BF16)


def _diff_attn(dq, dk, dvt, lam, out_g_col, batch, seq, lam_init, *, tq=256):
    n = dq.shape[0]
    tq = min(tq, seq)
    nq = seq // tq
    nsub = seq // KSUB
    kern = functools.partial(_diff_kernel, tq=tq, lam_init=lam_init)
    return pl.pallas_call(
        kern,
        grid=(batch, nq),
        in_specs=[
            pl.BlockSpec((tq, 512), lambda b, i: (b * nq + i, 0)),
            pl.BlockSpec((seq, 512), lambda b, i: (b, 0)),
            pl.BlockSpec((nsub, DA_HEADS * DA_VROWS, KSUB), lambda b, i: (b, 0, 0)),
            pl.BlockSpec((4, ROPE_DIM), lambda b, i: (0, 0)),
            pl.BlockSpec((DA_V_DIM, 1), lambda b, i: (0, 0)),
        ],
        out_specs=pl.BlockSpec((tq, DA_WIDTH), lambda b, i: (b * nq + i, 0)),
        out_shape=jax.ShapeDtypeStruct((n, DA_WIDTH), BF16),
        scratch_shapes=[
            pltpu.VMEM((DA_MAPS, HEAD_GROUP, tq), BF16),
            pltpu.VMEM((DA_MAPS * DA_VROWS, tq), F32),
            pltpu.VMEM((DA_MAPS, tq), F32),
            pltpu.VMEM((DA_MAPS, 2 * KSUB, tq), F32),
            pltpu.VMEM((DA_MAPS, 2 * KSUB, tq), F32),
        ],
        compiler_params=_cparams(("parallel", "arbitrary")),
        name="diff_attn",
    )(dq, dk, dvt, lam, out_g_col)


def _merge_kernel(x_ref, ya_ref, yb_ref, yc_ref, gp_ref, bg_ref, wb_ref, wo_ref, ng_ref, xo_ref, xn_ref):
    merged = None
    for br, y_ref in enumerate((ya_ref, yb_ref, yc_ref)):
        sl = slice(br * D_MODEL, (br + 1) * D_MODEL)
        gate = _sigmoid(gp_ref[:, sl].astype(F32) + bg_ref[:, sl])
        term = gate * _dot(y_ref[...], wb_ref[br])
        merged = term if merged is None else merged + term
    xo = x_ref[...] + _dot(merged.astype(BF16), wo_ref[...])
    xo_ref[...] = xo
    ms = jnp.mean(xo * xo, axis=-1, keepdims=True)
    xn_ref[...] = (xo * lax.rsqrt(ms + EPS) * ng_ref[...]).astype(BF16)


def _merge(x2, ya, yb, yc, z, b_gate, w_branch, w_out, norm_g, *, tm=512):
    n = x2.shape[0]
    tm = min(tm, n)
    row = lambda w: pl.BlockSpec((tm, w), lambda i: (i, 0))
    return pl.pallas_call(
        _merge_kernel,
        grid=(n // tm,),
        in_specs=[
            row(D_MODEL), row(512), row(512), row(512),
            pl.BlockSpec((tm, N_BRANCH * D_MODEL), lambda i: (i, COL_GATE)),
            pl.BlockSpec((1, N_BRANCH * D_MODEL), lambda i: (0, 0)),
            pl.BlockSpec((N_BRANCH, 512, D_MODEL), lambda i: (0, 0, 0)),
            pl.BlockSpec((D_MODEL, D_MODEL), lambda i: (0, 0)),
            pl.BlockSpec((1, D_MODEL), lambda i: (0, 0)),
        ],
        out_specs=[row(D_MODEL), row(D_MODEL)],
        out_shape=[jax.ShapeDtypeStruct((n, D_MODEL), F32), jax.ShapeDtypeStruct((n, D_MODEL), BF16)],
        compiler_params=_cparams(("parallel",)),
        name="merge_out",
    )(x2, ya, yb, yc, z, b_gate, w_branch, w_out, norm_g)


def _ffn_kernel(xn_ref, x_ref, wg_ref, wu_ref, wd_ref, o_ref, *, fc):
    xn = xn_ref[...]
    acc = x_ref[...]
    for c in range(FF_DIM // fc):
        cols = slice(c * fc, (c + 1) * fc)
        g = _dot(xn, wg_ref[:, cols])
        u = _dot(xn, wu_ref[:, cols])
        acc = acc + _dot((g * _sigmoid(g) * u).astype(BF16), wd_ref[cols, :])
    o_ref[...] = acc


def _ffn(xn, x2, wg, wu, wd, *, tm=512, fc=256):
    n = x2.shape[0]
    tm = min(tm, n)
    resident = lambda shape: pl.BlockSpec(shape, lambda i: (0, 0), pipeline_mode=pl.Buffered(1))
    return pl.pallas_call(
        functools.partial(_ffn_kernel, fc=fc),
        grid=(n // tm,),
        in_specs=[
            pl.BlockSpec((tm, D_MODEL), lambda i: (i, 0)),
            pl.BlockSpec((tm, D_MODEL), lambda i: (i, 0)),
            resident((D_MODEL, FF_DIM)),
            resident((D_MODEL, FF_DIM)),
            resident((FF_DIM, D_MODEL)),
        ],
        out_specs=pl.BlockSpec((tm, D_MODEL), lambda i: (i, 0)),
        out_shape=jax.ShapeDtypeStruct((n, D_MODEL), F32),
        compiler_params=_cparams(("parallel",)),
        name="ffn",
    )(xn, x2, wg, wu, wd)


def _rope_tables(seq):
    inv = 1.0 / jnp.power(ROPE_THETA, jnp.arange(0, ROPE_DIM, 2, dtype=F32) / ROPE_DIM)
    ang = jnp.arange(seq, dtype=F32)[:, None] * inv[None, :]
    cos, sin = jnp.cos(ang), jnp.sin(ang)
    c64 = jnp.concatenate([cos, cos], axis=1)
    s64 = jnp.concatenate([-sin, sin], axis=1)
    return jnp.tile(c64, (1, 4)), jnp.tile(s64, (1, 4))


def _layer(x2, li, batch, seq, consts, norm_mix_g, w_in_l, b_gate, conv_w, conv_b, gate_b, m_norm_g,
           kv_norm_g, w_kv_up, sq_g, sk_g, dq_g, dk_g, lam, d_out_g, w_branch, w_out,
           norm_ffn_g, w_gate_up, w_down):
    rope_c, rope_s, grp, tri = consts
    lam_init = 0.8 - 0.6 * math.exp(-0.3 * li)
    tile8 = lambda g: jnp.tile(g, 8)[None, :]

    z, misc = _inproj(x2, norm_mix_g[None, :], w_in_l)

    gate_row = jnp.zeros((1, LANE), F32).at[0, MISC_IF:MISC_IF + 2 * M_HEADS].set(gate_b)
    ya = _mlstm(z, misc, conv_w, conv_b[None, :], gate_row, m_norm_g.reshape(1, M_WIDTH), tri, batch, seq)

    sq, sk, svt, iq, ik4, dq, dk, dvt = _prep(
        z, rope_c, rope_s, grp, kv_norm_g[None, :], w_kv_up.astype(BF16),
        tile8(sq_g), tile8(sk_g), tile8(dq_g), tile8(dk_g), seq)
    yb = _dsa(misc, sq, sk, svt, iq, ik4, batch, seq)
    yc = _diff_attn(dq, dk, dvt, lam, d_out_g[:, None], batch, seq, lam_init)

    xo, xn = _merge(x2, ya, yb, yc, z, b_gate[None, :], w_branch.astype(BF16), w_out.astype(BF16),
                    norm_ffn_g[None, :])
    return _ffn(xn, xo, w_gate_up[:, :FF_DIM].astype(BF16), w_gate_up[:, FF_DIM:].astype(BF16),
                w_down.astype(BF16))


def kernel(x, norm_mix_g, w_in, b_gate, mlstm_conv_w, mlstm_conv_b, mlstm_gate_b, mlstm_norm_g,
           dsa_kv_norm_g, dsa_w_kv_up, dsa_q_norm_g, dsa_k_norm_g, diff_q_norm_g, diff_k_norm_g,
           diff_lambda, diff_out_norm_g, w_branch, w_out, norm_ffn_g, w_gate_up, w_down):
    batch, seq, d = x.shape
    depth = w_in.shape[0]
    assert w_in.shape[2] == IN_COLS and d == D_MODEL
    rope_c, rope_s = _rope_tables(seq)
    gi = jnp.arange(512) // ROPE_DIM
    grp = (gi[:, None] == gi[None, :]).astype(BF16)
    ti = jnp.arange(M_CHUNK)
    tri = (ti[:, None] >= ti[None, :]).astype(BF16)
    consts = (rope_c, rope_s, grp, tri)
    w_in_l = _wlayout(w_in)
    x2 = x.reshape(batch * seq, d)
    for li in range(depth):
        x2 = _layer(x2, li, batch, seq, consts, norm_mix_g[li], w_in_l[li], b_gate[li], mlstm_conv_w[li],
                    mlstm_conv_b[li], mlstm_gate_b[li], mlstm_norm_g[li], dsa_kv_norm_g[li],
                    dsa_w_kv_up[li], dsa_q_norm_g[li], dsa_k_norm_g[li], diff_q_norm_g[li],
                    diff_k_norm_g[li], diff_lambda[li], diff_out_norm_g[li], w_branch[li], w_out[li],
                    norm_ffn_g[li], w_gate_up[li], w_down[li])
    return x2.reshape(batch, seq, d)
```

```python
import functools
import math

import jax
import jax.numpy as jnp
from jax import lax
from jax.experimental import pallas as pl
from jax.experimental.pallas import tpu as pltpu

F32 = jnp.float32
BF16 = jnp.bfloat16
I32 = jnp.int32
I16 = jnp.int16

D_MODEL = 1024
EPS = 1e-6
ROPE_DIM = 64
ROPE_HALF = ROPE_DIM // 2
ROPE_THETA = 10000.0

M_HEADS = 4
M_HEAD_DIM = 128
M_WIDTH = M_HEADS * M_HEAD_DIM
M_CONV = 4
M_CHUNK = 128
M_TAIL = 16
M_INIT = -1e30

S_HEADS = 8
S_WIDTH = S_HEADS * ROPE_DIM
S_KV_RANK = 256
IDX_HEADS = 4
IDX_WIDTH = IDX_HEADS * ROPE_DIM
IDX_TOPK_MAX = 256

DA_HEADS = 4
DA_MAPS = 2 * DA_HEADS
DA_V_DIM = 2 * ROPE_DIM
DA_WIDTH = DA_HEADS * DA_V_DIM

N_BRANCH = 3
FF_DIM = 2816

NEG = -1e30
LOG2E = math.log2(math.e)
INT_MIN = -(2**31)
HALF16 = 2**15
LANE = 128
HEAD_GROUP = 256
KSUB = 128
ONES_ROWS = 16
S_VROWS = ROPE_DIM + ONES_ROWS
DA_VROWS = DA_V_DIM + ONES_ROWS

W_IN_SIZES = (2 * M_WIDTH, M_WIDTH, M_WIDTH, 2 * M_HEADS, S_WIDTH, S_KV_RANK, IDX_WIDTH, ROPE_DIM,
              IDX_HEADS, 2 * DA_HEADS * ROPE_DIM, 2 * DA_HEADS * ROPE_DIM, DA_WIDTH, N_BRANCH * D_MODEL)
IN_COLS = sum(W_IN_SIZES)

COL_GATE = 0
COL_MQK = 3072
COL_MV = 4096
COL_MO = 4608
COL_SQ = 5120
COL_DQ = 5632
COL_DK = 6144
COL_DV = 6656
COL_CKV = 7168
COL_IQ = 7424
COL_IK4 = 7680
COL_MISC = 7936
MISC_IF = 0
MISC_IW = 8
Z_COLS = 8192

VMEM_LIMIT = 56 * 1024 * 1024


def _cparams(sem, flags=None):
    return pltpu.CompilerParams(dimension_semantics=sem, vmem_limit_bytes=VMEM_LIMIT, flags=flags)


def _sigmoid(x):
    return 1.0 / (1.0 + jnp.exp(-x))


def _dot(a, b):
    return jnp.dot(a, b, preferred_element_type=F32)


def _dot_nt(a, b):
    return lax.dot_general(a, b, (((1,), (1,)), ((), ())), preferred_element_type=F32)


def _dot_tn(a, b):
    return lax.dot_general(a, b, (((0,), (0,)), ((), ())), preferred_element_type=F32)


def _split3(x):
    hi = x.astype(BF16)
    r1 = x - hi.astype(F32)
    mid = r1.astype(BF16)
    lo = (r1 - mid.astype(F32)).astype(BF16)
    return hi, mid, lo


def _wlayout_kernel(w_ref, o_ref):
    w = w_ref[...]
    offs = [0]
    for s in W_IN_SIZES:
        offs.append(offs[-1] + s)
    seg = [w[:, offs[k]:offs[k + 1]] for k in range(len(W_IN_SIZES))]
    m_qk, m_v, m_o, m_if, s_q, s_ckv, i_q, i_k, i_w, d_q, d_k, d_v, g_pre = seg
    rows = w.shape[0]
    misc = jnp.concatenate([m_if, i_w, jnp.zeros((rows, LANE - 2 * M_HEADS - IDX_HEADS), F32)], axis=1)
    ik4 = jnp.concatenate([i_k] * IDX_HEADS, axis=1)
    tail = jnp.zeros((rows, Z_COLS - COL_MISC - LANE), F32)
    for col, val in ((COL_GATE, g_pre), (COL_MQK, m_qk), (COL_MV, m_v), (COL_MO, m_o), (COL_SQ, s_q),
                     (COL_DQ, d_q), (COL_DK, d_k), (COL_DV, d_v), (COL_CKV, s_ckv), (COL_IQ, i_q),
                     (COL_IK4, ik4), (COL_MISC, misc), (COL_MISC + LANE, tail)):
        o_ref[:, col:col + val.shape[1]] = val.astype(BF16)


def _wlayout(w_in, *, tr=256):
    depth, rows, cols = w_in.shape
    return pl.pallas_call(
        _wlayout_kernel,
        grid=(depth, rows // tr),
        in_specs=[pl.BlockSpec((None, tr, cols), lambda l, i: (l, i, 0))],
        out_specs=pl.BlockSpec((None, tr, Z_COLS), lambda l, i: (l, i, 0)),
        out_shape=jax.ShapeDtypeStruct((depth, rows, Z_COLS), BF16),
        compiler_params=_cparams(("parallel", "parallel")),
        name="w_layout",
    )(w_in)


def _inproj_kernel(x_ref, g_ref, w_ref, z_ref, misc_ref, *, tn):
    x = x_ref[...]
    ms = jnp.mean(x * x, axis=-1, keepdims=True)
    xn = (x * lax.rsqrt(ms + EPS) * g_ref[...]).astype(BF16)
    for c in range(Z_COLS // tn):
        r = _dot(xn, w_ref[:, c * tn:(c + 1) * tn])
        z_ref[:, c * tn:(c + 1) * tn] = r.astype(BF16)
        if c == COL_MISC // tn:
            misc_ref[...] = r[:, COL_MISC % tn:COL_MISC % tn + LANE]


def _inproj(x2, g, w, *, tm=512, tn=1024):
    n = x2.shape[0]
    tm = min(tm, n)
    return pl.pallas_call(
        functools.partial(_inproj_kernel, tn=tn),
        grid=(n // tm,),
        in_specs=[
            pl.BlockSpec((tm, D_MODEL), lambda i: (i, 0)),
            pl.BlockSpec((1, D_MODEL), lambda i: (0, 0)),
            pl.BlockSpec((D_MODEL, Z_COLS), lambda i: (0, 0), pipeline_mode=pl.Buffered(1)),
        ],
        out_specs=[pl.BlockSpec((tm, Z_COLS), lambda i: (i, 0)),
                   pl.BlockSpec((tm, LANE), lambda i: (i, 0))],
        out_shape=[jax.ShapeDtypeStruct((n, Z_COLS), BF16), jax.ShapeDtypeStruct((n, LANE), F32)],
        compiler_params=_cparams(("parallel",)),
        name="inproj",
    )(x2, g, w)


def _mlstm_kernel(qk_ref, v_ref, o_ref, misc_ref, cw_ref, cb_ref, gb_ref, ng_ref, tri_ref, sh_ref, y_ref,
                  tail_ref, qkc_ref, c_ref, n_ref, m_ref):
    L = M_CHUNK
    dh = M_HEAD_DIM

    @pl.when(pl.program_id(1) == 0)
    def _():
        tail_ref[...] = jnp.zeros_like(tail_ref)
        c_ref[...] = jnp.zeros_like(c_ref)
        n_ref[...] = jnp.zeros_like(n_ref)
        m_ref[...] = jnp.full_like(m_ref, M_INIT)

    xb = qk_ref[...]
    ext = jnp.concatenate([tail_ref[...], xb], axis=0)
    cw = cw_ref[...]
    acc = xb.astype(F32) * cw[M_CONV - 1:M_CONV] + cb_ref[...]
    for j in range(1, M_CONV):
        acc = acc + _dot(sh_ref[j - 1], ext) * cw[M_CONV - 1 - j:M_CONV - j]
    qkc_ref[...] = acc * _sigmoid(acc)
    tail_ref[...] = xb[L - M_TAIL:L]

    gates = misc_ref[...] + gb_ref[...]
    lf = jnp.minimum(gates, 0.0) - jnp.log(1.0 + jnp.exp(-jnp.abs(gates)))
    tri = tri_ref[...]
    hi, mid, lo = _split3(lf)
    bcum = _dot(tri, hi) + _dot(tri, mid) + _dot(tri, lo)
    bcum_t = bcum.T
    gates_t = gates.T

    t_idx = lax.broadcasted_iota(I32, (L, L), 0)
    s_idx = lax.broadcasted_iota(I32, (L, L), 1)
    causal = t_idx >= s_idx

    ys, cs, ns, ms_new = [], [], [], []
    for h in range(M_HEADS):
        sl = slice(h * dh, (h + 1) * dh)
        q = qkc_ref[:, h * dh:(h + 1) * dh]
        k = qkc_ref[:, M_WIDTH + h * dh:M_WIDTH + (h + 1) * dh] * (dh ** -0.5)
        vb = v_ref[:, sl]
        v = vb.astype(F32)
        qb, kb = q.astype(BF16), k.astype(BF16)

        b_col = bcum[:, M_HEADS + h:M_HEADS + h + 1]
        i_col = gates[:, h:h + 1]
        b_row = bcum_t[M_HEADS + h:M_HEADS + h + 1, :]
        i_row = gates_t[h:h + 1, :]
        g_tot = bcum[L - 1:L, M_HEADS + h:M_HEADS + h + 1]

        c_prev = c_ref[sl, :]
        n_prev = n_ref[h:h + 1, :]
        m_prev = m_ref[h:h + 1, 0:1]

        rmat = jnp.where(causal, i_row - b_row, -jnp.inf)
        mx = jnp.maximum(m_prev, jnp.max(rmat, axis=-1, keepdims=True))
        m_t = b_col + mx
        sw = jnp.exp(rmat - mx) * _dot_nt(qb, kb)
        s_inter = jnp.exp(m_prev - mx)
        num = _dot(sw.astype(BF16), vb) + s_inter * _dot_nt(qb, c_prev.astype(BF16))
        den = jnp.sum(sw, axis=-1, keepdims=True) + s_inter * jnp.sum(q * n_prev, axis=-1, keepdims=True)
        dmax = jnp.maximum(jnp.abs(den), jnp.exp(-m_t))
        mnum = jnp.mean(num * num, axis=-1, keepdims=True)
        scale = lax.rsqrt(mnum + EPS * dmax * dmax)
        ys.append(_sigmoid(o_ref[:, sl].astype(F32)) * (num * scale * ng_ref[:, sl]))

        a_col = g_tot - b_col + i_col
        m_loc = jnp.max(a_col, axis=0, keepdims=True)
        w_loc = jnp.exp(a_col - m_loc)
        c_loc = _dot_tn((v * w_loc).astype(BF16), kb)
        n_loc = jnp.sum(k * w_loc, axis=0, keepdims=True)
        m_new = jnp.maximum(g_tot + m_prev, m_loc)
        s_old = jnp.exp(g_tot + m_prev - m_new)
        s_loc = jnp.exp(m_loc - m_new)
        cs.append(s_old * c_prev + s_loc * c_loc)
        ns.append(s_old * n_prev + s_loc * n_loc)
        ms_new.append(jnp.broadcast_to(m_new, (1, LANE)))

    y_ref[...] = jnp.concatenate(ys, axis=1).astype(BF16)
    c_ref[...] = jnp.concatenate(cs, axis=0)
    n_ref[...] = jnp.concatenate(ns, axis=0)
    m_ref[...] = jnp.concatenate(ms_new, axis=0)


def _mlstm(z, misc, conv_w, conv_b, gate_b_row, norm_g_row, tri, shifts, batch, seq):
    n = z.shape[0]
    nc = seq // M_CHUNK
    L = M_CHUNK
    row = lambda b, c: b * nc + c
    return pl.pallas_call(
        _mlstm_kernel,
        grid=(batch, nc),
        in_specs=[
            pl.BlockSpec((L, 2 * M_WIDTH), lambda b, c: (row(b, c), COL_MQK // (2 * M_WIDTH))),
            pl.BlockSpec((L, M_WIDTH), lambda b, c: (row(b, c), COL_MV // M_WIDTH)),
            pl.BlockSpec((L, M_WIDTH), lambda b, c: (row(b, c), COL_MO // M_WIDTH)),
            pl.BlockSpec((L, LANE), lambda b, c: (row(b, c), 0)),
            pl.BlockSpec((M_CONV, 2 * M_WIDTH), lambda b, c: (0, 0)),
            pl.BlockSpec((1, 2 * M_WIDTH), lambda b, c: (0, 0)),
            pl.BlockSpec((1, LANE), lambda b, c: (0, 0)),
            pl.BlockSpec((1, M_WIDTH), lambda b, c: (0, 0)),
            pl.BlockSpec((L, L), lambda b, c: (0, 0)),
            pl.BlockSpec((M_CONV - 1, L, M_TAIL + L), lambda b, c: (0, 0, 0)),
        ],
        out_specs=pl.BlockSpec((L, M_WIDTH), lambda b, c: (row(b, c), 0)),
        out_shape=jax.ShapeDtypeStruct((n, M_WIDTH), BF16),
        scratch_shapes=[
            pltpu.VMEM((M_TAIL, 2 * M_WIDTH), BF16),
            pltpu.VMEM((L, 2 * M_WIDTH), F32),
            pltpu.VMEM((M_HEADS * M_HEAD_DIM, M_HEAD_DIM), F32),
            pltpu.VMEM((M_HEADS, M_HEAD_DIM), F32),
            pltpu.VMEM((M_HEADS, LANE), F32),
        ],
        compiler_params=_cparams(("parallel", "arbitrary")),
        name="mlstm",
    )(z, z, z, misc, conv_w, conv_b, gate_b_row, norm_g_row, tri, shifts)


def _rope(x, c, s):
    w = x.shape[-1]
    lane = lax.broadcasted_iota(I32, (1, w), 1)
    fwd = pltpu.roll(x, ROPE_HALF, axis=1)
    bwd = pltpu.roll(x, w - ROPE_HALF, axis=1)
    swapped = jnp.where((lane & (ROPE_DIM - 1)) < ROPE_HALF, bwd, fwd)
    return x * c + swapped * s


def _head_rms(x, g, grp):
    sq = x * x
    hi = sq.astype(BF16)
    lo = (sq - hi.astype(F32)).astype(BF16)
    ss = _dot(hi, grp) + _dot(lo, grp)
    return x * lax.rsqrt(ss * (1.0 / ROPE_DIM) + EPS) * g


def _store_transposed(o_ref, v, dv):
    ones = jnp.ones((ONES_ROWS, KSUB), BF16)
    for j in range(o_ref.shape[0]):
        vt = v[j * KSUB:(j + 1) * KSUB, :].T.astype(BF16)
        parts = []
        for h in range(vt.shape[0] // dv):
            parts += [vt[h * dv:(h + 1) * dv, :], ones]
        o_ref[j] = jnp.concatenate(parts, axis=0)


def _prep_kernel(sq_ref, dq_ref, dk_ref, dv_ref, ckv_ref, iq_ref, ik_ref, c_ref, s_ref, grp_ref,
                 kvg_ref, wkv_ref, sqg_ref, skg_ref, dqg_ref, dkg_ref,
                 sq_o, sk_o, sv_o, iq_o, ik_o, dq_o, dk_o, dv_o):
    c256 = c_ref[...]
    s256 = s_ref[...]
    c512 = jnp.concatenate([c256, c256], axis=1)
    s512 = jnp.concatenate([s256, s256], axis=1)
    grp = grp_ref[...]
    qscale = ROPE_DIM ** -0.5 * LOG2E

    f32 = lambda ref: ref[...].astype(F32)

    sq_o[...] = (_rope(_head_rms(f32(sq_ref), sqg_ref[...], grp), c512, s512) * qscale).astype(BF16)

    ckv = f32(ckv_ref)
    ms = jnp.mean(ckv * ckv, axis=-1, keepdims=True)
    ckvn = (ckv * lax.rsqrt(ms + EPS) * kvg_ref[...]).astype(BF16)
    kv = _dot(ckvn, wkv_ref[...])
    sk_o[...] = _rope(_head_rms(kv[:, :S_WIDTH], skg_ref[...], grp), c512, s512).astype(BF16)
    _store_transposed(sv_o, kv[:, S_WIDTH:], ROPE_DIM)

    iq_o[...] = _rope(f32(iq_ref), c256, s256).astype(BF16)
    ik_o[...] = _rope(f32(ik_ref), c256, s256).astype(BF16)

    dq_o[...] = (_rope(_head_rms(f32(dq_ref), dqg_ref[...], grp), c512, s512) * qscale).astype(BF16)
    dk_o[...] = _rope(_head_rms(f32(dk_ref), dkg_ref[...], grp), c512, s512).astype(BF16)
    _store_transposed(dv_o, f32(dv_ref), DA_V_DIM)


def _prep(z, rope_c, rope_s, grp, kv_g, w_kv, sq_g, sk_g, dq_g, dk_g, seq, *, tm=512):
    n = z.shape[0]
    tm = min(tm, seq)
    nt = seq // tm
    zspec = lambda w, col: pl.BlockSpec((tm, w), lambda i: (i, col // w))
    const = lambda shape: pl.BlockSpec(shape, lambda i: (0, 0))
    o512 = pl.BlockSpec((tm, 512), lambda i: (i, 0))
    o256 = pl.BlockSpec((tm, 256), lambda i: (i, 0))
    s512 = jax.ShapeDtypeStruct((n, 512), BF16)
    s256 = jax.ShapeDtypeStruct((n, 256), BF16)
    tspec = lambda rows: pl.BlockSpec((tm // KSUB, rows, KSUB), lambda i: (i, 0, 0))
    tshape = lambda rows: jax.ShapeDtypeStruct((n // KSUB, rows, KSUB), BF16)
    sv_rows, dv_rows = S_HEADS * S_VROWS, DA_HEADS * DA_VROWS
    return pl.pallas_call(
        _prep_kernel,
        grid=(n // tm,),
        in_specs=[
            zspec(512, COL_SQ), zspec(512, COL_DQ), zspec(512, COL_DK), zspec(512, COL_DV),
            zspec(256, COL_CKV), zspec(256, COL_IQ), zspec(256, COL_IK4),
            pl.BlockSpec((tm, 256), lambda i: (i % nt, 0)),
            pl.BlockSpec((tm, 256), lambda i: (i % nt, 0)),
            const((512, 512)),
            const((1, S_KV_RANK)), const((S_KV_RANK, 2 * S_WIDTH)),
            const((1, 512)), const((1, 512)), const((1, 512)), const((1, 512)),
        ],
        out_specs=[o512, o512, tspec(sv_rows), o256, o256, o512, o512, tspec(dv_rows)],
        out_shape=[s512, s512, tshape(sv_rows), s256, s256, s512, s512, tshape(dv_rows)],
        compiler_params=_cparams(("parallel",)),
        name="attn_prep",
    )(z, z, z, z, z, z, z, rope_c, rope_s, grp, kv_g, w_kv, sq_g, sk_g, dq_g, dk_g)


def _mask_heads(q_ref, qm_ref, n_heads):
    lane_grp = lax.broadcasted_iota(I32, (1, HEAD_GROUP), 1) // ROPE_DIM
    for h in range(n_heads):
        g = h // 4
        qg = q_ref[:, g * HEAD_GROUP:(g + 1) * HEAD_GROUP].astype(F32)
        qm_ref[h] = jnp.where(lane_grp == (h % 4), qg, 0.0).T.astype(BF16)


def _softmax_group(logits, values, rows, acc_ref, m_ref):
    nh = len(logits)
    m_old = m_ref[...]
    m_new = jnp.maximum(m_old, jnp.concatenate([jnp.max(s, axis=0, keepdims=True) for s in logits], axis=0))
    alpha = jnp.exp2(m_old - m_new)
    ps = [jnp.exp2((s - m_new[j:j + 1, :]).astype(BF16)) for j, s in enumerate(logits)]
    m_ref[...] = m_new
    acc_old = acc_ref[...]
    acc_ref[...] = jnp.concatenate(
        [alpha[j:j + 1, :] * acc_old[j * rows:(j + 1) * rows, :] + _dot(values[j], ps[j])
         for j in range(nh)], axis=0)


def _attend(n_groups, logits_fn, values_fn, bufs0, bufs1, rows, acc_ref, m_ref):
    last = n_groups - 1
    group = len(bufs0)
    keys = bufs0[0].shape[1]

    def logits_stage(g, dst_refs):
        pen = jnp.where(g <= last, 0.0, NEG).astype(F32)
        for h, s in enumerate(logits_fn(jnp.minimum(g, last), pen)):
            for b in range(group):
                dst_refs[b][h] = s[b * keys:(b + 1) * keys, :]

    def softmax_stages(g, src_refs):
        for b in range(group):
            _softmax_group([src_refs[b][h] for h in range(src_refs[b].shape[0])],
                           values_fn(jnp.minimum(g, last) * group + b), rows, acc_ref, m_ref)

    logits_stage(0, bufs0)

    def body(gg, carry):
        logits_stage(2 * gg + 1, bufs1)
        softmax_stages(2 * gg, bufs0)
        logits_stage(2 * gg + 2, bufs0)
        softmax_stages(2 * gg + 1, bufs1)
        return carry

    lax.fori_loop(0, (n_groups + 1) // 2, body, 0)


def _sortable(x):
    bits = pltpu.bitcast(x, I32)
    return bits ^ ((bits >> 31) & 0x7FFFFFFF)


def _tree_sum(parts):
    while len(parts) > 1:
        parts = [parts[j] + parts[j + 1] for j in range(0, len(parts), 2)]
    return parts[0]


def _bit_planes(words):
    a = list(words)
    j, m = 16, 0x0000FFFF
    while j:
        k = 0
        while k < 32:
            t = (a[k] ^ lax.shift_right_logical(a[k + j], jnp.int32(j))) & m
            a[k] = a[k] ^ t
            a[k + j] = a[k + j] ^ (t << j)
            k = (k + j + 1) & ~j
        j >>= 1
        m = (m ^ (m << j)) & 0xFFFFFFFF
    return a


def _dsa_kernel(sq_ref, iq_ref, misc_ref, sk_ref, svt_ref, ik_ref, y_ref,
                sc_ref, bp_ref, qm_ref, iqm_ref, acc_ref, m_ref, sa_ref, sb_ref,
                *, tq, topk, idx_bits):
    i = pl.program_id(1)

    @pl.when((pl.program_id(0) == 0) & (i == 0))
    def _():
        bp_ref[...] = jnp.zeros_like(bp_ref)

    nks = (i + 1) * (tq // KSUB)
    q_idx = i * tq + lax.broadcasted_iota(I32, (1, tq), 1)
    k_sub = lax.broadcasted_iota(I32, (KSUB, 1), 0)

    _mask_heads(iq_ref, iqm_ref, IDX_HEADS)
    w_t = (misc_ref[...] * (IDX_WIDTH ** -0.5)).T
    w_rows = [w_t[MISC_IW + h:MISC_IW + h + 1, :] for h in range(IDX_HEADS)]

    k_pair = lax.broadcasted_iota(I32, (2 * KSUB, 1), 0)

    def score_body(j, carry):
        ikb = ik_ref[pl.ds(pl.multiple_of(j * (2 * KSUB), 2 * KSUB), 2 * KSUB), :]
        s = jnp.zeros((2 * KSUB, tq), F32)
        for h in range(IDX_HEADS):
            s = s + w_rows[h] * jnp.maximum(_dot(ikb, iqm_ref[h]), 0.0)
        key = jnp.where(j * (2 * KSUB) + k_pair <= q_idx, _sortable(s), INT_MIN)
        sc_ref[pl.ds(2 * j, 2)] = key.reshape(2, KSUB, tq)
        ob = key ^ INT_MIN
        planes = _bit_planes([ob[v * 8:(v + 1) * 8, :] for v in range(32)])
        bp_ref[:, pl.ds(pl.multiple_of(j * 8, 8), 8), :] = jnp.concatenate(planes, axis=0).reshape(32, 8, tq)
        return carry

    lax.fori_loop(0, nks // 2, score_body, 0)

    nblk = bp_ref.shape[1] // 8
    live_rows = lax.broadcasted_iota(I32, (nblk * 8, 1), 0) < (nks // 2) * 8

    def bit_body(t, carry):
        alive, need, thr_ob = carry
        plane = bp_ref[t]
        ones = alive & plane
        cnt = jnp.sum(lax.population_count(ones), axis=0, keepdims=True)
        take = cnt >= need
        alive = jnp.where(take, ones, alive ^ ones)
        need = jnp.where(take, need, need - cnt)
        thr_ob = thr_ob | jnp.where(take, jnp.left_shift(jnp.int32(1), 31 - t), 0)
        return alive, need, thr_ob

    alive, need, thr_ob = lax.fori_loop(
        0, 32, bit_body,
        (jnp.broadcast_to(jnp.where(live_rows, -1, 0), (nblk * 8, tq)),
         jnp.full((1, tq), topk, I32), jnp.zeros((1, tq), I32)))
    thr = thr_ob ^ INT_MIN
    n_ties = jnp.sum(lax.population_count(alive), axis=0, keepdims=True)

    def count32(pred):
        def body(ks, acc):
            c = jnp.where(pred(sc_ref[ks], ks * KSUB + k_sub), 1, 0)
            return acc + _tree_sum([c[r * 8:(r + 1) * 8] for r in range(KSUB // 8)])
        acc = lax.fori_loop(0, nks, body, jnp.zeros((8, tq), I32))
        return jnp.sum(acc, axis=0, keepdims=True)

    m_ref[0:1, :] = jnp.full((1, tq), 2.0**30, F32)

    @pl.when(jnp.max(n_ties - need) > 0)
    def _():
        def idx_body(t, cut):
            bit = jnp.left_shift(jnp.int32(1), idx_bits - 1 - t)
            trial = cut | bit
            cnt = count32(lambda blk, kidx: (blk == thr) & (kidx < trial))
            return jnp.where(cnt < need, trial, cut)
        cut = lax.fori_loop(0, idx_bits, idx_body, jnp.zeros((1, tq), I32))
        m_ref[0:1, :] = cut.astype(F32)

    cut = m_ref[0:1, :].astype(I32)

    thr_sel = jnp.maximum(thr, INT_MIN + 1)

    def bias_body(j, carry):
        pair = pl.ds(2 * j, 2)
        key = sc_ref[pair].reshape(2 * KSUB, tq)
        kidx = j * (2 * KSUB) + k_pair
        sel = key >= thr_sel + jnp.where(kidx <= cut, 0, 1)
        sc_ref[pair] = pltpu.bitcast(jnp.where(sel, 0.0, NEG).astype(F32), I32).reshape(2, KSUB, tq)
        return carry

    lax.fori_loop(0, nks // 2, bias_body, 0)

    _mask_heads(sq_ref, qm_ref, S_HEADS)
    acc_ref[...] = jnp.zeros_like(acc_ref)
    m_ref[...] = jnp.full_like(m_ref, NEG)

    def logits_fn(ks, pen):
        off = pl.multiple_of(ks * KSUB, KSUB)
        bias = pltpu.bitcast(sc_ref[ks], F32) + pen
        kgs = [sk_ref[pl.ds(off, KSUB), g * HEAD_GROUP:(g + 1) * HEAD_GROUP] for g in range(S_HEADS // 4)]
        return [_dot(kgs[h // 4], qm_ref[h]) + bias for h in range(S_HEADS)]

    def values_fn(ks):
        return [svt_ref[ks, h * S_VROWS:(h + 1) * S_VROWS, :] for h in range(S_HEADS)]

    _attend(nks, logits_fn, values_fn, (sa_ref,), (sb_ref,), S_VROWS, acc_ref, m_ref)

    outs = []
    for h in range(S_HEADS):
        r0 = h * S_VROWS
        outs.append(acc_ref[r0:r0 + ROPE_DIM, :] / acc_ref[r0 + ROPE_DIM:r0 + ROPE_DIM + 1, :])
    y_ref[...] = jnp.concatenate(outs, axis=0).T.astype(BF16)


def _dsa(misc, sq, sk, svt, iq, ik4, batch, seq, *, tq=256):
    n = misc.shape[0]
    tq = min(tq, seq)
    nq = seq // tq
    nsub = seq // KSUB
    topk = min(IDX_TOPK_MAX, seq // 4)
    kern = functools.partial(_dsa_kernel, tq=tq, topk=topk, idx_bits=seq.bit_length())
    return pl.pallas_call(
        kern,
        grid=(batch, nq),
        in_specs=[
            pl.BlockSpec((tq, S_WIDTH), lambda b, i: (b * nq + i, 0)),
            pl.BlockSpec((tq, IDX_WIDTH), lambda b, i: (b * nq + i, 0)),
            pl.BlockSpec((tq, LANE), lambda b, i: (b * nq + i, 0)),
            pl.BlockSpec((seq, S_WIDTH), lambda b, i: (b, 0)),
            pl.BlockSpec((nsub, S_HEADS * S_VROWS, KSUB), lambda b, i: (b, 0, 0)),
            pl.BlockSpec((seq, IDX_WIDTH), lambda b, i: (b, 0)),
        ],
        out_specs=pl.BlockSpec((tq, S_WIDTH), lambda b, i: (b * nq + i, 0)),
        out_shape=jax.ShapeDtypeStruct((n, S_WIDTH), BF16),
        scratch_shapes=[
            pltpu.VMEM((nsub, KSUB, tq), I32),
            pltpu.VMEM((32, (nsub // 2) * 8, tq), I32),
            pltpu.VMEM((S_HEADS, HEAD_GROUP, tq), BF16),
            pltpu.VMEM((IDX_HEADS, HEAD_GROUP, tq), BF16),
            pltpu.VMEM((S_HEADS * S_VROWS, tq), F32),
            pltpu.VMEM((S_HEADS, tq), F32),
            pltpu.VMEM((S_HEADS, KSUB, tq), F32),
            pltpu.VMEM((S_HEADS, KSUB, tq), F32),
        ],
        compiler_params=_cparams(("arbitrary", "arbitrary")),
        name="dsa_attn",
    )(sq, iq, misc, sk, svt, ik4)


def _diff_kernel(q_ref, k_ref, vt_ref, lam_ref, og_ref, y_ref, qm_ref, acc_ref, m_ref,
                 sa_ref, sb_ref, *, tq, lam_init):
    i = pl.program_id(1)
    q_idx = i * tq + lax.broadcasted_iota(I32, (1, tq), 1)
    k_pair = lax.broadcasted_iota(I32, (2 * KSUB, 1), 0)

    _mask_heads(q_ref, qm_ref, DA_MAPS)
    acc_ref[...] = jnp.zeros_like(acc_ref)
    m_ref[...] = jnp.full_like(m_ref, NEG)

    def logits_fn(j, pen):
        off = pl.multiple_of(j * (2 * KSUB), 2 * KSUB)
        bias = jnp.where(j * (2 * KSUB) + k_pair <= q_idx, 0.0, NEG).astype(F32) + pen
        kgs = [k_ref[pl.ds(off, 2 * KSUB), g * HEAD_GROUP:(g + 1) * HEAD_GROUP] for g in range(DA_MAPS // 4)]
        return [_dot(kgs[m // 4], qm_ref[m]) + bias for m in range(DA_MAPS)]

    def values_fn(j):
        heads = [jnp.concatenate([vt_ref[2 * j, hd * DA_VROWS:(hd + 1) * DA_VROWS, :],
                                  vt_ref[2 * j + 1, hd * DA_VROWS:(hd + 1) * DA_VROWS, :]], axis=1)
                 for hd in range(DA_HEADS)]
        return [heads[m // 2] for m in range(DA_MAPS)]

    assert tq == 2 * KSUB
    _attend(i + 1, logits_fn, values_fn, (sa_ref,), (sb_ref,), DA_VROWS, acc_ref, m_ref)

    lam = lam_ref[...]
    p01 = jnp.sum(lam[0:1] * lam[1:2], axis=-1, keepdims=True)
    p23 = jnp.sum(lam[2:3] * lam[3:4], axis=-1, keepdims=True)
    lam_val = jnp.exp(p01) - jnp.exp(p23) + lam_init
    def normalised(m):
        r0 = m * DA_VROWS
        return acc_ref[r0:r0 + DA_V_DIM, :] / acc_ref[r0 + DA_V_DIM:r0 + DA_V_DIM + 1, :]

    outs = []
    for hd in range(DA_HEADS):
        o = normalised(2 * hd) - lam_val * normalised(2 * hd + 1)
        ms = jnp.mean(o * o, axis=0, keepdims=True)
        outs.append(o * lax.rsqrt(ms + EPS) * og_ref[...] * (1.0 - lam_init))
    y_ref[...] = jnp.concatenate(outs, axis=0).T.astype(BF16)


def _diff_attn(dq, dk, dvt, lam, out_g_col, batch, seq, lam_init, *, tq=256):
    n = dq.shape[0]
    tq = min(tq, seq)
    nq = seq // tq
    nsub = seq // KSUB
    kern = functools.partial(_diff_kernel, tq=tq, lam_init=lam_init)
    return pl.pallas_call(
        kern,
        grid=(batch, nq),
        in_specs=[
            pl.BlockSpec((tq, 512), lambda b, i: (b * nq + i, 0)),
            pl.BlockSpec((seq, 512), lambda b, i: (b, 0)),
            pl.BlockSpec((nsub, DA_HEADS * DA_VROWS, KSUB), lambda b, i: (b, 0, 0)),
            pl.BlockSpec((4, ROPE_DIM), lambda b, i: (0, 0)),
            pl.BlockSpec((DA_V_DIM, 1), lambda b, i: (0, 0)),
        ],
        out_specs=pl.BlockSpec((tq, DA_WIDTH), lambda b, i: (b * nq + i, 0)),
        out_shape=jax.ShapeDtypeStruct((n, DA_WIDTH), BF16),
        scratch_shapes=[
            pltpu.VMEM((DA_MAPS, HEAD_GROUP, tq), BF16),
            pltpu.VMEM((DA_MAPS * DA_VROWS, tq), F32),
            pltpu.VMEM((DA_MAPS, tq), F32),
            pltpu.VMEM((DA_MAPS, 2 * KSUB, tq), F32),
            pltpu.VMEM((DA_MAPS, 2 * KSUB, tq), F32),
        ],
        compiler_params=_cparams(("parallel", "arbitrary")),
        name="diff_attn",
    )(dq, dk, dvt, lam, out_g_col)


def _merge_kernel(x_ref, ya_ref, yb_ref, yc_ref, gp_ref, bg_ref, wb_ref, wo_ref, ng_ref, xo_ref, xn_ref):
    merged = None
    for br, y_ref in enumerate((ya_ref, yb_ref, yc_ref)):
        sl = slice(br * D_MODEL, (br + 1) * D_MODEL)
        gate = _sigmoid(gp_ref[:, sl].astype(F32) + bg_ref[:, sl])
        term = gate * _dot(y_ref[...], wb_ref[br])
        merged = term if merged is None else merged + term
    xo = x_ref[...] + _dot(merged.astype(BF16), wo_ref[...])
    xo_ref[...] = xo
    ms = jnp.mean(xo * xo, axis=-1, keepdims=True)
    xn_ref[...] = (xo * lax.rsqrt(ms + EPS) * ng_ref[...]).astype(BF16)


def _merge(x2, ya, yb, yc, z, b_gate, w_branch, w_out, norm_g, *, tm=512):
    n = x2.shape[0]
    tm = min(tm, n)
    row = lambda w: pl.BlockSpec((tm, w), lambda i: (i, 0))
    return pl.pallas_call(
        _merge_kernel,
        grid=(n // tm,),
        in_specs=[
            row(D_MODEL), row(512), row(512), row(512),
            pl.BlockSpec((tm, N_BRANCH * D_MODEL), lambda i: (i, COL_GATE)),
            pl.BlockSpec((1, N_BRANCH * D_MODEL), lambda i: (0, 0)),
            pl.BlockSpec((N_BRANCH, 512, D_MODEL), lambda i: (0, 0, 0)),
            pl.BlockSpec((D_MODEL, D_MODEL), lambda i: (0, 0)),
            pl.BlockSpec((1, D_MODEL), lambda i: (0, 0)),
        ],
        out_specs=[row(D_MODEL), row(D_MODEL)],
        out_shape=[jax.ShapeDtypeStruct((n, D_MODEL), F32), jax.ShapeDtypeStruct((n, D_MODEL), BF16)],
        compiler_params=_cparams(("parallel",)),
        name="merge_out",
    )(x2, ya, yb, yc, z, b_gate, w_branch, w_out, norm_g)


def _ffn_kernel(xn_ref, x_ref, wg_ref, wu_ref, wd_ref, o_ref, *, fc):
    xn = xn_ref[...]
    acc = x_ref[...]
    for c in range(FF_DIM // fc):
        cols = slice(c * fc, (c + 1) * fc)
        g = _dot(xn, wg_ref[:, cols])
        u = _dot(xn, wu_ref[:, cols])
        acc = acc + _dot((g * _sigmoid(g) * u).astype(BF16), wd_ref[cols, :])
    o_ref[...] = acc


def _ffn(xn, x2, wg, wu, wd, *, tm=512, fc=256):
    n = x2.shape[0]
    tm = min(tm, n)
    resident = lambda shape: pl.BlockSpec(shape, lambda i: (0, 0), pipeline_mode=pl.Buffered(1))
    return pl.pallas_call(
        functools.partial(_ffn_kernel, fc=fc),
        grid=(n // tm,),
        in_specs=[
            pl.BlockSpec((tm, D_MODEL), lambda i: (i, 0)),
            pl.BlockSpec((tm, D_MODEL), lambda i: (i, 0)),
            resident((D_MODEL, FF_DIM)),
            resident((D_MODEL, FF_DIM)),
            resident((FF_DIM, D_MODEL)),
        ],
        out_specs=pl.BlockSpec((tm, D_MODEL), lambda i: (i, 0)),
        out_shape=jax.ShapeDtypeStruct((n, D_MODEL), F32),
        compiler_params=_cparams(("parallel",)),
        name="ffn",
    )(xn, x2, wg, wu, wd)


def _rope_tables(seq):
    inv = 1.0 / jnp.power(ROPE_THETA, jnp.arange(0, ROPE_DIM, 2, dtype=F32) / ROPE_DIM)
    ang = jnp.arange(seq, dtype=F32)[:, None] * inv[None, :]
    cos, sin = jnp.cos(ang), jnp.sin(ang)
    c64 = jnp.concatenate([cos, cos], axis=1)
    s64 = jnp.concatenate([-sin, sin], axis=1)
    return jnp.tile(c64, (1, 4)), jnp.tile(s64, (1, 4))


def _layer(x2, li, batch, seq, consts, norm_mix_g, w_in_l, b_gate, conv_w, conv_b, gate_b, m_norm_g,
           kv_norm_g, w_kv_up, sq_g, sk_g, dq_g, dk_g, lam, d_out_g, w_branch, w_out,
           norm_ffn_g, w_gate_up, w_down):
    rope_c, rope_s, grp, tri, shifts = consts
    lam_init = 0.8 - 0.6 * math.exp(-0.3 * li)
    tile8 = lambda g: jnp.tile(g, 8)[None, :]

    z, misc = _inproj(x2, norm_mix_g[None, :], w_in_l)

    gate_row = jnp.zeros((1, LANE), F32).at[0, MISC_IF:MISC_IF + 2 * M_HEADS].set(gate_b)
    ya = _mlstm(z, misc, conv_w, conv_b[None, :], gate_row, m_norm_g.reshape(1, M_WIDTH), tri, shifts,
                batch, seq)

    sq, sk, svt, iq, ik4, dq, dk, dvt = _prep(
        z, rope_c, rope_s, grp, kv_norm_g[None, :], w_kv_up.astype(BF16),
        tile8(sq_g), tile8(sk_g), tile8(dq_g), tile8(dk_g), seq)
    yb = _dsa(misc, sq, sk, svt, iq, ik4, batch, seq)
    yc = _diff_attn(dq, dk, dvt, lam, d_out_g[:, None], batch, seq, lam_init)

    xo, xn = _merge(x2, ya, yb, yc, z, b_gate[None, :], w_branch.astype(BF16), w_out.astype(BF16),
                    norm_ffn_g[None, :])
    return _ffn(xn, xo, w_gate_up[:, :FF_DIM].astype(BF16), w_gate_up[:, FF_DIM:].astype(BF16),
                w_down.astype(BF16))


def kernel(x, norm_mix_g, w_in, b_gate, mlstm_conv_w, mlstm_conv_b, mlstm_gate_b, mlstm_norm_g,
           dsa_kv_norm_g, dsa_w_kv_up, dsa_q_norm_g, dsa_k_norm_g, diff_q_norm_g, diff_k_norm_g,
           diff_lambda, diff_out_norm_g, w_branch, w_out, norm_ffn_g, w_gate_up, w_down):
    batch, seq, d = x.shape
    depth = w_in.shape[0]
    assert w_in.shape[2] == IN_COLS and d == D_MODEL
    rope_c, rope_s = _rope_tables(seq)
    gi = jnp.arange(512) // ROPE_DIM
    grp = (gi[:, None] == gi[None, :]).astype(BF16)
    ti = jnp.arange(M_CHUNK)
    tri = (ti[:, None] >= ti[None, :]).astype(BF16)
    ri = jnp.arange(M_TAIL + M_CHUNK)
    shifts = jnp.stack([(ri[None, :] == ti[:, None] + M_TAIL - j) for j in range(1, M_CONV)]).astype(BF16)
    consts = (rope_c, rope_s, grp, tri, shifts)
    w_in_l = _wlayout(w_in)
    x2 = x.reshape(batch * seq, d)
    for li in range(depth):
        x2 = _layer(x2, li, batch, seq, consts, norm_mix_g[li], w_in_l[li], b_gate[li], mlstm_conv_w[li],
                    mlstm_conv_b[li], mlstm_gate_b[li], mlstm_norm_g[li], dsa_kv_norm_g[li],
                    dsa_w_kv_up[li], dsa_q_norm_g[li], dsa_k_norm_g[li], diff_q_norm_g[li],
                    diff_k_norm_g[li], diff_lambda[li], diff_out_norm_g[li], w_branch[li], w_out[li],
                    norm_ffn_g[li], w_gate_up[li], w_down[li])
    return x2.reshape(batch, seq, d)
```

```python
import functools
import math

import jax
import jax.numpy as jnp
from jax import lax
from jax.experimental import pallas as pl
from jax.experimental.pallas import tpu as pltpu

F32 = jnp.float32
BF16 = jnp.bfloat16
I32 = jnp.int32
I16 = jnp.int16

D_MODEL = 1024
EPS = 1e-6
ROPE_DIM = 64
ROPE_HALF = ROPE_DIM // 2
ROPE_THETA = 10000.0

M_HEADS = 4
M_HEAD_DIM = 128
M_WIDTH = M_HEADS * M_HEAD_DIM
M_CONV = 4
M_CHUNK = 128
M_TAIL = 16
M_INIT = -1e30

S_HEADS = 8
S_WIDTH = S_HEADS * ROPE_DIM
S_KV_RANK = 256
IDX_HEADS = 4
IDX_WIDTH = IDX_HEADS * ROPE_DIM
IDX_TOPK_MAX = 256

DA_HEADS = 4
DA_MAPS = 2 * DA_HEADS
DA_V_DIM = 2 * ROPE_DIM
DA_WIDTH = DA_HEADS * DA_V_DIM

N_BRANCH = 3
FF_DIM = 2816

NEG = -1e30
LOG2E = math.log2(math.e)
INT_MIN = -(2**31)
HALF16 = 2**15
LANE = 128
HEAD_GROUP = 256
KSUB = 128
ONES_ROWS = 16
S_VROWS = ROPE_DIM + ONES_ROWS
DA_VROWS = DA_V_DIM + ONES_ROWS

W_IN_SIZES = (2 * M_WIDTH, M_WIDTH, M_WIDTH, 2 * M_HEADS, S_WIDTH, S_KV_RANK, IDX_WIDTH, ROPE_DIM,
              IDX_HEADS, 2 * DA_HEADS * ROPE_DIM, 2 * DA_HEADS * ROPE_DIM, DA_WIDTH, N_BRANCH * D_MODEL)
IN_COLS = sum(W_IN_SIZES)

COL_GATE = 0
COL_MQK = 3072
COL_MV = 4096
COL_MO = 4608
COL_SQ = 5120
COL_DQ = 5632
COL_DK = 6144
COL_DV = 6656
COL_CKV = 7168
COL_IQ = 7424
COL_IK4 = 7680
COL_MISC = 7936
MISC_IF = 0
MISC_IW = 8
Z_COLS = 8192

VMEM_LIMIT = 56 * 1024 * 1024


def _cparams(sem, flags=None):
    return pltpu.CompilerParams(dimension_semantics=sem, vmem_limit_bytes=VMEM_LIMIT, flags=flags)


def _sigmoid(x):
    return 1.0 / (1.0 + jnp.exp(-x))


def _dot(a, b):
    return jnp.dot(a, b, preferred_element_type=F32)


def _dot_nt(a, b):
    return lax.dot_general(a, b, (((1,), (1,)), ((), ())), preferred_element_type=F32)


def _dot_tn(a, b):
    return lax.dot_general(a, b, (((0,), (0,)), ((), ())), preferred_element_type=F32)


def _split3(x):
    hi = x.astype(BF16)
    r1 = x - hi.astype(F32)
    mid = r1.astype(BF16)
    lo = (r1 - mid.astype(F32)).astype(BF16)
    return hi, mid, lo


def _wlayout_kernel(w_ref, o_ref):
    w = w_ref[...]
    offs = [0]
    for s in W_IN_SIZES:
        offs.append(offs[-1] + s)
    seg = [w[:, offs[k]:offs[k + 1]] for k in range(len(W_IN_SIZES))]
    m_qk, m_v, m_o, m_if, s_q, s_ckv, i_q, i_k, i_w, d_q, d_k, d_v, g_pre = seg
    rows = w.shape[0]
    misc = jnp.concatenate([m_if, i_w, jnp.zeros((rows, LANE - 2 * M_HEADS - IDX_HEADS), F32)], axis=1)
    ik4 = jnp.concatenate([i_k] * IDX_HEADS, axis=1)
    tail = jnp.zeros((rows, Z_COLS - COL_MISC - LANE), F32)
    for col, val in ((COL_GATE, g_pre), (COL_MQK, m_qk), (COL_MV, m_v), (COL_MO, m_o), (COL_SQ, s_q),
                     (COL_DQ, d_q), (COL_DK, d_k), (COL_DV, d_v), (COL_CKV, s_ckv), (COL_IQ, i_q),
                     (COL_IK4, ik4), (COL_MISC, misc), (COL_MISC + LANE, tail)):
        o_ref[:, col:col + val.shape[1]] = val.astype(BF16)


def _wlayout(w_in, *, tr=256):
    depth, rows, cols = w_in.shape
    return pl.pallas_call(
        _wlayout_kernel,
        grid=(depth, rows // tr),
        in_specs=[pl.BlockSpec((None, tr, cols), lambda l, i: (l, i, 0))],
        out_specs=pl.BlockSpec((None, tr, Z_COLS), lambda l, i: (l, i, 0)),
        out_shape=jax.ShapeDtypeStruct((depth, rows, Z_COLS), BF16),
        compiler_params=_cparams(("parallel", "parallel")),
        name="w_layout",
    )(w_in)


def _inproj_kernel(x_ref, g_ref, w_ref, z_ref, misc_ref, *, tn):
    x = x_ref[...]
    ms = jnp.mean(x * x, axis=-1, keepdims=True)
    xn = (x * lax.rsqrt(ms + EPS) * g_ref[...]).astype(BF16)
    for c in range(Z_COLS // tn):
        r = _dot(xn, w_ref[:, c * tn:(c + 1) * tn])
        z_ref[:, c * tn:(c + 1) * tn] = r.astype(BF16)
        if c == COL_MISC // tn:
            misc_ref[...] = r[:, COL_MISC % tn:COL_MISC % tn + LANE]


def _inproj(x2, g, w, *, tm=512, tn=1024):
    n = x2.shape[0]
    tm = min(tm, n)
    return pl.pallas_call(
        functools.partial(_inproj_kernel, tn=tn),
        grid=(n // tm,),
        in_specs=[
            pl.BlockSpec((tm, D_MODEL), lambda i: (i, 0)),
            pl.BlockSpec((1, D_MODEL), lambda i: (0, 0)),
            pl.BlockSpec((D_MODEL, Z_COLS), lambda i: (0, 0), pipeline_mode=pl.Buffered(1)),
        ],
        out_specs=[pl.BlockSpec((tm, Z_COLS), lambda i: (i, 0)),
                   pl.BlockSpec((tm, LANE), lambda i: (i, 0))],
        out_shape=[jax.ShapeDtypeStruct((n, Z_COLS), BF16), jax.ShapeDtypeStruct((n, LANE), F32)],
        compiler_params=_cparams(("parallel",)),
        name="inproj",
    )(x2, g, w)


def _mlstm_kernel(qk_ref, v_ref, o_ref, misc_ref, cw_ref, cb_ref, gb_ref, ng_ref, tri_ref, sh_ref, y_ref,
                  tail_ref, qkc_ref, c_ref, n_ref, m_ref):
    L = M_CHUNK
    dh = M_HEAD_DIM

    @pl.when(pl.program_id(1) == 0)
    def _():
        tail_ref[...] = jnp.zeros_like(tail_ref)
        c_ref[...] = jnp.zeros_like(c_ref)
        n_ref[...] = jnp.zeros_like(n_ref)
        m_ref[...] = jnp.full_like(m_ref, M_INIT)

    xb = qk_ref[...]
    ext = jnp.concatenate([tail_ref[...], xb], axis=0)
    cw = cw_ref[...]
    acc = xb.astype(F32) * cw[M_CONV - 1:M_CONV] + cb_ref[...]
    for j in range(1, M_CONV):
        acc = acc + _dot(sh_ref[j - 1], ext) * cw[M_CONV - 1 - j:M_CONV - j]
    qkc_ref[...] = acc * _sigmoid(acc)
    tail_ref[...] = xb[L - M_TAIL:L]

    gates = misc_ref[...] + gb_ref[...]
    lf = jnp.minimum(gates, 0.0) - jnp.log(1.0 + jnp.exp(-jnp.abs(gates)))
    tri = tri_ref[...]
    hi, mid, lo = _split3(lf)
    bcum = _dot(tri, hi) + _dot(tri, mid) + _dot(tri, lo)
    bcum_t = bcum.T
    gates_t = gates.T

    t_idx = lax.broadcasted_iota(I32, (L, L), 0)
    s_idx = lax.broadcasted_iota(I32, (L, L), 1)
    causal = t_idx >= s_idx

    ys, cs, ns, ms_new = [], [], [], []
    for h in range(M_HEADS):
        sl = slice(h * dh, (h + 1) * dh)
        q = qkc_ref[:, h * dh:(h + 1) * dh]
        k = qkc_ref[:, M_WIDTH + h * dh:M_WIDTH + (h + 1) * dh] * (dh ** -0.5)
        vb = v_ref[:, sl]
        v = vb.astype(F32)
        qb, kb = q.astype(BF16), k.astype(BF16)

        b_col = bcum[:, M_HEADS + h:M_HEADS + h + 1]
        i_col = gates[:, h:h + 1]
        b_row = bcum_t[M_HEADS + h:M_HEADS + h + 1, :]
        i_row = gates_t[h:h + 1, :]
        g_tot = bcum[L - 1:L, M_HEADS + h:M_HEADS + h + 1]

        c_prev = c_ref[sl, :]
        n_prev = n_ref[h:h + 1, :]
        m_prev = m_ref[h:h + 1, 0:1]

        rmat = jnp.where(causal, i_row - b_row, -jnp.inf)
        mx = jnp.maximum(m_prev, jnp.max(rmat, axis=-1, keepdims=True))
        m_t = b_col + mx
        sw = jnp.exp(rmat - mx) * _dot_nt(qb, kb)
        s_inter = jnp.exp(m_prev - mx)
        num = _dot(sw.astype(BF16), vb) + s_inter * _dot_nt(qb, c_prev.astype(BF16))
        den = jnp.sum(sw, axis=-1, keepdims=True) + s_inter * jnp.sum(q * n_prev, axis=-1, keepdims=True)
        dmax = jnp.maximum(jnp.abs(den), jnp.exp(-m_t))
        mnum = jnp.mean(num * num, axis=-1, keepdims=True)
        scale = lax.rsqrt(mnum + EPS * dmax * dmax)
        ys.append(_sigmoid(o_ref[:, sl].astype(F32)) * (num * scale * ng_ref[:, sl]))

        a_col = g_tot - b_col + i_col
        m_loc = jnp.max(a_col, axis=0, keepdims=True)
        w_loc = jnp.exp(a_col - m_loc)
        c_loc = _dot_tn((v * w_loc).astype(BF16), kb)
        n_loc = jnp.sum(k * w_loc, axis=0, keepdims=True)
        m_new = jnp.maximum(g_tot + m_prev, m_loc)
        s_old = jnp.exp(g_tot + m_prev - m_new)
        s_loc = jnp.exp(m_loc - m_new)
        cs.append(s_old * c_prev + s_loc * c_loc)
        ns.append(s_old * n_prev + s_loc * n_loc)
        ms_new.append(jnp.broadcast_to(m_new, (1, LANE)))

    y_ref[...] = jnp.concatenate(ys, axis=1).astype(BF16)
    c_ref[...] = jnp.concatenate(cs, axis=0)
    n_ref[...] = jnp.concatenate(ns, axis=0)
    m_ref[...] = jnp.concatenate(ms_new, axis=0)


def _mlstm(z, misc, conv_w, conv_b, gate_b_row, norm_g_row, tri, shifts, batch, seq):
    n = z.shape[0]
    nc = seq // M_CHUNK
    L = M_CHUNK
    row = lambda b, c: b * nc + c
    return pl.pallas_call(
        _mlstm_kernel,
        grid=(batch, nc),
        in_specs=[
            pl.BlockSpec((L, 2 * M_WIDTH), lambda b, c: (row(b, c), COL_MQK // (2 * M_WIDTH))),
            pl.BlockSpec((L, M_WIDTH), lambda b, c: (row(b, c), COL_MV // M_WIDTH)),
            pl.BlockSpec((L, M_WIDTH), lambda b, c: (row(b, c), COL_MO // M_WIDTH)),
            pl.BlockSpec((L, LANE), lambda b, c: (row(b, c), 0)),
            pl.BlockSpec((M_CONV, 2 * M_WIDTH), lambda b, c: (0, 0)),
            pl.BlockSpec((1, 2 * M_WIDTH), lambda b, c: (0, 0)),
            pl.BlockSpec((1, LANE), lambda b, c: (0, 0)),
            pl.BlockSpec((1, M_WIDTH), lambda b, c: (0, 0)),
            pl.BlockSpec((L, L), lambda b, c: (0, 0)),
            pl.BlockSpec((M_CONV - 1, L, M_TAIL + L), lambda b, c: (0, 0, 0)),
        ],
        out_specs=pl.BlockSpec((L, M_WIDTH), lambda b, c: (row(b, c), 0)),
        out_shape=jax.ShapeDtypeStruct((n, M_WIDTH), BF16),
        scratch_shapes=[
            pltpu.VMEM((M_TAIL, 2 * M_WIDTH), BF16),
            pltpu.VMEM((L, 2 * M_WIDTH), F32),
            pltpu.VMEM((M_HEADS * M_HEAD_DIM, M_HEAD_DIM), F32),
            pltpu.VMEM((M_HEADS, M_HEAD_DIM), F32),
            pltpu.VMEM((M_HEADS, LANE), F32),
        ],
        compiler_params=_cparams(("parallel", "arbitrary")),
        name="mlstm",
    )(z, z, z, misc, conv_w, conv_b, gate_b_row, norm_g_row, tri, shifts)


def _rope(x, c, s):
    w = x.shape[-1]
    lane = lax.broadcasted_iota(I32, (1, w), 1)
    fwd = pltpu.roll(x, ROPE_HALF, axis=1)
    bwd = pltpu.roll(x, w - ROPE_HALF, axis=1)
    swapped = jnp.where((lane & (ROPE_DIM - 1)) < ROPE_HALF, bwd, fwd)
    return x * c + swapped * s


def _head_rms(x, g, grp):
    sq = x * x
    hi = sq.astype(BF16)
    lo = (sq - hi.astype(F32)).astype(BF16)
    ss = _dot(hi, grp) + _dot(lo, grp)
    return x * lax.rsqrt(ss * (1.0 / ROPE_DIM) + EPS) * g


def _store_transposed(o_ref, v, dv):
    ones = jnp.ones((ONES_ROWS, KSUB), BF16)
    for j in range(o_ref.shape[0]):
        vt = v[j * KSUB:(j + 1) * KSUB, :].T.astype(BF16)
        parts = []
        for h in range(vt.shape[0] // dv):
            parts += [vt[h * dv:(h + 1) * dv, :], ones]
        o_ref[j] = jnp.concatenate(parts, axis=0)


def _prep_kernel(sq_ref, dq_ref, dk_ref, dv_ref, ckv_ref, iq_ref, ik_ref, c_ref, s_ref, grp_ref,
                 kvg_ref, wkv_ref, sqg_ref, skg_ref, dqg_ref, dkg_ref,
                 sq_o, sk_o, sv_o, iq_o, ik_o, dq_o, dk_o, dv_o):
    c256 = c_ref[...]
    s256 = s_ref[...]
    c512 = jnp.concatenate([c256, c256], axis=1)
    s512 = jnp.concatenate([s256, s256], axis=1)
    grp = grp_ref[...]
    qscale = ROPE_DIM ** -0.5 * LOG2E

    f32 = lambda ref: ref[...].astype(F32)

    sq_o[...] = (_rope(_head_rms(f32(sq_ref), sqg_ref[...], grp), c512, s512) * qscale).astype(BF16)

    ckv = f32(ckv_ref)
    ms = jnp.mean(ckv * ckv, axis=-1, keepdims=True)
    ckvn = (ckv * lax.rsqrt(ms + EPS) * kvg_ref[...]).astype(BF16)
    kv = _dot(ckvn, wkv_ref[...])
    sk_o[...] = _rope(_head_rms(kv[:, :S_WIDTH], skg_ref[...], grp), c512, s512).astype(BF16)
    _store_transposed(sv_o, kv[:, S_WIDTH:], ROPE_DIM)

    iq_o[...] = _rope(f32(iq_ref), c256, s256).astype(BF16)
    ik_o[...] = _rope(f32(ik_ref), c256, s256).astype(BF16)

    dq_o[...] = (_rope(_head_rms(f32(dq_ref), dqg_ref[...], grp), c512, s512) * qscale).astype(BF16)
    dk_o[...] = _rope(_head_rms(f32(dk_ref), dkg_ref[...], grp), c512, s512).astype(BF16)
    _store_transposed(dv_o, f32(dv_ref), DA_V_DIM)


def _prep(z, rope_c, rope_s, grp, kv_g, w_kv, sq_g, sk_g, dq_g, dk_g, seq, *, tm=512):
    n = z.shape[0]
    tm = min(tm, seq)
    nt = seq // tm
    zspec = lambda w, col: pl.BlockSpec((tm, w), lambda i: (i, col // w))
    const = lambda shape: pl.BlockSpec(shape, lambda i: (0, 0))
    o512 = pl.BlockSpec((tm, 512), lambda i: (i, 0))
    o256 = pl.BlockSpec((tm, 256), lambda i: (i, 0))
    s512 = jax.ShapeDtypeStruct((n, 512), BF16)
    s256 = jax.ShapeDtypeStruct((n, 256), BF16)
    tspec = lambda rows: pl.BlockSpec((tm // KSUB, rows, KSUB), lambda i: (i, 0, 0))
    tshape = lambda rows: jax.ShapeDtypeStruct((n // KSUB, rows, KSUB), BF16)
    sv_rows, dv_rows = S_HEADS * S_VROWS, DA_HEADS * DA_VROWS
    return pl.pallas_call(
        _prep_kernel,
        grid=(n // tm,),
        in_specs=[
            zspec(512, COL_SQ), zspec(512, COL_DQ), zspec(512, COL_DK), zspec(512, COL_DV),
            zspec(256, COL_CKV), zspec(256, COL_IQ), zspec(256, COL_IK4),
            pl.BlockSpec((tm, 256), lambda i: (i % nt, 0)),
            pl.BlockSpec((tm, 256), lambda i: (i % nt, 0)),
            const((512, 512)),
            const((1, S_KV_RANK)), const((S_KV_RANK, 2 * S_WIDTH)),
            const((1, 512)), const((1, 512)), const((1, 512)), const((1, 512)),
        ],
        out_specs=[o512, o512, tspec(sv_rows), o256, o256, o512, o512, tspec(dv_rows)],
        out_shape=[s512, s512, tshape(sv_rows), s256, s256, s512, s512, tshape(dv_rows)],
        compiler_params=_cparams(("parallel",)),
        name="attn_prep",
    )(z, z, z, z, z, z, z, rope_c, rope_s, grp, kv_g, w_kv, sq_g, sk_g, dq_g, dk_g)


def _mask_heads(q_ref, qm_ref, n_heads):
    lane_grp = lax.broadcasted_iota(I32, (1, HEAD_GROUP), 1) // ROPE_DIM
    for h in range(n_heads):
        g = h // 4
        qg = q_ref[:, g * HEAD_GROUP:(g + 1) * HEAD_GROUP].astype(F32)
        qm_ref[h] = jnp.where(lane_grp == (h % 4), qg, 0.0).T.astype(BF16)


def _softmax_group(logits, values, rows, acc_ref, m_ref):
    nh = len(logits)
    m_old = m_ref[...]
    m_new = jnp.maximum(m_old, jnp.concatenate([jnp.max(s, axis=0, keepdims=True) for s in logits], axis=0))
    alpha = jnp.exp2(m_old - m_new)
    ps = [jnp.exp2((s - m_new[j:j + 1, :]).astype(BF16)) for j, s in enumerate(logits)]
    m_ref[...] = m_new
    acc_old = acc_ref[...]
    acc_ref[...] = jnp.concatenate(
        [alpha[j:j + 1, :] * acc_old[j * rows:(j + 1) * rows, :] + _dot(values[j], ps[j])
         for j in range(nh)], axis=0)


def _attend(n_groups, logits_fn, values_fn, bufs0, bufs1, rows, acc_ref, m_ref):
    last = n_groups - 1
    group = len(bufs0)
    keys = bufs0[0].shape[1]

    def logits_stage(g, dst_refs):
        pen = jnp.where(g <= last, 0.0, NEG).astype(F32)
        for h, s in enumerate(logits_fn(jnp.minimum(g, last), pen)):
            for b in range(group):
                dst_refs[b][h] = s[b * keys:(b + 1) * keys, :]

    def softmax_stages(g, src_refs):
        for b in range(group):
            _softmax_group([src_refs[b][h] for h in range(src_refs[b].shape[0])],
                           values_fn(jnp.minimum(g, last) * group + b), rows, acc_ref, m_ref)

    logits_stage(0, bufs0)

    def body(gg, carry):
        logits_stage(2 * gg + 1, bufs1)
        softmax_stages(2 * gg, bufs0)
        logits_stage(2 * gg + 2, bufs0)
        softmax_stages(2 * gg + 1, bufs1)
        return carry

    lax.fori_loop(0, (n_groups + 1) // 2, body, 0)


def _sortable(x):
    bits = pltpu.bitcast(x, I32)
    return bits ^ ((bits >> 31) & 0x7FFFFFFF)


def _bit_planes(words):
    a = list(words)
    j, m = 16, 0x0000FFFF
    while j:
        k = 0
        while k < 32:
            t = (a[k] ^ lax.shift_right_logical(a[k + j], jnp.int32(j))) & m
            a[k] = a[k] ^ t
            a[k + j] = a[k + j] ^ (t << j)
            k = (k + j + 1) & ~j
        j >>= 1
        m = (m ^ (m << j)) & 0xFFFFFFFF
    return a


def _dsa_kernel(sq_ref, iq_ref, misc_ref, sk_ref, svt_ref, ik_ref, y_ref,
                sc_ref, bp_ref, qm_ref, iqm_ref, acc_ref, m_ref, sa_ref, sb_ref,
                *, tq, topk, idx_bits):
    i = pl.program_id(1)

    @pl.when((pl.program_id(0) == 0) & (i == 0))
    def _():
        bp_ref[...] = jnp.zeros_like(bp_ref)

    nks = (i + 1) * (tq // KSUB)
    q_idx = i * tq + lax.broadcasted_iota(I32, (1, tq), 1)

    _mask_heads(iq_ref, iqm_ref, IDX_HEADS)
    w_t = (misc_ref[...] * (IDX_WIDTH ** -0.5)).T
    w_rows = [w_t[MISC_IW + h:MISC_IW + h + 1, :] for h in range(IDX_HEADS)]

    k_pair = lax.broadcasted_iota(I32, (2 * KSUB, 1), 0)

    def score_body(j, carry):
        ikb = ik_ref[pl.ds(pl.multiple_of(j * (2 * KSUB), 2 * KSUB), 2 * KSUB), :]
        s = jnp.zeros((2 * KSUB, tq), F32)
        for h in range(IDX_HEADS):
            s = s + w_rows[h] * jnp.maximum(_dot(ikb, iqm_ref[h]), 0.0)
        key = jnp.where(j * (2 * KSUB) + k_pair <= q_idx, _sortable(s), INT_MIN)
        sc_ref[pl.ds(2 * j, 2)] = key.reshape(2, KSUB, tq)
        ob = key ^ INT_MIN
        planes = _bit_planes([ob[v * 8:(v + 1) * 8, :] for v in range(32)])
        bp_ref[:, pl.ds(pl.multiple_of(j * 8, 8), 8), :] = jnp.concatenate(planes, axis=0).reshape(32, 8, tq)
        return carry

    lax.fori_loop(0, nks // 2, score_body, 0)

    nblk = bp_ref.shape[1] // 8
    live_rows = lax.broadcasted_iota(I32, (nblk * 8, 1), 0) < (nks // 2) * 8

    def bit_body(t, carry):
        alive, need, thr_ob = carry
        plane = bp_ref[t]
        ones = alive & plane
        cnt = jnp.sum(lax.population_count(ones), axis=0, keepdims=True)
        take = cnt >= need
        alive = jnp.where(take, ones, alive ^ ones)
        need = jnp.where(take, need, need - cnt)
        thr_ob = thr_ob | jnp.where(take, jnp.left_shift(jnp.int32(1), 31 - t), 0)
        return alive, need, thr_ob

    alive, need, thr_ob = lax.fori_loop(
        0, 32, bit_body,
        (jnp.broadcast_to(jnp.where(live_rows, -1, 0), (nblk * 8, tq)),
         jnp.full((1, tq), topk, I32), jnp.zeros((1, tq), I32)))
    thr = thr_ob ^ INT_MIN

    word_row = lax.broadcasted_iota(I32, (nblk * 8, 1), 0)
    word_base = (word_row >> 3) * (2 * KSUB) + (word_row & 7)

    def ties_below(trial):
        nv = jnp.clip((trial - word_base + 7) >> 3, 0, 32)
        low = lax.shift_right_logical(jnp.full_like(nv, -1), jnp.minimum(nv, 31))
        mask = jnp.where(nv >= 32, -1, ~low)
        return jnp.sum(lax.population_count(alive & mask), axis=0, keepdims=True)

    def idx_body(t, cut):
        trial = cut | jnp.left_shift(jnp.int32(1), idx_bits - 1 - t)
        return jnp.where(ties_below(trial) < need, trial, cut)

    cut = lax.fori_loop(0, idx_bits, idx_body, jnp.zeros((1, tq), I32))

    thr_sel = jnp.maximum(thr, INT_MIN + 1)

    def bias_body(j, carry):
        pair = pl.ds(2 * j, 2)
        key = sc_ref[pair].reshape(2 * KSUB, tq)
        kidx = j * (2 * KSUB) + k_pair
        sel = key >= thr_sel + jnp.where(kidx <= cut, 0, 1)
        sc_ref[pair] = pltpu.bitcast(jnp.where(sel, 0.0, NEG).astype(F32), I32).reshape(2, KSUB, tq)
        return carry

    lax.fori_loop(0, nks // 2, bias_body, 0)

    _mask_heads(sq_ref, qm_ref, S_HEADS)
    acc_ref[...] = jnp.zeros_like(acc_ref)
    m_ref[...] = jnp.full_like(m_ref, NEG)

    def logits_fn(ks, pen):
        off = pl.multiple_of(ks * KSUB, KSUB)
        bias = pltpu.bitcast(sc_ref[ks], F32) + pen
        kgs = [sk_ref[pl.ds(off, KSUB), g * HEAD_GROUP:(g + 1) * HEAD_GROUP] for g in range(S_HEADS // 4)]
        return [_dot(kgs[h // 4], qm_ref[h]) + bias for h in range(S_HEADS)]

    def values_fn(ks):
        return [svt_ref[ks, h * S_VROWS:(h + 1) * S_VROWS, :] for h in range(S_HEADS)]

    _attend(nks, logits_fn, values_fn, (sa_ref,), (sb_ref,), S_VROWS, acc_ref, m_ref)

    outs = []
    for h in range(S_HEADS):
        r0 = h * S_VROWS
        outs.append(acc_ref[r0:r0 + ROPE_DIM, :] / acc_ref[r0 + ROPE_DIM:r0 + ROPE_DIM + 1, :])
    y_ref[...] = jnp.concatenate(outs, axis=0).T.astype(BF16)


def _dsa(misc, sq, sk, svt, iq, ik4, batch, seq, *, tq=256):
    n = misc.shape[0]
    tq = min(tq, seq)
    nq = seq // tq
    nsub = seq // KSUB
    topk = min(IDX_TOPK_MAX, seq // 4)
    kern = functools.partial(_dsa_kernel, tq=tq, topk=topk, idx_bits=seq.bit_length())
    return pl.pallas_call(
        kern,
        grid=(batch, nq),
        in_specs=[
            pl.BlockSpec((tq, S_WIDTH), lambda b, i: (b * nq + i, 0)),
            pl.BlockSpec((tq, IDX_WIDTH), lambda b, i: (b * nq + i, 0)),
            pl.BlockSpec((tq, LANE), lambda b, i: (b * nq + i, 0)),
            pl.BlockSpec((seq, S_WIDTH), lambda b, i: (b, 0)),
            pl.BlockSpec((nsub, S_HEADS * S_VROWS, KSUB), lambda b, i: (b, 0, 0)),
            pl.BlockSpec((seq, IDX_WIDTH), lambda b, i: (b, 0)),
        ],
        out_specs=pl.BlockSpec((tq, S_WIDTH), lambda b, i: (b * nq + i, 0)),
        out_shape=jax.ShapeDtypeStruct((n, S_WIDTH), BF16),
        scratch_shapes=[
            pltpu.VMEM((nsub, KSUB, tq), I32),
            pltpu.VMEM((32, (nsub // 2) * 8, tq), I32),
            pltpu.VMEM((S_HEADS, HEAD_GROUP, tq), BF16),
            pltpu.VMEM((IDX_HEADS, HEAD_GROUP, tq), BF16),
            pltpu.VMEM((S_HEADS * S_VROWS, tq), F32),
            pltpu.VMEM((S_HEADS, tq), F32),
            pltpu.VMEM((S_HEADS, KSUB, tq), F32),
            pltpu.VMEM((S_HEADS, KSUB, tq), F32),
        ],
        compiler_params=_cparams(("arbitrary", "arbitrary")),
        name="dsa_attn",
    )(sq, iq, misc, sk, svt, ik4)


def _diff_kernel(q_ref, k_ref, vt_ref, lam_ref, og_ref, y_ref, qm_ref, acc_ref, m_ref,
                 sa_ref, sb_ref, *, tq, lam_init):
    i = pl.program_id(1)
    q_idx = i * tq + lax.broadcasted_iota(I32, (1, tq), 1)
    k_pair = lax.broadcasted_iota(I32, (2 * KSUB, 1), 0)

    _mask_heads(q_ref, qm_ref, DA_MAPS)
    acc_ref[...] = jnp.zeros_like(acc_ref)
    m_ref[...] = jnp.full_like(m_ref, NEG)

    def logits_fn(j, pen):
        off = pl.multiple_of(j * (2 * KSUB), 2 * KSUB)
        bias = jnp.where(j * (2 * KSUB) + k_pair <= q_idx, 0.0, NEG).astype(F32) + pen
        kgs = [k_ref[pl.ds(off, 2 * KSUB), g * HEAD_GROUP:(g + 1) * HEAD_GROUP] for g in range(DA_MAPS // 4)]
        return [_dot(kgs[m // 4], qm_ref[m]) + bias for m in range(DA_MAPS)]

    def values_fn(j):
        heads = [jnp.concatenate([vt_ref[2 * j, hd * DA_VROWS:(hd + 1) * DA_VROWS, :],
                                  vt_ref[2 * j + 1, hd * DA_VROWS:(hd + 1) * DA_VROWS, :]], axis=1)
                 for hd in range(DA_HEADS)]
        return [heads[m // 2] for m in range(DA_MAPS)]

    assert tq == 2 * KSUB
    _attend(i + 1, logits_fn, values_fn, (sa_ref,), (sb_ref,), DA_VROWS, acc_ref, m_ref)

    lam = lam_ref[...]
    p01 = jnp.sum(lam[0:1] * lam[1:2], axis=-1, keepdims=True)
    p23 = jnp.sum(lam[2:3] * lam[3:4], axis=-1, keepdims=True)
    lam_val = jnp.exp(p01) - jnp.exp(p23) + lam_init
    def normalised(m):
        r0 = m * DA_VROWS
        return acc_ref[r0:r0 + DA_V_DIM, :] / acc_ref[r0 + DA_V_DIM:r0 + DA_V_DIM + 1, :]

    outs = []
    for hd in range(DA_HEADS):
        o = normalised(2 * hd) - lam_val * normalised(2 * hd + 1)
        ms = jnp.mean(o * o, axis=0, keepdims=True)
        outs.append(o * lax.rsqrt(ms + EPS) * og_ref[...] * (1.0 - lam_init))
    y_ref[...] = jnp.concatenate(outs, axis=0).T.astype(BF16)


def _diff_attn(dq, dk, dvt, lam, out_g_col, batch, seq, lam_init, *, tq=256):
    n = dq.shape[0]
    tq = min(tq, seq)
    nq = seq // tq
    nsub = seq // KSUB
    kern = functools.partial(_diff_kernel, tq=tq, lam_init=lam_init)
    return pl.pallas_call(
        kern,
        grid=(batch, nq),
        in_specs=[
            pl.BlockSpec((tq, 512), lambda b, i: (b * nq + i, 0)),
            pl.BlockSpec((seq, 512), lambda b, i: (b, 0)),
            pl.BlockSpec((nsub, DA_HEADS * DA_VROWS, KSUB), lambda b, i: (b, 0, 0)),
            pl.BlockSpec((4, ROPE_DIM), lambda b, i: (0, 0)),
            pl.BlockSpec((DA_V_DIM, 1), lambda b, i: (0, 0)),
        ],
        out_specs=pl.BlockSpec((tq, DA_WIDTH), lambda b, i: (b * nq + i, 0)),
        out_shape=jax.ShapeDtypeStruct((n, DA_WIDTH), BF16),
        scratch_shapes=[
            pltpu.VMEM((DA_MAPS, HEAD_GROUP, tq), BF16),
            pltpu.VMEM((DA_MAPS * DA_VROWS, tq), F32),
            pltpu.VMEM((DA_MAPS, tq), F32),
            pltpu.VMEM((DA_MAPS, 2 * KSUB, tq), F32),
            pltpu.VMEM((DA_MAPS, 2 * KSUB, tq), F32),
        ],
        compiler_params=_cparams(("parallel", "arbitrary")),
        name="diff_attn",
    )(dq, dk, dvt, lam, out_g_col)


def _merge_kernel(x_ref, ya_ref, yb_ref, yc_ref, gp_ref, bg_ref, wb_ref, wo_ref, ng_ref, xo_ref, xn_ref):
    merged = None
    for br, y_ref in enumerate((ya_ref, yb_ref, yc_ref)):
        sl = slice(br * D_MODEL, (br + 1) * D_MODEL)
        gate = _sigmoid(gp_ref[:, sl].astype(F32) + bg_ref[:, sl])
        term = gate * _dot(y_ref[...], wb_ref[br])
        merged = term if merged is None else merged + term
    xo = x_ref[...] + _dot(merged.astype(BF16), wo_ref[...])
    xo_ref[...] = xo
    ms = jnp.mean(xo * xo, axis=-1, keepdims=True)
    xn_ref[...] = (xo * lax.rsqrt(ms + EPS) * ng_ref[...]).astype(BF16)


def _merge(x2, ya, yb, yc, z, b_gate, w_branch, w_out, norm_g, *, tm=512):
    n = x2.shape[0]
    tm = min(tm, n)
    row = lambda w: pl.BlockSpec((tm, w), lambda i: (i, 0))
    return pl.pallas_call(
        _merge_kernel,
        grid=(n // tm,),
        in_specs=[
            row(D_MODEL), row(512), row(512), row(512),
            pl.BlockSpec((tm, N_BRANCH * D_MODEL), lambda i: (i, COL_GATE)),
            pl.BlockSpec((1, N_BRANCH * D_MODEL), lambda i: (0, 0)),
            pl.BlockSpec((N_BRANCH, 512, D_MODEL), lambda i: (0, 0, 0)),
            pl.BlockSpec((D_MODEL, D_MODEL), lambda i: (0, 0)),
            pl.BlockSpec((1, D_MODEL), lambda i: (0, 0)),
        ],
        out_specs=[row(D_MODEL), row(D_MODEL)],
        out_shape=[jax.ShapeDtypeStruct((n, D_MODEL), F32), jax.ShapeDtypeStruct((n, D_MODEL), BF16)],
        compiler_params=_cparams(("parallel",)),
        name="merge_out",
    )(x2, ya, yb, yc, z, b_gate, w_branch, w_out, norm_g)


def _ffn_kernel(xn_ref, x_ref, wg_ref, wu_ref, wd_ref, o_ref, *, fc):
    xn = xn_ref[...]
    acc = x_ref[...]
    for c in range(FF_DIM // fc):
        cols = slice(c * fc, (c + 1) * fc)
        g = _dot(xn, wg_ref[:, cols])
        u = _dot(xn, wu_ref[:, cols])
        acc = acc + _dot((g * _sigmoid(g) * u).astype(BF16), wd_ref[cols, :])
    o_ref[...] = acc


def _ffn(xn, x2, wg, wu, wd, *, tm=512, fc=256):
    n = x2.shape[0]
    tm = min(tm, n)
    resident = lambda shape: pl.BlockSpec(shape, lambda i: (0, 0), pipeline_mode=pl.Buffered(1))
    return pl.pallas_call(
        functools.partial(_ffn_kernel, fc=fc),
        grid=(n // tm,),
        in_specs=[
            pl.BlockSpec((tm, D_MODEL), lambda i: (i, 0)),
            pl.BlockSpec((tm, D_MODEL), lambda i: (i, 0)),
            resident((D_MODEL, FF_DIM)),
            resident((D_MODEL, FF_DIM)),
            resident((FF_DIM, D_MODEL)),
        ],
        out_specs=pl.BlockSpec((tm, D_MODEL), lambda i: (i, 0)),
        out_shape=jax.ShapeDtypeStruct((n, D_MODEL), F32),
        compiler_params=_cparams(("parallel",)),
        name="ffn",
    )(xn, x2, wg, wu, wd)


def _rope_tables(seq):
    inv = 1.0 / jnp.power(ROPE_THETA, jnp.arange(0, ROPE_DIM, 2, dtype=F32) / ROPE_DIM)
    ang = jnp.arange(seq, dtype=F32)[:, None] * inv[None, :]
    cos, sin = jnp.cos(ang), jnp.sin(ang)
    c64 = jnp.concatenate([cos, cos], axis=1)
    s64 = jnp.concatenate([-sin, sin], axis=1)
    return jnp.tile(c64, (1, 4)), jnp.tile(s64, (1, 4))


def _layer(x2, li, batch, seq, consts, norm_mix_g, w_in_l, b_gate, conv_w, conv_b, gate_b, m_norm_g,
           kv_norm_g, w_kv_up, sq_g, sk_g, dq_g, dk_g, lam, d_out_g, w_branch, w_out,
           norm_ffn_g, w_gate_up, w_down):
    rope_c, rope_s, grp, tri, shifts = consts
    lam_init = 0.8 - 0.6 * math.exp(-0.3 * li)
    tile8 = lambda g: jnp.tile(g, 8)[None, :]

    z, misc = _inproj(x2, norm_mix_g[None, :], w_in_l)

    gate_row = jnp.zeros((1, LANE), F32).at[0, MISC_IF:MISC_IF + 2 * M_HEADS].set(gate_b)
    ya = _mlstm(z, misc, conv_w, conv_b[None, :], gate_row, m_norm_g.reshape(1, M_WIDTH), tri, shifts,
                batch, seq)

    sq, sk, svt, iq, ik4, dq, dk, dvt = _prep(
        z, rope_c, rope_s, grp, kv_norm_g[None, :], w_kv_up.astype(BF16),
        tile8(sq_g), tile8(sk_g), tile8(dq_g), tile8(dk_g), seq)
    yb = _dsa(misc, sq, sk, svt, iq, ik4, batch, seq)
    yc = _diff_attn(dq, dk, dvt, lam, d_out_g[:, None], batch, seq, lam_init)

    xo, xn = _merge(x2, ya, yb, yc, z, b_gate[None, :], w_branch.astype(BF16), w_out.astype(BF16),
                    norm_ffn_g[None, :])
    return _ffn(xn, xo, w_gate_up[:, :FF_DIM].astype(BF16), w_gate_up[:, FF_DIM:].astype(BF16),
                w_down.astype(BF16))


def kernel(x, norm_mix_g, w_in, b_gate, mlstm_conv_w, mlstm_conv_b, mlstm_gate_b, mlstm_norm_g,
           dsa_kv_norm_g, dsa_w_kv_up, dsa_q_norm_g, dsa_k_norm_g, diff_q_norm_g, diff_k_norm_g,
           diff_lambda, diff_out_norm_g, w_branch, w_out, norm_ffn_g, w_gate_up, w_down):
    batch, seq, d = x.shape
    depth = w_in.shape[0]
    assert w_in.shape[2] == IN_COLS and d == D_MODEL
    rope_c, rope_s = _rope_tables(seq)
    gi = jnp.arange(512) // ROPE_DIM
    grp = (gi[:, None] == gi[None, :]).astype(BF16)
    ti = jnp.arange(M_CHUNK)
    tri = (ti[:, None] >= ti[None, :]).astype(BF16)
    ri = jnp.arange(M_TAIL + M_CHUNK)
    shifts = jnp.stack([(ri[None, :] == ti[:, None] + M_TAIL - j) for j in range(1, M_CONV)]).astype(BF16)
    consts = (rope_c, rope_s, grp, tri, shifts)
    w_in_l = _wlayout(w_in)
    x2 = x.reshape(batch * seq, d)
    for li in range(depth):
        x2 = _layer(x2, li, batch, seq, consts, norm_mix_g[li], w_in_l[li], b_gate[li], mlstm_conv_w[li],
                    mlstm_conv_b[li], mlstm_gate_b[li], mlstm_norm_g[li], dsa_kv_norm_g[li],
                    dsa_w_kv_up[li], dsa_q_norm_g[li], dsa_k_norm_g[li], diff_q_norm_g[li],
                    diff_k_norm_g[li], diff_lambda[li], diff_out_norm_g[li], w_branch[li], w_out[li],
                    norm_ffn_g[li], w_gate_up[li], w_down[li])
    return x2.reshape(batch, seq, d)
```

```python
import functools
import math

import jax
import jax.numpy as jnp
from jax import lax
from jax.experimental import pallas as pl
from jax.experimental.pallas import tpu as pltpu

F32 = jnp.float32
BF16 = jnp.bfloat16
I32 = jnp.int32
I16 = jnp.int16

D_MODEL = 1024
EPS = 1e-6
ROPE_DIM = 64
ROPE_HALF = ROPE_DIM // 2
ROPE_THETA = 10000.0

M_HEADS = 4
M_HEAD_DIM = 128
M_WIDTH = M_HEADS * M_HEAD_DIM
M_CONV = 4
M_CHUNK = 128
M_TAIL = 16
M_INIT = -1e30

S_HEADS = 8
S_WIDTH = S_HEADS * ROPE_DIM
S_KV_RANK = 256
IDX_HEADS = 4
IDX_WIDTH = IDX_HEADS * ROPE_DIM
IDX_TOPK_MAX = 256

DA_HEADS = 4
DA_MAPS = 2 * DA_HEADS
DA_V_DIM = 2 * ROPE_DIM
DA_WIDTH = DA_HEADS * DA_V_DIM

N_BRANCH = 3
FF_DIM = 2816

NEG = -1e30
LOG2E = math.log2(math.e)
INT_MIN = -(2**31)
HALF16 = 2**15
LANE = 128
HEAD_GROUP = 256
KSUB = 128
ONES_ROWS = 16
S_VROWS = ROPE_DIM + ONES_ROWS
DA_VROWS = DA_V_DIM + ONES_ROWS

W_IN_SIZES = (2 * M_WIDTH, M_WIDTH, M_WIDTH, 2 * M_HEADS, S_WIDTH, S_KV_RANK, IDX_WIDTH, ROPE_DIM,
              IDX_HEADS, 2 * DA_HEADS * ROPE_DIM, 2 * DA_HEADS * ROPE_DIM, DA_WIDTH, N_BRANCH * D_MODEL)
IN_COLS = sum(W_IN_SIZES)

COL_GATE = 0
COL_MQK = 3072
COL_MV = 4096
COL_MO = 4608
COL_SQ = 5120
COL_DQ = 5632
COL_DK = 6144
COL_DV = 6656
COL_CKV = 7168
COL_IQ = 7424
COL_IK4 = 7680
COL_MISC = 7936
MISC_IF = 0
MISC_IW = 8
Z_COLS = 8192

VMEM_LIMIT = 56 * 1024 * 1024


def _cparams(sem, flags=None):
    return pltpu.CompilerParams(dimension_semantics=sem, vmem_limit_bytes=VMEM_LIMIT, flags=flags)


def _sigmoid(x):
    return 1.0 / (1.0 + jnp.exp(-x))


def _dot(a, b):
    return jnp.dot(a, b, preferred_element_type=F32)


def _dot_nt(a, b):
    return lax.dot_general(a, b, (((1,), (1,)), ((), ())), preferred_element_type=F32)


def _dot_tn(a, b):
    return lax.dot_general(a, b, (((0,), (0,)), ((), ())), preferred_element_type=F32)


def _split3(x):
    hi = x.astype(BF16)
    r1 = x - hi.astype(F32)
    mid = r1.astype(BF16)
    lo = (r1 - mid.astype(F32)).astype(BF16)
    return hi, mid, lo


def _wlayout_kernel(w_ref, o_ref):
    w = w_ref[...]
    offs = [0]
    for s in W_IN_SIZES:
        offs.append(offs[-1] + s)
    seg = [w[:, offs[k]:offs[k + 1]] for k in range(len(W_IN_SIZES))]
    m_qk, m_v, m_o, m_if, s_q, s_ckv, i_q, i_k, i_w, d_q, d_k, d_v, g_pre = seg
    rows = w.shape[0]
    misc = jnp.concatenate([m_if, i_w, jnp.zeros((rows, LANE - 2 * M_HEADS - IDX_HEADS), F32)], axis=1)
    ik4 = jnp.concatenate([i_k] * IDX_HEADS, axis=1)
    tail = jnp.zeros((rows, Z_COLS - COL_MISC - LANE), F32)
    for col, val in ((COL_GATE, g_pre), (COL_MQK, m_qk), (COL_MV, m_v), (COL_MO, m_o), (COL_SQ, s_q),
                     (COL_DQ, d_q), (COL_DK, d_k), (COL_DV, d_v), (COL_CKV, s_ckv), (COL_IQ, i_q),
                     (COL_IK4, ik4), (COL_MISC, misc), (COL_MISC + LANE, tail)):
        o_ref[:, col:col + val.shape[1]] = val.astype(BF16)


def _wlayout(w_in, *, tr=256):
    depth, rows, cols = w_in.shape
    return pl.pallas_call(
        _wlayout_kernel,
        grid=(depth, rows // tr),
        in_specs=[pl.BlockSpec((None, tr, cols), lambda l, i: (l, i, 0))],
        out_specs=pl.BlockSpec((None, tr, Z_COLS), lambda l, i: (l, i, 0)),
        out_shape=jax.ShapeDtypeStruct((depth, rows, Z_COLS), BF16),
        compiler_params=_cparams(("parallel", "parallel")),
        name="w_layout",
    )(w_in)


def _inproj_kernel(x_ref, g_ref, w_ref, z_ref, misc_ref, *, tn):
    x = x_ref[...]
    ms = jnp.mean(x * x, axis=-1, keepdims=True)
    xn = (x * lax.rsqrt(ms + EPS) * g_ref[...]).astype(BF16)
    for c in range(Z_COLS // tn):
        r = _dot(xn, w_ref[:, c * tn:(c + 1) * tn])
        z_ref[:, c * tn:(c + 1) * tn] = r.astype(BF16)
        if c == COL_MISC // tn:
            misc_ref[...] = r[:, COL_MISC % tn:COL_MISC % tn + LANE]


def _inproj(x2, g, w, *, tm=512, tn=1024):
    n = x2.shape[0]
    tm = min(tm, n)
    return pl.pallas_call(
        functools.partial(_inproj_kernel, tn=tn),
        grid=(n // tm,),
        in_specs=[
            pl.BlockSpec((tm, D_MODEL), lambda i: (i, 0)),
            pl.BlockSpec((1, D_MODEL), lambda i: (0, 0)),
            pl.BlockSpec((D_MODEL, Z_COLS), lambda i: (0, 0), pipeline_mode=pl.Buffered(1)),
        ],
        out_specs=[pl.BlockSpec((tm, Z_COLS), lambda i: (i, 0)),
                   pl.BlockSpec((tm, LANE), lambda i: (i, 0))],
        out_shape=[jax.ShapeDtypeStruct((n, Z_COLS), BF16), jax.ShapeDtypeStruct((n, LANE), F32)],
        compiler_params=_cparams(("parallel",)),
        name="inproj",
    )(x2, g, w)


def _mlstm_kernel(qk_ref, v_ref, o_ref, misc_ref, cw_ref, cb_ref, gb_ref, ng_ref, tri_ref, sh_ref, y_ref,
                  tail_ref, qkc_ref, c_ref, n_ref, m_ref):
    L = M_CHUNK
    dh = M_HEAD_DIM

    @pl.when(pl.program_id(1) == 0)
    def _():
        tail_ref[...] = jnp.zeros_like(tail_ref)
        c_ref[...] = jnp.zeros_like(c_ref)
        n_ref[...] = jnp.zeros_like(n_ref)
        m_ref[...] = jnp.full_like(m_ref, M_INIT)

    xb = qk_ref[...]
    ext = jnp.concatenate([tail_ref[...], xb], axis=0)
    cw = cw_ref[...]
    acc = xb.astype(F32) * cw[M_CONV - 1:M_CONV] + cb_ref[...]
    for j in range(1, M_CONV):
        acc = acc + _dot(sh_ref[j - 1], ext) * cw[M_CONV - 1 - j:M_CONV - j]
    qkc_ref[...] = acc * _sigmoid(acc)
    tail_ref[...] = xb[L - M_TAIL:L]

    gates = misc_ref[...] + gb_ref[...]
    lf = jnp.minimum(gates, 0.0) - jnp.log(1.0 + jnp.exp(-jnp.abs(gates)))
    tri = tri_ref[...]
    hi, mid, lo = _split3(lf)
    bcum = _dot(tri, hi) + _dot(tri, mid) + _dot(tri, lo)
    bcum_t = bcum.T
    gates_t = gates.T

    t_idx = lax.broadcasted_iota(I32, (L, L), 0)
    s_idx = lax.broadcasted_iota(I32, (L, L), 1)
    causal = t_idx >= s_idx

    ys, cs, ns, ms_new = [], [], [], []
    for h in range(M_HEADS):
        sl = slice(h * dh, (h + 1) * dh)
        q = qkc_ref[:, h * dh:(h + 1) * dh]
        k = qkc_ref[:, M_WIDTH + h * dh:M_WIDTH + (h + 1) * dh] * (dh ** -0.5)
        vb = v_ref[:, sl]
        v = vb.astype(F32)
        qb, kb = q.astype(BF16), k.astype(BF16)

        b_col = bcum[:, M_HEADS + h:M_HEADS + h + 1]
        i_col = gates[:, h:h + 1]
        b_row = bcum_t[M_HEADS + h:M_HEADS + h + 1, :]
        i_row = gates_t[h:h + 1, :]
        g_tot = bcum[L - 1:L, M_HEADS + h:M_HEADS + h + 1]

        c_prev = c_ref[sl, :]
        n_prev = n_ref[h:h + 1, :]
        m_prev = m_ref[h:h + 1, 0:1]

        rmat = jnp.where(causal, i_row - b_row, -jnp.inf)
        mx = jnp.maximum(m_prev, jnp.max(rmat, axis=-1, keepdims=True))
        m_t = b_col + mx
        sw = jnp.exp(rmat - mx) * _dot_nt(qb, kb)
        s_inter = jnp.exp(m_prev - mx)
        num = _dot(sw.astype(BF16), vb) + s_inter * _dot_nt(qb, c_prev.astype(BF16))
        den = jnp.sum(sw, axis=-1, keepdims=True) + s_inter * jnp.sum(q * n_prev, axis=-1, keepdims=True)
        dmax = jnp.maximum(jnp.abs(den), jnp.exp(-m_t))
        mnum = jnp.mean(num * num, axis=-1, keepdims=True)
        scale = lax.rsqrt(mnum + EPS * dmax * dmax)
        ys.append(_sigmoid(o_ref[:, sl].astype(F32)) * (num * scale * ng_ref[:, sl]))

        a_col = g_tot - b_col + i_col
        m_loc = jnp.max(a_col, axis=0, keepdims=True)
        w_loc = jnp.exp(a_col - m_loc)
        c_loc = _dot_tn((v * w_loc).astype(BF16), kb)
        n_loc = jnp.sum(k * w_loc, axis=0, keepdims=True)
        m_new = jnp.maximum(g_tot + m_prev, m_loc)
        s_old = jnp.exp(g_tot + m_prev - m_new)
        s_loc = jnp.exp(m_loc - m_new)
        cs.append(s_old * c_prev + s_loc * c_loc)
        ns.append(s_old * n_prev + s_loc * n_loc)
        ms_new.append(jnp.broadcast_to(m_new, (1, LANE)))

    y_ref[...] = jnp.concatenate(ys, axis=1).astype(BF16)
    c_ref[...] = jnp.concatenate(cs, axis=0)
    n_ref[...] = jnp.concatenate(ns, axis=0)
    m_ref[...] = jnp.concatenate(ms_new, axis=0)


def _mlstm(z, misc, conv_w, conv_b, gate_b_row, norm_g_row, tri, shifts, batch, seq):
    n = z.shape[0]
    nc = seq // M_CHUNK
    L = M_CHUNK
    row = lambda b, c: b * nc + c
    return pl.pallas_call(
        _mlstm_kernel,
        grid=(batch, nc),
        in_specs=[
            pl.BlockSpec((L, 2 * M_WIDTH), lambda b, c: (row(b, c), COL_MQK // (2 * M_WIDTH))),
            pl.BlockSpec((L, M_WIDTH), lambda b, c: (row(b, c), COL_MV // M_WIDTH)),
            pl.BlockSpec((L, M_WIDTH), lambda b, c: (row(b, c), COL_MO // M_WIDTH)),
            pl.BlockSpec((L, LANE), lambda b, c: (row(b, c), 0)),
            pl.BlockSpec((M_CONV, 2 * M_WIDTH), lambda b, c: (0, 0)),
            pl.BlockSpec((1, 2 * M_WIDTH), lambda b, c: (0, 0)),
            pl.BlockSpec((1, LANE), lambda b, c: (0, 0)),
            pl.BlockSpec((1, M_WIDTH), lambda b, c: (0, 0)),
            pl.BlockSpec((L, L), lambda b, c: (0, 0)),
            pl.BlockSpec((M_CONV - 1, L, M_TAIL + L), lambda b, c: (0, 0, 0)),
        ],
        out_specs=pl.BlockSpec((L, M_WIDTH), lambda b, c: (row(b, c), 0)),
        out_shape=jax.ShapeDtypeStruct((n, M_WIDTH), BF16),
        scratch_shapes=[
            pltpu.VMEM((M_TAIL, 2 * M_WIDTH), BF16),
            pltpu.VMEM((L, 2 * M_WIDTH), F32),
            pltpu.VMEM((M_HEADS * M_HEAD_DIM, M_HEAD_DIM), F32),
            pltpu.VMEM((M_HEADS, M_HEAD_DIM), F32),
            pltpu.VMEM((M_HEADS, LANE), F32),
        ],
        compiler_params=_cparams(("parallel", "arbitrary")),
        name="mlstm",
    )(z, z, z, misc, conv_w, conv_b, gate_b_row, norm_g_row, tri, shifts)


def _rope(x, c, s):
    w = x.shape[-1]
    lane = lax.broadcasted_iota(I32, (1, w), 1)
    fwd = pltpu.roll(x, ROPE_HALF, axis=1)
    bwd = pltpu.roll(x, w - ROPE_HALF, axis=1)
    swapped = jnp.where((lane & (ROPE_DIM - 1)) < ROPE_HALF, bwd, fwd)
    return x * c + swapped * s


def _head_rms(x, g, grp):
    sq = x * x
    hi = sq.astype(BF16)
    lo = (sq - hi.astype(F32)).astype(BF16)
    ss = _dot(hi, grp) + _dot(lo, grp)
    return x * lax.rsqrt(ss * (1.0 / ROPE_DIM) + EPS) * g


def _store_transposed(o_ref, v, dv):
    ones = jnp.ones((ONES_ROWS, KSUB), BF16)
    for j in range(o_ref.shape[0]):
        vt = v[j * KSUB:(j + 1) * KSUB, :].T.astype(BF16)
        parts = []
        for h in range(vt.shape[0] // dv):
            parts += [vt[h * dv:(h + 1) * dv, :], ones]
        o_ref[j] = jnp.concatenate(parts, axis=0)


def _prep_kernel(sq_ref, dq_ref, dk_ref, dv_ref, ckv_ref, iq_ref, ik_ref, c_ref, s_ref, grp_ref,
                 kvg_ref, wkv_ref, sqg_ref, skg_ref, dqg_ref, dkg_ref,
                 sq_o, sk_o, sv_o, iq_o, ik_o, dq_o, dk_o, dv_o):
    c256 = c_ref[...]
    s256 = s_ref[...]
    c512 = jnp.concatenate([c256, c256], axis=1)
    s512 = jnp.concatenate([s256, s256], axis=1)
    grp = grp_ref[...]
    qscale = ROPE_DIM ** -0.5 * LOG2E

    f32 = lambda ref: ref[...].astype(F32)

    sq_o[...] = (_rope(_head_rms(f32(sq_ref), sqg_ref[...], grp), c512, s512) * qscale).astype(BF16)

    ckv = f32(ckv_ref)
    ms = jnp.mean(ckv * ckv, axis=-1, keepdims=True)
    ckvn = (ckv * lax.rsqrt(ms + EPS) * kvg_ref[...]).astype(BF16)
    kv = _dot(ckvn, wkv_ref[...])
    sk_o[...] = _rope(_head_rms(kv[:, :S_WIDTH], skg_ref[...], grp), c512, s512).astype(BF16)
    _store_transposed(sv_o, kv[:, S_WIDTH:], ROPE_DIM)

    iq_o[...] = _rope(f32(iq_ref), c256, s256).astype(BF16)
    ik_o[...] = _rope(f32(ik_ref), c256, s256).astype(BF16)

    dq_o[...] = (_rope(_head_rms(f32(dq_ref), dqg_ref[...], grp), c512, s512) * qscale).astype(BF16)
    dk_o[...] = _rope(_head_rms(f32(dk_ref), dkg_ref[...], grp), c512, s512).astype(BF16)
    _store_transposed(dv_o, f32(dv_ref), DA_V_DIM)


def _prep(z, rope_c, rope_s, grp, kv_g, w_kv, sq_g, sk_g, dq_g, dk_g, seq, *, tm=512):
    n = z.shape[0]
    tm = min(tm, seq)
    nt = seq // tm
    zspec = lambda w, col: pl.BlockSpec((tm, w), lambda i: (i, col // w))
    const = lambda shape: pl.BlockSpec(shape, lambda i: (0, 0))
    o512 = pl.BlockSpec((tm, 512), lambda i: (i, 0))
    o256 = pl.BlockSpec((tm, 256), lambda i: (i, 0))
    s512 = jax.ShapeDtypeStruct((n, 512), BF16)
    s256 = jax.ShapeDtypeStruct((n, 256), BF16)
    tspec = lambda rows: pl.BlockSpec((tm // KSUB, rows, KSUB), lambda i: (i, 0, 0))
    tshape = lambda rows: jax.ShapeDtypeStruct((n // KSUB, rows, KSUB), BF16)
    sv_rows, dv_rows = S_HEADS * S_VROWS, DA_HEADS * DA_VROWS
    return pl.pallas_call(
        _prep_kernel,
        grid=(n // tm,),
        in_specs=[
            zspec(512, COL_SQ), zspec(512, COL_DQ), zspec(512, COL_DK), zspec(512, COL_DV),
            zspec(256, COL_CKV), zspec(256, COL_IQ), zspec(256, COL_IK4),
            pl.BlockSpec((tm, 256), lambda i: (i % nt, 0)),
            pl.BlockSpec((tm, 256), lambda i: (i % nt, 0)),
            const((512, 512)),
            const((1, S_KV_RANK)), const((S_KV_RANK, 2 * S_WIDTH)),
            const((1, 512)), const((1, 512)), const((1, 512)), const((1, 512)),
        ],
        out_specs=[o512, o512, tspec(sv_rows), o256, o256, o512, o512, tspec(dv_rows)],
        out_shape=[s512, s512, tshape(sv_rows), s256, s256, s512, s512, tshape(dv_rows)],
        compiler_params=_cparams(("parallel",)),
        name="attn_prep",
    )(z, z, z, z, z, z, z, rope_c, rope_s, grp, kv_g, w_kv, sq_g, sk_g, dq_g, dk_g)


def _mask_heads(q_ref, qm_ref, n_heads):
    lane_grp = lax.broadcasted_iota(I32, (1, HEAD_GROUP), 1) // ROPE_DIM
    for h in range(n_heads):
        g = h // 4
        qg = q_ref[:, g * HEAD_GROUP:(g + 1) * HEAD_GROUP].astype(F32)
        qm_ref[h] = jnp.where(lane_grp == (h % 4), qg, 0.0).T.astype(BF16)


def _softmax_group(logits, values, rows, acc_ref, m_ref):
    nh = len(logits)
    m_old = m_ref[...]
    m_new = jnp.maximum(m_old, jnp.concatenate([jnp.max(s, axis=0, keepdims=True) for s in logits], axis=0))
    alpha = jnp.exp2(m_old - m_new)
    ps = [jnp.exp2((s - m_new[j:j + 1, :]).astype(BF16)) for j, s in enumerate(logits)]
    m_ref[...] = m_new
    acc_old = acc_ref[...]
    acc_ref[...] = jnp.concatenate(
        [alpha[j:j + 1, :] * acc_old[j * rows:(j + 1) * rows, :] + _dot(values[j], ps[j])
         for j in range(nh)], axis=0)


def _attend(n_groups, logits_fn, values_fn, bufs0, bufs1, rows, acc_ref, m_ref):
    last = n_groups - 1
    group = len(bufs0)
    keys = bufs0[0].shape[1]

    def logits_stage(g, dst_refs):
        pen = jnp.where(g <= last, 0.0, NEG).astype(F32)
        for h, s in enumerate(logits_fn(jnp.minimum(g, last), pen)):
            for b in range(group):
                dst_refs[b][h] = s[b * keys:(b + 1) * keys, :]

    def softmax_stages(g, src_refs):
        for b in range(group):
            _softmax_group([src_refs[b][h] for h in range(src_refs[b].shape[0])],
                           values_fn(jnp.minimum(g, last) * group + b), rows, acc_ref, m_ref)

    logits_stage(0, bufs0)

    def body(gg, carry):
        logits_stage(2 * gg + 1, bufs1)
        softmax_stages(2 * gg, bufs0)
        logits_stage(2 * gg + 2, bufs0)
        softmax_stages(2 * gg + 1, bufs1)
        return carry

    lax.fori_loop(0, (n_groups + 1) // 2, body, 0)


def _sortable(x):
    bits = pltpu.bitcast(x, I32)
    return bits ^ ((bits >> 31) & 0x7FFFFFFF)


def _bit_planes(words):
    a = list(words)
    j, m = 16, 0x0000FFFF
    while j:
        k = 0
        while k < 32:
            t = (a[k] ^ lax.shift_right_logical(a[k + j], jnp.int32(j))) & m
            a[k] = a[k] ^ t
            a[k + j] = a[k + j] ^ (t << j)
            k = (k + j + 1) & ~j
        j >>= 1
        m = (m ^ (m << j)) & 0xFFFFFFFF
    return a


def _dsa_kernel(sq_ref, iq_ref, misc_ref, sk_ref, svt_ref, ik_ref, y_ref,
                sc_ref, bp_ref, qm_ref, iqm_ref, acc_ref, m_ref, sa_ref, sb_ref, res_ref,
                *, tq, topk):
    i = pl.program_id(1)

    @pl.when((pl.program_id(0) == 0) & (i == 0))
    def _():
        bp_ref[...] = jnp.zeros_like(bp_ref)

    nks = (i + 1) * (tq // KSUB)
    q_idx = i * tq + lax.broadcasted_iota(I32, (1, tq), 1)

    _mask_heads(iq_ref, iqm_ref, IDX_HEADS)
    w_t = (misc_ref[...] * (IDX_WIDTH ** -0.5)).T
    w_rows = [w_t[MISC_IW + h:MISC_IW + h + 1, :] for h in range(IDX_HEADS)]

    k_pair = lax.broadcasted_iota(I32, (2 * KSUB, 1), 0)

    def score_body(j, carry):
        ikb = ik_ref[pl.ds(pl.multiple_of(j * (2 * KSUB), 2 * KSUB), 2 * KSUB), :]
        s = jnp.zeros((2 * KSUB, tq), F32)
        for h in range(IDX_HEADS):
            s = s + w_rows[h] * jnp.maximum(_dot(ikb, iqm_ref[h]), 0.0)
        key = jnp.where(j * (2 * KSUB) + k_pair <= q_idx, _sortable(s), INT_MIN)
        sc_ref[pl.ds(2 * j, 2)] = key.reshape(2, KSUB, tq)
        ob = key ^ INT_MIN
        planes = _bit_planes([ob[v * 8:(v + 1) * 8, :] for v in range(32)])
        bp_ref[:, pl.ds(pl.multiple_of(j * 8, 8), 8), :] = jnp.concatenate(planes, axis=0).reshape(32, 8, tq)
        return carry

    lax.fori_loop(0, nks // 2, score_body, 0)

    nblk = bp_ref.shape[1] // 8

    def search(rows):
        live_rows = lax.broadcasted_iota(I32, (rows, 1), 0) < (nks // 2) * 8

        def bit_body(t, carry):
            alive, need, thr_ob = carry
            plane = bp_ref[t, 0:rows, :]
            ones = alive & plane
            cnt = jnp.sum(lax.population_count(ones), axis=0, keepdims=True)
            take = cnt >= need
            alive = jnp.where(take, ones, alive ^ ones)
            need = jnp.where(take, need, need - cnt)
            thr_ob = thr_ob | jnp.where(take, jnp.left_shift(jnp.int32(1), 31 - t), 0)
            return alive, need, thr_ob

        alive, need, thr_ob = lax.fori_loop(
            0, 32, bit_body,
            (jnp.broadcast_to(jnp.where(live_rows, -1, 0), (rows, tq)),
             jnp.full((1, tq), topk, I32), jnp.zeros((1, tq), I32)))

        word_row = lax.broadcasted_iota(I32, (rows, 1), 0)
        word_base = (word_row >> 3) * (2 * KSUB) + (word_row & 7)
        idx_bits = (rows * 32 - 1).bit_length()

        def ties_below(trial):
            nv = jnp.clip((trial - word_base + 7) >> 3, 0, 32)
            low = lax.shift_right_logical(jnp.full_like(nv, -1), jnp.minimum(nv, 31))
            mask = jnp.where(nv >= 32, -1, ~low)
            return jnp.sum(lax.population_count(alive & mask), axis=0, keepdims=True)

        def idx_body(t, cut):
            trial = cut | jnp.left_shift(jnp.int32(1), idx_bits - 1 - t)
            return jnp.where(ties_below(trial) < need, trial, cut)

        res_ref[0:1, :] = thr_ob ^ INT_MIN
        res_ref[1:2, :] = lax.fori_loop(0, idx_bits, idx_body, jnp.zeros((1, tq), I32))

    if nblk % 2 == 0:
        @pl.when(nks // 2 <= nblk // 2)
        def _():
            search(nblk * 4)

        @pl.when(nks // 2 > nblk // 2)
        def _():
            search(nblk * 8)
    else:
        search(nblk * 8)
    thr = res_ref[0:1, :]
    cut = res_ref[1:2, :]

    thr_sel = jnp.maximum(thr, INT_MIN + 1)

    def bias_body(j, carry):
        pair = pl.ds(2 * j, 2)
        key = sc_ref[pair].reshape(2 * KSUB, tq)
        kidx = j * (2 * KSUB) + k_pair
        sel = key >= thr_sel + jnp.where(kidx <= cut, 0, 1)
        sc_ref[pair] = pltpu.bitcast(jnp.where(sel, 0.0, NEG).astype(F32), I32).reshape(2, KSUB, tq)
        return carry

    lax.fori_loop(0, nks // 2, bias_body, 0)

    _mask_heads(sq_ref, qm_ref, S_HEADS)
    acc_ref[...] = jnp.zeros_like(acc_ref)
    m_ref[...] = jnp.full_like(m_ref, NEG)

    def logits_fn(ks, pen):
        off = pl.multiple_of(ks * KSUB, KSUB)
        bias = pltpu.bitcast(sc_ref[ks], F32) + pen
        kgs = [sk_ref[pl.ds(off, KSUB), g * HEAD_GROUP:(g + 1) * HEAD_GROUP] for g in range(S_HEADS // 4)]
        return [_dot(kgs[h // 4], qm_ref[h]) + bias for h in range(S_HEADS)]

    def values_fn(ks):
        return [svt_ref[ks, h * S_VROWS:(h + 1) * S_VROWS, :] for h in range(S_HEADS)]

    _attend(nks, logits_fn, values_fn, (sa_ref,), (sb_ref,), S_VROWS, acc_ref, m_ref)

    outs = []
    for h in range(S_HEADS):
        r0 = h * S_VROWS
        outs.append(acc_ref[r0:r0 + ROPE_DIM, :] / acc_ref[r0 + ROPE_DIM:r0 + ROPE_DIM + 1, :])
    y_ref[...] = jnp.concatenate(outs, axis=0).T.astype(BF16)


def _dsa(misc, sq, sk, svt, iq, ik4, batch, seq, *, tq=256):
    n = misc.shape[0]
    tq = min(tq, seq)
    nq = seq // tq
    nsub = seq // KSUB
    topk = min(IDX_TOPK_MAX, seq // 4)
    kern = functools.partial(_dsa_kernel, tq=tq, topk=topk)
    return pl.pallas_call(
        kern,
        grid=(batch, nq),
        in_specs=[
            pl.BlockSpec((tq, S_WIDTH), lambda b, i: (b * nq + i, 0)),
            pl.BlockSpec((tq, IDX_WIDTH), lambda b, i: (b * nq + i, 0)),
            pl.BlockSpec((tq, LANE), lambda b, i: (b * nq + i, 0)),
            pl.BlockSpec((seq, S_WIDTH), lambda b, i: (b, 0)),
            pl.BlockSpec((nsub, S_HEADS * S_VROWS, KSUB), lambda b, i: (b, 0, 0)),
            pl.BlockSpec((seq, IDX_WIDTH), lambda b, i: (b, 0)),
        ],
        out_specs=pl.BlockSpec((tq, S_WIDTH), lambda b, i: (b * nq + i, 0)),
        out_shape=jax.ShapeDtypeStruct((n, S_WIDTH), BF16),
        scratch_shapes=[
            pltpu.VMEM((nsub, KSUB, tq), I32),
            pltpu.VMEM((32, (nsub // 2) * 8, tq), I32),
            pltpu.VMEM((S_HEADS, HEAD_GROUP, tq), BF16),
            pltpu.VMEM((IDX_HEADS, HEAD_GROUP, tq), BF16),
            pltpu.VMEM((S_HEADS * S_VROWS, tq), F32),
            pltpu.VMEM((S_HEADS, tq), F32),
            pltpu.VMEM((S_HEADS, KSUB, tq), F32),
            pltpu.VMEM((S_HEADS, KSUB, tq), F32),
            pltpu.VMEM((8, tq), I32),
        ],
        compiler_params=_cparams(("arbitrary", "arbitrary")),
        name="dsa_attn",
    )(sq, iq, misc, sk, svt, ik4)


def _diff_kernel(q_ref, k_ref, vt_ref, lam_ref, og_ref, y_ref, qm_ref, acc_ref, m_ref,
                 sa_ref, sb_ref, *, tq, lam_init):
    i = pl.program_id(1)
    q_idx = i * tq + lax.broadcasted_iota(I32, (1, tq), 1)
    k_pair = lax.broadcasted_iota(I32, (2 * KSUB, 1), 0)

    _mask_heads(q_ref, qm_ref, DA_MAPS)
    acc_ref[...] = jnp.zeros_like(acc_ref)
    m_ref[...] = jnp.full_like(m_ref, NEG)

    def logits_fn(j, pen):
        off = pl.multiple_of(j * (2 * KSUB), 2 * KSUB)
        bias = jnp.where(j * (2 * KSUB) + k_pair <= q_idx, 0.0, NEG).astype(F32) + pen
        kgs = [k_ref[pl.ds(off, 2 * KSUB), g * HEAD_GROUP:(g + 1) * HEAD_GROUP] for g in range(DA_MAPS // 4)]
        return [_dot(kgs[m // 4], qm_ref[m]) + bias for m in range(DA_MAPS)]

    def values_fn(j):
        heads = [jnp.concatenate([vt_ref[2 * j, hd * DA_VROWS:(hd + 1) * DA_VROWS, :],
                                  vt_ref[2 * j + 1, hd * DA_VROWS:(hd + 1) * DA_VROWS, :]], axis=1)
                 for hd in range(DA_HEADS)]
        return [heads[m // 2] for m in range(DA_MAPS)]

    assert tq == 2 * KSUB
    _attend(i + 1, logits_fn, values_fn, (sa_ref,), (sb_ref,), DA_VROWS, acc_ref, m_ref)

    lam = lam_ref[...]
    p01 = jnp.sum(lam[0:1] * lam[1:2], axis=-1, keepdims=True)
    p23 = jnp.sum(lam[2:3] * lam[3:4], axis=-1, keepdims=True)
    lam_val = jnp.exp(p01) - jnp.exp(p23) + lam_init
    def normalised(m):
        r0 = m * DA_VROWS
        return acc_ref[r0:r0 + DA_V_DIM, :] / acc_ref[r0 + DA_V_DIM:r0 + DA_V_DIM + 1, :]

    outs = []
    for hd in range(DA_HEADS):
        o = normalised(2 * hd) - lam_val * normalised(2 * hd + 1)
        ms = jnp.mean(o * o, axis=0, keepdims=True)
        outs.append(o * lax.rsqrt(ms + EPS) * og_ref[...] * (1.0 - lam_init))
    y_ref[...] = jnp.concatenate(outs, axis=0).T.astype(BF16)


def _diff_attn(dq, dk, dvt, lam, out_g_col, batch, seq, lam_init, *, tq=256):
    n = dq.shape[0]
    tq = min(tq, seq)
    nq = seq // tq
    nsub = seq // KSUB
    kern = functools.partial(_diff_kernel, tq=tq, lam_init=lam_init)
    return pl.pallas_call(
        kern,
        grid=(batch, nq),
        in_specs=[
            pl.BlockSpec((tq, 512), lambda b, i: (b * nq + i, 0)),
            pl.BlockSpec((seq, 512), lambda b, i: (b, 0)),
            pl.BlockSpec((nsub, DA_HEADS * DA_VROWS, KSUB), lambda b, i: (b, 0, 0)),
            pl.BlockSpec((4, ROPE_DIM), lambda b, i: (0, 0)),
            pl.BlockSpec((DA_V_DIM, 1), lambda b, i: (0, 0)),
        ],
        out_specs=pl.BlockSpec((tq, DA_WIDTH), lambda b, i: (b * nq + i, 0)),
        out_shape=jax.ShapeDtypeStruct((n, DA_WIDTH), BF16),
        scratch_shapes=[
            pltpu.VMEM((DA_MAPS, HEAD_GROUP, tq), BF16),
            pltpu.VMEM((DA_MAPS * DA_VROWS, tq), F32),
            pltpu.VMEM((DA_MAPS, tq), F32),
            pltpu.VMEM((DA_MAPS, 2 * KSUB, tq), F32),
            pltpu.VMEM((DA_MAPS, 2 * KSUB, tq), F32),
        ],
        compiler_params=_cparams(("parallel", "arbitrary")),
        name="diff_attn",
    )(dq, dk, dvt, lam, out_g_col)


def _merge_kernel(x_ref, ya_ref, yb_ref, yc_ref, gp_ref, bg_ref, wb_ref, wo_ref, ng_ref, xo_ref, xn_ref):
    merged = None
    for br, y_ref in enumerate((ya_ref, yb_ref, yc_ref)):
        sl = slice(br * D_MODEL, (br + 1) * D_MODEL)
        gate = _sigmoid(gp_ref[:, sl].astype(F32) + bg_ref[:, sl])
        term = gate * _dot(y_ref[...], wb_ref[br])
        merged = term if merged is None else merged + term
    xo = x_ref[...] + _dot(merged.astype(BF16), wo_ref[...])
    xo_ref[...] = xo
    ms = jnp.mean(xo * xo, axis=-1, keepdims=True)
    xn_ref[...] = (xo * lax.rsqrt(ms + EPS) * ng_ref[...]).astype(BF16)


def _merge(x2, ya, yb, yc, z, b_gate, w_branch, w_out, norm_g, *, tm=512):
    n = x2.shape[0]
    tm = min(tm, n)
    row = lambda w: pl.BlockSpec((tm, w), lambda i: (i, 0))
    return pl.pallas_call(
        _merge_kernel,
        grid=(n // tm,),
        in_specs=[
            row(D_MODEL), row(512), row(512), row(512),
            pl.BlockSpec((tm, N_BRANCH * D_MODEL), lambda i: (i, COL_GATE)),
            pl.BlockSpec((1, N_BRANCH * D_MODEL), lambda i: (0, 0)),
            pl.BlockSpec((N_BRANCH, 512, D_MODEL), lambda i: (0, 0, 0)),
            pl.BlockSpec((D_MODEL, D_MODEL), lambda i: (0, 0)),
            pl.BlockSpec((1, D_MODEL), lambda i: (0, 0)),
        ],
        out_specs=[row(D_MODEL), row(D_MODEL)],
        out_shape=[jax.ShapeDtypeStruct((n, D_MODEL), F32), jax.ShapeDtypeStruct((n, D_MODEL), BF16)],
        compiler_params=_cparams(("parallel",)),
        name="merge_out",
    )(x2, ya, yb, yc, z, b_gate, w_branch, w_out, norm_g)


def _ffn_kernel(xn_ref, x_ref, wg_ref, wu_ref, wd_ref, o_ref, *, fc):
    xn = xn_ref[...]
    acc = x_ref[...]
    for c in range(FF_DIM // fc):
        cols = slice(c * fc, (c + 1) * fc)
        g = _dot(xn, wg_ref[:, cols])
        u = _dot(xn, wu_ref[:, cols])
        acc = acc + _dot((g * _sigmoid(g) * u).astype(BF16), wd_ref[cols, :])
    o_ref[...] = acc


def _ffn(xn, x2, wg, wu, wd, *, tm=512, fc=256):
    n = x2.shape[0]
    tm = min(tm, n)
    resident = lambda shape: pl.BlockSpec(shape, lambda i: (0, 0), pipeline_mode=pl.Buffered(1))
    return pl.pallas_call(
        functools.partial(_ffn_kernel, fc=fc),
        grid=(n // tm,),
        in_specs=[
            pl.BlockSpec((tm, D_MODEL), lambda i: (i, 0)),
            pl.BlockSpec((tm, D_MODEL), lambda i: (i, 0)),
            resident((D_MODEL, FF_DIM)),
            resident((D_MODEL, FF_DIM)),
            resident((FF_DIM, D_MODEL)),
        ],
        out_specs=pl.BlockSpec((tm, D_MODEL), lambda i: (i, 0)),
        out_shape=jax.ShapeDtypeStruct((n, D_MODEL), F32),
        compiler_params=_cparams(("parallel",)),
        name="ffn",
    )(xn, x2, wg, wu, wd)


def _rope_tables(seq):
    inv = 1.0 / jnp.power(ROPE_THETA, jnp.arange(0, ROPE_DIM, 2, dtype=F32) / ROPE_DIM)
    ang = jnp.arange(seq, dtype=F32)[:, None] * inv[None, :]
    cos, sin = jnp.cos(ang), jnp.sin(ang)
    c64 = jnp.concatenate([cos, cos], axis=1)
    s64 = jnp.concatenate([-sin, sin], axis=1)
    return jnp.tile(c64, (1, 4)), jnp.tile(s64, (1, 4))


def _layer(x2, li, batch, seq, consts, norm_mix_g, w_in_l, b_gate, conv_w, conv_b, gate_b, m_norm_g,
           kv_norm_g, w_kv_up, sq_g, sk_g, dq_g, dk_g, lam, d_out_g, w_branch, w_out,
           norm_ffn_g, w_gate_up, w_down):
    rope_c, rope_s, grp, tri, shifts = consts
    lam_init = 0.8 - 0.6 * math.exp(-0.3 * li)
    tile8 = lambda g: jnp.tile(g, 8)[None, :]

    z, misc = _inproj(x2, norm_mix_g[None, :], w_in_l)

    gate_row = jnp.zeros((1, LANE), F32).at[0, MISC_IF:MISC_IF + 2 * M_HEADS].set(gate_b)
    ya = _mlstm(z, misc, conv_w, conv_b[None, :], gate_row, m_norm_g.reshape(1, M_WIDTH), tri, shifts,
                batch, seq)

    sq, sk, svt, iq, ik4, dq, dk, dvt = _prep(
        z, rope_c, rope_s, grp, kv_norm_g[None, :], w_kv_up.astype(BF16),
        tile8(sq_g), tile8(sk_g), tile8(dq_g), tile8(dk_g), seq)
    yb = _dsa(misc, sq, sk, svt, iq, ik4, batch, seq)
    yc = _diff_attn(dq, dk, dvt, lam, d_out_g[:, None], batch, seq, lam_init)

    xo, xn = _merge(x2, ya, yb, yc, z, b_gate[None, :], w_branch.astype(BF16), w_out.astype(BF16),
                    norm_ffn_g[None, :])
    return _ffn(xn, xo, w_gate_up[:, :FF_DIM].astype(BF16), w_gate_up[:, FF_DIM:].astype(BF16),
                w_down.astype(BF16))


def kernel(x, norm_mix_g, w_in, b_gate, mlstm_conv_w, mlstm_conv_b, mlstm_gate_b, mlstm_norm_g,
           dsa_kv_norm_g, dsa_w_kv_up, dsa_q_norm_g, dsa_k_norm_g, diff_q_norm_g, diff_k_norm_g,
           diff_lambda, diff_out_norm_g, w_branch, w_out, norm_ffn_g, w_gate_up, w_down):
    batch, seq, d = x.shape
    depth = w_in.shape[0]
    assert w_in.shape[2] == IN_COLS and d == D_MODEL
    rope_c, rope_s = _rope_tables(seq)
    gi = jnp.arange(512) // ROPE_DIM
    grp = (gi[:, None] == gi[None, :]).astype(BF16)
    ti = jnp.arange(M_CHUNK)
    tri = (ti[:, None] >= ti[None, :]).astype(BF16)
    ri = jnp.arange(M_TAIL + M_CHUNK)
    shifts = jnp.stack([(ri[None, :] == ti[:, None] + M_TAIL - j) for j in range(1, M_CONV)]).astype(BF16)
    consts = (rope_c, rope_s, grp, tri, shifts)
    w_in_l = _wlayout(w_in)
    x2 = x.reshape(batch * seq, d)
    for li in range(depth):
        x2 = _layer(x2, li, batch, seq, consts, norm_mix_g[li], w_in_l[li], b_gate[li], mlstm_conv_w[li],
                    mlstm_conv_b[li], mlstm_gate_b[li], mlstm_norm_g[li], dsa_kv_norm_g[li],
                    dsa_w_kv_up[li], dsa_q_norm_g[li], dsa_k_norm_g[li], diff_q_norm_g[li],
                    diff_k_norm_g[li], diff_lambda[li], diff_out_norm_g[li], w_branch[li], w_out[li],
                    norm_ffn_g[li], w_gate_up[li], w_down[li])
    return x2.reshape(batch, seq, d)
```

```python
import functools
import math

import jax
import jax.numpy as jnp
from jax import lax
from jax.experimental import pallas as pl
from jax.experimental.pallas import tpu as pltpu

F32 = jnp.float32
BF16 = jnp.bfloat16
I32 = jnp.int32
I16 = jnp.int16

D_MODEL = 1024
EPS = 1e-6
ROPE_DIM = 64
ROPE_HALF = ROPE_DIM // 2
ROPE_THETA = 10000.0

M_HEADS = 4
M_HEAD_DIM = 128
M_WIDTH = M_HEADS * M_HEAD_DIM
M_CONV = 4
M_CHUNK = 128
M_TAIL = 16
M_INIT = -1e30

S_HEADS = 8
S_WIDTH = S_HEADS * ROPE_DIM
S_KV_RANK = 256
IDX_HEADS = 4
IDX_WIDTH = IDX_HEADS * ROPE_DIM
IDX_TOPK_MAX = 256

DA_HEADS = 4
DA_MAPS = 2 * DA_HEADS
DA_V_DIM = 2 * ROPE_DIM
DA_WIDTH = DA_HEADS * DA_V_DIM

N_BRANCH = 3
FF_DIM = 2816

NEG = -1e30
LOG2E = math.log2(math.e)
INT_MIN = -(2**31)
HALF16 = 2**15
LANE = 128
HEAD_GROUP = 256
KSUB = 128
ONES_ROWS = 16
S_VROWS = ROPE_DIM + ONES_ROWS
DA_VROWS = DA_V_DIM + ONES_ROWS

W_IN_SIZES = (2 * M_WIDTH, M_WIDTH, M_WIDTH, 2 * M_HEADS, S_WIDTH, S_KV_RANK, IDX_WIDTH, ROPE_DIM,
              IDX_HEADS, 2 * DA_HEADS * ROPE_DIM, 2 * DA_HEADS * ROPE_DIM, DA_WIDTH, N_BRANCH * D_MODEL)
IN_COLS = sum(W_IN_SIZES)

COL_GATE = 0
COL_MQK = 3072
COL_MV = 4096
COL_MO = 4608
COL_SQ = 5120
COL_DQ = 5632
COL_DK = 6144
COL_DV = 6656
COL_CKV = 7168
COL_IQ = 7424
COL_IK4 = 7680
COL_MISC = 7936
MISC_IF = 0
MISC_IW = 8
Z_COLS = 8192

VMEM_LIMIT = 56 * 1024 * 1024


def _cparams(sem, flags=None):
    return pltpu.CompilerParams(dimension_semantics=sem, vmem_limit_bytes=VMEM_LIMIT, flags=flags)


def _sigmoid(x):
    return 1.0 / (1.0 + jnp.exp(-x))


def _dot(a, b):
    return jnp.dot(a, b, preferred_element_type=F32)


def _dot_nt(a, b):
    return lax.dot_general(a, b, (((1,), (1,)), ((), ())), preferred_element_type=F32)


def _dot_tn(a, b):
    return lax.dot_general(a, b, (((0,), (0,)), ((), ())), preferred_element_type=F32)


def _split3(x):
    hi = x.astype(BF16)
    r1 = x - hi.astype(F32)
    mid = r1.astype(BF16)
    lo = (r1 - mid.astype(F32)).astype(BF16)
    return hi, mid, lo


def _wlayout_kernel(w_ref, o_ref):
    w = w_ref[...]
    offs = [0]
    for s in W_IN_SIZES:
        offs.append(offs[-1] + s)
    seg = [w[:, offs[k]:offs[k + 1]] for k in range(len(W_IN_SIZES))]
    m_qk, m_v, m_o, m_if, s_q, s_ckv, i_q, i_k, i_w, d_q, d_k, d_v, g_pre = seg
    rows = w.shape[0]
    misc = jnp.concatenate([m_if, i_w, jnp.zeros((rows, LANE - 2 * M_HEADS - IDX_HEADS), F32)], axis=1)
    ik4 = jnp.concatenate([i_k] * IDX_HEADS, axis=1)
    tail = jnp.zeros((rows, Z_COLS - COL_MISC - LANE), F32)
    for col, val in ((COL_GATE, g_pre), (COL_MQK, m_qk), (COL_MV, m_v), (COL_MO, m_o), (COL_SQ, s_q),
                     (COL_DQ, d_q), (COL_DK, d_k), (COL_DV, d_v), (COL_CKV, s_ckv), (COL_IQ, i_q),
                     (COL_IK4, ik4), (COL_MISC, misc), (COL_MISC + LANE, tail)):
        o_ref[:, col:col + val.shape[1]] = val.astype(BF16)


def _wlayout(w_in, *, tr=256):
    depth, rows, cols = w_in.shape
    return pl.pallas_call(
        _wlayout_kernel,
        grid=(depth, rows // tr),
        in_specs=[pl.BlockSpec((None, tr, cols), lambda l, i: (l, i, 0))],
        out_specs=pl.BlockSpec((None, tr, Z_COLS), lambda l, i: (l, i, 0)),
        out_shape=jax.ShapeDtypeStruct((depth, rows, Z_COLS), BF16),
        compiler_params=_cparams(("parallel", "parallel")),
        name="w_layout",
    )(w_in)


def _inproj_kernel(x_ref, g_ref, w_ref, z_ref, misc_ref, *, tn):
    x = x_ref[...]
    ms = jnp.mean(x * x, axis=-1, keepdims=True)
    xn = (x * lax.rsqrt(ms + EPS) * g_ref[...]).astype(BF16)
    for c in range(Z_COLS // tn):
        r = _dot(xn, w_ref[:, c * tn:(c + 1) * tn])
        z_ref[:, c * tn:(c + 1) * tn] = r.astype(BF16)
        if c == COL_MISC // tn:
            misc_ref[...] = r[:, COL_MISC % tn:COL_MISC % tn + LANE]


def _inproj(x2, g, w, *, tm=512, tn=1024):
    n = x2.shape[0]
    tm = min(tm, n)
    return pl.pallas_call(
        functools.partial(_inproj_kernel, tn=tn),
        grid=(n // tm,),
        in_specs=[
            pl.BlockSpec((tm, D_MODEL), lambda i: (i, 0)),
            pl.BlockSpec((1, D_MODEL), lambda i: (0, 0)),
            pl.BlockSpec((D_MODEL, Z_COLS), lambda i: (0, 0), pipeline_mode=pl.Buffered(1)),
        ],
        out_specs=[pl.BlockSpec((tm, Z_COLS), lambda i: (i, 0)),
                   pl.BlockSpec((tm, LANE), lambda i: (i, 0))],
        out_shape=[jax.ShapeDtypeStruct((n, Z_COLS), BF16), jax.ShapeDtypeStruct((n, LANE), F32)],
        compiler_params=_cparams(("parallel",)),
        name="inproj",
    )(x2, g, w)


def _mlstm_kernel(qk_ref, v_ref, o_ref, misc_ref, cw_ref, cb_ref, gb_ref, ng_ref, tri_ref, sh_ref, y_ref,
                  tail_ref, qkc_ref, c_ref, n_ref, m_ref):
    L = M_CHUNK
    dh = M_HEAD_DIM

    @pl.when(pl.program_id(1) == 0)
    def _():
        tail_ref[...] = jnp.zeros_like(tail_ref)
        c_ref[...] = jnp.zeros_like(c_ref)
        n_ref[...] = jnp.zeros_like(n_ref)
        m_ref[...] = jnp.full_like(m_ref, M_INIT)

    xb = qk_ref[...]
    ext = jnp.concatenate([tail_ref[...], xb], axis=0)
    cw = cw_ref[...]
    acc = xb.astype(F32) * cw[M_CONV - 1:M_CONV] + cb_ref[...]
    for j in range(1, M_CONV):
        acc = acc + _dot(sh_ref[j - 1], ext) * cw[M_CONV - 1 - j:M_CONV - j]
    qkc_ref[...] = acc * _sigmoid(acc)
    tail_ref[...] = xb[L - M_TAIL:L]

    gates = misc_ref[...] + gb_ref[...]
    lf = jnp.minimum(gates, 0.0) - jnp.log(1.0 + jnp.exp(-jnp.abs(gates)))
    tri = tri_ref[...]
    hi, mid, lo = _split3(lf)
    bcum = _dot(tri, hi) + _dot(tri, mid) + _dot(tri, lo)
    bcum_t = bcum.T
    gates_t = gates.T

    t_idx = lax.broadcasted_iota(I32, (L, L), 0)
    s_idx = lax.broadcasted_iota(I32, (L, L), 1)
    causal = t_idx >= s_idx

    ys, cs, ns, ms_new = [], [], [], []
    for h in range(M_HEADS):
        sl = slice(h * dh, (h + 1) * dh)
        q = qkc_ref[:, h * dh:(h + 1) * dh]
        k = qkc_ref[:, M_WIDTH + h * dh:M_WIDTH + (h + 1) * dh] * (dh ** -0.5)
        vb = v_ref[:, sl]
        v = vb.astype(F32)
        qb, kb = q.astype(BF16), k.astype(BF16)

        b_col = bcum[:, M_HEADS + h:M_HEADS + h + 1]
        i_col = gates[:, h:h + 1]
        b_row = bcum_t[M_HEADS + h:M_HEADS + h + 1, :]
        i_row = gates_t[h:h + 1, :]
        g_tot = bcum[L - 1:L, M_HEADS + h:M_HEADS + h + 1]

        c_prev = c_ref[sl, :]
        n_prev = n_ref[h:h + 1, :]
        m_prev = m_ref[h:h + 1, 0:1]

        rmat = jnp.where(causal, i_row - b_row, -jnp.inf)
        mx = jnp.maximum(m_prev, jnp.max(rmat, axis=-1, keepdims=True))
        m_t = b_col + mx
        sw = jnp.exp(rmat - mx) * _dot_nt(qb, kb)
        s_inter = jnp.exp(m_prev - mx)
        num = _dot(sw.astype(BF16), vb) + s_inter * _dot_nt(qb, c_prev.astype(BF16))
        den = jnp.sum(sw, axis=-1, keepdims=True) + s_inter * jnp.sum(q * n_prev, axis=-1, keepdims=True)
        dmax = jnp.maximum(jnp.abs(den), jnp.exp(-m_t))
        mnum = jnp.mean(num * num, axis=-1, keepdims=True)
        scale = lax.rsqrt(mnum + EPS * dmax * dmax)
        ys.append(_sigmoid(o_ref[:, sl].astype(F32)) * (num * scale * ng_ref[:, sl]))

        a_col = g_tot - b_col + i_col
        m_loc = jnp.max(a_col, axis=0, keepdims=True)
        w_loc = jnp.exp(a_col - m_loc)
        c_loc = _dot_tn((v * w_loc).astype(BF16), kb)
        n_loc = jnp.sum(k * w_loc, axis=0, keepdims=True)
        m_new = jnp.maximum(g_tot + m_prev, m_loc)
        s_old = jnp.exp(g_tot + m_prev - m_new)
        s_loc = jnp.exp(m_loc - m_new)
        cs.append(s_old * c_prev + s_loc * c_loc)
        ns.append(s_old * n_prev + s_loc * n_loc)
        ms_new.append(jnp.broadcast_to(m_new, (1, LANE)))

    y_ref[...] = jnp.concatenate(ys, axis=1).astype(BF16)
    c_ref[...] = jnp.concatenate(cs, axis=0)
    n_ref[...] = jnp.concatenate(ns, axis=0)
    m_ref[...] = jnp.concatenate(ms_new, axis=0)


def _mlstm(z, misc, conv_w, conv_b, gate_b_row, norm_g_row, tri, shifts, batch, seq):
    n = z.shape[0]
    nc = seq // M_CHUNK
    L = M_CHUNK
    row = lambda b, c: b * nc + c
    return pl.pallas_call(
        _mlstm_kernel,
        grid=(batch, nc),
        in_specs=[
            pl.BlockSpec((L, 2 * M_WIDTH), lambda b, c: (row(b, c), COL_MQK // (2 * M_WIDTH))),
            pl.BlockSpec((L, M_WIDTH), lambda b, c: (row(b, c), COL_MV // M_WIDTH)),
            pl.BlockSpec((L, M_WIDTH), lambda b, c: (row(b, c), COL_MO // M_WIDTH)),
            pl.BlockSpec((L, LANE), lambda b, c: (row(b, c), 0)),
            pl.BlockSpec((M_CONV, 2 * M_WIDTH), lambda b, c: (0, 0)),
            pl.BlockSpec((1, 2 * M_WIDTH), lambda b, c: (0, 0)),
            pl.BlockSpec((1, LANE), lambda b, c: (0, 0)),
            pl.BlockSpec((1, M_WIDTH), lambda b, c: (0, 0)),
            pl.BlockSpec((L, L), lambda b, c: (0, 0)),
            pl.BlockSpec((M_CONV - 1, L, M_TAIL + L), lambda b, c: (0, 0, 0)),
        ],
        out_specs=pl.BlockSpec((L, M_WIDTH), lambda b, c: (row(b, c), 0)),
        out_shape=jax.ShapeDtypeStruct((n, M_WIDTH), BF16),
        scratch_shapes=[
            pltpu.VMEM((M_TAIL, 2 * M_WIDTH), BF16),
            pltpu.VMEM((L, 2 * M_WIDTH), F32),
            pltpu.VMEM((M_HEADS * M_HEAD_DIM, M_HEAD_DIM), F32),
            pltpu.VMEM((M_HEADS, M_HEAD_DIM), F32),
            pltpu.VMEM((M_HEADS, LANE), F32),
        ],
        compiler_params=_cparams(("parallel", "arbitrary")),
        name="mlstm",
    )(z, z, z, misc, conv_w, conv_b, gate_b_row, norm_g_row, tri, shifts)


def _rope(x, c, s):
    w = x.shape[-1]
    lane = lax.broadcasted_iota(I32, (1, w), 1)
    fwd = pltpu.roll(x, ROPE_HALF, axis=1)
    bwd = pltpu.roll(x, w - ROPE_HALF, axis=1)
    swapped = jnp.where((lane & (ROPE_DIM - 1)) < ROPE_HALF, bwd, fwd)
    return x * c + swapped * s


def _head_rms(x, g, grp):
    sq = x * x
    hi = sq.astype(BF16)
    lo = (sq - hi.astype(F32)).astype(BF16)
    ss = _dot(hi, grp) + _dot(lo, grp)
    return x * lax.rsqrt(ss * (1.0 / ROPE_DIM) + EPS) * g


def _store_transposed(o_ref, v, dv):
    ones = jnp.ones((ONES_ROWS, KSUB), BF16)
    for j in range(o_ref.shape[0]):
        vt = v[j * KSUB:(j + 1) * KSUB, :].T.astype(BF16)
        parts = []
        for h in range(vt.shape[0] // dv):
            parts += [vt[h * dv:(h + 1) * dv, :], ones]
        o_ref[j] = jnp.concatenate(parts, axis=0)


def _prep_kernel(sq_ref, dq_ref, dk_ref, dv_ref, ckv_ref, iq_ref, ik_ref, c_ref, s_ref, grp_ref,
                 kvg_ref, wkv_ref, sqg_ref, skg_ref, dqg_ref, dkg_ref,
                 sq_o, sk_o, sv_o, iq_o, ik_o, dq_o, dk_o, dv_o):
    c256 = c_ref[...]
    s256 = s_ref[...]
    c512 = jnp.concatenate([c256, c256], axis=1)
    s512 = jnp.concatenate([s256, s256], axis=1)
    grp = grp_ref[...]
    qscale = ROPE_DIM ** -0.5 * LOG2E

    f32 = lambda ref: ref[...].astype(F32)

    sq_o[...] = (_rope(_head_rms(f32(sq_ref), sqg_ref[...], grp), c512, s512) * qscale).astype(BF16)

    ckv = f32(ckv_ref)
    ms = jnp.mean(ckv * ckv, axis=-1, keepdims=True)
    ckvn = (ckv * lax.rsqrt(ms + EPS) * kvg_ref[...]).astype(BF16)
    kv = _dot(ckvn, wkv_ref[...])
    sk_o[...] = _rope(_head_rms(kv[:, :S_WIDTH], skg_ref[...], grp), c512, s512).astype(BF16)
    _store_transposed(sv_o, kv[:, S_WIDTH:], ROPE_DIM)

    iq_o[...] = _rope(f32(iq_ref), c256, s256).astype(BF16)
    ik_o[...] = _rope(f32(ik_ref), c256, s256).astype(BF16)

    dq_o[...] = (_rope(_head_rms(f32(dq_ref), dqg_ref[...], grp), c512, s512) * qscale).astype(BF16)
    dk_o[...] = _rope(_head_rms(f32(dk_ref), dkg_ref[...], grp), c512, s512).astype(BF16)
    _store_transposed(dv_o, f32(dv_ref), DA_V_DIM)


def _prep(z, rope_c, rope_s, grp, kv_g, w_kv, sq_g, sk_g, dq_g, dk_g, seq, *, tm=512):
    n = z.shape[0]
    tm = min(tm, seq)
    nt = seq // tm
    zspec = lambda w, col: pl.BlockSpec((tm, w), lambda i: (i, col // w))
    const = lambda shape: pl.BlockSpec(shape, lambda i: (0, 0))
    o512 = pl.BlockSpec((tm, 512), lambda i: (i, 0))
    o256 = pl.BlockSpec((tm, 256), lambda i: (i, 0))
    s512 = jax.ShapeDtypeStruct((n, 512), BF16)
    s256 = jax.ShapeDtypeStruct((n, 256), BF16)
    tspec = lambda rows: pl.BlockSpec((tm // KSUB, rows, KSUB), lambda i: (i, 0, 0))
    tshape = lambda rows: jax.ShapeDtypeStruct((n // KSUB, rows, KSUB), BF16)
    sv_rows, dv_rows = S_HEADS * S_VROWS, DA_HEADS * DA_VROWS
    return pl.pallas_call(
        _prep_kernel,
        grid=(n // tm,),
        in_specs=[
            zspec(512, COL_SQ), zspec(512, COL_DQ), zspec(512, COL_DK), zspec(512, COL_DV),
            zspec(256, COL_CKV), zspec(256, COL_IQ), zspec(256, COL_IK4),
            pl.BlockSpec((tm, 256), lambda i: (i % nt, 0)),
            pl.BlockSpec((tm, 256), lambda i: (i % nt, 0)),
            const((512, 512)),
            const((1, S_KV_RANK)), const((S_KV_RANK, 2 * S_WIDTH)),
            const((1, 512)), const((1, 512)), const((1, 512)), const((1, 512)),
        ],
        out_specs=[o512, o512, tspec(sv_rows), o256, o256, o512, o512, tspec(dv_rows)],
        out_shape=[s512, s512, tshape(sv_rows), s256, s256, s512, s512, tshape(dv_rows)],
        compiler_params=_cparams(("parallel",)),
        name="attn_prep",
    )(z, z, z, z, z, z, z, rope_c, rope_s, grp, kv_g, w_kv, sq_g, sk_g, dq_g, dk_g)


def _mask_heads(q_ref, qm_ref, n_heads):
    row_grp = lax.broadcasted_iota(I32, (HEAD_GROUP, 1), 0) // ROPE_DIM
    for g in range(n_heads // 4):
        qt = q_ref[:, g * HEAD_GROUP:(g + 1) * HEAD_GROUP].astype(F32).T
        for hh in range(4):
            qm_ref[4 * g + hh] = jnp.where(row_grp == hh, qt, 0.0).astype(BF16)


def _softmax_group(logits, values, rows, acc_ref, m_ref):
    nh = len(logits)
    m_old = m_ref[...]
    m_new = jnp.maximum(m_old, jnp.concatenate([jnp.max(s, axis=0, keepdims=True) for s in logits], axis=0))
    alpha = jnp.exp2(m_old - m_new)
    ps = [jnp.exp2((s - m_new[j:j + 1, :]).astype(BF16)) for j, s in enumerate(logits)]
    m_ref[...] = m_new
    acc_old = acc_ref[...]
    acc_ref[...] = jnp.concatenate(
        [alpha[j:j + 1, :] * acc_old[j * rows:(j + 1) * rows, :] + _dot(values[j], ps[j])
         for j in range(nh)], axis=0)


def _attend(n_groups, logits_fn, values_fn, bufs0, bufs1, rows, acc_ref, m_ref):
    last = n_groups - 1
    group = len(bufs0)
    keys = bufs0[0].shape[1]

    def logits_stage(g, dst_refs):
        pen = jnp.where(g <= last, 0.0, NEG).astype(F32)
        for h, s in enumerate(logits_fn(jnp.minimum(g, last), pen)):
            for b in range(group):
                dst_refs[b][h] = s[b * keys:(b + 1) * keys, :]

    def softmax_stages(g, src_refs):
        for b in range(group):
            _softmax_group([src_refs[b][h] for h in range(src_refs[b].shape[0])],
                           values_fn(jnp.minimum(g, last) * group + b), rows, acc_ref, m_ref)

    logits_stage(0, bufs0)

    def body(gg, carry):
        logits_stage(2 * gg + 1, bufs1)
        softmax_stages(2 * gg, bufs0)
        logits_stage(2 * gg + 2, bufs0)
        softmax_stages(2 * gg + 1, bufs1)
        return carry

    lax.fori_loop(0, (n_groups + 1) // 2, body, 0)


def _sortable(x):
    bits = pltpu.bitcast(x, I32)
    return bits ^ ((bits >> 31) & 0x7FFFFFFF)


def _bit_planes(words):
    a = list(words)
    j, m = 16, 0x0000FFFF
    while j:
        k = 0
        while k < 32:
            t = (a[k] ^ lax.shift_right_logical(a[k + j], jnp.int32(j))) & m
            a[k] = a[k] ^ t
            a[k + j] = a[k + j] ^ (t << j)
            k = (k + j + 1) & ~j
        j >>= 1
        m = (m ^ (m << j)) & 0xFFFFFFFF
    return a


def _dsa_kernel(sq_ref, iq_ref, misc_ref, sk_ref, svt_ref, ik_ref, y_ref,
                sc_ref, bp_ref, qm_ref, iqm_ref, acc_ref, m_ref, sa_ref, sb_ref, res_ref,
                *, tq, topk):
    i = pl.program_id(1)

    @pl.when((pl.program_id(0) == 0) & (i == 0))
    def _():
        bp_ref[...] = jnp.zeros_like(bp_ref)

    nks = (i + 1) * (tq // KSUB)
    q_idx = i * tq + lax.broadcasted_iota(I32, (1, tq), 1)

    _mask_heads(iq_ref, iqm_ref, IDX_HEADS)
    w_t = (misc_ref[...] * (IDX_WIDTH ** -0.5)).T
    w_rows = [w_t[MISC_IW + h:MISC_IW + h + 1, :] for h in range(IDX_HEADS)]

    k_pair = lax.broadcasted_iota(I32, (2 * KSUB, 1), 0)

    def score_body(j, carry):
        ikb = ik_ref[pl.ds(pl.multiple_of(j * (2 * KSUB), 2 * KSUB), 2 * KSUB), :]
        s = jnp.zeros((2 * KSUB, tq), F32)
        for h in range(IDX_HEADS):
            s = s + w_rows[h] * jnp.maximum(_dot(ikb, iqm_ref[h]), 0.0)
        key = jnp.where(j * (2 * KSUB) + k_pair <= q_idx, _sortable(s), INT_MIN)
        sc_ref[pl.ds(2 * j, 2)] = key.reshape(2, KSUB, tq)
        ob = key ^ INT_MIN
        planes = _bit_planes([ob[v * 8:(v + 1) * 8, :] for v in range(32)])
        bp_ref[:, pl.ds(pl.multiple_of(j * 8, 8), 8), :] = jnp.concatenate(planes, axis=0).reshape(32, 8, tq)
        return carry

    lax.fori_loop(0, nks // 2, score_body, 0)

    nblk = bp_ref.shape[1] // 8

    def search(rows):
        live_rows = lax.broadcasted_iota(I32, (rows, 1), 0) < (nks // 2) * 8

        def bit_body(t, carry):
            alive, need, thr_ob = carry
            plane = bp_ref[t, 0:rows, :]
            ones = alive & plane
            cnt = jnp.sum(lax.population_count(ones), axis=0, keepdims=True)
            take = cnt >= need
            alive = jnp.where(take, ones, alive ^ ones)
            need = jnp.where(take, need, need - cnt)
            thr_ob = thr_ob | jnp.where(take, jnp.left_shift(jnp.int32(1), 31 - t), 0)
            return alive, need, thr_ob

        alive, need, thr_ob = lax.fori_loop(
            0, 32, bit_body,
            (jnp.broadcast_to(jnp.where(live_rows, -1, 0), (rows, tq)),
             jnp.full((1, tq), topk, I32), jnp.zeros((1, tq), I32)))

        word_row = lax.broadcasted_iota(I32, (rows, 1), 0)
        word_base = (word_row >> 3) * (2 * KSUB) + (word_row & 7)
        idx_bits = (rows * 32 - 1).bit_length()

        def ties_below(trial):
            nv = jnp.clip((trial - word_base + 7) >> 3, 0, 32)
            low = lax.shift_right_logical(jnp.full_like(nv, -1), jnp.minimum(nv, 31))
            mask = jnp.where(nv >= 32, -1, ~low)
            return jnp.sum(lax.population_count(alive & mask), axis=0, keepdims=True)

        def idx_body(t, cut):
            trial = cut | jnp.left_shift(jnp.int32(1), idx_bits - 1 - t)
            return jnp.where(ties_below(trial) < need, trial, cut)

        res_ref[0:1, :] = thr_ob ^ INT_MIN
        res_ref[1:2, :] = lax.fori_loop(0, idx_bits, idx_body, jnp.zeros((1, tq), I32))

    if nblk % 2 == 0:
        @pl.when(nks // 2 <= nblk // 2)
        def _():
            search(nblk * 4)

        @pl.when(nks // 2 > nblk // 2)
        def _():
            search(nblk * 8)
    else:
        search(nblk * 8)
    thr = res_ref[0:1, :]
    cut = res_ref[1:2, :]

    thr_sel = jnp.maximum(thr, INT_MIN + 1)

    def bias_body(j, carry):
        pair = pl.ds(2 * j, 2)
        key = sc_ref[pair].reshape(2 * KSUB, tq)
        kidx = j * (2 * KSUB) + k_pair
        sel = key >= thr_sel + jnp.where(kidx <= cut, 0, 1)
        sc_ref[pair] = pltpu.bitcast(jnp.where(sel, 0.0, NEG).astype(F32), I32).reshape(2, KSUB, tq)
        return carry

    lax.fori_loop(0, nks // 2, bias_body, 0)

    _mask_heads(sq_ref, qm_ref, S_HEADS)
    acc_ref[...] = jnp.zeros_like(acc_ref)
    m_ref[...] = jnp.full_like(m_ref, NEG)

    def logits_fn(ks, pen):
        off = pl.multiple_of(ks * KSUB, KSUB)
        bias = pltpu.bitcast(sc_ref[ks], F32) + pen
        kgs = [sk_ref[pl.ds(off, KSUB), g * HEAD_GROUP:(g + 1) * HEAD_GROUP] for g in range(S_HEADS // 4)]
        return [_dot(kgs[h // 4], qm_ref[h]) + bias for h in range(S_HEADS)]

    def values_fn(ks):
        return [svt_ref[ks, h * S_VROWS:(h + 1) * S_VROWS, :] for h in range(S_HEADS)]

    _attend(nks, logits_fn, values_fn, (sa_ref,), (sb_ref,), S_VROWS, acc_ref, m_ref)

    outs = []
    for h in range(S_HEADS):
        r0 = h * S_VROWS
        outs.append(acc_ref[r0:r0 + ROPE_DIM, :] / acc_ref[r0 + ROPE_DIM:r0 + ROPE_DIM + 1, :])
    y_ref[...] = jnp.concatenate(outs, axis=0).T.astype(BF16)


def _dsa(misc, sq, sk, svt, iq, ik4, batch, seq, *, tq=256):
    n = misc.shape[0]
    tq = min(tq, seq)
    nq = seq // tq
    nsub = seq // KSUB
    topk = min(IDX_TOPK_MAX, seq // 4)
    kern = functools.partial(_dsa_kernel, tq=tq, topk=topk)
    return pl.pallas_call(
        kern,
        grid=(batch, nq),
        in_specs=[
            pl.BlockSpec((tq, S_WIDTH), lambda b, i: (b * nq + i, 0)),
            pl.BlockSpec((tq, IDX_WIDTH), lambda b, i: (b * nq + i, 0)),
            pl.BlockSpec((tq, LANE), lambda b, i: (b * nq + i, 0)),
            pl.BlockSpec((seq, S_WIDTH), lambda b, i: (b, 0)),
            pl.BlockSpec((nsub, S_HEADS * S_VROWS, KSUB), lambda b, i: (b, 0, 0)),
            pl.BlockSpec((seq, IDX_WIDTH), lambda b, i: (b, 0)),
        ],
        out_specs=pl.BlockSpec((tq, S_WIDTH), lambda b, i: (b * nq + i, 0)),
        out_shape=jax.ShapeDtypeStruct((n, S_WIDTH), BF16),
        scratch_shapes=[
            pltpu.VMEM((nsub, KSUB, tq), I32),
            pltpu.VMEM((32, (nsub // 2) * 8, tq), I32),
            pltpu.VMEM((S_HEADS, HEAD_GROUP, tq), BF16),
            pltpu.VMEM((IDX_HEADS, HEAD_GROUP, tq), BF16),
            pltpu.VMEM((S_HEADS * S_VROWS, tq), F32),
            pltpu.VMEM((S_HEADS, tq), F32),
            pltpu.VMEM((S_HEADS, KSUB, tq), F32),
            pltpu.VMEM((S_HEADS, KSUB, tq), F32),
            pltpu.VMEM((8, tq), I32),
        ],
        compiler_params=_cparams(("arbitrary", "arbitrary")),
        name="dsa_attn",
    )(sq, iq, misc, sk, svt, ik4)


def _diff_kernel(q_ref, k_ref, vt_ref, lam_ref, og_ref, y_ref, qm_ref, acc_ref, m_ref,
                 sa_ref, sb_ref, *, tq, lam_init):
    i = pl.program_id(1)
    q_idx = i * tq + lax.broadcasted_iota(I32, (1, tq), 1)
    k_pair = lax.broadcasted_iota(I32, (2 * KSUB, 1), 0)

    _mask_heads(q_ref, qm_ref, DA_MAPS)
    acc_ref[...] = jnp.zeros_like(acc_ref)
    m_ref[...] = jnp.full_like(m_ref, NEG)

    def logits_fn(j, pen):
        off = pl.multiple_of(j * (2 * KSUB), 2 * KSUB)
        bias = jnp.where(j * (2 * KSUB) + k_pair <= q_idx, 0.0, NEG).astype(F32) + pen
        kgs = [k_ref[pl.ds(off, 2 * KSUB), g * HEAD_GROUP:(g + 1) * HEAD_GROUP] for g in range(DA_MAPS // 4)]
        return [_dot(kgs[m // 4], qm_ref[m]) + bias for m in range(DA_MAPS)]

    def values_fn(j):
        heads = [jnp.concatenate([vt_ref[2 * j, hd * DA_VROWS:(hd + 1) * DA_VROWS, :],
                                  vt_ref[2 * j + 1, hd * DA_VROWS:(hd + 1) * DA_VROWS, :]], axis=1)
                 for hd in range(DA_HEADS)]
        return [heads[m // 2] for m in range(DA_MAPS)]

    assert tq == 2 * KSUB
    _attend(i + 1, logits_fn, values_fn, (sa_ref,), (sb_ref,), DA_VROWS, acc_ref, m_ref)

    lam = lam_ref[...]
    p01 = jnp.sum(lam[0:1] * lam[1:2], axis=-1, keepdims=True)
    p23 = jnp.sum(lam[2:3] * lam[3:4], axis=-1, keepdims=True)
    lam_val = jnp.exp(p01) - jnp.exp(p23) + lam_init
    def normalised(m):
        r0 = m * DA_VROWS
        return acc_ref[r0:r0 + DA_V_DIM, :] / acc_ref[r0 + DA_V_DIM:r0 + DA_V_DIM + 1, :]

    outs = []
    for hd in range(DA_HEADS):
        o = normalised(2 * hd) - lam_val * normalised(2 * hd + 1)
        ms = jnp.mean(o * o, axis=0, keepdims=True)
        outs.append(o * lax.rsqrt(ms + EPS) * og_ref[...] * (1.0 - lam_init))
    y_ref[...] = jnp.concatenate(outs, axis=0).T.astype(BF16)


def _diff_attn(dq, dk, dvt, lam, out_g_col, batch, seq, lam_init, *, tq=256):
    n = dq.shape[0]
    tq = min(tq, seq)
    nq = seq // tq
    nsub = seq // KSUB
    kern = functools.partial(_diff_kernel, tq=tq, lam_init=lam_init)
    return pl.pallas_call(
        kern,
        grid=(batch, nq),
        in_specs=[
            pl.BlockSpec((tq, 512), lambda b, i: (b * nq + i, 0)),
            pl.BlockSpec((seq, 512), lambda b, i: (b, 0)),
            pl.BlockSpec((nsub, DA_HEADS * DA_VROWS, KSUB), lambda b, i: (b, 0, 0)),
            pl.BlockSpec((4, ROPE_DIM), lambda b, i: (0, 0)),
            pl.BlockSpec((DA_V_DIM, 1), lambda b, i: (0, 0)),
        ],
        out_specs=pl.BlockSpec((tq, DA_WIDTH), lambda b, i: (b * nq + i, 0)),
        out_shape=jax.ShapeDtypeStruct((n, DA_WIDTH), BF16),
        scratch_shapes=[
            pltpu.VMEM((DA_MAPS, HEAD_GROUP, tq), BF16),
            pltpu.VMEM((DA_MAPS * DA_VROWS, tq), F32),
            pltpu.VMEM((DA_MAPS, tq), F32),
            pltpu.VMEM((DA_MAPS, 2 * KSUB, tq), F32),
            pltpu.VMEM((DA_MAPS, 2 * KSUB, tq), F32),
        ],
        compiler_params=_cparams(("parallel", "arbitrary")),
        name="diff_attn",
    )(dq, dk, dvt, lam, out_g_col)


def _merge_kernel(x_ref, ya_ref, yb_ref, yc_ref, gp_ref, bg_ref, wb_ref, wo_ref, ng_ref, xo_ref, xn_ref):
    merged = None
    for br, y_ref in enumerate((ya_ref, yb_ref, yc_ref)):
        sl = slice(br * D_MODEL, (br + 1) * D_MODEL)
        gate = _sigmoid(gp_ref[:, sl].astype(F32) + bg_ref[:, sl])
        term = gate * _dot(y_ref[...], wb_ref[br])
        merged = term if merged is None else merged + term
    xo = x_ref[...] + _dot(merged.astype(BF16), wo_ref[...])
    xo_ref[...] = xo
    ms = jnp.mean(xo * xo, axis=-1, keepdims=True)
    xn_ref[...] = (xo * lax.rsqrt(ms + EPS) * ng_ref[...]).astype(BF16)


def _merge(x2, ya, yb, yc, z, b_gate, w_branch, w_out, norm_g, *, tm=512):
    n = x2.shape[0]
    tm = min(tm, n)
    row = lambda w: pl.BlockSpec((tm, w), lambda i: (i, 0))
    return pl.pallas_call(
        _merge_kernel,
        grid=(n // tm,),
        in_specs=[
            row(D_MODEL), row(512), row(512), row(512),
            pl.BlockSpec((tm, N_BRANCH * D_MODEL), lambda i: (i, COL_GATE)),
            pl.BlockSpec((1, N_BRANCH * D_MODEL), lambda i: (0, 0)),
            pl.BlockSpec((N_BRANCH, 512, D_MODEL), lambda i: (0, 0, 0)),
            pl.BlockSpec((D_MODEL, D_MODEL), lambda i: (0, 0)),
            pl.BlockSpec((1, D_MODEL), lambda i: (0, 0)),
        ],
        out_specs=[row(D_MODEL), row(D_MODEL)],
        out_shape=[jax.ShapeDtypeStruct((n, D_MODEL), F32), jax.ShapeDtypeStruct((n, D_MODEL), BF16)],
        compiler_params=_cparams(("parallel",)),
        name="merge_out",
    )(x2, ya, yb, yc, z, b_gate, w_branch, w_out, norm_g)


def _ffn_kernel(xn_ref, x_ref, wg_ref, wu_ref, wd_ref, o_ref, *, fc):
    xn = xn_ref[...]
    acc = x_ref[...]
    for c in range(FF_DIM // fc):
        cols = slice(c * fc, (c + 1) * fc)
        g = _dot(xn, wg_ref[:, cols])
        u = _dot(xn, wu_ref[:, cols])
        acc = acc + _dot((g * _sigmoid(g) * u).astype(BF16), wd_ref[cols, :])
    o_ref[...] = acc


def _ffn(xn, x2, wg, wu, wd, *, tm=512, fc=256):
    n = x2.shape[0]
    tm = min(tm, n)
    resident = lambda shape: pl.BlockSpec(shape, lambda i: (0, 0), pipeline_mode=pl.Buffered(1))
    return pl.pallas_call(
        functools.partial(_ffn_kernel, fc=fc),
        grid=(n // tm,),
        in_specs=[
            pl.BlockSpec((tm, D_MODEL), lambda i: (i, 0)),
            pl.BlockSpec((tm, D_MODEL), lambda i: (i, 0)),
            resident((D_MODEL, FF_DIM)),
            resident((D_MODEL, FF_DIM)),
            resident((FF_DIM, D_MODEL)),
        ],
        out_specs=pl.BlockSpec((tm, D_MODEL), lambda i: (i, 0)),
        out_shape=jax.ShapeDtypeStruct((n, D_MODEL), F32),
        compiler_params=_cparams(("parallel",)),
        name="ffn",
    )(xn, x2, wg, wu, wd)


def _rope_tables(seq):
    inv = 1.0 / jnp.power(ROPE_THETA, jnp.arange(0, ROPE_DIM, 2, dtype=F32) / ROPE_DIM)
    ang = jnp.arange(seq, dtype=F32)[:, None] * inv[None, :]
    cos, sin = jnp.cos(ang), jnp.sin(ang)
    c64 = jnp.concatenate([cos, cos], axis=1)
    s64 = jnp.concatenate([-sin, sin], axis=1)
    return jnp.tile(c64, (1, 4)), jnp.tile(s64, (1, 4))


def _layer(x2, li, batch, seq, consts, norm_mix_g, w_in_l, b_gate, conv_w, conv_b, gate_b, m_norm_g,
           kv_norm_g, w_kv_up, sq_g, sk_g, dq_g, dk_g, lam, d_out_g, w_branch, w_out,
           norm_ffn_g, w_gate_up, w_down):
    rope_c, rope_s, grp, tri, shifts = consts
    lam_init = 0.8 - 0.6 * math.exp(-0.3 * li)
    tile8 = lambda g: jnp.tile(g, 8)[None, :]

    z, misc = _inproj(x2, norm_mix_g[None, :], w_in_l)

    gate_row = jnp.zeros((1, LANE), F32).at[0, MISC_IF:MISC_IF + 2 * M_HEADS].set(gate_b)
    ya = _mlstm(z, misc, conv_w, conv_b[None, :], gate_row, m_norm_g.reshape(1, M_WIDTH), tri, shifts,
                batch, seq)

    sq, sk, svt, iq, ik4, dq, dk, dvt = _prep(
        z, rope_c, rope_s, grp, kv_norm_g[None, :], w_kv_up.astype(BF16),
        tile8(sq_g), tile8(sk_g), tile8(dq_g), tile8(dk_g), seq)
    yb = _dsa(misc, sq, sk, svt, iq, ik4, batch, seq)
    yc = _diff_attn(dq, dk, dvt, lam, d_out_g[:, None], batch, seq, lam_init)

    xo, xn = _merge(x2, ya, yb, yc, z, b_gate[None, :], w_branch.astype(BF16), w_out.astype(BF16),
                    norm_ffn_g[None, :])
    return _ffn(xn, xo, w_gate_up[:, :FF_DIM].astype(BF16), w_gate_up[:, FF_DIM:].astype(BF16),
                w_down.astype(BF16))


def kernel(x, norm_mix_g, w_in, b_gate, mlstm_conv_w, mlstm_conv_b, mlstm_gate_b, mlstm_norm_g,
           dsa_kv_norm_g, dsa_w_kv_up, dsa_q_norm_g, dsa_k_norm_g, diff_q_norm_g, diff_k_norm_g,
           diff_lambda, diff_out_norm_g, w_branch, w_out, norm_ffn_g, w_gate_up, w_down):
    batch, seq, d = x.shape
    depth = w_in.shape[0]
    assert w_in.shape[2] == IN_COLS and d == D_MODEL
    rope_c, rope_s = _rope_tables(seq)
    gi = jnp.arange(512) // ROPE_DIM
    grp = (gi[:, None] == gi[None, :]).astype(BF16)
    ti = jnp.arange(M_CHUNK)
    tri = (ti[:, None] >= ti[None, :]).astype(BF16)
    ri = jnp.arange(M_TAIL + M_CHUNK)
    shifts = jnp.stack([(ri[None, :] == ti[:, None] + M_TAIL - j) for j in range(1, M_CONV)]).astype(BF16)
    consts = (rope_c, rope_s, grp, tri, shifts)
    w_in_l = _wlayout(w_in)
    x2 = x.reshape(batch * seq, d)
    for li in range(depth):
        x2 = _layer(x2, li, batch, seq, consts, norm_mix_g[li], w_in_l[li], b_gate[li], mlstm_conv_w[li],
                    mlstm_conv_b[li], mlstm_gate_b[li], mlstm_norm_g[li], dsa_kv_norm_g[li],
                    dsa_w_kv_up[li], dsa_q_norm_g[li], dsa_k_norm_g[li], diff_q_norm_g[li],
                    diff_k_norm_g[li], diff_lambda[li], diff_out_norm_g[li], w_branch[li], w_out[li],
                    norm_ffn_g[li], w_gate_up[li], w_down[li])
    return x2.reshape(batch, seq, d)
```

```python
import functools
import math

import jax
import jax.numpy as jnp
from jax import lax
from jax.experimental import pallas as pl
from jax.experimental.pallas import tpu as pltpu

F32 = jnp.float32
BF16 = jnp.bfloat16
I32 = jnp.int32
I16 = jnp.int16

D_MODEL = 1024
EPS = 1e-6
ROPE_DIM = 64
ROPE_HALF = ROPE_DIM // 2
ROPE_THETA = 10000.0

M_HEADS = 4
M_HEAD_DIM = 128
M_WIDTH = M_HEADS * M_HEAD_DIM
M_CONV = 4
M_CHUNK = 128
M_TAIL = 16
M_STEP_CHUNKS = 4
M_INIT = -1e30

S_HEADS = 8
S_WIDTH = S_HEADS * ROPE_DIM
S_KV_RANK = 256
IDX_HEADS = 4
IDX_WIDTH = IDX_HEADS * ROPE_DIM
IDX_TOPK_MAX = 256

DA_HEADS = 4
DA_MAPS = 2 * DA_HEADS
DA_V_DIM = 2 * ROPE_DIM
DA_WIDTH = DA_HEADS * DA_V_DIM

N_BRANCH = 3
FF_DIM = 2816

NEG = -1e30
LOG2E = math.log2(math.e)
INT_MIN = -(2**31)
HALF16 = 2**15
LANE = 128
HEAD_GROUP = 256
KSUB = 128
ONES_ROWS = 16
S_VROWS = ROPE_DIM + ONES_ROWS
DA_VROWS = DA_V_DIM + ONES_ROWS

W_IN_SIZES = (2 * M_WIDTH, M_WIDTH, M_WIDTH, 2 * M_HEADS, S_WIDTH, S_KV_RANK, IDX_WIDTH, ROPE_DIM,
              IDX_HEADS, 2 * DA_HEADS * ROPE_DIM, 2 * DA_HEADS * ROPE_DIM, DA_WIDTH, N_BRANCH * D_MODEL)
IN_COLS = sum(W_IN_SIZES)

COL_GATE = 0
COL_MQK = 3072
COL_MV = 4096
COL_MO = 4608
COL_SQ = 5120
COL_DQ = 5632
COL_DK = 6144
COL_DV = 6656
COL_CKV = 7168
COL_IQ = 7424
COL_IK4 = 7680
COL_MISC = 7936
MISC_IF = 0
MISC_IW = 8
Z_COLS = 8192

VMEM_LIMIT = 56 * 1024 * 1024


def _cparams(sem, flags=None):
    return pltpu.CompilerParams(dimension_semantics=sem, vmem_limit_bytes=VMEM_LIMIT, flags=flags)


def _sigmoid(x):
    return 1.0 / (1.0 + jnp.exp(-x))


def _dot(a, b):
    return jnp.dot(a, b, preferred_element_type=F32)


def _dot_nt(a, b):
    return lax.dot_general(a, b, (((1,), (1,)), ((), ())), preferred_element_type=F32)


def _dot_tn(a, b):
    return lax.dot_general(a, b, (((0,), (0,)), ((), ())), preferred_element_type=F32)


def _split3(x):
    hi = x.astype(BF16)
    r1 = x - hi.astype(F32)
    mid = r1.astype(BF16)
    lo = (r1 - mid.astype(F32)).astype(BF16)
    return hi, mid, lo


def _wlayout_kernel(w_ref, o_ref):
    w = w_ref[...]
    offs = [0]
    for s in W_IN_SIZES:
        offs.append(offs[-1] + s)
    seg = [w[:, offs[k]:offs[k + 1]] for k in range(len(W_IN_SIZES))]
    m_qk, m_v, m_o, m_if, s_q, s_ckv, i_q, i_k, i_w, d_q, d_k, d_v, g_pre = seg
    rows = w.shape[0]
    misc = jnp.concatenate([m_if, i_w, jnp.zeros((rows, LANE - 2 * M_HEADS - IDX_HEADS), F32)], axis=1)
    ik4 = jnp.concatenate([i_k] * IDX_HEADS, axis=1)
    tail = jnp.zeros((rows, Z_COLS - COL_MISC - LANE), F32)
    for col, val in ((COL_GATE, g_pre), (COL_MQK, m_qk), (COL_MV, m_v), (COL_MO, m_o), (COL_SQ, s_q),
                     (COL_DQ, d_q), (COL_DK, d_k), (COL_DV, d_v), (COL_CKV, s_ckv), (COL_IQ, i_q),
                     (COL_IK4, ik4), (COL_MISC, misc), (COL_MISC + LANE, tail)):
        o_ref[:, col:col + val.shape[1]] = val.astype(BF16)


def _wlayout(w_in, *, tr=256):
    depth, rows, cols = w_in.shape
    return pl.pallas_call(
        _wlayout_kernel,
        grid=(depth, rows // tr),
        in_specs=[pl.BlockSpec((None, tr, cols), lambda l, i: (l, i, 0))],
        out_specs=pl.BlockSpec((None, tr, Z_COLS), lambda l, i: (l, i, 0)),
        out_shape=jax.ShapeDtypeStruct((depth, rows, Z_COLS), BF16),
        compiler_params=_cparams(("parallel", "parallel")),
        name="w_layout",
    )(w_in)


def _inproj_kernel(x_ref, g_ref, w_ref, z_ref, misc_ref, *, tn):
    x = x_ref[...]
    ms = jnp.mean(x * x, axis=-1, keepdims=True)
    xn = (x * lax.rsqrt(ms + EPS) * g_ref[...]).astype(BF16)
    for c in range(Z_COLS // tn):
        r = _dot(xn, w_ref[:, c * tn:(c + 1) * tn])
        z_ref[:, c * tn:(c + 1) * tn] = r.astype(BF16)
        if c == COL_MISC // tn:
            misc_ref[...] = r[:, COL_MISC % tn:COL_MISC % tn + LANE]


def _inproj(x2, g, w, *, tm=512, tn=1024):
    n = x2.shape[0]
    tm = min(tm, n)
    return pl.pallas_call(
        functools.partial(_inproj_kernel, tn=tn),
        grid=(n // tm,),
        in_specs=[
            pl.BlockSpec((tm, D_MODEL), lambda i: (i, 0)),
            pl.BlockSpec((1, D_MODEL), lambda i: (0, 0)),
            pl.BlockSpec((D_MODEL, Z_COLS), lambda i: (0, 0), pipeline_mode=pl.Buffered(1)),
        ],
        out_specs=[pl.BlockSpec((tm, Z_COLS), lambda i: (i, 0)),
                   pl.BlockSpec((tm, LANE), lambda i: (i, 0))],
        out_shape=[jax.ShapeDtypeStruct((n, Z_COLS), BF16), jax.ShapeDtypeStruct((n, LANE), F32)],
        compiler_params=_cparams(("parallel",)),
        name="inproj",
    )(x2, g, w)


def _mlstm_kernel(qk_ref, v_ref, o_ref, misc_ref, cw_ref, cb_ref, gb_ref, ng_ref, tri_ref, sh_ref, y_ref,
                  tail_ref, qkc_ref, c_ref, n_ref, m_ref):
    @pl.when(pl.program_id(1) == 0)
    def _():
        tail_ref[...] = jnp.zeros_like(tail_ref)
        c_ref[...] = jnp.zeros_like(c_ref)
        n_ref[...] = jnp.zeros_like(n_ref)
        m_ref[...] = jnp.full_like(m_ref, M_INIT)

    for cc in range(qk_ref.shape[0] // M_CHUNK):
        rows = pl.ds(cc * M_CHUNK, M_CHUNK)
        _mlstm_chunk(qk_ref.at[rows], v_ref.at[rows], o_ref.at[rows], misc_ref.at[rows], cw_ref, cb_ref,
                     gb_ref, ng_ref, tri_ref, sh_ref, y_ref.at[rows], tail_ref, qkc_ref, c_ref, n_ref, m_ref)


def _mlstm_chunk(qk_ref, v_ref, o_ref, misc_ref, cw_ref, cb_ref, gb_ref, ng_ref, tri_ref, sh_ref, y_ref,
                 tail_ref, qkc_ref, c_ref, n_ref, m_ref):
    L = M_CHUNK
    dh = M_HEAD_DIM

    xb = qk_ref[...]
    ext = jnp.concatenate([tail_ref[...], xb], axis=0)
    cw = cw_ref[...]
    acc = xb.astype(F32) * cw[M_CONV - 1:M_CONV] + cb_ref[...]
    for j in range(1, M_CONV):
        acc = acc + _dot(sh_ref[j - 1], ext) * cw[M_CONV - 1 - j:M_CONV - j]
    qkc_ref[...] = acc * _sigmoid(acc)
    tail_ref[...] = xb[L - M_TAIL:L]

    gates = misc_ref[...] + gb_ref[...]
    lf = jnp.minimum(gates, 0.0) - jnp.log(1.0 + jnp.exp(-jnp.abs(gates)))
    tri = tri_ref[...]
    hi, mid, lo = _split3(lf)
    bcum = _dot(tri, hi) + _dot(tri, mid) + _dot(tri, lo)
    bcum_t = bcum.T
    gates_t = gates.T

    t_idx = lax.broadcasted_iota(I32, (L, L), 0)
    s_idx = lax.broadcasted_iota(I32, (L, L), 1)
    causal = t_idx >= s_idx

    ys, cs, ns, ms_new = [], [], [], []
    for h in range(M_HEADS):
        sl = slice(h * dh, (h + 1) * dh)
        q = qkc_ref[:, h * dh:(h + 1) * dh]
        k = qkc_ref[:, M_WIDTH + h * dh:M_WIDTH + (h + 1) * dh] * (dh ** -0.5)
        vb = v_ref[:, sl]
        v = vb.astype(F32)
        qb, kb = q.astype(BF16), k.astype(BF16)

        b_col = bcum[:, M_HEADS + h:M_HEADS + h + 1]
        i_col = gates[:, h:h + 1]
        b_row = bcum_t[M_HEADS + h:M_HEADS + h + 1, :]
        i_row = gates_t[h:h + 1, :]
        g_tot = bcum[L - 1:L, M_HEADS + h:M_HEADS + h + 1]

        c_prev = c_ref[sl, :]
        n_prev = n_ref[h:h + 1, :]
        m_prev = m_ref[h:h + 1, 0:1]

        rmat = jnp.where(causal, i_row - b_row, -jnp.inf)
        mx = jnp.maximum(m_prev, jnp.max(rmat, axis=-1, keepdims=True))
        m_t = b_col + mx
        sw = jnp.exp(rmat - mx) * _dot_nt(qb, kb)
        s_inter = jnp.exp(m_prev - mx)
        num = _dot(sw.astype(BF16), vb) + s_inter * _dot_nt(qb, c_prev.astype(BF16))
        den = jnp.sum(sw, axis=-1, keepdims=True) + s_inter * jnp.sum(q * n_prev, axis=-1, keepdims=True)
        dmax = jnp.maximum(jnp.abs(den), jnp.exp(-m_t))
        mnum = jnp.mean(num * num, axis=-1, keepdims=True)
        scale = lax.rsqrt(mnum + EPS * dmax * dmax)
        ys.append(_sigmoid(o_ref[:, sl].astype(F32)) * (num * scale * ng_ref[:, sl]))

        a_col = g_tot - b_col + i_col
        m_loc = jnp.max(a_col, axis=0, keepdims=True)
        w_loc = jnp.exp(a_col - m_loc)
        c_loc = _dot_tn((v * w_loc).astype(BF16), kb)
        n_loc = jnp.sum(k * w_loc, axis=0, keepdims=True)
        m_new = jnp.maximum(g_tot + m_prev, m_loc)
        s_old = jnp.exp(g_tot + m_prev - m_new)
        s_loc = jnp.exp(m_loc - m_new)
        cs.append(s_old * c_prev + s_loc * c_loc)
        ns.append(s_old * n_prev + s_loc * n_loc)
        ms_new.append(jnp.broadcast_to(m_new, (1, LANE)))

    y_ref[...] = jnp.concatenate(ys, axis=1).astype(BF16)
    c_ref[...] = jnp.concatenate(cs, axis=0)
    n_ref[...] = jnp.concatenate(ns, axis=0)
    m_ref[...] = jnp.concatenate(ms_new, axis=0)


def _mlstm(z, misc, conv_w, conv_b, gate_b_row, norm_g_row, tri, shifts, batch, seq):
    n = z.shape[0]
    L = M_CHUNK
    rows = M_STEP_CHUNKS * L
    nc = seq // rows
    row = lambda b, c: b * nc + c
    return pl.pallas_call(
        _mlstm_kernel,
        grid=(batch, nc),
        in_specs=[
            pl.BlockSpec((rows, 2 * M_WIDTH), lambda b, c: (row(b, c), COL_MQK // (2 * M_WIDTH))),
            pl.BlockSpec((rows, M_WIDTH), lambda b, c: (row(b, c), COL_MV // M_WIDTH)),
            pl.BlockSpec((rows, M_WIDTH), lambda b, c: (row(b, c), COL_MO // M_WIDTH)),
            pl.BlockSpec((rows, LANE), lambda b, c: (row(b, c), 0)),
            pl.BlockSpec((M_CONV, 2 * M_WIDTH), lambda b, c: (0, 0)),
            pl.BlockSpec((1, 2 * M_WIDTH), lambda b, c: (0, 0)),
            pl.BlockSpec((1, LANE), lambda b, c: (0, 0)),
            pl.BlockSpec((1, M_WIDTH), lambda b, c: (0, 0)),
            pl.BlockSpec((L, L), lambda b, c: (0, 0)),
            pl.BlockSpec((M_CONV - 1, L, M_TAIL + L), lambda b, c: (0, 0, 0)),
        ],
        out_specs=pl.BlockSpec((rows, M_WIDTH), lambda b, c: (row(b, c), 0)),
        out_shape=jax.ShapeDtypeStruct((n, M_WIDTH), BF16),
        scratch_shapes=[
            pltpu.VMEM((M_TAIL, 2 * M_WIDTH), BF16),
            pltpu.VMEM((L, 2 * M_WIDTH), F32),
            pltpu.VMEM((M_HEADS * M_HEAD_DIM, M_HEAD_DIM), F32),
            pltpu.VMEM((M_HEADS, M_HEAD_DIM), F32),
            pltpu.VMEM((M_HEADS, LANE), F32),
        ],
        compiler_params=_cparams(("parallel", "arbitrary")),
        name="mlstm",
    )(z, z, z, misc, conv_w, conv_b, gate_b_row, norm_g_row, tri, shifts)


def _rope(x, c, s):
    w = x.shape[-1]
    lane = lax.broadcasted_iota(I32, (1, w), 1)
    fwd = pltpu.roll(x, ROPE_HALF, axis=1)
    bwd = pltpu.roll(x, w - ROPE_HALF, axis=1)
    swapped = jnp.where((lane & (ROPE_DIM - 1)) < ROPE_HALF, bwd, fwd)
    return x * c + swapped * s


def _head_rms(x, g, grp):
    sq = x * x
    hi = sq.astype(BF16)
    lo = (sq - hi.astype(F32)).astype(BF16)
    ss = _dot(hi, grp) + _dot(lo, grp)
    return x * lax.rsqrt(ss * (1.0 / ROPE_DIM) + EPS) * g


def _store_transposed(o_ref, v, dv):
    ones = jnp.ones((ONES_ROWS, KSUB), BF16)
    for j in range(o_ref.shape[0]):
        vt = v[j * KSUB:(j + 1) * KSUB, :].T.astype(BF16)
        parts = []
        for h in range(vt.shape[0] // dv):
            parts += [vt[h * dv:(h + 1) * dv, :], ones]
        o_ref[j] = jnp.concatenate(parts, axis=0)


def _prep_kernel(sq_ref, dq_ref, dk_ref, dv_ref, ckv_ref, iq_ref, ik_ref, c_ref, s_ref, grp_ref,
                 kvg_ref, wkv_ref, sqg_ref, skg_ref, dqg_ref, dkg_ref,
                 sq_o, sk_o, sv_o, iq_o, ik_o, dq_o, dk_o, dv_o):
    c256 = c_ref[...]
    s256 = s_ref[...]
    c512 = jnp.concatenate([c256, c256], axis=1)
    s512 = jnp.concatenate([s256, s256], axis=1)
    grp = grp_ref[...]
    qscale = ROPE_DIM ** -0.5 * LOG2E

    f32 = lambda ref: ref[...].astype(F32)

    sq_o[...] = (_rope(_head_rms(f32(sq_ref), sqg_ref[...], grp), c512, s512) * qscale).astype(BF16)

    ckv = f32(ckv_ref)
    ms = jnp.mean(ckv * ckv, axis=-1, keepdims=True)
    ckvn = (ckv * lax.rsqrt(ms + EPS) * kvg_ref[...]).astype(BF16)
    kv = _dot(ckvn, wkv_ref[...])
    sk_o[...] = _rope(_head_rms(kv[:, :S_WIDTH], skg_ref[...], grp), c512, s512).astype(BF16)
    _store_transposed(sv_o, kv[:, S_WIDTH:], ROPE_DIM)

    iq_o[...] = _rope(f32(iq_ref), c256, s256).astype(BF16)
    ik_o[...] = _rope(f32(ik_ref), c256, s256).astype(BF16)

    dq_o[...] = (_rope(_head_rms(f32(dq_ref), dqg_ref[...], grp), c512, s512) * qscale).astype(BF16)
    dk_o[...] = _rope(_head_rms(f32(dk_ref), dkg_ref[...], grp), c512, s512).astype(BF16)
    _store_transposed(dv_o, f32(dv_ref), DA_V_DIM)


def _prep(z, rope_c, rope_s, grp, kv_g, w_kv, sq_g, sk_g, dq_g, dk_g, seq, *, tm=512):
    n = z.shape[0]
    tm = min(tm, seq)
    nt = seq // tm
    zspec = lambda w, col: pl.BlockSpec((tm, w), lambda i: (i, col // w))
    const = lambda shape: pl.BlockSpec(shape, lambda i: (0, 0))
    o512 = pl.BlockSpec((tm, 512), lambda i: (i, 0))
    o256 = pl.BlockSpec((tm, 256), lambda i: (i, 0))
    s512 = jax.ShapeDtypeStruct((n, 512), BF16)
    s256 = jax.ShapeDtypeStruct((n, 256), BF16)
    tspec = lambda rows: pl.BlockSpec((tm // KSUB, rows, KSUB), lambda i: (i, 0, 0))
    tshape = lambda rows: jax.ShapeDtypeStruct((n // KSUB, rows, KSUB), BF16)
    sv_rows, dv_rows = S_HEADS * S_VROWS, DA_HEADS * DA_VROWS
    return pl.pallas_call(
        _prep_kernel,
        grid=(n // tm,),
        in_specs=[
            zspec(512, COL_SQ), zspec(512, COL_DQ), zspec(512, COL_DK), zspec(512, COL_DV),
            zspec(256, COL_CKV), zspec(256, COL_IQ), zspec(256, COL_IK4),
            pl.BlockSpec((tm, 256), lambda i: (i % nt, 0)),
            pl.BlockSpec((tm, 256), lambda i: (i % nt, 0)),
            const((512, 512)),
            const((1, S_KV_RANK)), const((S_KV_RANK, 2 * S_WIDTH)),
            const((1, 512)), const((1, 512)), const((1, 512)), const((1, 512)),
        ],
        out_specs=[o512, o512, tspec(sv_rows), o256, o256, o512, o512, tspec(dv_rows)],
        out_shape=[s512, s512, tshape(sv_rows), s256, s256, s512, s512, tshape(dv_rows)],
        compiler_params=_cparams(("parallel",)),
        name="attn_prep",
    )(z, z, z, z, z, z, z, rope_c, rope_s, grp, kv_g, w_kv, sq_g, sk_g, dq_g, dk_g)


def _mask_heads(q_ref, qm_ref, n_heads):
    row_grp = lax.broadcasted_iota(I32, (HEAD_GROUP, 1), 0) // ROPE_DIM
    for g in range(n_heads // 4):
        qt = q_ref[:, g * HEAD_GROUP:(g + 1) * HEAD_GROUP].astype(F32).T
        for hh in range(4):
            qm_ref[4 * g + hh] = jnp.where(row_grp == hh, qt, 0.0).astype(BF16)


def _softmax_group(logits, values, rows, acc_ref, m_ref):
    nh = len(logits)
    m_old = m_ref[...]
    m_new = jnp.maximum(m_old, jnp.concatenate([jnp.max(s, axis=0, keepdims=True) for s in logits], axis=0))
    alpha = jnp.exp2(m_old - m_new)
    ps = [jnp.exp2((s - m_new[j:j + 1, :]).astype(BF16)) for j, s in enumerate(logits)]
    m_ref[...] = m_new
    acc_old = acc_ref[...]
    acc_ref[...] = jnp.concatenate(
        [alpha[j:j + 1, :] * acc_old[j * rows:(j + 1) * rows, :] + _dot(values[j], ps[j])
         for j in range(nh)], axis=0)


def _attend(n_groups, logits_fn, values_fn, bufs0, bufs1, rows, acc_ref, m_ref):
    last = n_groups - 1
    group = len(bufs0)
    keys = bufs0[0].shape[1]

    def logits_stage(g, dst_refs):
        pen = jnp.where(g <= last, 0.0, NEG).astype(F32)
        for h, s in enumerate(logits_fn(jnp.minimum(g, last), pen)):
            for b in range(group):
                dst_refs[b][h] = s[b * keys:(b + 1) * keys, :]

    def softmax_stages(g, src_refs):
        for b in range(group):
            _softmax_group([src_refs[b][h] for h in range(src_refs[b].shape[0])],
                           values_fn(jnp.minimum(g, last) * group + b), rows, acc_ref, m_ref)

    logits_stage(0, bufs0)

    def body(gg, carry):
        logits_stage(2 * gg + 1, bufs1)
        softmax_stages(2 * gg, bufs0)
        logits_stage(2 * gg + 2, bufs0)
        softmax_stages(2 * gg + 1, bufs1)
        return carry

    lax.fori_loop(0, (n_groups + 1) // 2, body, 0)


def _sortable(x):
    bits = pltpu.bitcast(x, I32)
    return bits ^ ((bits >> 31) & 0x7FFFFFFF)


def _bit_planes(words):
    a = list(words)
    j, m = 16, 0x0000FFFF
    while j:
        k = 0
        while k < 32:
            t = (a[k] ^ lax.shift_right_logical(a[k + j], jnp.int32(j))) & m
            a[k] = a[k] ^ t
            a[k + j] = a[k + j] ^ (t << j)
            k = (k + j + 1) & ~j
        j >>= 1
        m = (m ^ (m << j)) & 0xFFFFFFFF
    return a


def _dsa_kernel(sq_ref, iq_ref, misc_ref, sk_ref, svt_ref, ik_ref, y_ref,
                sc_ref, bp_ref, qm_ref, iqm_ref, acc_ref, m_ref, sa_ref, sb_ref, res_ref,
                *, tq, topk):
    i = pl.program_id(1)

    @pl.when((pl.program_id(0) == 0) & (i == 0))
    def _():
        bp_ref[...] = jnp.zeros_like(bp_ref)

    nks = (i + 1) * (tq // KSUB)
    q_idx = i * tq + lax.broadcasted_iota(I32, (1, tq), 1)

    _mask_heads(iq_ref, iqm_ref, IDX_HEADS)
    w_t = (misc_ref[...] * (IDX_WIDTH ** -0.5)).T
    w_rows = [w_t[MISC_IW + h:MISC_IW + h + 1, :] for h in range(IDX_HEADS)]

    k_pair = lax.broadcasted_iota(I32, (2 * KSUB, 1), 0)

    def score_block(j, diagonal):
        ikb = ik_ref[pl.ds(pl.multiple_of(j * (2 * KSUB), 2 * KSUB), 2 * KSUB), :]
        s = jnp.zeros((2 * KSUB, tq), F32)
        for h in range(IDX_HEADS):
            s = s + w_rows[h] * jnp.maximum(_dot(ikb, iqm_ref[h]), 0.0)
        key = _sortable(s)
        if diagonal:
            key = jnp.where(j * (2 * KSUB) + k_pair <= q_idx, key, INT_MIN)
        sc_ref[pl.ds(2 * j, 2)] = key.reshape(2, KSUB, tq)
        ob = key ^ INT_MIN
        planes = _bit_planes([ob[v * 8:(v + 1) * 8, :] for v in range(32)])
        bp_ref[:, pl.ds(pl.multiple_of(j * 8, 8), 8), :] = jnp.concatenate(planes, axis=0).reshape(32, 8, tq)

    def score_body(j, carry):
        score_block(j, False)
        return carry

    assert tq == 2 * KSUB
    lax.fori_loop(0, i, score_body, 0)
    score_block(i, True)

    nblk = bp_ref.shape[1] // 8

    def search(rows):
        live_rows = lax.broadcasted_iota(I32, (rows, 1), 0) < (nks // 2) * 8

        def bit_body(t, carry):
            alive, need, thr_ob = carry
            plane = bp_ref[t, 0:rows, :]
            ones = alive & plane
            cnt = jnp.sum(lax.population_count(ones), axis=0, keepdims=True)
            take = cnt >= need
            alive = jnp.where(take, ones, alive ^ ones)
            need = jnp.where(take, need, need - cnt)
            thr_ob = thr_ob | jnp.where(take, jnp.left_shift(jnp.int32(1), 31 - t), 0)
            return alive, need, thr_ob

        alive, need, thr_ob = lax.fori_loop(
            0, 32, bit_body,
            (jnp.broadcast_to(jnp.where(live_rows, -1, 0), (rows, tq)),
             jnp.full((1, tq), topk, I32), jnp.zeros((1, tq), I32)))

        word_row = lax.broadcasted_iota(I32, (rows, 1), 0)
        word_base = (word_row >> 3) * (2 * KSUB) + (word_row & 7)
        idx_bits = (rows * 32 - 1).bit_length()

        def ties_below(trial):
            nv = jnp.clip((trial - word_base + 7) >> 3, 0, 32)
            low = lax.shift_right_logical(jnp.full_like(nv, -1), jnp.minimum(nv, 31))
            mask = jnp.where(nv >= 32, -1, ~low)
            return jnp.sum(lax.population_count(alive & mask), axis=0, keepdims=True)

        def idx_body(t, cut):
            trial = cut | jnp.left_shift(jnp.int32(1), idx_bits - 1 - t)
            return jnp.where(ties_below(trial) < need, trial, cut)

        res_ref[0:1, :] = thr_ob ^ INT_MIN
        res_ref[1:2, :] = lax.fori_loop(0, idx_bits, idx_body, jnp.zeros((1, tq), I32))

    if nblk % 2 == 0:
        @pl.when(nks // 2 <= nblk // 2)
        def _():
            search(nblk * 4)

        @pl.when(nks // 2 > nblk // 2)
        def _():
            search(nblk * 8)
    else:
        search(nblk * 8)
    thr = res_ref[0:1, :]
    cut = res_ref[1:2, :]

    thr_sel = jnp.maximum(thr, INT_MIN + 1)

    def bias_body(j, carry):
        pair = pl.ds(2 * j, 2)
        key = sc_ref[pair].reshape(2 * KSUB, tq)
        kidx = j * (2 * KSUB) + k_pair
        sel = key >= thr_sel + jnp.where(kidx <= cut, 0, 1)
        sc_ref[pair] = pltpu.bitcast(jnp.where(sel, 0.0, NEG).astype(F32), I32).reshape(2, KSUB, tq)
        return carry

    lax.fori_loop(0, nks // 2, bias_body, 0)

    _mask_heads(sq_ref, qm_ref, S_HEADS)
    acc_ref[...] = jnp.zeros_like(acc_ref)
    m_ref[...] = jnp.full_like(m_ref, NEG)

    def logits_fn(ks, pen):
        off = pl.multiple_of(ks * KSUB, KSUB)
        bias = pltpu.bitcast(sc_ref[ks], F32) + pen
        kgs = [sk_ref[pl.ds(off, KSUB), g * HEAD_GROUP:(g + 1) * HEAD_GROUP] for g in range(S_HEADS // 4)]
        return [_dot(kgs[h // 4], qm_ref[h]) + bias for h in range(S_HEADS)]

    def values_fn(ks):
        return [svt_ref[ks, h * S_VROWS:(h + 1) * S_VROWS, :] for h in range(S_HEADS)]

    _attend(nks, logits_fn, values_fn, (sa_ref,), (sb_ref,), S_VROWS, acc_ref, m_ref)

    outs = []
    for h in range(S_HEADS):
        r0 = h * S_VROWS
        outs.append(acc_ref[r0:r0 + ROPE_DIM, :] / acc_ref[r0 + ROPE_DIM:r0 + ROPE_DIM + 1, :])
    y_ref[...] = jnp.concatenate(outs, axis=0).T.astype(BF16)


def _dsa(misc, sq, sk, svt, iq, ik4, batch, seq, *, tq=256):
    n = misc.shape[0]
    tq = min(tq, seq)
    nq = seq // tq
    nsub = seq // KSUB
    topk = min(IDX_TOPK_MAX, seq // 4)
    kern = functools.partial(_dsa_kernel, tq=tq, topk=topk)
    return pl.pallas_call(
        kern,
        grid=(batch, nq),
        in_specs=[
            pl.BlockSpec((tq, S_WIDTH), lambda b, i: (b * nq + i, 0)),
            pl.BlockSpec((tq, IDX_WIDTH), lambda b, i: (b * nq + i, 0)),
            pl.BlockSpec((tq, LANE), lambda b, i: (b * nq + i, 0)),
            pl.BlockSpec((seq, S_WIDTH), lambda b, i: (b, 0)),
            pl.BlockSpec((nsub, S_HEADS * S_VROWS, KSUB), lambda b, i: (b, 0, 0)),
            pl.BlockSpec((seq, IDX_WIDTH), lambda b, i: (b, 0)),
        ],
        out_specs=pl.BlockSpec((tq, S_WIDTH), lambda b, i: (b * nq + i, 0)),
        out_shape=jax.ShapeDtypeStruct((n, S_WIDTH), BF16),
        scratch_shapes=[
            pltpu.VMEM((nsub, KSUB, tq), I32),
            pltpu.VMEM((32, (nsub // 2) * 8, tq), I32),
            pltpu.VMEM((S_HEADS, HEAD_GROUP, tq), BF16),
            pltpu.VMEM((IDX_HEADS, HEAD_GROUP, tq), BF16),
            pltpu.VMEM((S_HEADS * S_VROWS, tq), F32),
            pltpu.VMEM((S_HEADS, tq), F32),
            pltpu.VMEM((S_HEADS, KSUB, tq), F32),
            pltpu.VMEM((S_HEADS, KSUB, tq), F32),
            pltpu.VMEM((8, tq), I32),
        ],
        compiler_params=_cparams(("arbitrary", "arbitrary")),
        name="dsa_attn",
    )(sq, iq, misc, sk, svt, ik4)


def _diff_kernel(q_ref, k_ref, vt_ref, lam_ref, og_ref, y_ref, qm_ref, acc_ref, m_ref,
                 sa_ref, sb_ref, *, tq, lam_init):
    i = pl.program_id(1)
    q_idx = i * tq + lax.broadcasted_iota(I32, (1, tq), 1)
    k_pair = lax.broadcasted_iota(I32, (2 * KSUB, 1), 0)

    _mask_heads(q_ref, qm_ref, DA_MAPS)
    acc_ref[...] = jnp.zeros_like(acc_ref)
    m_ref[...] = jnp.full_like(m_ref, NEG)

    def logits_fn(j, pen):
        off = pl.multiple_of(j * (2 * KSUB), 2 * KSUB)
        bias = jnp.where(j * (2 * KSUB) + k_pair <= q_idx, 0.0, NEG).astype(F32) + pen
        kgs = [k_ref[pl.ds(off, 2 * KSUB), g * HEAD_GROUP:(g + 1) * HEAD_GROUP] for g in range(DA_MAPS // 4)]
        return [_dot(kgs[m // 4], qm_ref[m]) + bias for m in range(DA_MAPS)]

    def values_fn(j):
        heads = [jnp.concatenate([vt_ref[2 * j, hd * DA_VROWS:(hd + 1) * DA_VROWS, :],
                                  vt_ref[2 * j + 1, hd * DA_VROWS:(hd + 1) * DA_VROWS, :]], axis=1)
                 for hd in range(DA_HEADS)]
        return [heads[m // 2] for m in range(DA_MAPS)]

    assert tq == 2 * KSUB
    _attend(i + 1, logits_fn, values_fn, (sa_ref,), (sb_ref,), DA_VROWS, acc_ref, m_ref)

    lam = lam_ref[...]
    p01 = jnp.sum(lam[0:1] * lam[1:2], axis=-1, keepdims=True)
    p23 = jnp.sum(lam[2:3] * lam[3:4], axis=-1, keepdims=True)
    lam_val = jnp.exp(p01) - jnp.exp(p23) + lam_init
    def normalised(m):
        r0 = m * DA_VROWS
        return acc_ref[r0:r0 + DA_V_DIM, :] / acc_ref[r0 + DA_V_DIM:r0 + DA_V_DIM + 1, :]

    outs = []
    for hd in range(DA_HEADS):
        o = normalised(2 * hd) - lam_val * normalised(2 * hd + 1)
        ms = jnp.mean(o * o, axis=0, keepdims=True)
        outs.append(o * lax.rsqrt(ms + EPS) * og_ref[...] * (1.0 - lam_init))
    y_ref[...] = jnp.concatenate(outs, axis=0).T.astype(BF16)


def _diff_attn(dq, dk, dvt, lam, out_g_col, batch, seq, lam_init, *, tq=256):
    n = dq.shape[0]
    tq = min(tq, seq)
    nq = seq // tq
    nsub = seq // KSUB
    kern = functools.partial(_diff_kernel, tq=tq, lam_init=lam_init)
    return pl.pallas_call(
        kern,
        grid=(batch, nq),
        in_specs=[
            pl.BlockSpec((tq, 512), lambda b, i: (b * nq + i, 0)),
            pl.BlockSpec((seq, 512), lambda b, i: (b, 0)),
            pl.BlockSpec((nsub, DA_HEADS * DA_VROWS, KSUB), lambda b, i: (b, 0, 0)),
            pl.BlockSpec((4, ROPE_DIM), lambda b, i: (0, 0)),
            pl.BlockSpec((DA_V_DIM, 1), lambda b, i: (0, 0)),
        ],
        out_specs=pl.BlockSpec((tq, DA_WIDTH), lambda b, i: (b * nq + i, 0)),
        out_shape=jax.ShapeDtypeStruct((n, DA_WIDTH), BF16),
        scratch_shapes=[
            pltpu.VMEM((DA_MAPS, HEAD_GROUP, tq), BF16),
            pltpu.VMEM((DA_MAPS * DA_VROWS, tq), F32),
            pltpu.VMEM((DA_MAPS, tq), F32),
            pltpu.VMEM((DA_MAPS, 2 * KSUB, tq), F32),
            pltpu.VMEM((DA_MAPS, 2 * KSUB, tq), F32),
        ],
        compiler_params=_cparams(("parallel", "arbitrary")),
        name="diff_attn",
    )(dq, dk, dvt, lam, out_g_col)


def _merge_kernel(x_ref, ya_ref, yb_ref, yc_ref, gp_ref, bg_ref, wb_ref, wo_ref, ng_ref, xo_ref, xn_ref):
    merged = None
    for br, y_ref in enumerate((ya_ref, yb_ref, yc_ref)):
        sl = slice(br * D_MODEL, (br + 1) * D_MODEL)
        gate = _sigmoid(gp_ref[:, sl].astype(F32) + bg_ref[:, sl])
        term = gate * _dot(y_ref[...], wb_ref[br])
        merged = term if merged is None else merged + term
    xo = x_ref[...] + _dot(merged.astype(BF16), wo_ref[...])
    xo_ref[...] = xo
    ms = jnp.mean(xo * xo, axis=-1, keepdims=True)
    xn_ref[...] = (xo * lax.rsqrt(ms + EPS) * ng_ref[...]).astype(BF16)


def _merge(x2, ya, yb, yc, z, b_gate, w_branch, w_out, norm_g, *, tm=512):
    n = x2.shape[0]
    tm = min(tm, n)
    row = lambda w: pl.BlockSpec((tm, w), lambda i: (i, 0))
    return pl.pallas_call(
        _merge_kernel,
        grid=(n // tm,),
        in_specs=[
            row(D_MODEL), row(512), row(512), row(512),
            pl.BlockSpec((tm, N_BRANCH * D_MODEL), lambda i: (i, COL_GATE)),
            pl.BlockSpec((1, N_BRANCH * D_MODEL), lambda i: (0, 0)),
            pl.BlockSpec((N_BRANCH, 512, D_MODEL), lambda i: (0, 0, 0)),
            pl.BlockSpec((D_MODEL, D_MODEL), lambda i: (0, 0)),
            pl.BlockSpec((1, D_MODEL), lambda i: (0, 0)),
        ],
        out_specs=[row(D_MODEL), row(D_MODEL)],
        out_shape=[jax.ShapeDtypeStruct((n, D_MODEL), F32), jax.ShapeDtypeStruct((n, D_MODEL), BF16)],
        compiler_params=_cparams(("parallel",)),
        name="merge_out",
    )(x2, ya, yb, yc, z, b_gate, w_branch, w_out, norm_g)


def _ffn_kernel(xn_ref, x_ref, wg_ref, wu_ref, wd_ref, o_ref, *, fc):
    xn = xn_ref[...]
    acc = x_ref[...]
    for c in range(FF_DIM // fc):
        cols = slice(c * fc, (c + 1) * fc)
        g = _dot(xn, wg_ref[:, cols])
        u = _dot(xn, wu_ref[:, cols])
        acc = acc + _dot((g * _sigmoid(g) * u).astype(BF16), wd_ref[cols, :])
    o_ref[...] = acc


def _ffn(xn, x2, wg, wu, wd, *, tm=512, fc=256):
    n = x2.shape[0]
    tm = min(tm, n)
    resident = lambda shape: pl.BlockSpec(shape, lambda i: (0, 0), pipeline_mode=pl.Buffered(1))
    return pl.pallas_call(
        functools.partial(_ffn_kernel, fc=fc),
        grid=(n // tm,),
        in_specs=[
            pl.BlockSpec((tm, D_MODEL), lambda i: (i, 0)),
            pl.BlockSpec((tm, D_MODEL), lambda i: (i, 0)),
            resident((D_MODEL, FF_DIM)),
            resident((D_MODEL, FF_DIM)),
            resident((FF_DIM, D_MODEL)),
        ],
        out_specs=pl.BlockSpec((tm, D_MODEL), lambda i: (i, 0)),
        out_shape=jax.ShapeDtypeStruct((n, D_MODEL), F32),
        compiler_params=_cparams(("parallel",)),
        name="ffn",
    )(xn, x2, wg, wu, wd)


def _rope_tables(seq):
    inv = 1.0 / jnp.power(ROPE_THETA, jnp.arange(0, ROPE_DIM, 2, dtype=F32) / ROPE_DIM)
    ang = jnp.arange(seq, dtype=F32)[:, None] * inv[None, :]
    cos, sin = jnp.cos(ang), jnp.sin(ang)
    c64 = jnp.concatenate([cos, cos], axis=1)
    s64 = jnp.concatenate([-sin, sin], axis=1)
    return jnp.tile(c64, (1, 4)), jnp.tile(s64, (1, 4))


def _layer(x2, li, batch, seq, consts, norm_mix_g, w_in_l, b_gate, conv_w, conv_b, gate_b, m_norm_g,
           kv_norm_g, w_kv_up, sq_g, sk_g, dq_g, dk_g, lam, d_out_g, w_branch, w_out,
           norm_ffn_g, w_gate_up, w_down):
    rope_c, rope_s, grp, tri, shifts = consts
    lam_init = 0.8 - 0.6 * math.exp(-0.3 * li)
    tile8 = lambda g: jnp.tile(g, 8)[None, :]

    z, misc = _inproj(x2, norm_mix_g[None, :], w_in_l)

    gate_row = jnp.zeros((1, LANE), F32).at[0, MISC_IF:MISC_IF + 2 * M_HEADS].set(gate_b)
    ya = _mlstm(z, misc, conv_w, conv_b[None, :], gate_row, m_norm_g.reshape(1, M_WIDTH), tri, shifts,
                batch, seq)

    sq, sk, svt, iq, ik4, dq, dk, dvt = _prep(
        z, rope_c, rope_s, grp, kv_norm_g[None, :], w_kv_up.astype(BF16),
        tile8(sq_g), tile8(sk_g), tile8(dq_g), tile8(dk_g), seq)
    yb = _dsa(misc, sq, sk, svt, iq, ik4, batch, seq)
    yc = _diff_attn(dq, dk, dvt, lam, d_out_g[:, None], batch, seq, lam_init)

    xo, xn = _merge(x2, ya, yb, yc, z, b_gate[None, :], w_branch.astype(BF16), w_out.astype(BF16),
                    norm_ffn_g[None, :])
    return _ffn(xn, xo, w_gate_up[:, :FF_DIM].astype(BF16), w_gate_up[:, FF_DIM:].astype(BF16),
                w_down.astype(BF16))


def kernel(x, norm_mix_g, w_in, b_gate, mlstm_conv_w, mlstm_conv_b, mlstm_gate_b, mlstm_norm_g,
           dsa_kv_norm_g, dsa_w_kv_up, dsa_q_norm_g, dsa_k_norm_g, diff_q_norm_g, diff_k_norm_g,
           diff_lambda, diff_out_norm_g, w_branch, w_out, norm_ffn_g, w_gate_up, w_down):
    batch, seq, d = x.shape
    depth = w_in.shape[0]
    assert w_in.shape[2] == IN_COLS and d == D_MODEL
    rope_c, rope_s = _rope_tables(seq)
    gi = jnp.arange(512) // ROPE_DIM
    grp = (gi[:, None] == gi[None, :]).astype(BF16)
    ti = jnp.arange(M_CHUNK)
    tri = (ti[:, None] >= ti[None, :]).astype(BF16)
    ri = jnp.arange(M_TAIL + M_CHUNK)
    shifts = jnp.stack([(ri[None, :] == ti[:, None] + M_TAIL - j) for j in range(1, M_CONV)]).astype(BF16)
    consts = (rope_c, rope_s, grp, tri, shifts)
    w_in_l = _wlayout(w_in)
    x2 = x.reshape(batch * seq, d)
    for li in range(depth):
        x2 = _layer(x2, li, batch, seq, consts, norm_mix_g[li], w_in_l[li], b_gate[li], mlstm_conv_w[li],
                    mlstm_conv_b[li], mlstm_gate_b[li], mlstm_norm_g[li], dsa_kv_norm_g[li],
                    dsa_w_kv_up[li], dsa_q_norm_g[li], dsa_k_norm_g[li], diff_q_norm_g[li],
                    diff_k_norm_g[li], diff_lambda[li], diff_out_norm_g[li], w_branch[li], w_out[li],
                    norm_ffn_g[li], w_gate_up[li], w_down[li])
    return x2.reshape(batch, seq, d)
```

```python
import functools
import math

import jax
import jax.numpy as jnp
from jax import lax
from jax.experimental import pallas as pl
from jax.experimental.pallas import tpu as pltpu

F32 = jnp.float32
BF16 = jnp.bfloat16
I32 = jnp.int32
I16 = jnp.int16

D_MODEL = 1024
EPS = 1e-6
ROPE_DIM = 64
ROPE_HALF = ROPE_DIM // 2
ROPE_THETA = 10000.0

M_HEADS = 4
M_HEAD_DIM = 128
M_WIDTH = M_HEADS * M_HEAD_DIM
M_CONV = 4
M_CHUNK = 128
M_TAIL = 16
M_STEP_CHUNKS = 4
M_INIT = -1e30

S_HEADS = 8
S_WIDTH = S_HEADS * ROPE_DIM
S_KV_RANK = 256
IDX_HEADS = 4
IDX_WIDTH = IDX_HEADS * ROPE_DIM
IDX_TOPK_MAX = 256

DA_HEADS = 4
DA_MAPS = 2 * DA_HEADS
DA_V_DIM = 2 * ROPE_DIM
DA_WIDTH = DA_HEADS * DA_V_DIM

N_BRANCH = 3
FF_DIM = 2816

NEG = -1e30
LOG2E = math.log2(math.e)
INT_MIN = -(2**31)
HALF16 = 2**15
LANE = 128
HEAD_GROUP = 256
KSUB = 128
ONES_ROWS = 16
S_VROWS = ROPE_DIM + ONES_ROWS
DA_VROWS = DA_V_DIM + ONES_ROWS

W_IN_SIZES = (2 * M_WIDTH, M_WIDTH, M_WIDTH, 2 * M_HEADS, S_WIDTH, S_KV_RANK, IDX_WIDTH, ROPE_DIM,
              IDX_HEADS, 2 * DA_HEADS * ROPE_DIM, 2 * DA_HEADS * ROPE_DIM, DA_WIDTH, N_BRANCH * D_MODEL)
IN_COLS = sum(W_IN_SIZES)

COL_GATE = 0
COL_MQK = 3072
COL_MV = 4096
COL_MO = 4608
COL_SQ = 5120
COL_DQ = 5632
COL_DK = 6144
COL_DV = 6656
COL_CKV = 7168
COL_IQ = 7424
COL_IK4 = 7680
COL_MISC = 7936
MISC_IF = 0
MISC_IW = 8
Z_COLS = 8192

VMEM_LIMIT = 56 * 1024 * 1024


def _cparams(sem, flags=None):
    return pltpu.CompilerParams(dimension_semantics=sem, vmem_limit_bytes=VMEM_LIMIT, flags=flags)


def _sigmoid(x):
    return 1.0 / (1.0 + jnp.exp(-x))


def _dot(a, b):
    return jnp.dot(a, b, preferred_element_type=F32)


def _dot_nt(a, b):
    return lax.dot_general(a, b, (((1,), (1,)), ((), ())), preferred_element_type=F32)


def _dot_tn(a, b):
    return lax.dot_general(a, b, (((0,), (0,)), ((), ())), preferred_element_type=F32)


def _split3(x):
    hi = x.astype(BF16)
    r1 = x - hi.astype(F32)
    mid = r1.astype(BF16)
    lo = (r1 - mid.astype(F32)).astype(BF16)
    return hi, mid, lo


def _wlayout_kernel(w_ref, o_ref):
    w = w_ref[...]
    offs = [0]
    for s in W_IN_SIZES:
        offs.append(offs[-1] + s)
    seg = [w[:, offs[k]:offs[k + 1]] for k in range(len(W_IN_SIZES))]
    m_qk, m_v, m_o, m_if, s_q, s_ckv, i_q, i_k, i_w, d_q, d_k, d_v, g_pre = seg
    rows = w.shape[0]
    misc = jnp.concatenate([m_if, i_w, jnp.zeros((rows, LANE - 2 * M_HEADS - IDX_HEADS), F32)], axis=1)
    ik4 = jnp.concatenate([i_k] * IDX_HEADS, axis=1)
    tail = jnp.zeros((rows, Z_COLS - COL_MISC - LANE), F32)
    for col, val in ((COL_GATE, g_pre), (COL_MQK, m_qk), (COL_MV, m_v), (COL_MO, m_o), (COL_SQ, s_q),
                     (COL_DQ, d_q), (COL_DK, d_k), (COL_DV, d_v), (COL_CKV, s_ckv), (COL_IQ, i_q),
                     (COL_IK4, ik4), (COL_MISC, misc), (COL_MISC + LANE, tail)):
        o_ref[:, col:col + val.shape[1]] = val.astype(BF16)


def _wlayout(w_in, *, tr=256):
    depth, rows, cols = w_in.shape
    return pl.pallas_call(
        _wlayout_kernel,
        grid=(depth, rows // tr),
        in_specs=[pl.BlockSpec((None, tr, cols), lambda l, i: (l, i, 0))],
        out_specs=pl.BlockSpec((None, tr, Z_COLS), lambda l, i: (l, i, 0)),
        out_shape=jax.ShapeDtypeStruct((depth, rows, Z_COLS), BF16),
        compiler_params=_cparams(("parallel", "parallel")),
        name="w_layout",
    )(w_in)


def _inproj_kernel(x_ref, g_ref, w_ref, z_ref, misc_ref, *, tn):
    x = x_ref[...]
    ms = jnp.mean(x * x, axis=-1, keepdims=True)
    xn = (x * lax.rsqrt(ms + EPS) * g_ref[...]).astype(BF16)
    for c in range(Z_COLS // tn):
        r = _dot(xn, w_ref[:, c * tn:(c + 1) * tn])
        z_ref[:, c * tn:(c + 1) * tn] = r.astype(BF16)
        if c == COL_MISC // tn:
            misc_ref[...] = r[:, COL_MISC % tn:COL_MISC % tn + LANE]


def _inproj(x2, g, w, *, tm=512, tn=1024):
    n = x2.shape[0]
    tm = min(tm, n)
    return pl.pallas_call(
        functools.partial(_inproj_kernel, tn=tn),
        grid=(n // tm,),
        in_specs=[
            pl.BlockSpec((tm, D_MODEL), lambda i: (i, 0)),
            pl.BlockSpec((1, D_MODEL), lambda i: (0, 0)),
            pl.BlockSpec((D_MODEL, Z_COLS), lambda i: (0, 0), pipeline_mode=pl.Buffered(1)),
        ],
        out_specs=[pl.BlockSpec((tm, Z_COLS), lambda i: (i, 0)),
                   pl.BlockSpec((tm, LANE), lambda i: (i, 0))],
        out_shape=[jax.ShapeDtypeStruct((n, Z_COLS), BF16), jax.ShapeDtypeStruct((n, LANE), F32)],
        compiler_params=_cparams(("parallel",)),
        name="inproj",
    )(x2, g, w)


def _mlstm_kernel(qk_ref, v_ref, o_ref, misc_ref, cw_ref, cb_ref, gb_ref, ng_ref, tri_ref, sh_ref, y_ref,
                  tail_ref, qkc_ref, c_ref, n_ref, m_ref):
    @pl.when(pl.program_id(1) == 0)
    def _():
        tail_ref[...] = jnp.zeros_like(tail_ref)
        c_ref[...] = jnp.zeros_like(c_ref)
        n_ref[...] = jnp.zeros_like(n_ref)
        m_ref[...] = jnp.full_like(m_ref, M_INIT)

    for cc in range(qk_ref.shape[0] // M_CHUNK):
        rows = pl.ds(cc * M_CHUNK, M_CHUNK)
        _mlstm_chunk(qk_ref.at[rows], v_ref.at[rows], o_ref.at[rows], misc_ref.at[rows], cw_ref, cb_ref,
                     gb_ref, ng_ref, tri_ref, sh_ref, y_ref.at[rows], tail_ref, qkc_ref, c_ref, n_ref, m_ref)


def _mlstm_chunk(qk_ref, v_ref, o_ref, misc_ref, cw_ref, cb_ref, gb_ref, ng_ref, tri_ref, sh_ref, y_ref,
                 tail_ref, qkc_ref, c_ref, n_ref, m_ref):
    L = M_CHUNK
    dh = M_HEAD_DIM

    xb = qk_ref[...]
    ext = jnp.concatenate([tail_ref[...], xb], axis=0)
    cw = cw_ref[...]
    acc = xb.astype(F32) * cw[M_CONV - 1:M_CONV] + cb_ref[...]
    for j in range(1, M_CONV):
        acc = acc + _dot(sh_ref[j - 1], ext) * cw[M_CONV - 1 - j:M_CONV - j]
    qkc_ref[...] = acc * _sigmoid(acc)
    tail_ref[...] = xb[L - M_TAIL:L]

    gates = misc_ref[...] + gb_ref[...]
    lf = jnp.minimum(gates, 0.0) - jnp.log(1.0 + jnp.exp(-jnp.abs(gates)))
    tri = tri_ref[...]
    hi, mid, lo = _split3(lf)
    bcum = _dot(tri, hi) + _dot(tri, mid) + _dot(tri, lo)
    bcum_t = bcum.T
    gates_t = gates.T

    t_idx = lax.broadcasted_iota(I32, (L, L), 0)
    s_idx = lax.broadcasted_iota(I32, (L, L), 1)
    causal = t_idx >= s_idx

    ys, cs, ns, ms_new = [], [], [], []
    for h in range(M_HEADS):
        sl = slice(h * dh, (h + 1) * dh)
        q = qkc_ref[:, h * dh:(h + 1) * dh]
        k = qkc_ref[:, M_WIDTH + h * dh:M_WIDTH + (h + 1) * dh] * (dh ** -0.5)
        vb = v_ref[:, sl]
        v = vb.astype(F32)
        qb, kb = q.astype(BF16), k.astype(BF16)

        b_col = bcum[:, M_HEADS + h:M_HEADS + h + 1]
        i_col = gates[:, h:h + 1]
        b_row = bcum_t[M_HEADS + h:M_HEADS + h + 1, :]
        i_row = gates_t[h:h + 1, :]
        g_tot = bcum[L - 1:L, M_HEADS + h:M_HEADS + h + 1]

        c_prev = c_ref[sl, :]
        n_prev = n_ref[h:h + 1, :]
        m_prev = m_ref[h:h + 1, 0:1]

        rmat = jnp.where(causal, i_row - b_row, -jnp.inf)
        mx = jnp.maximum(m_prev, jnp.max(rmat, axis=-1, keepdims=True))
        m_t = b_col + mx
        sw = jnp.exp(rmat - mx) * _dot_nt(qb, kb)
        s_inter = jnp.exp(m_prev - mx)
        num = _dot(sw.astype(BF16), vb) + s_inter * _dot_nt(qb, c_prev.astype(BF16))
        den = jnp.sum(sw, axis=-1, keepdims=True) + s_inter * jnp.sum(q * n_prev, axis=-1, keepdims=True)
        dmax = jnp.maximum(jnp.abs(den), jnp.exp(-m_t))
        mnum = jnp.mean(num * num, axis=-1, keepdims=True)
        scale = lax.rsqrt(mnum + EPS * dmax * dmax)
        ys.append(_sigmoid(o_ref[:, sl].astype(F32)) * (num * scale * ng_ref[:, sl]))

        a_col = g_tot - b_col + i_col
        m_loc = jnp.max(a_col, axis=0, keepdims=True)
        w_loc = jnp.exp(a_col - m_loc)
        c_loc = _dot_tn((v * w_loc).astype(BF16), kb)
        n_loc = jnp.sum(k * w_loc, axis=0, keepdims=True)
        m_new = jnp.maximum(g_tot + m_prev, m_loc)
        s_old = jnp.exp(g_tot + m_prev - m_new)
        s_loc = jnp.exp(m_loc - m_new)
        cs.append(s_old * c_prev + s_loc * c_loc)
        ns.append(s_old * n_prev + s_loc * n_loc)
        ms_new.append(jnp.broadcast_to(m_new, (1, LANE)))

    y_ref[...] = jnp.concatenate(ys, axis=1).astype(BF16)
    c_ref[...] = jnp.concatenate(cs, axis=0)
    n_ref[...] = jnp.concatenate(ns, axis=0)
    m_ref[...] = jnp.concatenate(ms_new, axis=0)


def _mlstm(z, misc, conv_w, conv_b, gate_b_row, norm_g_row, tri, shifts, batch, seq):
    n = z.shape[0]
    L = M_CHUNK
    rows = M_STEP_CHUNKS * L
    nc = seq // rows
    row = lambda b, c: b * nc + c
    return pl.pallas_call(
        _mlstm_kernel,
        grid=(batch, nc),
        in_specs=[
            pl.BlockSpec((rows, 2 * M_WIDTH), lambda b, c: (row(b, c), COL_MQK // (2 * M_WIDTH))),
            pl.BlockSpec((rows, M_WIDTH), lambda b, c: (row(b, c), COL_MV // M_WIDTH)),
            pl.BlockSpec((rows, M_WIDTH), lambda b, c: (row(b, c), COL_MO // M_WIDTH)),
            pl.BlockSpec((rows, LANE), lambda b, c: (row(b, c), 0)),
            pl.BlockSpec((M_CONV, 2 * M_WIDTH), lambda b, c: (0, 0)),
            pl.BlockSpec((1, 2 * M_WIDTH), lambda b, c: (0, 0)),
            pl.BlockSpec((1, LANE), lambda b, c: (0, 0)),
            pl.BlockSpec((1, M_WIDTH), lambda b, c: (0, 0)),
            pl.BlockSpec((L, L), lambda b, c: (0, 0)),
            pl.BlockSpec((M_CONV - 1, L, M_TAIL + L), lambda b, c: (0, 0, 0)),
        ],
        out_specs=pl.BlockSpec((rows, M_WIDTH), lambda b, c: (row(b, c), 0)),
        out_shape=jax.ShapeDtypeStruct((n, M_WIDTH), BF16),
        scratch_shapes=[
            pltpu.VMEM((M_TAIL, 2 * M_WIDTH), BF16),
            pltpu.VMEM((L, 2 * M_WIDTH), F32),
            pltpu.VMEM((M_HEADS * M_HEAD_DIM, M_HEAD_DIM), F32),
            pltpu.VMEM((M_HEADS, M_HEAD_DIM), F32),
            pltpu.VMEM((M_HEADS, LANE), F32),
        ],
        compiler_params=_cparams(("parallel", "arbitrary")),
        name="mlstm",
    )(z, z, z, misc, conv_w, conv_b, gate_b_row, norm_g_row, tri, shifts)


def _rope(x, c, s):
    w = x.shape[-1]
    lane = lax.broadcasted_iota(I32, (1, w), 1)
    fwd = pltpu.roll(x, ROPE_HALF, axis=1)
    bwd = pltpu.roll(x, w - ROPE_HALF, axis=1)
    swapped = jnp.where((lane & (ROPE_DIM - 1)) < ROPE_HALF, bwd, fwd)
    return x * c + swapped * s


def _head_rms(x, g, grp):
    sq = x * x
    hi = sq.astype(BF16)
    lo = (sq - hi.astype(F32)).astype(BF16)
    ss = _dot(hi, grp) + _dot(lo, grp)
    return x * lax.rsqrt(ss * (1.0 / ROPE_DIM) + EPS) * g


def _store_transposed(o_ref, v, dv):
    ones = jnp.ones((ONES_ROWS, KSUB), BF16)
    for j in range(o_ref.shape[0]):
        vt = v[j * KSUB:(j + 1) * KSUB, :].T.astype(BF16)
        parts = []
        for h in range(vt.shape[0] // dv):
            parts += [vt[h * dv:(h + 1) * dv, :], ones]
        o_ref[j] = jnp.concatenate(parts, axis=0)


def _prep_kernel(sq_ref, dq_ref, dk_ref, dv_ref, ckv_ref, iq_ref, ik_ref, c_ref, s_ref, grp_ref,
                 kvg_ref, wkv_ref, sqg_ref, skg_ref, dqg_ref, dkg_ref,
                 sq_o, sk_o, sv_o, iq_o, ik_o, dq_o, dk_o, dv_o):
    c256 = c_ref[...]
    s256 = s_ref[...]
    c512 = jnp.concatenate([c256, c256], axis=1)
    s512 = jnp.concatenate([s256, s256], axis=1)
    grp = grp_ref[...]
    qscale = ROPE_DIM ** -0.5 * LOG2E

    f32 = lambda ref: ref[...].astype(F32)

    sq_o[...] = (_rope(_head_rms(f32(sq_ref), sqg_ref[...], grp), c512, s512) * qscale).astype(BF16)

    ckv = f32(ckv_ref)
    ms = jnp.mean(ckv * ckv, axis=-1, keepdims=True)
    ckvn = (ckv * lax.rsqrt(ms + EPS) * kvg_ref[...]).astype(BF16)
    kv = _dot(ckvn, wkv_ref[...])
    sk_o[...] = _rope(_head_rms(kv[:, :S_WIDTH], skg_ref[...], grp), c512, s512).astype(BF16)
    _store_transposed(sv_o, kv[:, S_WIDTH:], ROPE_DIM)

    iq_o[...] = _rope(f32(iq_ref), c256, s256).astype(BF16)
    ik_o[...] = _rope(f32(ik_ref), c256, s256).astype(BF16)

    dq_o[...] = (_rope(_head_rms(f32(dq_ref), dqg_ref[...], grp), c512, s512) * qscale).astype(BF16)
    dk_o[...] = _rope(_head_rms(f32(dk_ref), dkg_ref[...], grp), c512, s512).astype(BF16)
    _store_transposed(dv_o, f32(dv_ref), DA_V_DIM)


def _prep(z, rope_c, rope_s, grp, kv_g, w_kv, sq_g, sk_g, dq_g, dk_g, seq, *, tm=512):
    n = z.shape[0]
    tm = min(tm, seq)
    nt = seq // tm
    zspec = lambda w, col: pl.BlockSpec((tm, w), lambda i: (i, col // w))
    const = lambda shape: pl.BlockSpec(shape, lambda i: (0, 0))
    o512 = pl.BlockSpec((tm, 512), lambda i: (i, 0))
    o256 = pl.BlockSpec((tm, 256), lambda i: (i, 0))
    s512 = jax.ShapeDtypeStruct((n, 512), BF16)
    s256 = jax.ShapeDtypeStruct((n, 256), BF16)
    tspec = lambda rows: pl.BlockSpec((tm // KSUB, rows, KSUB), lambda i: (i, 0, 0))
    tshape = lambda rows: jax.ShapeDtypeStruct((n // KSUB, rows, KSUB), BF16)
    sv_rows, dv_rows = S_HEADS * S_VROWS, DA_HEADS * DA_VROWS
    return pl.pallas_call(
        _prep_kernel,
        grid=(n // tm,),
        in_specs=[
            zspec(512, COL_SQ), zspec(512, COL_DQ), zspec(512, COL_DK), zspec(512, COL_DV),
            zspec(256, COL_CKV), zspec(256, COL_IQ), zspec(256, COL_IK4),
            pl.BlockSpec((tm, 256), lambda i: (i % nt, 0)),
            pl.BlockSpec((tm, 256), lambda i: (i % nt, 0)),
            const((512, 512)),
            const((1, S_KV_RANK)), const((S_KV_RANK, 2 * S_WIDTH)),
            const((1, 512)), const((1, 512)), const((1, 512)), const((1, 512)),
        ],
        out_specs=[o512, o512, tspec(sv_rows), o256, o256, o512, o512, tspec(dv_rows)],
        out_shape=[s512, s512, tshape(sv_rows), s256, s256, s512, s512, tshape(dv_rows)],
        compiler_params=_cparams(("parallel",)),
        name="attn_prep",
    )(z, z, z, z, z, z, z, rope_c, rope_s, grp, kv_g, w_kv, sq_g, sk_g, dq_g, dk_g)


def _mask_heads(q_ref, qm_ref, n_heads):
    row_grp = lax.broadcasted_iota(I32, (HEAD_GROUP, 1), 0) // ROPE_DIM
    for g in range(n_heads // 4):
        qt = q_ref[:, g * HEAD_GROUP:(g + 1) * HEAD_GROUP].astype(F32).T
        for hh in range(4):
            qm_ref[4 * g + hh] = jnp.where(row_grp == hh, qt, 0.0).astype(BF16)


def _softmax_group(logits, values, rows, acc_ref, m_ref):
    nh = len(logits)
    m_old = m_ref[...]
    m_new = jnp.maximum(m_old, jnp.concatenate([jnp.max(s, axis=0, keepdims=True) for s in logits], axis=0))
    alpha = jnp.exp2(m_old - m_new)
    ps = [jnp.exp2((s - m_new[j:j + 1, :]).astype(BF16)) for j, s in enumerate(logits)]
    m_ref[...] = m_new
    acc_old = acc_ref[...]
    acc_ref[...] = jnp.concatenate(
        [alpha[j:j + 1, :] * acc_old[j * rows:(j + 1) * rows, :] + _dot(values[j], ps[j])
         for j in range(nh)], axis=0)


def _attend(n_groups, logits_fn, values_fn, bufs0, bufs1, rows, acc_ref, m_ref, side_fn=None):
    last = n_groups - 1
    group = len(bufs0)
    keys = bufs0[0].shape[1]

    def logits_stage(g, dst_refs):
        pen = jnp.where(g <= last, 0.0, NEG).astype(F32)
        for h, s in enumerate(logits_fn(jnp.minimum(g, last), pen)):
            for b in range(group):
                dst_refs[b][h] = s[b * keys:(b + 1) * keys, :]

    def softmax_stages(g, src_refs):
        for b in range(group):
            _softmax_group([src_refs[b][h] for h in range(src_refs[b].shape[0])],
                           values_fn(jnp.minimum(g, last) * group + b), rows, acc_ref, m_ref)

    logits_stage(0, bufs0)

    def body(gg, carry):
        logits_stage(2 * gg + 1, bufs1)
        softmax_stages(2 * gg, bufs0)
        logits_stage(2 * gg + 2, bufs0)
        softmax_stages(2 * gg + 1, bufs1)
        if side_fn is not None:
            side_fn(gg)
        return carry

    lax.fori_loop(0, (n_groups + 1) // 2, body, 0)


def _sortable(x):
    bits = pltpu.bitcast(x, I32)
    return bits ^ ((bits >> 31) & 0x7FFFFFFF)


def _bit_planes(words):
    a = list(words)
    j, m = 16, 0x0000FFFF
    while j:
        k = 0
        while k < 32:
            t = (a[k] ^ lax.shift_right_logical(a[k + j], jnp.int32(j))) & m
            a[k] = a[k] ^ t
            a[k + j] = a[k + j] ^ (t << j)
            k = (k + j + 1) & ~j
        j >>= 1
        m = (m ^ (m << j)) & 0xFFFFFFFF
    return a


def _dsa_kernel(sq_ref, iq_ref, misc_ref, iqn_ref, miscn_ref, sk_ref, svt_ref, ik_ref, y_ref,
                sc2_ref, bp2_ref, qm_ref, iqm_ref, acc_ref, m_ref, sa_ref, sb_ref, res_ref,
                *, tq, topk):
    i = pl.program_id(1)

    @pl.when((pl.program_id(0) == 0) & (i == 0))
    def _():
        bp2_ref[...] = jnp.zeros_like(bp2_ref)

    assert tq == 2 * KSUB
    nks = (i + 1) * (tq // KSUB)
    lane_q = lax.broadcasted_iota(I32, (1, tq), 1)
    k_pair = lax.broadcasted_iota(I32, (2 * KSUB, 1), 0)

    cur = lax.rem(i, 2)
    sc_ref, bp_ref = sc2_ref.at[cur], bp2_ref.at[cur]
    sc_next, bp_next = sc2_ref.at[1 - cur], bp2_ref.at[1 - cur]

    def head_weights(m_ref_):
        w_t = (m_ref_[...] * (IDX_WIDTH ** -0.5)).T
        return [w_t[MISC_IW + h:MISC_IW + h + 1, :] for h in range(IDX_HEADS)]

    def score_block(j, q_idx, w_rows, sc_dst, bp_dst, diagonal):
        ikb = ik_ref[pl.ds(pl.multiple_of(j * (2 * KSUB), 2 * KSUB), 2 * KSUB), :]
        s = jnp.zeros((2 * KSUB, tq), F32)
        for h in range(IDX_HEADS):
            s = s + w_rows[h] * jnp.maximum(_dot(ikb, iqm_ref[h]), 0.0)
        key = _sortable(s)
        if diagonal:
            key = jnp.where(j * (2 * KSUB) + k_pair <= q_idx, key, INT_MIN)
        sc_dst[pl.ds(2 * j, 2)] = key.reshape(2, KSUB, tq)
        ob = key ^ INT_MIN
        planes = _bit_planes([ob[v * 8:(v + 1) * 8, :] for v in range(32)])
        bp_dst[:, pl.ds(pl.multiple_of(j * 8, 8), 8), :] = jnp.concatenate(planes, axis=0).reshape(32, 8, tq)

    @pl.when(i == 0)
    def _():
        _mask_heads(iq_ref, iqm_ref, IDX_HEADS)
        score_block(0, lane_q, head_weights(misc_ref), sc_ref, bp_ref, True)

    _mask_heads(iqn_ref, iqm_ref, IDX_HEADS)
    w_next = head_weights(miscn_ref)
    q_idx_next = (i + 1) * tq + lane_q

    nblk = bp_ref.shape[1] // 8

    def search(rows):
        live_rows = lax.broadcasted_iota(I32, (rows, 1), 0) < (nks // 2) * 8

        def bit_body(t, carry):
            alive, need, thr_ob = carry
            plane = bp_ref[t, 0:rows, :]
            ones = alive & plane
            cnt = jnp.sum(lax.population_count(ones), axis=0, keepdims=True)
            take = cnt >= need
            alive = jnp.where(take, ones, alive ^ ones)
            need = jnp.where(take, need, need - cnt)
            thr_ob = thr_ob | jnp.where(take, jnp.left_shift(jnp.int32(1), 31 - t), 0)
            return alive, need, thr_ob

        alive, need, thr_ob = lax.fori_loop(
            0, 32, bit_body,
            (jnp.broadcast_to(jnp.where(live_rows, -1, 0), (rows, tq)),
             jnp.full((1, tq), topk, I32), jnp.zeros((1, tq), I32)))

        word_row = lax.broadcasted_iota(I32, (rows, 1), 0)
        word_base = (word_row >> 3) * (2 * KSUB) + (word_row & 7)
        idx_bits = (rows * 32 - 1).bit_length()

        def ties_below(trial):
            nv = jnp.clip((trial - word_base + 7) >> 3, 0, 32)
            low = lax.shift_right_logical(jnp.full_like(nv, -1), jnp.minimum(nv, 31))
            mask = jnp.where(nv >= 32, -1, ~low)
            return jnp.sum(lax.population_count(alive & mask), axis=0, keepdims=True)

        def idx_body(t, cut):
            trial = cut | jnp.left_shift(jnp.int32(1), idx_bits - 1 - t)
            return jnp.where(ties_below(trial) < need, trial, cut)

        res_ref[0:1, :] = thr_ob ^ INT_MIN
        res_ref[1:2, :] = lax.fori_loop(0, idx_bits, idx_body, jnp.zeros((1, tq), I32))

    if nblk % 2 == 0:
        @pl.when(nks // 2 <= nblk // 2)
        def _():
            search(nblk * 4)

        @pl.when(nks // 2 > nblk // 2)
        def _():
            search(nblk * 8)
    else:
        search(nblk * 8)
    thr = res_ref[0:1, :]
    cut = res_ref[1:2, :]

    thr_sel = jnp.maximum(thr, INT_MIN + 1)

    def bias_body(j, carry):
        pair = pl.ds(2 * j, 2)
        key = sc_ref[pair].reshape(2 * KSUB, tq)
        kidx = j * (2 * KSUB) + k_pair
        sel = key >= thr_sel + jnp.where(kidx <= cut, 0, 1)
        sc_ref[pair] = pltpu.bitcast(jnp.where(sel, 0.0, NEG).astype(F32), I32).reshape(2, KSUB, tq)
        return carry

    lax.fori_loop(0, nks // 2, bias_body, 0)

    _mask_heads(sq_ref, qm_ref, S_HEADS)
    acc_ref[...] = jnp.zeros_like(acc_ref)
    m_ref[...] = jnp.full_like(m_ref, NEG)

    def logits_fn(ks, pen):
        off = pl.multiple_of(ks * KSUB, KSUB)
        bias = pltpu.bitcast(sc_ref[ks], F32) + pen
        kgs = [sk_ref[pl.ds(off, KSUB), g * HEAD_GROUP:(g + 1) * HEAD_GROUP] for g in range(S_HEADS // 4)]
        return [_dot(kgs[h // 4], qm_ref[h]) + bias for h in range(S_HEADS)]

    def values_fn(ks):
        return [svt_ref[ks, h * S_VROWS:(h + 1) * S_VROWS, :] for h in range(S_HEADS)]

    def score_next(g):
        score_block(g, q_idx_next, w_next, sc_next, bp_next, False)

    _attend(nks, logits_fn, values_fn, (sa_ref,), (sb_ref,), S_VROWS, acc_ref, m_ref, side_fn=score_next)

    @pl.when(i + 1 < pl.num_programs(1))
    def _():
        score_block(i + 1, q_idx_next, w_next, sc_next, bp_next, True)

    outs = []
    for h in range(S_HEADS):
        r0 = h * S_VROWS
        outs.append(acc_ref[r0:r0 + ROPE_DIM, :] / acc_ref[r0 + ROPE_DIM:r0 + ROPE_DIM + 1, :])
    y_ref[...] = jnp.concatenate(outs, axis=0).T.astype(BF16)


def _dsa(misc, sq, sk, svt, iq, ik4, batch, seq, *, tq=256):
    n = misc.shape[0]
    tq = min(tq, seq)
    nq = seq // tq
    nsub = seq // KSUB
    topk = min(IDX_TOPK_MAX, seq // 4)
    kern = functools.partial(_dsa_kernel, tq=tq, topk=topk)
    nxt = lambda b, i: (b * nq + jnp.minimum(i + 1, nq - 1), 0)
    return pl.pallas_call(
        kern,
        grid=(batch, nq),
        in_specs=[
            pl.BlockSpec((tq, S_WIDTH), lambda b, i: (b * nq + i, 0)),
            pl.BlockSpec((tq, IDX_WIDTH), lambda b, i: (b * nq + i, 0)),
            pl.BlockSpec((tq, LANE), lambda b, i: (b * nq + i, 0)),
            pl.BlockSpec((tq, IDX_WIDTH), nxt),
            pl.BlockSpec((tq, LANE), nxt),
            pl.BlockSpec((seq, S_WIDTH), lambda b, i: (b, 0)),
            pl.BlockSpec((nsub, S_HEADS * S_VROWS, KSUB), lambda b, i: (b, 0, 0)),
            pl.BlockSpec((seq, IDX_WIDTH), lambda b, i: (b, 0)),
        ],
        out_specs=pl.BlockSpec((tq, S_WIDTH), lambda b, i: (b * nq + i, 0)),
        out_shape=jax.ShapeDtypeStruct((n, S_WIDTH), BF16),
        scratch_shapes=[
            pltpu.VMEM((2, nsub, KSUB, tq), I32),
            pltpu.VMEM((2, 32, (nsub // 2) * 8, tq), I32),
            pltpu.VMEM((S_HEADS, HEAD_GROUP, tq), BF16),
            pltpu.VMEM((IDX_HEADS, HEAD_GROUP, tq), BF16),
            pltpu.VMEM((S_HEADS * S_VROWS, tq), F32),
            pltpu.VMEM((S_HEADS, tq), F32),
            pltpu.VMEM((S_HEADS, KSUB, tq), F32),
            pltpu.VMEM((S_HEADS, KSUB, tq), F32),
            pltpu.VMEM((8, tq), I32),
        ],
        compiler_params=_cparams(("arbitrary", "arbitrary")),
        name="dsa_attn",
    )(sq, iq, misc, iq, misc, sk, svt, ik4)


def _diff_kernel(q_ref, k_ref, vt_ref, lam_ref, og_ref, y_ref, qm_ref, acc_ref, m_ref,
                 sa_ref, sb_ref, *, tq, lam_init):
    i = pl.program_id(1)
    q_idx = i * tq + lax.broadcasted_iota(I32, (1, tq), 1)
    k_pair = lax.broadcasted_iota(I32, (2 * KSUB, 1), 0)

    _mask_heads(q_ref, qm_ref, DA_MAPS)
    acc_ref[...] = jnp.zeros_like(acc_ref)
    m_ref[...] = jnp.full_like(m_ref, NEG)

    def logits_fn(j, pen):
        off = pl.multiple_of(j * (2 * KSUB), 2 * KSUB)
        bias = jnp.where(j * (2 * KSUB) + k_pair <= q_idx, 0.0, NEG).astype(F32) + pen
        kgs = [k_ref[pl.ds(off, 2 * KSUB), g * HEAD_GROUP:(g + 1) * HEAD_GROUP] for g in range(DA_MAPS // 4)]
        return [_dot(kgs[m // 4], qm_ref[m]) + bias for m in range(DA_MAPS)]

    def values_fn(j):
        heads = [jnp.concatenate([vt_ref[2 * j, hd * DA_VROWS:(hd + 1) * DA_VROWS, :],
                                  vt_ref[2 * j + 1, hd * DA_VROWS:(hd + 1) * DA_VROWS, :]], axis=1)
                 for hd in range(DA_HEADS)]
        return [heads[m // 2] for m in range(DA_MAPS)]

    assert tq == 2 * KSUB
    _attend(i + 1, logits_fn, values_fn, (sa_ref,), (sb_ref,), DA_VROWS, acc_ref, m_ref)

    lam = lam_ref[...]
    p01 = jnp.sum(lam[0:1] * lam[1:2], axis=-1, keepdims=True)
    p23 = jnp.sum(lam[2:3] * lam[3:4], axis=-1, keepdims=True)
    lam_val = jnp.exp(p01) - jnp.exp(p23) + lam_init
    def normalised(m):
        r0 = m * DA_VROWS
        return acc_ref[r0:r0 + DA_V_DIM, :] / acc_ref[r0 + DA_V_DIM:r0 + DA_V_DIM + 1, :]

    outs = []
    for hd in range(DA_HEADS):
        o = normalised(2 * hd) - lam_val * normalised(2 * hd + 1)
        ms = jnp.mean(o * o, axis=0, keepdims=True)
        outs.append(o * lax.rsqrt(ms + EPS) * og_ref[...] * (1.0 - lam_init))
    y_ref[...] = jnp.concatenate(outs, axis=0).T.astype(BF16)


def _diff_attn(dq, dk, dvt, lam, out_g_col, batch, seq, lam_init, *, tq=256):
    n = dq.shape[0]
    tq = min(tq, seq)
    nq = seq // tq
    nsub = seq // KSUB
    kern = functools.partial(_diff_kernel, tq=tq, lam_init=lam_init)
    return pl.pallas_call(
        kern,
        grid=(batch, nq),
        in_specs=[
            pl.BlockSpec((tq, 512), lambda b, i: (b * nq + i, 0)),
            pl.BlockSpec((seq, 512), lambda b, i: (b, 0)),
            pl.BlockSpec((nsub, DA_HEADS * DA_VROWS, KSUB), lambda b, i: (b, 0, 0)),
            pl.BlockSpec((4, ROPE_DIM), lambda b, i: (0, 0)),
            pl.BlockSpec((DA_V_DIM, 1), lambda b, i: (0, 0)),
        ],
        out_specs=pl.BlockSpec((tq, DA_WIDTH), lambda b, i: (b * nq + i, 0)),
        out_shape=jax.ShapeDtypeStruct((n, DA_WIDTH), BF16),
        scratch_shapes=[
            pltpu.VMEM((DA_MAPS, HEAD_GROUP, tq), BF16),
            pltpu.VMEM((DA_MAPS * DA_VROWS, tq), F32),
            pltpu.VMEM((DA_MAPS, tq), F32),
            pltpu.VMEM((DA_MAPS, 2 * KSUB, tq), F32),
            pltpu.VMEM((DA_MAPS, 2 * KSUB, tq), F32),
        ],
        compiler_params=_cparams(("parallel", "arbitrary")),
        name="diff_attn",
    )(dq, dk, dvt, lam, out_g_col)


def _merge_kernel(x_ref, ya_ref, yb_ref, yc_ref, gp_ref, bg_ref, wb_ref, wo_ref, ng_ref, xo_ref, xn_ref):
    merged = None
    for br, y_ref in enumerate((ya_ref, yb_ref, yc_ref)):
        sl = slice(br * D_MODEL, (br + 1) * D_MODEL)
        gate = _sigmoid(gp_ref[:, sl].astype(F32) + bg_ref[:, sl])
        term = gate * _dot(y_ref[...], wb_ref[br])
        merged = term if merged is None else merged + term
    xo = x_ref[...] + _dot(merged.astype(BF16), wo_ref[...])
    xo_ref[...] = xo
    ms = jnp.mean(xo * xo, axis=-1, keepdims=True)
    xn_ref[...] = (xo * lax.rsqrt(ms + EPS) * ng_ref[...]).astype(BF16)


def _merge(x2, ya, yb, yc, z, b_gate, w_branch, w_out, norm_g, *, tm=512):
    n = x2.shape[0]
    tm = min(tm, n)
    row = lambda w: pl.BlockSpec((tm, w), lambda i: (i, 0))
    return pl.pallas_call(
        _merge_kernel,
        grid=(n // tm,),
        in_specs=[
            row(D_MODEL), row(512), row(512), row(512),
            pl.BlockSpec((tm, N_BRANCH * D_MODEL), lambda i: (i, COL_GATE)),
            pl.BlockSpec((1, N_BRANCH * D_MODEL), lambda i: (0, 0)),
            pl.BlockSpec((N_BRANCH, 512, D_MODEL), lambda i: (0, 0, 0)),
            pl.BlockSpec((D_MODEL, D_MODEL), lambda i: (0, 0)),
            pl.BlockSpec((1, D_MODEL), lambda i: (0, 0)),
        ],
        out_specs=[row(D_MODEL), row(D_MODEL)],
        out_shape=[jax.ShapeDtypeStruct((n, D_MODEL), F32), jax.ShapeDtypeStruct((n, D_MODEL), BF16)],
        compiler_params=_cparams(("parallel",)),
        name="merge_out",
    )(x2, ya, yb, yc, z, b_gate, w_branch, w_out, norm_g)


def _ffn_kernel(xn_ref, x_ref, wg_ref, wu_ref, wd_ref, o_ref, *, fc):
    xn = xn_ref[...]
    acc = x_ref[...]
    for c in range(FF_DIM // fc):
        cols = slice(c * fc, (c + 1) * fc)
        g = _dot(xn, wg_ref[:, cols])
        u = _dot(xn, wu_ref[:, cols])
        acc = acc + _dot((g * _sigmoid(g) * u).astype(BF16), wd_ref[cols, :])
    o_ref[...] = acc


def _ffn(xn, x2, wg, wu, wd, *, tm=512, fc=256):
    n = x2.shape[0]
    tm = min(tm, n)
    resident = lambda shape: pl.BlockSpec(shape, lambda i: (0, 0), pipeline_mode=pl.Buffered(1))
    return pl.pallas_call(
        functools.partial(_ffn_kernel, fc=fc),
        grid=(n // tm,),
        in_specs=[
            pl.BlockSpec((tm, D_MODEL), lambda i: (i, 0)),
            pl.BlockSpec((tm, D_MODEL), lambda i: (i, 0)),
            resident((D_MODEL, FF_DIM)),
            resident((D_MODEL, FF_DIM)),
            resident((FF_DIM, D_MODEL)),
        ],
        out_specs=pl.BlockSpec((tm, D_MODEL), lambda i: (i, 0)),
        out_shape=jax.ShapeDtypeStruct((n, D_MODEL), F32),
        compiler_params=_cparams(("parallel",)),
        name="ffn",
    )(xn, x2, wg, wu, wd)


def _rope_tables(seq):
    inv = 1.0 / jnp.power(ROPE_THETA, jnp.arange(0, ROPE_DIM, 2, dtype=F32) / ROPE_DIM)
    ang = jnp.arange(seq, dtype=F32)[:, None] * inv[None, :]
    cos, sin = jnp.cos(ang), jnp.sin(ang)
    c64 = jnp.concatenate([cos, cos], axis=1)
    s64 = jnp.concatenate([-sin, sin], axis=1)
    return jnp.tile(c64, (1, 4)), jnp.tile(s64, (1, 4))


def _layer(x2, li, batch, seq, consts, norm_mix_g, w_in_l, b_gate, conv_w, conv_b, gate_b, m_norm_g,
           kv_norm_g, w_kv_up, sq_g, sk_g, dq_g, dk_g, lam, d_out_g, w_branch, w_out,
           norm_ffn_g, w_gate_up, w_down):
    rope_c, rope_s, grp, tri, shifts = consts
    lam_init = 0.8 - 0.6 * math.exp(-0.3 * li)
    tile8 = lambda g: jnp.tile(g, 8)[None, :]

    z, misc = _inproj(x2, norm_mix_g[None, :], w_in_l)

    gate_row = jnp.zeros((1, LANE), F32).at[0, MISC_IF:MISC_IF + 2 * M_HEADS].set(gate_b)
    ya = _mlstm(z, misc, conv_w, conv_b[None, :], gate_row, m_norm_g.reshape(1, M_WIDTH), tri, shifts,
                batch, seq)

    sq, sk, svt, iq, ik4, dq, dk, dvt = _prep(
        z, rope_c, rope_s, grp, kv_norm_g[None, :], w_kv_up.astype(BF16),
        tile8(sq_g), tile8(sk_g), tile8(dq_g), tile8(dk_g), seq)
    yb = _dsa(misc, sq, sk, svt, iq, ik4, batch, seq)
    yc = _diff_attn(dq, dk, dvt, lam, d_out_g[:, None], batch, seq, lam_init)

    xo, xn = _merge(x2, ya, yb, yc, z, b_gate[None, :], w_branch.astype(BF16), w_out.astype(BF16),
                    norm_ffn_g[None, :])
    return _ffn(xn, xo, w_gate_up[:, :FF_DIM].astype(BF16), w_gate_up[:, FF_DIM:].astype(BF16),
                w_down.astype(BF16))


def kernel(x, norm_mix_g, w_in, b_gate, mlstm_conv_w, mlstm_conv_b, mlstm_gate_b, mlstm_norm_g,
           dsa_kv_norm_g, dsa_w_kv_up, dsa_q_norm_g, dsa_k_norm_g, diff_q_norm_g, diff_k_norm_g,
           diff_lambda, diff_out_norm_g, w_branch, w_out, norm_ffn_g, w_gate_up, w_down):
    batch, seq, d = x.shape
    depth = w_in.shape[0]
    assert w_in.shape[2] == IN_COLS and d == D_MODEL
    rope_c, rope_s = _rope_tables(seq)
    gi = jnp.arange(512) // ROPE_DIM
    grp = (gi[:, None] == gi[None, :]).astype(BF16)
    ti = jnp.arange(M_CHUNK)
    tri = (ti[:, None] >= ti[None, :]).astype(BF16)
    ri = jnp.arange(M_TAIL + M_CHUNK)
    shifts = jnp.stack([(ri[None, :] == ti[:, None] + M_TAIL - j) for j in range(1, M_CONV)]).astype(BF16)
    consts = (rope_c, rope_s, grp, tri, shifts)
    w_in_l = _wlayout(w_in)
    x2 = x.reshape(batch * seq, d)
    for li in range(depth):
        x2 = _layer(x2, li, batch, seq, consts, norm_mix_g[li], w_in_l[li], b_gate[li], mlstm_conv_w[li],
                    mlstm_conv_b[li], mlstm_gate_b[li], mlstm_norm_g[li], dsa_kv_norm_g[li],
                    dsa_w_kv_up[li], dsa_q_norm_g[li], dsa_k_norm_g[li], diff_q_norm_g[li],
                    diff_k_norm_g[li], diff_lambda[li], diff_out_norm_g[li], w_branch[li], w_out[li],
                    norm_ffn_g[li], w_gate_up[li], w_down[li])
    return x2.reshape(batch, seq, d)
```

```python
import functools
import math

import jax
import jax.numpy as jnp
from jax import lax
from jax.experimental import pallas as pl
from jax.experimental.pallas import tpu as pltpu

F32 = jnp.float32
BF16 = jnp.bfloat16
I32 = jnp.int32

D_MODEL = 1024
EPS = 1e-6
ROPE_DIM = 64
ROPE_HALF = ROPE_DIM // 2
ROPE_THETA = 10000.0

M_HEADS = 4
M_HEAD_DIM = 128
M_WIDTH = M_HEADS * M_HEAD_DIM
M_CONV = 4
M_CHUNK = 128
M_TAIL = 16
M_STEP_CHUNKS = 4
M_INIT = -1e30

S_HEADS = 8
S_WIDTH = S_HEADS * ROPE_DIM
S_KV_RANK = 256
IDX_HEADS = 4
IDX_WIDTH = IDX_HEADS * ROPE_DIM
IDX_TOPK_MAX = 256

DA_HEADS = 4
DA_MAPS = 2 * DA_HEADS
DA_V_DIM = 2 * ROPE_DIM
DA_WIDTH = DA_HEADS * DA_V_DIM

N_BRANCH = 3
FF_DIM = 2816

NEG = -1e30
LOG2E = math.log2(math.e)
INT_MIN = -(2**31)
LANE = 128
HEAD_GROUP = 256
KSUB = 128
ONES_ROWS = 16
S_VROWS = ROPE_DIM + ONES_ROWS
DA_VROWS = DA_V_DIM + ONES_ROWS

W_IN_SIZES = (2 * M_WIDTH, M_WIDTH, M_WIDTH, 2 * M_HEADS, S_WIDTH, S_KV_RANK, IDX_WIDTH, ROPE_DIM,
              IDX_HEADS, 2 * DA_HEADS * ROPE_DIM, 2 * DA_HEADS * ROPE_DIM, DA_WIDTH, N_BRANCH * D_MODEL)
IN_COLS = sum(W_IN_SIZES)

COL_GATE = 0
COL_MQK = 3072
COL_MV = 4096
COL_MO = 4608
COL_SQ = 5120
COL_DQ = 5632
COL_DK = 6144
COL_DV = 6656
COL_CKV = 7168
COL_IQ = 7424
COL_IK4 = 7680
COL_MISC = 7936
MISC_IF = 0
MISC_IW = 8
Z_COLS = 8192

VMEM_LIMIT = 56 * 1024 * 1024


def _cparams(sem, flags=None):
    return pltpu.CompilerParams(dimension_semantics=sem, vmem_limit_bytes=VMEM_LIMIT, flags=flags)


def _sigmoid(x):
    return 1.0 / (1.0 + jnp.exp(-x))


def _dot(a, b):
    return jnp.dot(a, b, preferred_element_type=F32)


def _dot_nt(a, b):
    return lax.dot_general(a, b, (((1,), (1,)), ((), ())), preferred_element_type=F32)


def _dot_tn(a, b):
    return lax.dot_general(a, b, (((0,), (0,)), ((), ())), preferred_element_type=F32)


def _split3(x):
    hi = x.astype(BF16)
    r1 = x - hi.astype(F32)
    mid = r1.astype(BF16)
    lo = (r1 - mid.astype(F32)).astype(BF16)
    return hi, mid, lo


def _wlayout_kernel(w_ref, o_ref):
    w = w_ref[...]
    offs = [0]
    for s in W_IN_SIZES:
        offs.append(offs[-1] + s)
    seg = [w[:, offs[k]:offs[k + 1]] for k in range(len(W_IN_SIZES))]
    m_qk, m_v, m_o, m_if, s_q, s_ckv, i_q, i_k, i_w, d_q, d_k, d_v, g_pre = seg
    rows = w.shape[0]
    misc = jnp.concatenate([m_if, i_w, jnp.zeros((rows, LANE - 2 * M_HEADS - IDX_HEADS), F32)], axis=1)
    ik4 = jnp.concatenate([i_k] * IDX_HEADS, axis=1)
    tail = jnp.zeros((rows, Z_COLS - COL_MISC - LANE), F32)
    for col, val in ((COL_GATE, g_pre), (COL_MQK, m_qk), (COL_MV, m_v), (COL_MO, m_o), (COL_SQ, s_q),
                     (COL_DQ, d_q), (COL_DK, d_k), (COL_DV, d_v), (COL_CKV, s_ckv), (COL_IQ, i_q),
                     (COL_IK4, ik4), (COL_MISC, misc), (COL_MISC + LANE, tail)):
        o_ref[:, col:col + val.shape[1]] = val.astype(BF16)


def _wlayout(w_in, *, tr=256):
    depth, rows, cols = w_in.shape
    return pl.pallas_call(
        _wlayout_kernel,
        grid=(depth, rows // tr),
        in_specs=[pl.BlockSpec((None, tr, cols), lambda l, i: (l, i, 0))],
        out_specs=pl.BlockSpec((None, tr, Z_COLS), lambda l, i: (l, i, 0)),
        out_shape=jax.ShapeDtypeStruct((depth, rows, Z_COLS), BF16),
        compiler_params=_cparams(("parallel", "parallel")),
        name="w_layout",
    )(w_in)


def _inproj_kernel(x_ref, g_ref, w_ref, z_ref, misc_ref, *, tn):
    x = x_ref[...]
    ms = jnp.mean(x * x, axis=-1, keepdims=True)
    xn = (x * lax.rsqrt(ms + EPS) * g_ref[...]).astype(BF16)
    for c in range(Z_COLS // tn):
        r = _dot(xn, w_ref[:, c * tn:(c + 1) * tn])
        z_ref[:, c * tn:(c + 1) * tn] = r.astype(BF16)
        if c == COL_MISC // tn:
            misc_ref[...] = r[:, COL_MISC % tn:COL_MISC % tn + LANE]


def _inproj(x2, g, w, *, tm=512, tn=1024):
    n = x2.shape[0]
    tm = min(tm, n)
    return pl.pallas_call(
        functools.partial(_inproj_kernel, tn=tn),
        grid=(n // tm,),
        in_specs=[
            pl.BlockSpec((tm, D_MODEL), lambda i: (i, 0)),
            pl.BlockSpec((1, D_MODEL), lambda i: (0, 0)),
            pl.BlockSpec((D_MODEL, Z_COLS), lambda i: (0, 0), pipeline_mode=pl.Buffered(1)),
        ],
        out_specs=[pl.BlockSpec((tm, Z_COLS), lambda i: (i, 0)),
                   pl.BlockSpec((tm, LANE), lambda i: (i, 0))],
        out_shape=[jax.ShapeDtypeStruct((n, Z_COLS), BF16), jax.ShapeDtypeStruct((n, LANE), F32)],
        compiler_params=_cparams(("parallel",)),
        name="inproj",
    )(x2, g, w)


def _mlstm_kernel(qk_ref, v_ref, o_ref, misc_ref, cw_ref, cb_ref, gb_ref, ng_ref, tri_ref, sh_ref, y_ref,
                  tail_ref, qkc_ref, c_ref, n_ref, m_ref):
    @pl.when(pl.program_id(1) == 0)
    def _():
        tail_ref[...] = jnp.zeros_like(tail_ref)
        c_ref[...] = jnp.zeros_like(c_ref)
        n_ref[...] = jnp.zeros_like(n_ref)
        m_ref[...] = jnp.full_like(m_ref, M_INIT)

    for cc in range(qk_ref.shape[0] // M_CHUNK):
        rows = pl.ds(cc * M_CHUNK, M_CHUNK)
        _mlstm_chunk(qk_ref.at[rows], v_ref.at[rows], o_ref.at[rows], misc_ref.at[rows], cw_ref, cb_ref,
                     gb_ref, ng_ref, tri_ref, sh_ref, y_ref.at[rows], tail_ref, qkc_ref, c_ref, n_ref, m_ref)


def _mlstm_chunk(qk_ref, v_ref, o_ref, misc_ref, cw_ref, cb_ref, gb_ref, ng_ref, tri_ref, sh_ref, y_ref,
                 tail_ref, qkc_ref, c_ref, n_ref, m_ref):
    L = M_CHUNK
    dh = M_HEAD_DIM

    xb = qk_ref[...]
    ext = jnp.concatenate([tail_ref[...], xb], axis=0)
    cw = cw_ref[...]
    acc = xb.astype(F32) * cw[M_CONV - 1:M_CONV] + cb_ref[...]
    for j in range(1, M_CONV):
        acc = acc + _dot(sh_ref[j - 1], ext) * cw[M_CONV - 1 - j:M_CONV - j]
    qkc_ref[...] = acc * _sigmoid(acc)
    tail_ref[...] = xb[L - M_TAIL:L]

    gates = misc_ref[...] + gb_ref[...]
    lf = jnp.minimum(gates, 0.0) - jnp.log(1.0 + jnp.exp(-jnp.abs(gates)))
    tri = tri_ref[...]
    hi, mid, lo = _split3(lf)
    bcum = _dot(tri, hi) + _dot(tri, mid) + _dot(tri, lo)
    bcum_t = bcum.T
    gates_t = gates.T

    t_idx = lax.broadcasted_iota(I32, (L, L), 0)
    s_idx = lax.broadcasted_iota(I32, (L, L), 1)
    causal = t_idx >= s_idx

    ys, cs, ns, ms_new = [], [], [], []
    for h in range(M_HEADS):
        sl = slice(h * dh, (h + 1) * dh)
        q = qkc_ref[:, h * dh:(h + 1) * dh]
        k = qkc_ref[:, M_WIDTH + h * dh:M_WIDTH + (h + 1) * dh] * (dh ** -0.5)
        vb = v_ref[:, sl]
        v = vb.astype(F32)
        qb, kb = q.astype(BF16), k.astype(BF16)

        b_col = bcum[:, M_HEADS + h:M_HEADS + h + 1]
        i_col = gates[:, h:h + 1]
        b_row = bcum_t[M_HEADS + h:M_HEADS + h + 1, :]
        i_row = gates_t[h:h + 1, :]
        g_tot = bcum[L - 1:L, M_HEADS + h:M_HEADS + h + 1]

        c_prev = c_ref[sl, :]
        n_prev = n_ref[h:h + 1, :]
        m_prev = m_ref[h:h + 1, 0:1]

        rmat = jnp.where(causal, i_row - b_row, -jnp.inf)
        mx = jnp.maximum(m_prev, jnp.max(rmat, axis=-1, keepdims=True))
        m_t = b_col + mx
        sw = jnp.exp(rmat - mx) * _dot_nt(qb, kb)
        s_inter = jnp.exp(m_prev - mx)
        num = _dot(sw.astype(BF16), vb) + s_inter * _dot_nt(qb, c_prev.astype(BF16))
        den = jnp.sum(sw, axis=-1, keepdims=True) + s_inter * jnp.sum(q * n_prev, axis=-1, keepdims=True)
        dmax = jnp.maximum(jnp.abs(den), jnp.exp(-m_t))
        mnum = jnp.mean(num * num, axis=-1, keepdims=True)
        scale = lax.rsqrt(mnum + EPS * dmax * dmax)
        ys.append(_sigmoid(o_ref[:, sl].astype(F32)) * (num * scale * ng_ref[:, sl]))

        a_col = g_tot - b_col + i_col
        m_loc = jnp.max(a_col, axis=0, keepdims=True)
        w_loc = jnp.exp(a_col - m_loc)
        c_loc = _dot_tn((v * w_loc).astype(BF16), kb)
        n_loc = jnp.sum(k * w_loc, axis=0, keepdims=True)
        m_new = jnp.maximum(g_tot + m_prev, m_loc)
        s_old = jnp.exp(g_tot + m_prev - m_new)
        s_loc = jnp.exp(m_loc - m_new)
        cs.append(s_old * c_prev + s_loc * c_loc)
        ns.append(s_old * n_prev + s_loc * n_loc)
        ms_new.append(jnp.broadcast_to(m_new, (1, LANE)))

    y_ref[...] = jnp.concatenate(ys, axis=1).astype(BF16)
    c_ref[...] = jnp.concatenate(cs, axis=0)
    n_ref[...] = jnp.concatenate(ns, axis=0)
    m_ref[...] = jnp.concatenate(ms_new, axis=0)


def _mlstm(z, misc, conv_w, conv_b, gate_b_row, norm_g_row, tri, shifts, batch, seq):
    n = z.shape[0]
    L = M_CHUNK
    rows = M_STEP_CHUNKS * L
    nc = seq // rows
    row = lambda b, c: b * nc + c
    return pl.pallas_call(
        _mlstm_kernel,
        grid=(batch, nc),
        in_specs=[
            pl.BlockSpec((rows, 2 * M_WIDTH), lambda b, c: (row(b, c), COL_MQK // (2 * M_WIDTH))),
            pl.BlockSpec((rows, M_WIDTH), lambda b, c: (row(b, c), COL_MV // M_WIDTH)),
            pl.BlockSpec((rows, M_WIDTH), lambda b, c: (row(b, c), COL_MO // M_WIDTH)),
            pl.BlockSpec((rows, LANE), lambda b, c: (row(b, c), 0)),
            pl.BlockSpec((M_CONV, 2 * M_WIDTH), lambda b, c: (0, 0)),
            pl.BlockSpec((1, 2 * M_WIDTH), lambda b, c: (0, 0)),
            pl.BlockSpec((1, LANE), lambda b, c: (0, 0)),
            pl.BlockSpec((1, M_WIDTH), lambda b, c: (0, 0)),
            pl.BlockSpec((L, L), lambda b, c: (0, 0)),
            pl.BlockSpec((M_CONV - 1, L, M_TAIL + L), lambda b, c: (0, 0, 0)),
        ],
        out_specs=pl.BlockSpec((rows, M_WIDTH), lambda b, c: (row(b, c), 0)),
        out_shape=jax.ShapeDtypeStruct((n, M_WIDTH), BF16),
        scratch_shapes=[
            pltpu.VMEM((M_TAIL, 2 * M_WIDTH), BF16),
            pltpu.VMEM((L, 2 * M_WIDTH), F32),
            pltpu.VMEM((M_HEADS * M_HEAD_DIM, M_HEAD_DIM), F32),
            pltpu.VMEM((M_HEADS, M_HEAD_DIM), F32),
            pltpu.VMEM((M_HEADS, LANE), F32),
        ],
        compiler_params=_cparams(("parallel", "arbitrary")),
        name="mlstm",
    )(z, z, z, misc, conv_w, conv_b, gate_b_row, norm_g_row, tri, shifts)


def _rope(x, c, s):
    w = x.shape[-1]
    lane = lax.broadcasted_iota(I32, (1, w), 1)
    fwd = pltpu.roll(x, ROPE_HALF, axis=1)
    bwd = pltpu.roll(x, w - ROPE_HALF, axis=1)
    swapped = jnp.where((lane & (ROPE_DIM - 1)) < ROPE_HALF, bwd, fwd)
    return x * c + swapped * s


def _head_rms(x, g, grp):
    sq = x * x
    hi = sq.astype(BF16)
    lo = (sq - hi.astype(F32)).astype(BF16)
    ss = _dot(hi, grp) + _dot(lo, grp)
    return x * lax.rsqrt(ss * (1.0 / ROPE_DIM) + EPS) * g


def _store_transposed(o_ref, v, dv):
    ones = jnp.ones((ONES_ROWS, KSUB), BF16)
    for j in range(o_ref.shape[0]):
        vt = v[j * KSUB:(j + 1) * KSUB, :].T.astype(BF16)
        parts = []
        for h in range(vt.shape[0] // dv):
            parts += [vt[h * dv:(h + 1) * dv, :], ones]
        o_ref[j] = jnp.concatenate(parts, axis=0)


def _prep_kernel(sq_ref, dq_ref, dk_ref, dv_ref, ckv_ref, iq_ref, ik_ref, c_ref, s_ref, grp_ref,
                 kvg_ref, wkv_ref, sqg_ref, skg_ref, dqg_ref, dkg_ref,
                 sq_o, sk_o, sv_o, iq_o, ik_o, dq_o, dk_o, dv_o):
    c256 = c_ref[...]
    s256 = s_ref[...]
    c512 = jnp.concatenate([c256, c256], axis=1)
    s512 = jnp.concatenate([s256, s256], axis=1)
    grp = grp_ref[...]
    qscale = ROPE_DIM ** -0.5 * LOG2E

    f32 = lambda ref: ref[...].astype(F32)

    sq_o[...] = (_rope(_head_rms(f32(sq_ref), sqg_ref[...], grp), c512, s512) * qscale).astype(BF16)

    ckv = f32(ckv_ref)
    ms = jnp.mean(ckv * ckv, axis=-1, keepdims=True)
    ckvn = (ckv * lax.rsqrt(ms + EPS) * kvg_ref[...]).astype(BF16)
    kv = _dot(ckvn, wkv_ref[...])
    sk_o[...] = _rope(_head_rms(kv[:, :S_WIDTH], skg_ref[...], grp), c512, s512).astype(BF16)
    _store_transposed(sv_o, kv[:, S_WIDTH:], ROPE_DIM)

    iq_o[...] = _rope(f32(iq_ref), c256, s256).astype(BF16)
    ik_o[...] = _rope(f32(ik_ref), c256, s256).astype(BF16)

    dq_o[...] = (_rope(_head_rms(f32(dq_ref), dqg_ref[...], grp), c512, s512) * qscale).astype(BF16)
    dk_o[...] = _rope(_head_rms(f32(dk_ref), dkg_ref[...], grp), c512, s512).astype(BF16)
    _store_transposed(dv_o, f32(dv_ref), DA_V_DIM)


def _prep(z, rope_c, rope_s, grp, kv_g, w_kv, sq_g, sk_g, dq_g, dk_g, seq, *, tm=512):
    n = z.shape[0]
    tm = min(tm, seq)
    nt = seq // tm
    zspec = lambda w, col: pl.BlockSpec((tm, w), lambda i: (i, col // w))
    const = lambda shape: pl.BlockSpec(shape, lambda i: (0, 0))
    o512 = pl.BlockSpec((tm, 512), lambda i: (i, 0))
    o256 = pl.BlockSpec((tm, 256), lambda i: (i, 0))
    s512 = jax.ShapeDtypeStruct((n, 512), BF16)
    s256 = jax.ShapeDtypeStruct((n, 256), BF16)
    tspec = lambda rows: pl.BlockSpec((tm // KSUB, rows, KSUB), lambda i: (i, 0, 0))
    tshape = lambda rows: jax.ShapeDtypeStruct((n // KSUB, rows, KSUB), BF16)
    sv_rows, dv_rows = S_HEADS * S_VROWS, DA_HEADS * DA_VROWS
    return pl.pallas_call(
        _prep_kernel,
        grid=(n // tm,),
        in_specs=[
            zspec(512, COL_SQ), zspec(512, COL_DQ), zspec(512, COL_DK), zspec(512, COL_DV),
            zspec(256, COL_CKV), zspec(256, COL_IQ), zspec(256, COL_IK4),
            pl.BlockSpec((tm, 256), lambda i: (i % nt, 0)),
            pl.BlockSpec((tm, 256), lambda i: (i % nt, 0)),
            const((512, 512)),
            const((1, S_KV_RANK)), const((S_KV_RANK, 2 * S_WIDTH)),
            const((1, 512)), const((1, 512)), const((1, 512)), const((1, 512)),
        ],
        out_specs=[o512, o512, tspec(sv_rows), o256, o256, o512, o512, tspec(dv_rows)],
        out_shape=[s512, s512, tshape(sv_rows), s256, s256, s512, s512, tshape(dv_rows)],
        compiler_params=_cparams(("parallel",)),
        name="attn_prep",
    )(z, z, z, z, z, z, z, rope_c, rope_s, grp, kv_g, w_kv, sq_g, sk_g, dq_g, dk_g)


def _mask_heads(q_ref, qm_ref, n_heads):
    row_grp = lax.broadcasted_iota(I32, (HEAD_GROUP, 1), 0) // ROPE_DIM
    for g in range(n_heads // 4):
        qt = q_ref[:, g * HEAD_GROUP:(g + 1) * HEAD_GROUP].astype(F32).T
        for hh in range(4):
            qm_ref[4 * g + hh] = jnp.where(row_grp == hh, qt, 0.0).astype(BF16)


def _softmax_group(logits, values, rows, acc_ref, m_ref):
    nh = len(logits)
    m_old = m_ref[...]
    m_new = jnp.maximum(m_old, jnp.concatenate([jnp.max(s, axis=0, keepdims=True) for s in logits], axis=0))
    alpha = jnp.exp2(m_old - m_new)
    ps = [jnp.exp2((s - m_new[j:j + 1, :]).astype(BF16)) for j, s in enumerate(logits)]
    m_ref[...] = m_new
    acc_old = acc_ref[...]
    acc_ref[...] = jnp.concatenate(
        [alpha[j:j + 1, :] * acc_old[j * rows:(j + 1) * rows, :] + _dot(values[j], ps[j])
         for j in range(nh)], axis=0)


def _attend(n_groups, logits_fn, values_fn, bufs0, bufs1, rows, acc_ref, m_ref, side_fn=None):
    last = n_groups - 1
    group = len(bufs0)
    keys = bufs0[0].shape[1]

    def logits_stage(g, dst_refs):
        pen = jnp.where(g <= last, 0.0, NEG).astype(F32)
        for h, s in enumerate(logits_fn(jnp.minimum(g, last), pen)):
            for b in range(group):
                dst_refs[b][h] = s[b * keys:(b + 1) * keys, :]

    def softmax_stages(g, src_refs):
        for b in range(group):
            _softmax_group([src_refs[b][h] for h in range(src_refs[b].shape[0])],
                           values_fn(jnp.minimum(g, last) * group + b), rows, acc_ref, m_ref)

    logits_stage(0, bufs0)

    def body(gg, carry):
        logits_stage(2 * gg + 1, bufs1)
        softmax_stages(2 * gg, bufs0)
        logits_stage(2 * gg + 2, bufs0)
        softmax_stages(2 * gg + 1, bufs1)
        if side_fn is not None:
            side_fn(gg)
        return carry

    lax.fori_loop(0, (n_groups + 1) // 2, body, 0)


def _sortable(x):
    bits = pltpu.bitcast(x, I32)
    return bits ^ ((bits >> 31) & 0x7FFFFFFF)


def _bit_planes(words):
    a = list(words)
    j, m = 16, 0x0000FFFF
    while j:
        k = 0
        while k < 32:
            t = (a[k] ^ lax.shift_right_logical(a[k + j], jnp.int32(j))) & m
            a[k] = a[k] ^ t
            a[k + j] = a[k + j] ^ (t << j)
            k = (k + j + 1) & ~j
        j >>= 1
        m = (m ^ (m << j)) & 0xFFFFFFFF
    return a


def _dsa_kernel(sq_ref, iq_ref, misc_ref, iqn_ref, miscn_ref, sk_ref, svt_ref, ik_ref, y_ref,
                sc2_ref, bp2_ref, qm_ref, iqm_ref, acc_ref, m_ref, sa_ref, sb_ref, res_ref,
                *, tq, topk):
    i = pl.program_id(1)

    @pl.when((pl.program_id(0) == 0) & (i == 0))
    def _():
        bp2_ref[...] = jnp.zeros_like(bp2_ref)

    assert tq == 2 * KSUB
    nks = (i + 1) * (tq // KSUB)
    lane_q = lax.broadcasted_iota(I32, (1, tq), 1)
    k_pair = lax.broadcasted_iota(I32, (2 * KSUB, 1), 0)

    cur = lax.rem(i, 2)
    sc_ref, bp_ref = sc2_ref.at[cur], bp2_ref.at[cur]
    sc_next, bp_next = sc2_ref.at[1 - cur], bp2_ref.at[1 - cur]

    def head_weights(m_ref_):
        w_t = (m_ref_[...] * (IDX_WIDTH ** -0.5)).T
        return [w_t[MISC_IW + h:MISC_IW + h + 1, :] for h in range(IDX_HEADS)]

    def score_block(j, q_idx, w_rows, sc_dst, bp_dst, diagonal):
        ikb = ik_ref[pl.ds(pl.multiple_of(j * (2 * KSUB), 2 * KSUB), 2 * KSUB), :]
        s = jnp.zeros((2 * KSUB, tq), F32)
        for h in range(IDX_HEADS):
            s = s + w_rows[h] * jnp.maximum(_dot(ikb, iqm_ref[h]), 0.0)
        key = _sortable(s)
        if diagonal:
            key = jnp.where(j * (2 * KSUB) + k_pair <= q_idx, key, INT_MIN)
        sc_dst[pl.ds(2 * j, 2)] = key.reshape(2, KSUB, tq)
        ob = key ^ INT_MIN
        planes = _bit_planes([ob[v * 8:(v + 1) * 8, :] for v in range(32)])
        bp_dst[:, pl.ds(pl.multiple_of(j * 8, 8), 8), :] = jnp.concatenate(planes, axis=0).reshape(32, 8, tq)

    @pl.when(i == 0)
    def _():
        _mask_heads(iq_ref, iqm_ref, IDX_HEADS)
        score_block(0, lane_q, head_weights(misc_ref), sc_ref, bp_ref, True)

    _mask_heads(iqn_ref, iqm_ref, IDX_HEADS)
    w_next = head_weights(miscn_ref)
    q_idx_next = (i + 1) * tq + lane_q

    nblk = bp_ref.shape[1] // 8

    def search(rows):
        live_rows = lax.broadcasted_iota(I32, (rows, 1), 0) < (nks // 2) * 8

        def bit_body(t, carry):
            alive, need, thr_ob = carry
            plane = bp_ref[t, 0:rows, :]
            ones = alive & plane
            cnt = jnp.sum(lax.population_count(ones), axis=0, keepdims=True)
            take = cnt >= need
            alive = jnp.where(take, ones, alive ^ ones)
            need = jnp.where(take, need, need - cnt)
            thr_ob = thr_ob | jnp.where(take, jnp.left_shift(jnp.int32(1), 31 - t), 0)
            return alive, need, thr_ob

        alive, need, thr_ob = lax.fori_loop(
            0, 32, bit_body,
            (jnp.broadcast_to(jnp.where(live_rows, -1, 0), (rows, tq)),
             jnp.full((1, tq), topk, I32), jnp.zeros((1, tq), I32)))

        word_row = lax.broadcasted_iota(I32, (rows, 1), 0)
        word_base = (word_row >> 3) * (2 * KSUB) + (word_row & 7)
        idx_bits = (rows * 32 - 1).bit_length()

        def ties_below(trial):
            nv = jnp.clip((trial - word_base + 7) >> 3, 0, 32)
            low = lax.shift_right_logical(jnp.full_like(nv, -1), jnp.minimum(nv, 31))
            mask = jnp.where(nv >= 32, -1, ~low)
            return jnp.sum(lax.population_count(alive & mask), axis=0, keepdims=True)

        def idx_body(t, cut):
            trial = cut | jnp.left_shift(jnp.int32(1), idx_bits - 1 - t)
            return jnp.where(ties_below(trial) < need, trial, cut)

        res_ref[0:1, :] = thr_ob ^ INT_MIN
        res_ref[1:2, :] = lax.fori_loop(0, idx_bits, idx_body, jnp.zeros((1, tq), I32))

    if nblk % 2 == 0:
        @pl.when(nks // 2 <= nblk // 2)
        def _():
            search(nblk * 4)

        @pl.when(nks // 2 > nblk // 2)
        def _():
            search(nblk * 8)
    else:
        search(nblk * 8)
    thr = res_ref[0:1, :]
    cut = res_ref[1:2, :]

    thr_sel = jnp.maximum(thr, INT_MIN + 1)

    def bias_body(j, carry):
        pair = pl.ds(2 * j, 2)
        key = sc_ref[pair].reshape(2 * KSUB, tq)
        kidx = j * (2 * KSUB) + k_pair
        sel = key >= thr_sel + jnp.where(kidx <= cut, 0, 1)
        sc_ref[pair] = pltpu.bitcast(jnp.where(sel, 0.0, NEG).astype(F32), I32).reshape(2, KSUB, tq)
        return carry

    lax.fori_loop(0, nks // 2, bias_body, 0)

    _mask_heads(sq_ref, qm_ref, S_HEADS)
    acc_ref[...] = jnp.zeros_like(acc_ref)
    m_ref[...] = jnp.full_like(m_ref, NEG)

    def logits_fn(ks, pen):
        off = pl.multiple_of(ks * KSUB, KSUB)
        bias = pltpu.bitcast(sc_ref[ks], F32) + pen
        kgs = [sk_ref[pl.ds(off, KSUB), g * HEAD_GROUP:(g + 1) * HEAD_GROUP] for g in range(S_HEADS // 4)]
        return [_dot(kgs[h // 4], qm_ref[h]) + bias for h in range(S_HEADS)]

    def values_fn(ks):
        return [svt_ref[ks, h * S_VROWS:(h + 1) * S_VROWS, :] for h in range(S_HEADS)]

    def score_next(g):
        score_block(g, q_idx_next, w_next, sc_next, bp_next, False)

    _attend(nks, logits_fn, values_fn, (sa_ref,), (sb_ref,), S_VROWS, acc_ref, m_ref, side_fn=score_next)

    @pl.when(i + 1 < pl.num_programs(1))
    def _():
        score_block(i + 1, q_idx_next, w_next, sc_next, bp_next, True)

    outs = []
    for h in range(S_HEADS):
        r0 = h * S_VROWS
        outs.append(acc_ref[r0:r0 + ROPE_DIM, :] / acc_ref[r0 + ROPE_DIM:r0 + ROPE_DIM + 1, :])
    y_ref[...] = jnp.concatenate(outs, axis=0).T.astype(BF16)


def _dsa(misc, sq, sk, svt, iq, ik4, batch, seq, *, tq=256):
    n = misc.shape[0]
    tq = min(tq, seq)
    nq = seq // tq
    nsub = seq // KSUB
    topk = min(IDX_TOPK_MAX, seq // 4)
    kern = functools.partial(_dsa_kernel, tq=tq, topk=topk)
    nxt = lambda b, i: (b * nq + jnp.minimum(i + 1, nq - 1), 0)
    return pl.pallas_call(
        kern,
        grid=(batch, nq),
        in_specs=[
            pl.BlockSpec((tq, S_WIDTH), lambda b, i: (b * nq + i, 0)),
            pl.BlockSpec((tq, IDX_WIDTH), lambda b, i: (b * nq + i, 0)),
            pl.BlockSpec((tq, LANE), lambda b, i: (b * nq + i, 0)),
            pl.BlockSpec((tq, IDX_WIDTH), nxt),
            pl.BlockSpec((tq, LANE), nxt),
            pl.BlockSpec((seq, S_WIDTH), lambda b, i: (b, 0)),
            pl.BlockSpec((nsub, S_HEADS * S_VROWS, KSUB), lambda b, i: (b, 0, 0)),
            pl.BlockSpec((seq, IDX_WIDTH), lambda b, i: (b, 0)),
        ],
        out_specs=pl.BlockSpec((tq, S_WIDTH), lambda b, i: (b * nq + i, 0)),
        out_shape=jax.ShapeDtypeStruct((n, S_WIDTH), BF16),
        scratch_shapes=[
            pltpu.VMEM((2, nsub, KSUB, tq), I32),
            pltpu.VMEM((2, 32, (nsub // 2) * 8, tq), I32),
            pltpu.VMEM((S_HEADS, HEAD_GROUP, tq), BF16),
            pltpu.VMEM((IDX_HEADS, HEAD_GROUP, tq), BF16),
            pltpu.VMEM((S_HEADS * S_VROWS, tq), F32),
            pltpu.VMEM((S_HEADS, tq), F32),
            pltpu.VMEM((S_HEADS, KSUB, tq), F32),
            pltpu.VMEM((S_HEADS, KSUB, tq), F32),
            pltpu.VMEM((8, tq), I32),
        ],
        compiler_params=_cparams(("arbitrary", "arbitrary")),
        name="dsa_attn",
    )(sq, iq, misc, iq, misc, sk, svt, ik4)


def _diff_kernel(q_ref, k_ref, vt_ref, lam_ref, og_ref, y_ref, qm_ref, acc_ref, m_ref,
                 sa_ref, sb_ref, *, tq, lam_init):
    i = pl.program_id(1)
    q_idx = i * tq + lax.broadcasted_iota(I32, (1, tq), 1)
    k_pair = lax.broadcasted_iota(I32, (2 * KSUB, 1), 0)

    _mask_heads(q_ref, qm_ref, DA_MAPS)
    acc_ref[...] = jnp.zeros_like(acc_ref)
    m_ref[...] = jnp.full_like(m_ref, NEG)

    def logits_fn(j, pen):
        off = pl.multiple_of(j * (2 * KSUB), 2 * KSUB)
        bias = jnp.where(j * (2 * KSUB) + k_pair <= q_idx, 0.0, NEG).astype(F32) + pen
        kgs = [k_ref[pl.ds(off, 2 * KSUB), g * HEAD_GROUP:(g + 1) * HEAD_GROUP] for g in range(DA_MAPS // 4)]
        return [_dot(kgs[m // 4], qm_ref[m]) + bias for m in range(DA_MAPS)]

    def values_fn(j):
        heads = [jnp.concatenate([vt_ref[2 * j, hd * DA_VROWS:(hd + 1) * DA_VROWS, :],
                                  vt_ref[2 * j + 1, hd * DA_VROWS:(hd + 1) * DA_VROWS, :]], axis=1)
                 for hd in range(DA_HEADS)]
        return [heads[m // 2] for m in range(DA_MAPS)]

    assert tq == 2 * KSUB
    _attend(i + 1, logits_fn, values_fn, (sa_ref,), (sb_ref,), DA_VROWS, acc_ref, m_ref)

    lam = lam_ref[...]
    p01 = jnp.sum(lam[0:1] * lam[1:2], axis=-1, keepdims=True)
    p23 = jnp.sum(lam[2:3] * lam[3:4], axis=-1, keepdims=True)
    lam_val = jnp.exp(p01) - jnp.exp(p23) + lam_init
    def normalised(m):
        r0 = m * DA_VROWS
        return acc_ref[r0:r0 + DA_V_DIM, :] / acc_ref[r0 + DA_V_DIM:r0 + DA_V_DIM + 1, :]

    outs = []
    for hd in range(DA_HEADS):
        o = normalised(2 * hd) - lam_val * normalised(2 * hd + 1)
        ms = jnp.mean(o * o, axis=0, keepdims=True)
        outs.append(o * lax.rsqrt(ms + EPS) * og_ref[...] * (1.0 - lam_init))
    y_ref[...] = jnp.concatenate(outs, axis=0).T.astype(BF16)


def _diff_attn(dq, dk, dvt, lam, out_g_col, batch, seq, lam_init, *, tq=256):
    n = dq.shape[0]
    tq = min(tq, seq)
    nq = seq // tq
    nsub = seq // KSUB
    kern = functools.partial(_diff_kernel, tq=tq, lam_init=lam_init)
    return pl.pallas_call(
        kern,
        grid=(batch, nq),
        in_specs=[
            pl.BlockSpec((tq, 512), lambda b, i: (b * nq + i, 0)),
            pl.BlockSpec((seq, 512), lambda b, i: (b, 0)),
            pl.BlockSpec((nsub, DA_HEADS * DA_VROWS, KSUB), lambda b, i: (b, 0, 0)),
            pl.BlockSpec((4, ROPE_DIM), lambda b, i: (0, 0)),
            pl.BlockSpec((DA_V_DIM, 1), lambda b, i: (0, 0)),
        ],
        out_specs=pl.BlockSpec((tq, DA_WIDTH), lambda b, i: (b * nq + i, 0)),
        out_shape=jax.ShapeDtypeStruct((n, DA_WIDTH), BF16),
        scratch_shapes=[
            pltpu.VMEM((DA_MAPS, HEAD_GROUP, tq), BF16),
            pltpu.VMEM((DA_MAPS * DA_VROWS, tq), F32),
            pltpu.VMEM((DA_MAPS, tq), F32),
            pltpu.VMEM((DA_MAPS, 2 * KSUB, tq), F32),
            pltpu.VMEM((DA_MAPS, 2 * KSUB, tq), F32),
        ],
        compiler_params=_cparams(("parallel", "arbitrary")),
        name="diff_attn",
    )(dq, dk, dvt, lam, out_g_col)


def _merge_kernel(x_ref, ya_ref, yb_ref, yc_ref, gp_ref, bg_ref, wb_ref, wo_ref, ng_ref, xo_ref, xn_ref):
    merged = None
    for br, y_ref in enumerate((ya_ref, yb_ref, yc_ref)):
        sl = slice(br * D_MODEL, (br + 1) * D_MODEL)
        gate = _sigmoid(gp_ref[:, sl].astype(F32) + bg_ref[:, sl])
        term = gate * _dot(y_ref[...], wb_ref[br])
        merged = term if merged is None else merged + term
    xo = x_ref[...] + _dot(merged.astype(BF16), wo_ref[...])
    xo_ref[...] = xo
    ms = jnp.mean(xo * xo, axis=-1, keepdims=True)
    xn_ref[...] = (xo * lax.rsqrt(ms + EPS) * ng_ref[...]).astype(BF16)


def _merge(x2, ya, yb, yc, z, b_gate, w_branch, w_out, norm_g, *, tm=512):
    n = x2.shape[0]
    tm = min(tm, n)
    row = lambda w: pl.BlockSpec((tm, w), lambda i: (i, 0))
    return pl.pallas_call(
        _merge_kernel,
        grid=(n // tm,),
        in_specs=[
            row(D_MODEL), row(512), row(512), row(512),
            pl.BlockSpec((tm, N_BRANCH * D_MODEL), lambda i: (i, COL_GATE)),
            pl.BlockSpec((1, N_BRANCH * D_MODEL), lambda i: (0, 0)),
            pl.BlockSpec((N_BRANCH, 512, D_MODEL), lambda i: (0, 0, 0)),
            pl.BlockSpec((D_MODEL, D_MODEL), lambda i: (0, 0)),
            pl.BlockSpec((1, D_MODEL), lambda i: (0, 0)),
        ],
        out_specs=[row(D_MODEL), row(D_MODEL)],
        out_shape=[jax.ShapeDtypeStruct((n, D_MODEL), F32), jax.ShapeDtypeStruct((n, D_MODEL), BF16)],
        compiler_params=_cparams(("parallel",)),
        name="merge_out",
    )(x2, ya, yb, yc, z, b_gate, w_branch, w_out, norm_g)


def _ffn_kernel(xn_ref, x_ref, wg_ref, wu_ref, wd_ref, o_ref, *, fc):
    xn = xn_ref[...]
    acc = x_ref[...]
    for c in range(FF_DIM // fc):
        cols = slice(c * fc, (c + 1) * fc)
        g = _dot(xn, wg_ref[:, cols])
        u = _dot(xn, wu_ref[:, cols])
        acc = acc + _dot((g * _sigmoid(g) * u).astype(BF16), wd_ref[cols, :])
    o_ref[...] = acc


def _ffn(xn, x2, wg, wu, wd, *, tm=512, fc=256):
    n = x2.shape[0]
    tm = min(tm, n)
    resident = lambda shape: pl.BlockSpec(shape, lambda i: (0, 0), pipeline_mode=pl.Buffered(1))
    return pl.pallas_call(
        functools.partial(_ffn_kernel, fc=fc),
        grid=(n // tm,),
        in_specs=[
            pl.BlockSpec((tm, D_MODEL), lambda i: (i, 0)),
            pl.BlockSpec((tm, D_MODEL), lambda i: (i, 0)),
            resident((D_MODEL, FF_DIM)),
            resident((D_MODEL, FF_DIM)),
            resident((FF_DIM, D_MODEL)),
        ],
        out_specs=pl.BlockSpec((tm, D_MODEL), lambda i: (i, 0)),
        out_shape=jax.ShapeDtypeStruct((n, D_MODEL), F32),
        compiler_params=_cparams(("parallel",)),
        name="ffn",
    )(xn, x2, wg, wu, wd)


def _rope_tables(seq):
    inv = 1.0 / jnp.power(ROPE_THETA, jnp.arange(0, ROPE_DIM, 2, dtype=F32) / ROPE_DIM)
    ang = jnp.arange(seq, dtype=F32)[:, None] * inv[None, :]
    cos, sin = jnp.cos(ang), jnp.sin(ang)
    c64 = jnp.concatenate([cos, cos], axis=1)
    s64 = jnp.concatenate([-sin, sin], axis=1)
    return jnp.tile(c64, (1, 4)), jnp.tile(s64, (1, 4))


def _layer(x2, li, batch, seq, consts, norm_mix_g, w_in_l, b_gate, conv_w, conv_b, gate_b, m_norm_g,
           kv_norm_g, w_kv_up, sq_g, sk_g, dq_g, dk_g, lam, d_out_g, w_branch, w_out,
           norm_ffn_g, w_gate_up, w_down):
    rope_c, rope_s, grp, tri, shifts = consts
    lam_init = 0.8 - 0.6 * math.exp(-0.3 * li)
    tile8 = lambda g: jnp.tile(g, 8)[None, :]

    z, misc = _inproj(x2, norm_mix_g[None, :], w_in_l)

    gate_row = jnp.zeros((1, LANE), F32).at[0, MISC_IF:MISC_IF + 2 * M_HEADS].set(gate_b)
    ya = _mlstm(z, misc, conv_w, conv_b[None, :], gate_row, m_norm_g.reshape(1, M_WIDTH), tri, shifts,
                batch, seq)

    sq, sk, svt, iq, ik4, dq, dk, dvt = _prep(
        z, rope_c, rope_s, grp, kv_norm_g[None, :], w_kv_up.astype(BF16),
        tile8(sq_g), tile8(sk_g), tile8(dq_g), tile8(dk_g), seq)
    yb = _dsa(misc, sq, sk, svt, iq, ik4, batch, seq)
    yc = _diff_attn(dq, dk, dvt, lam, d_out_g[:, None], batch, seq, lam_init)

    xo, xn = _merge(x2, ya, yb, yc, z, b_gate[None, :], w_branch.astype(BF16), w_out.astype(BF16),
                    norm_ffn_g[None, :])
    return _ffn(xn, xo, w_gate_up[:, :FF_DIM].astype(BF16), w_gate_up[:, FF_DIM:].astype(BF16),
                w_down.astype(BF16))


def kernel(x, norm_mix_g, w_in, b_gate, mlstm_conv_w, mlstm_conv_b, mlstm_gate_b, mlstm_norm_g,
           dsa_kv_norm_g, dsa_w_kv_up, dsa_q_norm_g, dsa_k_norm_g, diff_q_norm_g, diff_k_norm_g,
           diff_lambda, diff_out_norm_g, w_branch, w_out, norm_ffn_g, w_gate_up, w_down):
    batch, seq, d = x.shape
    depth = w_in.shape[0]
    assert w_in.shape[2] == IN_COLS and d == D_MODEL
    rope_c, rope_s = _rope_tables(seq)
    gi = jnp.arange(512) // ROPE_DIM
    grp = (gi[:, None] == gi[None, :]).astype(BF16)
    ti = jnp.arange(M_CHUNK)
    tri = (ti[:, None] >= ti[None, :]).astype(BF16)
    ri = jnp.arange(M_TAIL + M_CHUNK)
    shifts = jnp.stack([(ri[None, :] == ti[:, None] + M_TAIL - j) for j in range(1, M_CONV)]).astype(BF16)
    consts = (rope_c, rope_s, grp, tri, shifts)
    w_in_l = _wlayout(w_in)
    x2 = x.reshape(batch * seq, d)
    for li in range(depth):
        x2 = _layer(x2, li, batch, seq, consts, norm_mix_g[li], w_in_l[li], b_gate[li], mlstm_conv_w[li],
                    mlstm_conv_b[li], mlstm_gate_b[li], mlstm_norm_g[li], dsa_kv_norm_g[li],
                    dsa_w_kv_up[li], dsa_q_norm_g[li], dsa_k_norm_g[li], diff_q_norm_g[li],
                    diff_k_norm_g[li], diff_lambda[li], diff_out_norm_g[li], w_branch[li], w_out[li],
                    norm_ffn_g[li], w_gate_up[li], w_down[li])
    return x2.reshape(batch, seq, d)
```

```python
import functools
import math

import jax
import jax.numpy as jnp
from jax import lax
from jax.experimental import pallas as pl
from jax.experimental.pallas import tpu as pltpu

F32 = jnp.float32
BF16 = jnp.bfloat16
I32 = jnp.int32

D_MODEL = 1024
EPS = 1e-6
ROPE_DIM = 64
ROPE_HALF = ROPE_DIM // 2
ROPE_THETA = 10000.0

M_HEADS = 4
M_HEAD_DIM = 128
M_WIDTH = M_HEADS * M_HEAD_DIM
M_CONV = 4
M_CHUNK = 128
M_TAIL = 16
M_STEP_CHUNKS = 4
M_INIT = -1e30

S_HEADS = 8
S_WIDTH = S_HEADS * ROPE_DIM
S_KV_RANK = 256
IDX_HEADS = 4
IDX_WIDTH = IDX_HEADS * ROPE_DIM
IDX_TOPK_MAX = 256

DA_HEADS = 4
DA_MAPS = 2 * DA_HEADS
DA_V_DIM = 2 * ROPE_DIM
DA_WIDTH = DA_HEADS * DA_V_DIM

N_BRANCH = 3
FF_DIM = 2816

NEG = -1e30
LOG2E = math.log2(math.e)
INT_MIN = -(2**31)
LANE = 128
HEAD_GROUP = 256
KSUB = 128
ONES_ROWS = 16
S_VROWS = ROPE_DIM + ONES_ROWS
DA_VROWS = DA_V_DIM + ONES_ROWS

W_IN_SIZES = (2 * M_WIDTH, M_WIDTH, M_WIDTH, 2 * M_HEADS, S_WIDTH, S_KV_RANK, IDX_WIDTH, ROPE_DIM,
              IDX_HEADS, 2 * DA_HEADS * ROPE_DIM, 2 * DA_HEADS * ROPE_DIM, DA_WIDTH, N_BRANCH * D_MODEL)
IN_COLS = sum(W_IN_SIZES)

COL_GATE = 0
COL_MQK = 3072
COL_MV = 4096
COL_MO = 4608
COL_SQ = 5120
COL_DQ = 5632
COL_DK = 6144
COL_DV = 6656
COL_CKV = 7168
COL_IQ = 7424
COL_IK4 = 7680
COL_MISC = 7936
MISC_IF = 0
MISC_IW = 8
Z_COLS = 8192

VMEM_LIMIT = 56 * 1024 * 1024


def _cparams(sem, flags=None):
    return pltpu.CompilerParams(dimension_semantics=sem, vmem_limit_bytes=VMEM_LIMIT, flags=flags)


def _sigmoid(x):
    return 1.0 / (1.0 + jnp.exp(-x))


def _dot(a, b):
    return jnp.dot(a, b, preferred_element_type=F32)


def _dot_nt(a, b):
    return lax.dot_general(a, b, (((1,), (1,)), ((), ())), preferred_element_type=F32)


def _dot_tn(a, b):
    return lax.dot_general(a, b, (((0,), (0,)), ((), ())), preferred_element_type=F32)


def _split3(x):
    hi = x.astype(BF16)
    r1 = x - hi.astype(F32)
    mid = r1.astype(BF16)
    lo = (r1 - mid.astype(F32)).astype(BF16)
    return hi, mid, lo


def _wlayout_kernel(w_ref, o_ref):
    w = w_ref[...]
    offs = [0]
    for s in W_IN_SIZES:
        offs.append(offs[-1] + s)
    seg = [w[:, offs[k]:offs[k + 1]] for k in range(len(W_IN_SIZES))]
    m_qk, m_v, m_o, m_if, s_q, s_ckv, i_q, i_k, i_w, d_q, d_k, d_v, g_pre = seg
    rows = w.shape[0]
    misc = jnp.concatenate([m_if, i_w, jnp.zeros((rows, LANE - 2 * M_HEADS - IDX_HEADS), F32)], axis=1)
    ik4 = jnp.concatenate([i_k] * IDX_HEADS, axis=1)
    tail = jnp.zeros((rows, Z_COLS - COL_MISC - LANE), F32)
    for col, val in ((COL_GATE, g_pre), (COL_MQK, m_qk), (COL_MV, m_v), (COL_MO, m_o), (COL_SQ, s_q),
                     (COL_DQ, d_q), (COL_DK, d_k), (COL_DV, d_v), (COL_CKV, s_ckv), (COL_IQ, i_q),
                     (COL_IK4, ik4), (COL_MISC, misc), (COL_MISC + LANE, tail)):
        o_ref[:, col:col + val.shape[1]] = val.astype(BF16)


def _wlayout(w_in, *, tr=256):
    depth, rows, cols = w_in.shape
    return pl.pallas_call(
        _wlayout_kernel,
        grid=(depth, rows // tr),
        in_specs=[pl.BlockSpec((None, tr, cols), lambda l, i: (l, i, 0))],
        out_specs=pl.BlockSpec((None, tr, Z_COLS), lambda l, i: (l, i, 0)),
        out_shape=jax.ShapeDtypeStruct((depth, rows, Z_COLS), BF16),
        compiler_params=_cparams(("parallel", "parallel")),
        name="w_layout",
    )(w_in)


def _inproj_kernel(x_ref, g_ref, w_ref, z_ref, misc_ref, *, tn):
    x = x_ref[...]
    ms = jnp.mean(x * x, axis=-1, keepdims=True)
    xn = (x * lax.rsqrt(ms + EPS) * g_ref[...]).astype(BF16)
    for c in range(Z_COLS // tn):
        r = _dot(xn, w_ref[:, c * tn:(c + 1) * tn])
        z_ref[:, c * tn:(c + 1) * tn] = r.astype(BF16)
        if c == COL_MISC // tn:
            misc_ref[...] = r[:, COL_MISC % tn:COL_MISC % tn + LANE]


def _inproj(x2, g, w, *, tm=512, tn=1024):
    n = x2.shape[0]
    tm = min(tm, n)
    return pl.pallas_call(
        functools.partial(_inproj_kernel, tn=tn),
        grid=(n // tm,),
        in_specs=[
            pl.BlockSpec((tm, D_MODEL), lambda i: (i, 0)),
            pl.BlockSpec((1, D_MODEL), lambda i: (0, 0)),
            pl.BlockSpec((D_MODEL, Z_COLS), lambda i: (0, 0), pipeline_mode=pl.Buffered(1)),
        ],
        out_specs=[pl.BlockSpec((tm, Z_COLS), lambda i: (i, 0)),
                   pl.BlockSpec((tm, LANE), lambda i: (i, 0))],
        out_shape=[jax.ShapeDtypeStruct((n, Z_COLS), BF16), jax.ShapeDtypeStruct((n, LANE), F32)],
        compiler_params=_cparams(("parallel",)),
        name="inproj",
    )(x2, g, w)


def _mlstm_kernel(qk_ref, v_ref, o_ref, misc_ref, cw_ref, cb_ref, gb_ref, ng_ref, tri_ref, sh_ref, y_ref,
                  tail_ref, qkc_ref, c_ref, n_ref, m_ref):
    @pl.when(pl.program_id(1) == 0)
    def _():
        tail_ref[...] = jnp.zeros_like(tail_ref)
        c_ref[...] = jnp.zeros_like(c_ref)
        n_ref[...] = jnp.zeros_like(n_ref)
        m_ref[...] = jnp.full_like(m_ref, M_INIT)

    for cc in range(qk_ref.shape[0] // M_CHUNK):
        rows = pl.ds(cc * M_CHUNK, M_CHUNK)
        _mlstm_chunk(qk_ref.at[rows], v_ref.at[rows], o_ref.at[rows], misc_ref.at[rows], cw_ref, cb_ref,
                     gb_ref, ng_ref, tri_ref, sh_ref, y_ref.at[rows], tail_ref, qkc_ref, c_ref, n_ref, m_ref)


def _mlstm_chunk(qk_ref, v_ref, o_ref, misc_ref, cw_ref, cb_ref, gb_ref, ng_ref, tri_ref, sh_ref, y_ref,
                 tail_ref, qkc_ref, c_ref, n_ref, m_ref):
    L = M_CHUNK
    dh = M_HEAD_DIM

    xb = qk_ref[...]
    ext = jnp.concatenate([tail_ref[...], xb], axis=0)
    cw = cw_ref[...]
    acc = xb.astype(F32) * cw[M_CONV - 1:M_CONV] + cb_ref[...]
    for j in range(1, M_CONV):
        acc = acc + _dot(sh_ref[j - 1], ext) * cw[M_CONV - 1 - j:M_CONV - j]
    qkc_ref[...] = acc * _sigmoid(acc)
    tail_ref[...] = xb[L - M_TAIL:L]

    gates = misc_ref[...] + gb_ref[...]
    lf = jnp.minimum(gates, 0.0) - jnp.log(1.0 + jnp.exp(-jnp.abs(gates)))
    tri = tri_ref[...]
    hi, mid, lo = _split3(lf)
    bcum = _dot(tri, hi) + _dot(tri, mid) + _dot(tri, lo)
    bcum_t = bcum.T
    gates_t = gates.T

    t_idx = lax.broadcasted_iota(I32, (L, L), 0)
    s_idx = lax.broadcasted_iota(I32, (L, L), 1)
    causal = t_idx >= s_idx

    ys, cs, ns, ms_new = [], [], [], []
    for h in range(M_HEADS):
        sl = slice(h * dh, (h + 1) * dh)
        q = qkc_ref[:, h * dh:(h + 1) * dh]
        k = qkc_ref[:, M_WIDTH + h * dh:M_WIDTH + (h + 1) * dh] * (dh ** -0.5)
        vb = v_ref[:, sl]
        v = vb.astype(F32)
        qb, kb = q.astype(BF16), k.astype(BF16)

        b_col = bcum[:, M_HEADS + h:M_HEADS + h + 1]
        i_col = gates[:, h:h + 1]
        b_row = bcum_t[M_HEADS + h:M_HEADS + h + 1, :]
        i_row = gates_t[h:h + 1, :]
        g_tot = bcum[L - 1:L, M_HEADS + h:M_HEADS + h + 1]

        c_prev = c_ref[sl, :]
        n_prev = n_ref[h:h + 1, :]
        m_prev = m_ref[h:h + 1, 0:1]

        rmat = jnp.where(causal, i_row - b_row, -jnp.inf)
        mx = jnp.maximum(m_prev, jnp.max(rmat, axis=-1, keepdims=True))
        m_t = b_col + mx
        sw = jnp.exp(rmat - mx) * _dot_nt(qb, kb)
        s_inter = jnp.exp(m_prev - mx)
        num = _dot(sw.astype(BF16), vb) + s_inter * _dot_nt(qb, c_prev.astype(BF16))
        den = jnp.sum(sw, axis=-1, keepdims=True) + s_inter * jnp.sum(q * n_prev, axis=-1, keepdims=True)
        dmax = jnp.maximum(jnp.abs(den), jnp.exp(-m_t))
        mnum = jnp.mean(num * num, axis=-1, keepdims=True)
        scale = lax.rsqrt(mnum + EPS * dmax * dmax)
        ys.append(_sigmoid(o_ref[:, sl].astype(F32)) * (num * scale * ng_ref[:, sl]))

        a_col = g_tot - b_col + i_col
        m_loc = jnp.max(a_col, axis=0, keepdims=True)
        w_loc = jnp.exp(a_col - m_loc)
        c_loc = _dot_tn((v * w_loc).astype(BF16), kb)
        n_loc = jnp.sum(k * w_loc, axis=0, keepdims=True)
        m_new = jnp.maximum(g_tot + m_prev, m_loc)
        s_old = jnp.exp(g_tot + m_prev - m_new)
        s_loc = jnp.exp(m_loc - m_new)
        cs.append(s_old * c_prev + s_loc * c_loc)
        ns.append(s_old * n_prev + s_loc * n_loc)
        ms_new.append(jnp.broadcast_to(m_new, (1, LANE)))

    y_ref[...] = jnp.concatenate(ys, axis=1).astype(BF16)
    c_ref[...] = jnp.concatenate(cs, axis=0)
    n_ref[...] = jnp.concatenate(ns, axis=0)
    m_ref[...] = jnp.concatenate(ms_new, axis=0)


def _mlstm(z, misc, conv_w, conv_b, gate_b_row, norm_g_row, tri, shifts, batch, seq):
    n = z.shape[0]
    L = M_CHUNK
    rows = M_STEP_CHUNKS * L
    nc = seq // rows
    row = lambda b, c: b * nc + c
    return pl.pallas_call(
        _mlstm_kernel,
        grid=(batch, nc),
        in_specs=[
            pl.BlockSpec((rows, 2 * M_WIDTH), lambda b, c: (row(b, c), COL_MQK // (2 * M_WIDTH))),
            pl.BlockSpec((rows, M_WIDTH), lambda b, c: (row(b, c), COL_MV // M_WIDTH)),
            pl.BlockSpec((rows, M_WIDTH), lambda b, c: (row(b, c), COL_MO // M_WIDTH)),
            pl.BlockSpec((rows, LANE), lambda b, c: (row(b, c), 0)),
            pl.BlockSpec((M_CONV, 2 * M_WIDTH), lambda b, c: (0, 0)),
            pl.BlockSpec((1, 2 * M_WIDTH), lambda b, c: (0, 0)),
            pl.BlockSpec((1, LANE), lambda b, c: (0, 0)),
            pl.BlockSpec((1, M_WIDTH), lambda b, c: (0, 0)),
            pl.BlockSpec((L, L), lambda b, c: (0, 0)),
            pl.BlockSpec((M_CONV - 1, L, M_TAIL + L), lambda b, c: (0, 0, 0)),
        ],
        out_specs=pl.BlockSpec((rows, M_WIDTH), lambda b, c: (row(b, c), 0)),
        out_shape=jax.ShapeDtypeStruct((n, M_WIDTH), BF16),
        scratch_shapes=[
            pltpu.VMEM((M_TAIL, 2 * M_WIDTH), BF16),
            pltpu.VMEM((L, 2 * M_WIDTH), F32),
            pltpu.VMEM((M_HEADS * M_HEAD_DIM, M_HEAD_DIM), F32),
            pltpu.VMEM((M_HEADS, M_HEAD_DIM), F32),
            pltpu.VMEM((M_HEADS, LANE), F32),
        ],
        compiler_params=_cparams(("parallel", "arbitrary")),
        name="mlstm",
    )(z, z, z, misc, conv_w, conv_b, gate_b_row, norm_g_row, tri, shifts)


def _rope(x, c, s):
    w = x.shape[-1]
    lane = lax.broadcasted_iota(I32, (1, w), 1)
    fwd = pltpu.roll(x, ROPE_HALF, axis=1)
    bwd = pltpu.roll(x, w - ROPE_HALF, axis=1)
    swapped = jnp.where((lane & (ROPE_DIM - 1)) < ROPE_HALF, bwd, fwd)
    return x * c + swapped * s


def _head_rms(x, g, grp):
    sq = x * x
    hi = sq.astype(BF16)
    lo = (sq - hi.astype(F32)).astype(BF16)
    ss = _dot(hi, grp) + _dot(lo, grp)
    return x * lax.rsqrt(ss * (1.0 / ROPE_DIM) + EPS) * g


def _store_transposed(o_ref, v, dv):
    ones = jnp.ones((ONES_ROWS, KSUB), BF16)
    for j in range(o_ref.shape[0]):
        vt = v[j * KSUB:(j + 1) * KSUB, :].T.astype(BF16)
        parts = []
        for h in range(vt.shape[0] // dv):
            parts += [vt[h * dv:(h + 1) * dv, :], ones]
        o_ref[j] = jnp.concatenate(parts, axis=0)


def _prep_kernel(sq_ref, dq_ref, dk_ref, dv_ref, ckv_ref, iq_ref, ik_ref, c_ref, s_ref, grp_ref,
                 kvg_ref, wkv_ref, sqg_ref, skg_ref, dqg_ref, dkg_ref,
                 sq_o, sk_o, sv_o, iq_o, ik_o, dq_o, dk_o, dv_o):
    c256 = c_ref[...]
    s256 = s_ref[...]
    c512 = jnp.concatenate([c256, c256], axis=1)
    s512 = jnp.concatenate([s256, s256], axis=1)
    grp = grp_ref[...]
    qscale = ROPE_DIM ** -0.5 * LOG2E

    f32 = lambda ref: ref[...].astype(F32)

    sq_o[...] = (_rope(_head_rms(f32(sq_ref), sqg_ref[...], grp), c512, s512) * qscale).astype(BF16)

    ckv = f32(ckv_ref)
    ms = jnp.mean(ckv * ckv, axis=-1, keepdims=True)
    ckvn = (ckv * lax.rsqrt(ms + EPS) * kvg_ref[...]).astype(BF16)
    kv = _dot(ckvn, wkv_ref[...])
    sk_o[...] = _rope(_head_rms(kv[:, :S_WIDTH], skg_ref[...], grp), c512, s512).astype(BF16)
    _store_transposed(sv_o, kv[:, S_WIDTH:], ROPE_DIM)

    iq_o[...] = _rope(f32(iq_ref), c256, s256).astype(BF16)
    ik_o[...] = _rope(f32(ik_ref), c256, s256).astype(BF16)

    dq_o[...] = (_rope(_head_rms(f32(dq_ref), dqg_ref[...], grp), c512, s512) * qscale).astype(BF16)
    dk_o[...] = _rope(_head_rms(f32(dk_ref), dkg_ref[...], grp), c512, s512).astype(BF16)
    _store_transposed(dv_o, f32(dv_ref), DA_V_DIM)


def _prep(z, rope_c, rope_s, grp, kv_g, w_kv, sq_g, sk_g, dq_g, dk_g, seq, *, tm=512):
    n = z.shape[0]
    tm = min(tm, seq)
    nt = seq // tm
    zspec = lambda w, col: pl.BlockSpec((tm, w), lambda i: (i, col // w))
    const = lambda shape: pl.BlockSpec(shape, lambda i: (0, 0))
    o512 = pl.BlockSpec((tm, 512), lambda i: (i, 0))
    o256 = pl.BlockSpec((tm, 256), lambda i: (i, 0))
    s512 = jax.ShapeDtypeStruct((n, 512), BF16)
    s256 = jax.ShapeDtypeStruct((n, 256), BF16)
    tspec = lambda rows: pl.BlockSpec((tm // KSUB, rows, KSUB), lambda i: (i, 0, 0))
    tshape = lambda rows: jax.ShapeDtypeStruct((n // KSUB, rows, KSUB), BF16)
    sv_rows, dv_rows = S_HEADS * S_VROWS, DA_HEADS * DA_VROWS
    return pl.pallas_call(
        _prep_kernel,
        grid=(n // tm,),
        in_specs=[
            zspec(512, COL_SQ), zspec(512, COL_DQ), zspec(512, COL_DK), zspec(512, COL_DV),
            zspec(256, COL_CKV), zspec(256, COL_IQ), zspec(256, COL_IK4),
            pl.BlockSpec((tm, 256), lambda i: (i % nt, 0)),
            pl.BlockSpec((tm, 256), lambda i: (i % nt, 0)),
            const((512, 512)),
            const((1, S_KV_RANK)), const((S_KV_RANK, 2 * S_WIDTH)),
            const((1, 512)), const((1, 512)), const((1, 512)), const((1, 512)),
        ],
        out_specs=[o512, o512, tspec(sv_rows), o256, o256, o512, o512, tspec(dv_rows)],
        out_shape=[s512, s512, tshape(sv_rows), s256, s256, s512, s512, tshape(dv_rows)],
        compiler_params=_cparams(("parallel",)),
        name="attn_prep",
    )(z, z, z, z, z, z, z, rope_c, rope_s, grp, kv_g, w_kv, sq_g, sk_g, dq_g, dk_g)


def _mask_heads(q_ref, qm_ref, n_heads):
    row_grp = lax.broadcasted_iota(I32, (HEAD_GROUP, 1), 0) // ROPE_DIM
    for g in range(n_heads // 4):
        qt = q_ref[:, g * HEAD_GROUP:(g + 1) * HEAD_GROUP].astype(F32).T
        for hh in range(4):
            qm_ref[4 * g + hh] = jnp.where(row_grp == hh, qt, 0.0).astype(BF16)


def _softmax_group(logits, values, rows, acc_ref, m_ref):
    nh = len(logits)
    m_old = m_ref[...]
    m_new = jnp.maximum(m_old, jnp.concatenate([jnp.max(s, axis=0, keepdims=True) for s in logits], axis=0))
    alpha = jnp.exp2(m_old - m_new)
    ps = [jnp.exp2((s - m_new[j:j + 1, :]).astype(BF16)) for j, s in enumerate(logits)]
    m_ref[...] = m_new
    acc_old = acc_ref[...]
    acc_ref[...] = jnp.concatenate(
        [alpha[j:j + 1, :] * acc_old[j * rows:(j + 1) * rows, :] + _dot(values[j], ps[j])
         for j in range(nh)], axis=0)


def _attend(n_groups, logits_fn, values_fn, bufs0, bufs1, rows, acc_ref, m_ref, side_fn=None):
    last = n_groups - 1
    group = len(bufs0)
    keys = bufs0[0].shape[1]

    def logits_stage(g, dst_refs):
        pen = jnp.where(g <= last, 0.0, NEG).astype(F32)
        for h, s in enumerate(logits_fn(jnp.minimum(g, last), pen)):
            for b in range(group):
                dst_refs[b][h] = s[b * keys:(b + 1) * keys, :]

    def softmax_stages(g, src_refs):
        for b in range(group):
            _softmax_group([src_refs[b][h] for h in range(src_refs[b].shape[0])],
                           values_fn(jnp.minimum(g, last) * group + b), rows, acc_ref, m_ref)

    logits_stage(0, bufs0)

    def body(gg, carry):
        logits_stage(2 * gg + 1, bufs1)
        softmax_stages(2 * gg, bufs0)
        logits_stage(2 * gg + 2, bufs0)
        softmax_stages(2 * gg + 1, bufs1)
        if side_fn is not None:
            side_fn(gg)
        return carry

    lax.fori_loop(0, (n_groups + 1) // 2, body, 0)


def _sortable(x):
    bits = pltpu.bitcast(x, I32)
    return bits ^ ((bits >> 31) & 0x7FFFFFFF)


def _bit_planes(words):
    a = list(words)
    j, m = 16, 0x0000FFFF
    while j:
        k = 0
        while k < 32:
            t = (a[k] ^ lax.shift_right_logical(a[k + j], jnp.int32(j))) & m
            a[k] = a[k] ^ t
            a[k + j] = a[k + j] ^ (t << j)
            k = (k + j + 1) & ~j
        j >>= 1
        m = (m ^ (m << j)) & 0xFFFFFFFF
    return a


def _dsa_kernel(sq_ref, iq_ref, misc_ref, iqn_ref, miscn_ref, sk_ref, svt_ref, ik_ref, y_ref,
                sc2_ref, bp2_ref, qm_ref, iqm_ref, acc_ref, m_ref, sa_ref, sb_ref, res_ref,
                *, tq, topk):
    i = pl.program_id(1)

    @pl.when((pl.program_id(0) == 0) & (i == 0))
    def _():
        bp2_ref[...] = jnp.zeros_like(bp2_ref)

    assert tq == 2 * KSUB
    nks = (i + 1) * (tq // KSUB)
    lane_q = lax.broadcasted_iota(I32, (1, tq), 1)
    k_pair = lax.broadcasted_iota(I32, (2 * KSUB, 1), 0)

    cur = lax.rem(i, 2)
    sc_ref, bp_ref = sc2_ref.at[cur], bp2_ref.at[cur]
    sc_next, bp_next = sc2_ref.at[1 - cur], bp2_ref.at[1 - cur]

    def head_weights(m_ref_):
        w_t = (m_ref_[...] * (IDX_WIDTH ** -0.5)).T
        return [w_t[MISC_IW + h:MISC_IW + h + 1, :] for h in range(IDX_HEADS)]

    def score_block(j, q_idx, w_rows, sc_dst, bp_dst, diagonal):
        ikb = ik_ref[pl.ds(pl.multiple_of(j * (2 * KSUB), 2 * KSUB), 2 * KSUB), :]
        s = jnp.zeros((2 * KSUB, tq), F32)
        for h in range(IDX_HEADS):
            s = s + w_rows[h] * jnp.maximum(_dot(ikb, iqm_ref[h]), 0.0)
        key = _sortable(s)
        if diagonal:
            key = jnp.where(j * (2 * KSUB) + k_pair <= q_idx, key, INT_MIN)
        sc_dst[pl.ds(2 * j, 2)] = key.reshape(2, KSUB, tq)
        ob = key ^ INT_MIN
        planes = _bit_planes([ob[v * 8:(v + 1) * 8, :] for v in range(32)])
        bp_dst[:, pl.ds(pl.multiple_of(j * 8, 8), 8), :] = jnp.concatenate(planes, axis=0).reshape(32, 8, tq)

    @pl.when(i == 0)
    def _():
        _mask_heads(iq_ref, iqm_ref, IDX_HEADS)
        score_block(0, lane_q, head_weights(misc_ref), sc_ref, bp_ref, True)

    _mask_heads(iqn_ref, iqm_ref, IDX_HEADS)
    w_next = head_weights(miscn_ref)
    q_idx_next = (i + 1) * tq + lane_q

    nblk = bp_ref.shape[1] // 8

    def search(rows):
        live_rows = lax.broadcasted_iota(I32, (rows, 1), 0) < (nks // 2) * 8

        def bit_body(t, carry):
            alive, need, thr_ob = carry
            plane = bp_ref[t, 0:rows, :]
            ones = alive & plane
            cnt = jnp.sum(lax.population_count(ones), axis=0, keepdims=True)
            take = cnt >= need
            alive = jnp.where(take, ones, alive ^ ones)
            need = jnp.where(take, need, need - cnt)
            thr_ob = thr_ob | jnp.where(take, jnp.left_shift(jnp.int32(1), 31 - t), 0)
            return alive, need, thr_ob

        alive, need, thr_ob = lax.fori_loop(
            0, 32, bit_body,
            (jnp.broadcast_to(jnp.where(live_rows, -1, 0), (rows, tq)),
             jnp.full((1, tq), topk, I32), jnp.zeros((1, tq), I32)))

        word_row = lax.broadcasted_iota(I32, (rows, 1), 0)
        word_base = (word_row >> 3) * (2 * KSUB) + (word_row & 7)
        idx_bits = (rows * 32 - 1).bit_length()

        def ties_below(trial):
            nv = jnp.clip((trial - word_base + 7) >> 3, 0, 32)
            low = lax.shift_right_logical(jnp.full_like(nv, -1), jnp.minimum(nv, 31))
            mask = jnp.where(nv >= 32, -1, ~low)
            return jnp.sum(lax.population_count(alive & mask), axis=0, keepdims=True)

        def idx_body(t, cut):
            trial = cut | jnp.left_shift(jnp.int32(1), idx_bits - 1 - t)
            return jnp.where(ties_below(trial) < need, trial, cut)

        res_ref[0:1, :] = thr_ob ^ INT_MIN
        res_ref[1:2, :] = lax.fori_loop(0, idx_bits, idx_body, jnp.zeros((1, tq), I32))

    if nblk % 2 == 0:
        @pl.when(nks // 2 <= nblk // 2)
        def _():
            search(nblk * 4)

        @pl.when(nks // 2 > nblk // 2)
        def _():
            search(nblk * 8)
    else:
        search(nblk * 8)
    thr = res_ref[0:1, :]
    cut = res_ref[1:2, :]

    thr_sel = jnp.maximum(thr, INT_MIN + 1)

    def bias_body(j, carry):
        pair = pl.ds(2 * j, 2)
        key = sc_ref[pair].reshape(2 * KSUB, tq)
        kidx = j * (2 * KSUB) + k_pair
        sel = key >= thr_sel + jnp.where(kidx <= cut, 0, 1)
        sc_ref[pair] = pltpu.bitcast(jnp.where(sel, 0.0, NEG).astype(F32), I32).reshape(2, KSUB, tq)
        return carry

    lax.fori_loop(0, nks // 2, bias_body, 0)

    _mask_heads(sq_ref, qm_ref, S_HEADS)
    acc_ref[...] = jnp.zeros_like(acc_ref)
    m_ref[...] = jnp.full_like(m_ref, NEG)

    def logits_fn(ks, pen):
        off = pl.multiple_of(ks * KSUB, KSUB)
        bias = pltpu.bitcast(sc_ref[ks], F32) + pen
        kgs = [sk_ref[pl.ds(off, KSUB), g * HEAD_GROUP:(g + 1) * HEAD_GROUP] for g in range(S_HEADS // 4)]
        return [_dot(kgs[h // 4], qm_ref[h]) + bias for h in range(S_HEADS)]

    def values_fn(ks):
        return [svt_ref[ks, h * S_VROWS:(h + 1) * S_VROWS, :] for h in range(S_HEADS)]

    def score_next(g):
        score_block(g, q_idx_next, w_next, sc_next, bp_next, False)

    _attend(nks, logits_fn, values_fn, (sa_ref,), (sb_ref,), S_VROWS, acc_ref, m_ref, side_fn=score_next)

    @pl.when(i + 1 < pl.num_programs(1))
    def _():
        score_block(i + 1, q_idx_next, w_next, sc_next, bp_next, True)

    outs = []
    for h in range(S_HEADS):
        r0 = h * S_VROWS
        outs.append(acc_ref[r0:r0 + ROPE_DIM, :] / acc_ref[r0 + ROPE_DIM:r0 + ROPE_DIM + 1, :])
    y_ref[...] = jnp.concatenate(outs, axis=0).T.astype(BF16)


def _dsa(misc, sq, sk, svt, iq, ik4, batch, seq, *, tq=256):
    n = misc.shape[0]
    tq = min(tq, seq)
    nq = seq // tq
    nsub = seq // KSUB
    topk = min(IDX_TOPK_MAX, seq // 4)
    kern = functools.partial(_dsa_kernel, tq=tq, topk=topk)
    nxt = lambda b, i: (b * nq + jnp.minimum(i + 1, nq - 1), 0)
    return pl.pallas_call(
        kern,
        grid=(batch, nq),
        in_specs=[
            pl.BlockSpec((tq, S_WIDTH), lambda b, i: (b * nq + i, 0)),
            pl.BlockSpec((tq, IDX_WIDTH), lambda b, i: (b * nq + i, 0)),
            pl.BlockSpec((tq, LANE), lambda b, i: (b * nq + i, 0)),
            pl.BlockSpec((tq, IDX_WIDTH), nxt),
            pl.BlockSpec((tq, LANE), nxt),
            pl.BlockSpec((seq, S_WIDTH), lambda b, i: (b, 0)),
            pl.BlockSpec((nsub, S_HEADS * S_VROWS, KSUB), lambda b, i: (b, 0, 0)),
            pl.BlockSpec((seq, IDX_WIDTH), lambda b, i: (b, 0)),
        ],
        out_specs=pl.BlockSpec((tq, S_WIDTH), lambda b, i: (b * nq + i, 0)),
        out_shape=jax.ShapeDtypeStruct((n, S_WIDTH), BF16),
        scratch_shapes=[
            pltpu.VMEM((2, nsub, KSUB, tq), I32),
            pltpu.VMEM((2, 32, (nsub // 2) * 8, tq), I32),
            pltpu.VMEM((S_HEADS, HEAD_GROUP, tq), BF16),
            pltpu.VMEM((IDX_HEADS, HEAD_GROUP, tq), BF16),
            pltpu.VMEM((S_HEADS * S_VROWS, tq), F32),
            pltpu.VMEM((S_HEADS, tq), F32),
            pltpu.VMEM((S_HEADS, KSUB, tq), F32),
            pltpu.VMEM((S_HEADS, KSUB, tq), F32),
            pltpu.VMEM((8, tq), I32),
        ],
        compiler_params=_cparams(("arbitrary", "arbitrary")),
        name="dsa_attn",
    )(sq, iq, misc, iq, misc, sk, svt, ik4)


def _diff_kernel(q_ref, k_ref, vt_ref, lam_ref, og_ref, y_ref, qm_ref, acc_ref, m_ref,
                 sa_ref, sb_ref, *, tq, lam_init):
    i = pl.program_id(1)
    q_idx = i * tq + lax.broadcasted_iota(I32, (1, tq), 1)
    k_pair = lax.broadcasted_iota(I32, (2 * KSUB, 1), 0)

    _mask_heads(q_ref, qm_ref, DA_MAPS)
    acc_ref[...] = jnp.zeros_like(acc_ref)
    m_ref[...] = jnp.full_like(m_ref, NEG)

    def logits_fn(j, pen):
        off = pl.multiple_of(j * (2 * KSUB), 2 * KSUB)
        bias = jnp.where(j * (2 * KSUB) + k_pair <= q_idx, 0.0, NEG).astype(F32) + pen
        kgs = [k_ref[pl.ds(off, 2 * KSUB), g * HEAD_GROUP:(g + 1) * HEAD_GROUP] for g in range(DA_MAPS // 4)]
        return [_dot(kgs[m // 4], qm_ref[m]) + bias for m in range(DA_MAPS)]

    def values_fn(j):
        heads = [jnp.concatenate([vt_ref[2 * j, hd * DA_VROWS:(hd + 1) * DA_VROWS, :],
                                  vt_ref[2 * j + 1, hd * DA_VROWS:(hd + 1) * DA_VROWS, :]], axis=1)
                 for hd in range(DA_HEADS)]
        return [heads[m // 2] for m in range(DA_MAPS)]

    assert tq == 2 * KSUB
    _attend(i + 1, logits_fn, values_fn, (sa_ref,), (sb_ref,), DA_VROWS, acc_ref, m_ref)

    lam = lam_ref[...]
    p01 = jnp.sum(lam[0:1] * lam[1:2], axis=-1, keepdims=True)
    p23 = jnp.sum(lam[2:3] * lam[3:4], axis=-1, keepdims=True)
    lam_val = jnp.exp(p01) - jnp.exp(p23) + lam_init
    def normalised(m):
        r0 = m * DA_VROWS
        return acc_ref[r0:r0 + DA_V_DIM, :] / acc_ref[r0 + DA_V_DIM:r0 + DA_V_DIM + 1, :]

    outs = []
    for hd in range(DA_HEADS):
        o = normalised(2 * hd) - lam_val * normalised(2 * hd + 1)
        ms = jnp.mean(o * o, axis=0, keepdims=True)
        outs.append(o * lax.rsqrt(ms + EPS) * og_ref[...] * (1.0 - lam_init))
    y_ref[...] = jnp.concatenate(outs, axis=0).T.astype(BF16)


def _diff_attn(dq, dk, dvt, lam, out_g_col, batch, seq, lam_init, *, tq=256):
    n = dq.shape[0]
    tq = min(tq, seq)
    nq = seq // tq
    nsub = seq // KSUB
    kern = functools.partial(_diff_kernel, tq=tq, lam_init=lam_init)
    return pl.pallas_call(
        kern,
        grid=(batch, nq),
        in_specs=[
            pl.BlockSpec((tq, 512), lambda b, i: (b * nq + i, 0)),
            pl.BlockSpec((seq, 512), lambda b, i: (b, 0)),
            pl.BlockSpec((nsub, DA_HEADS * DA_VROWS, KSUB), lambda b, i: (b, 0, 0)),
            pl.BlockSpec((4, ROPE_DIM), lambda b, i: (0, 0)),
            pl.BlockSpec((DA_V_DIM, 1), lambda b, i: (0, 0)),
        ],
        out_specs=pl.BlockSpec((tq, DA_WIDTH), lambda b, i: (b * nq + i, 0)),
        out_shape=jax.ShapeDtypeStruct((n, DA_WIDTH), BF16),
        scratch_shapes=[
            pltpu.VMEM((DA_MAPS, HEAD_GROUP, tq), BF16),
            pltpu.VMEM((DA_MAPS * DA_VROWS, tq), F32),
            pltpu.VMEM((DA_MAPS, tq), F32),
            pltpu.VMEM((DA_MAPS, 2 * KSUB, tq), F32),
            pltpu.VMEM((DA_MAPS, 2 * KSUB, tq), F32),
        ],
        compiler_params=_cparams(("parallel", "arbitrary")),
        name="diff_attn",
    )(dq, dk, dvt, lam, out_g_col)


def _merge_kernel(x_ref, ya_ref, yb_ref, yc_ref, gp_ref, bg_ref, wb_ref, wo_ref, ng_ref, xo_ref, xn_ref):
    merged = None
    for br, y_ref in enumerate((ya_ref, yb_ref, yc_ref)):
        sl = slice(br * D_MODEL, (br + 1) * D_MODEL)
        gate = _sigmoid(gp_ref[:, sl].astype(F32) + bg_ref[:, sl])
        term = gate * _dot(y_ref[...], wb_ref[br])
        merged = term if merged is None else merged + term
    xo = x_ref[...] + _dot(merged.astype(BF16), wo_ref[...])
    xo_ref[...] = xo
    ms = jnp.mean(xo * xo, axis=-1, keepdims=True)
    xn_ref[...] = (xo * lax.rsqrt(ms + EPS) * ng_ref[...]).astype(BF16)


def _merge(x2, ya, yb, yc, z, b_gate, w_branch, w_out, norm_g, *, tm=512):
    n = x2.shape[0]
    tm = min(tm, n)
    row = lambda w: pl.BlockSpec((tm, w), lambda i: (i, 0))
    return pl.pallas_call(
        _merge_kernel,
        grid=(n // tm,),
        in_specs=[
            row(D_MODEL), row(512), row(512), row(512),
            pl.BlockSpec((tm, N_BRANCH * D_MODEL), lambda i: (i, COL_GATE)),
            pl.BlockSpec((1, N_BRANCH * D_MODEL), lambda i: (0, 0)),
            pl.BlockSpec((N_BRANCH, 512, D_MODEL), lambda i: (0, 0, 0)),
            pl.BlockSpec((D_MODEL, D_MODEL), lambda i: (0, 0)),
            pl.BlockSpec((1, D_MODEL), lambda i: (0, 0)),
        ],
        out_specs=[row(D_MODEL), row(D_MODEL)],
        out_shape=[jax.ShapeDtypeStruct((n, D_MODEL), F32), jax.ShapeDtypeStruct((n, D_MODEL), BF16)],
        compiler_params=_cparams(("parallel",)),
        name="merge_out",
    )(x2, ya, yb, yc, z, b_gate, w_branch, w_out, norm_g)


def _ffn_kernel(xn_ref, x_ref, wg_ref, wu_ref, wd_ref, o_ref, *, fc):
    xn = xn_ref[...]
    acc = x_ref[...]
    for c in range(FF_DIM // fc):
        cols = slice(c * fc, (c + 1) * fc)
        g = _dot(xn, wg_ref[:, cols])
        u = _dot(xn, wu_ref[:, cols])
        acc = acc + _dot((g * _sigmoid(g) * u).astype(BF16), wd_ref[cols, :])
    o_ref[...] = acc


def _ffn(xn, x2, wg, wu, wd, *, tm=512, fc=256):
    n = x2.shape[0]
    tm = min(tm, n)
    resident = lambda shape: pl.BlockSpec(shape, lambda i: (0, 0), pipeline_mode=pl.Buffered(1))
    return pl.pallas_call(
        functools.partial(_ffn_kernel, fc=fc),
        grid=(n // tm,),
        in_specs=[
            pl.BlockSpec((tm, D_MODEL), lambda i: (i, 0)),
            pl.BlockSpec((tm, D_MODEL), lambda i: (i, 0)),
            resident((D_MODEL, FF_DIM)),
            resident((D_MODEL, FF_DIM)),
            resident((FF_DIM, D_MODEL)),
        ],
        out_specs=pl.BlockSpec((tm, D_MODEL), lambda i: (i, 0)),
        out_shape=jax.ShapeDtypeStruct((n, D_MODEL), F32),
        compiler_params=_cparams(("parallel",)),
        name="ffn",
    )(xn, x2, wg, wu, wd)


def _merge_ffn_kernel(x_ref, ya_ref, yb_ref, yc_ref, gp_ref, bg_ref, wb_ref, wo_ref, ng_ref,
                      wg_ref, wu_ref, wd_ref, o_ref, xo_ref, xn_ref, *, fc):
    _merge_kernel(x_ref, ya_ref, yb_ref, yc_ref, gp_ref, bg_ref, wb_ref, wo_ref, ng_ref, xo_ref, xn_ref)
    _ffn_kernel(xn_ref, xo_ref, wg_ref, wu_ref, wd_ref, o_ref, fc=fc)


def _merge_ffn(x2, ya, yb, yc, z, b_gate, w_branch, w_out, norm_g, wg, wu, wd, *, tm=512, fc=256):
    n = x2.shape[0]
    tm = min(tm, n)
    row = lambda w: pl.BlockSpec((tm, w), lambda i: (i, 0))
    resident = lambda shape: pl.BlockSpec(shape, lambda i: (0,) * len(shape), pipeline_mode=pl.Buffered(1))
    return pl.pallas_call(
        functools.partial(_merge_ffn_kernel, fc=fc),
        grid=(n // tm,),
        in_specs=[
            row(D_MODEL), row(512), row(512), row(512),
            pl.BlockSpec((tm, N_BRANCH * D_MODEL), lambda i: (i, COL_GATE)),
            resident((1, N_BRANCH * D_MODEL)),
            resident((N_BRANCH, 512, D_MODEL)),
            resident((D_MODEL, D_MODEL)),
            resident((1, D_MODEL)),
            resident((D_MODEL, FF_DIM)),
            resident((D_MODEL, FF_DIM)),
            resident((FF_DIM, D_MODEL)),
        ],
        out_specs=row(D_MODEL),
        out_shape=jax.ShapeDtypeStruct((n, D_MODEL), F32),
        scratch_shapes=[pltpu.VMEM((tm, D_MODEL), F32), pltpu.VMEM((tm, D_MODEL), BF16)],
        compiler_params=_cparams(("parallel",)),
        name="merge_ffn",
    )(x2, ya, yb, yc, z, b_gate, w_branch, w_out, norm_g, wg, wu, wd)


def _rope_tables(seq):
    inv = 1.0 / jnp.power(ROPE_THETA, jnp.arange(0, ROPE_DIM, 2, dtype=F32) / ROPE_DIM)
    ang = jnp.arange(seq, dtype=F32)[:, None] * inv[None, :]
    cos, sin = jnp.cos(ang), jnp.sin(ang)
    c64 = jnp.concatenate([cos, cos], axis=1)
    s64 = jnp.concatenate([-sin, sin], axis=1)
    return jnp.tile(c64, (1, 4)), jnp.tile(s64, (1, 4))


def _layer(x2, li, batch, seq, consts, norm_mix_g, w_in_l, b_gate, conv_w, conv_b, gate_b, m_norm_g,
           kv_norm_g, w_kv_up, sq_g, sk_g, dq_g, dk_g, lam, d_out_g, w_branch, w_out,
           norm_ffn_g, w_gate_up, w_down):
    rope_c, rope_s, grp, tri, shifts = consts
    lam_init = 0.8 - 0.6 * math.exp(-0.3 * li)
    tile8 = lambda g: jnp.tile(g, 8)[None, :]

    z, misc = _inproj(x2, norm_mix_g[None, :], w_in_l)

    gate_row = jnp.zeros((1, LANE), F32).at[0, MISC_IF:MISC_IF + 2 * M_HEADS].set(gate_b)
    ya = _mlstm(z, misc, conv_w, conv_b[None, :], gate_row, m_norm_g.reshape(1, M_WIDTH), tri, shifts,
                batch, seq)

    sq, sk, svt, iq, ik4, dq, dk, dvt = _prep(
        z, rope_c, rope_s, grp, kv_norm_g[None, :], w_kv_up.astype(BF16),
        tile8(sq_g), tile8(sk_g), tile8(dq_g), tile8(dk_g), seq)
    yb = _dsa(misc, sq, sk, svt, iq, ik4, batch, seq)
    yc = _diff_attn(dq, dk, dvt, lam, d_out_g[:, None], batch, seq, lam_init)

    return _merge_ffn(x2, ya, yb, yc, z, b_gate[None, :], w_branch.astype(BF16), w_out.astype(BF16),
                      norm_ffn_g[None, :], w_gate_up[:, :FF_DIM].astype(BF16),
                      w_gate_up[:, FF_DIM:].astype(BF16), w_down.astype(BF16))


def kernel(x, norm_mix_g, w_in, b_gate, mlstm_conv_w, mlstm_conv_b, mlstm_gate_b, mlstm_norm_g,
           dsa_kv_norm_g, dsa_w_kv_up, dsa_q_norm_g, dsa_k_norm_g, diff_q_norm_g, diff_k_norm_g,
           diff_lambda, diff_out_norm_g, w_branch, w_out, norm_ffn_g, w_gate_up, w_down):
    batch, seq, d = x.shape
    depth = w_in.shape[0]
    assert w_in.shape[2] == IN_COLS and d == D_MODEL
    rope_c, rope_s = _rope_tables(seq)
    gi = jnp.arange(512) // ROPE_DIM
    grp = (gi[:, None] == gi[None, :]).astype(BF16)
    ti = jnp.arange(M_CHUNK)
    tri = (ti[:, None] >= ti[None, :]).astype(BF16)
    ri = jnp.arange(M_TAIL + M_CHUNK)
    shifts = jnp.stack([(ri[None, :] == ti[:, None] + M_TAIL - j) for j in range(1, M_CONV)]).astype(BF16)
    consts = (rope_c, rope_s, grp, tri, shifts)
    w_in_l = _wlayout(w_in)
    x2 = x.reshape(batch * seq, d)
    for li in range(depth):
        x2 = _layer(x2, li, batch, seq, consts, norm_mix_g[li], w_in_l[li], b_gate[li], mlstm_conv_w[li],
                    mlstm_conv_b[li], mlstm_gate_b[li], mlstm_norm_g[li], dsa_kv_norm_g[li],
                    dsa_w_kv_up[li], dsa_q_norm_g[li], dsa_k_norm_g[li], diff_q_norm_g[li],
                    diff_k_norm_g[li], diff_lambda[li], diff_out_norm_g[li], w_branch[li], w_out[li],
                    norm_ffn_g[li], w_gate_up[li], w_down[li])
    return x2.reshape(batch, seq, d)
```
